```python
import math
import jax, jax.numpy as jnp
from jax import lax
import numpy as np

D_MODEL = 1024
BATCH = 4
SEQ = 4096
DEPTH = 1
DEC_BATCH = 32
DEC_SEQ = 8
PAST_LEN = 8192
PAGE_SIZE = 128

MIX_WIDTH = D_MODEL
A_HEADS = 4
A_DK = 128
A_DV = 128
A_WIDTH = A_HEADS * A_DV
CONV_W = 4
DELTA_CHUNK = 64
CONV_DIM = 2 * A_HEADS * A_DK + A_HEADS * A_DV
B_HEAD_DIM = 64
B_WIDTH = MIX_WIDTH - A_WIDTH
B_HEADS = B_WIDTH // B_HEAD_DIM
DILATED = ((128, 1), (512, 4), (2048, 16))
MAX_WINDOW = 2048
D_FF = ((8 * D_MODEL // 3 + 255) // 256) * 256
EPS = 1e-6
NEG = -1e30
IN_COLS = CONV_DIM + A_WIDTH + 2 * A_HEADS + 3 * B_WIDTH
SPLITS = (CONV_DIM, CONV_DIM + A_WIDTH, CONV_DIM + A_WIDTH + A_HEADS, CONV_DIM + A_WIDTH + 2 * A_HEADS)

kernel_name = "hymba_gdn_dilated_swa_decode_step"


def rmsnorm(x, w):
    xf = x.astype(jnp.float32)
    y = xf * lax.rsqrt(jnp.mean(xf * xf, axis=-1, keepdims=True) + EPS)
    return (y * w.astype(jnp.float32)).astype(x.dtype)


def l2norm(x):
    return x * lax.rsqrt(jnp.sum(x * x, axis=-1, keepdims=True) + EPS)


def alibi_slopes():
    return 2.0 ** (-8.0 * jnp.arange(1, B_HEADS + 1, dtype=jnp.float32) / B_HEADS)


def causal_conv(u, buf, w):
    ext = jnp.concatenate([buf.astype(u.dtype), u], axis=1)
    T = u.shape[1]
    out = sum(ext[:, i:i + T] * w[i] for i in range(CONV_W))
    return jax.nn.silu(out), ext[:, ext.shape[1] - (CONV_W - 1):]


def gated_delta_rule(q, k, v, g, beta, s0):
    B, T, H, DK = q.shape
    DV = v.shape[-1]
    C = min(DELTA_CHUNK, T)
    Tp = -(-T // C) * C
    pad = Tp - T
    if pad:
        pw = ((0, 0), (0, pad), (0, 0), (0, 0))
        q, k, v = jnp.pad(q, pw), jnp.pad(k, pw), jnp.pad(v, pw)
        g, beta = jnp.pad(g, pw[:3]), jnp.pad(beta, pw[:3])
    N = Tp // C

    def chunks(a):
        a = a.reshape((B, N, C, H) + a.shape[3:])
        return jnp.moveaxis(a, (1, 3), (0, 2))

    qc, kc, vc, bc = chunks(q), chunks(k), chunks(v), chunks(beta)
    gc = jnp.cumsum(chunks(g), axis=-1)
    idx = jnp.arange(C)
    incl = idx[:, None] >= idx[None, :]
    strict = idx[:, None] > idx[None, :]
    decay = jnp.exp(jnp.where(incl, gc[..., :, None] - gc[..., None, :], -jnp.inf))
    kb = kc * bc[..., None]
    lower = jnp.where(strict, jnp.einsum('nbhcd,nbhsd->nbhcs', kb, kc) * decay, 0.0)
    eye = jnp.eye(C, dtype=jnp.float32)
    t_inv = lax.linalg.triangular_solve(eye + lower, jnp.broadcast_to(eye, lower.shape),
                                        left_side=True, lower=True, unit_diagonal=True)
    u = t_inv @ (vc * bc[..., None])
    w = t_inv @ (kb * jnp.exp(gc)[..., None])
    qk = jnp.einsum('nbhcd,nbhsd->nbhcs', qc, kc) * decay
    g_last = gc[..., -1]

    def step(S, xs):
        q_i, k_i, u_i, w_i, qk_i, g_i, gl_i = xs
        e = u_i - jnp.einsum('bhcd,bhde->bhce', w_i, S)
        o = (jnp.einsum('bhcd,bhde->bhce', q_i * jnp.exp(g_i)[..., None], S)
             + jnp.einsum('bhcs,bhse->bhce', qk_i, e))
        S = (S * jnp.exp(gl_i)[..., None, None]
             + jnp.einsum('bhcd,bhce->bhde', k_i * jnp.exp(gl_i[..., None] - g_i)[..., None], e))
        return S, o

    s_final, o = lax.scan(step, s0, (qc, kc, u, w, qk, gc, g_last))
    o = jnp.moveaxis(o, (0, 2), (1, 3)).reshape(B, Tp, H, DV)[:, :T]
    return o, s_final


def dilated_branch_prompt(q, k, v, slopes, dil, steps):
    B, S, H, Dh = q.shape
    L = S // dil
    blk = steps
    nb = -(-L // blk)
    Lp = nb * blk

    def split_res(a):
        a = a.reshape(B, L, dil, H, Dh)
        return jnp.pad(a, ((0, 0), (0, Lp - L), (0, 0), (0, 0), (0, 0)))

    front = ((0, 0), (blk, 0), (0, 0), (0, 0), (0, 0))
    qr = split_res(q).reshape(B, nb, blk, dil, H, Dh)
    kr = jnp.pad(split_res(k), front).reshape(B, nb + 1, blk, dil, H, Dh)
    vr = jnp.pad(split_res(v), front).reshape(B, nb + 1, blk, dil, H, Dh)
    kw = jnp.concatenate([kr[:, :-1], kr[:, 1:]], axis=2)
    vw = jnp.concatenate([vr[:, :-1], vr[:, 1:]], axis=2)
    s = jnp.einsum('bnqrhd,bnkrhd->bnrhqk', qr, kw) * (Dh ** -0.5)
    qq = jnp.arange(blk)[:, None]
    kk = jnp.arange(2 * blk)[None, :]
    j = qq + blk - kk
    u_key = jnp.arange(nb)[:, None, None] * blk - blk + kk[None]
    valid = (j >= 0) & (j <= steps) & (u_key >= 0)
    s = s - slopes[:, None, None] * (j * dil).astype(jnp.float32)
    s = jnp.where(valid[None, :, None, None], s, NEG)
    m = jnp.max(s, axis=-1)
    p = jnp.exp(s - m[..., None])
    l = jnp.sum(p, axis=-1)
    num = jnp.einsum('bnrhqk,bnkrhd->bnqrhd', p, vw)
    m = m.transpose(0, 1, 4, 2, 3).reshape(B, Lp * dil, H)[:, :S]
    l = l.transpose(0, 1, 4, 2, 3).reshape(B, Lp * dil, H)[:, :S]
    num = num.reshape(B, Lp * dil, H, Dh)[:, :S]
    return m, l, num


def dilated_branch_sample(q, kc, vc, slopes, dil, steps, n_past):
    T = q.shape[1]
    j = jnp.arange(steps + 1)
    idx = n_past + jnp.arange(T)[:, None] - j[None, :] * dil
    valid = idx >= 0
    idxc = jnp.maximum(idx, 0)
    kg = jnp.take(kc, idxc, axis=1)
    vg = jnp.take(vc, idxc, axis=1)
    s = jnp.einsum('bthd,btjhd->bhtj', q, kg) * (q.shape[-1] ** -0.5)
    s = s - slopes[:, None, None] * (j * dil).astype(jnp.float32)[None, None, :]
    s = jnp.where(valid[None, None], s, NEG)
    m = jnp.max(s, axis=-1)
    p = jnp.exp(s - m[..., None])
    l = jnp.sum(p, axis=-1)
    num = jnp.einsum('bhtj,btjhd->bthd', p, vg)
    return m.transpose(0, 2, 1), l.transpose(0, 2, 1), num


def combine_by_denominator(parts):
    m = jnp.stack([pt[0] for pt in parts])
    l = jnp.stack([pt[1] for pt in parts])
    num = jnp.stack([pt[2] for pt in parts])
    wgt = jnp.exp(m - jnp.max(m, axis=0, keepdims=True))
    den = jnp.sum(wgt * l, axis=0)
    return jnp.sum(wgt[..., None] * num, axis=0) / den[..., None]


def token_mix(h, conv_buf, s0, win_k, win_v, w_in, w_conv, a_log, dt_bias, norm_out_a, norm_out_b, w_out):
    B, T, _ = h.shape
    f32 = jnp.float32
    proj = h @ w_in
    qkv_a, z_a, b_a, a_a, qkv_b = jnp.split(proj, SPLITS, axis=-1)
    conv_out, new_conv = causal_conv(qkv_a, conv_buf, w_conv)
    qa, ka, va = jnp.split(conv_out.astype(f32), [A_HEADS * A_DK, 2 * A_HEADS * A_DK], axis=-1)
    qa = l2norm(qa.reshape(B, T, A_HEADS, A_DK)) * (A_DK ** -0.5)
    ka = l2norm(ka.reshape(B, T, A_HEADS, A_DK))
    va = va.reshape(B, T, A_HEADS, A_DV)
    beta = jax.nn.sigmoid(b_a.astype(f32))
    g = -jnp.exp(a_log.astype(f32)) * jax.nn.softplus(a_a.astype(f32) + dt_bias.astype(f32))
    o_a, s_new = gated_delta_rule(qa, ka, va, g, beta, s0.astype(f32))
    o_a = rmsnorm(o_a, norm_out_a) * jax.nn.silu(z_a.astype(f32).reshape(B, T, A_HEADS, A_DV))
    qb, kb, vb = [t.reshape(B, T, B_HEADS, B_HEAD_DIM) for t in jnp.split(qkv_b, 3, axis=-1)]
    slopes = alibi_slopes()
    qf = qb.astype(f32)
    if win_k is None:
        kf, vf = kb.astype(f32), vb.astype(f32)
        parts = [dilated_branch_prompt(qf, kf, vf, slopes, d, w // d) for (w, d) in DILATED]
        keep = min(MAX_WINDOW, T)
        new_k, new_v = kb[:, T - keep:], vb[:, T - keep:]
    else:
        n_past = win_k.shape[1]
        kc = jnp.concatenate([win_k.astype(kb.dtype), kb], axis=1)
        vc = jnp.concatenate([win_v.astype(vb.dtype), vb], axis=1)
        kcf, vcf = kc.astype(f32), vc.astype(f32)
        parts = [dilated_branch_sample(qf, kcf, vcf, slopes, d, w // d, n_past) for (w, d) in DILATED]
        keep = min(MAX_WINDOW, n_past + T)
        new_k, new_v = kc[:, n_past + T - keep:], vc[:, n_past + T - keep:]
    o_b = rmsnorm(combine_by_denominator(parts), norm_out_b)
    mixed = jnp.concatenate([o_a.reshape(B, T, A_WIDTH), o_b.reshape(B, T, B_WIDTH)], axis=-1).astype(h.dtype)
    return mixed @ w_out, new_conv, s_new.astype(h.dtype), new_k, new_v


def hybrid_block(x, conv_buf, s0, win_k, win_v, norm_mix, w_in, w_conv, a_log, dt_bias,
                 norm_out_a, norm_out_b, w_out, norm_ffn, w_gate, w_up, w_down):
    mix, new_conv, new_rec, new_k, new_v = token_mix(rmsnorm(x, norm_mix), conv_buf, s0, win_k, win_v,
                                                     w_in, w_conv, a_log, dt_bias, norm_out_a, norm_out_b, w_out)
    x = x + mix
    hf = rmsnorm(x, norm_ffn)
    x = x + (jax.nn.silu(hf @ w_gate) * (hf @ w_up)) @ w_down
    return x, new_conv, new_rec, new_k, new_v


def setup_inputs(seed: int = 0) -> dict:
    key = jax.random.key(seed)
    ks = jax.random.split(key, 20)
    f32 = jnp.float32
    win_buf = min(MAX_WINDOW, PAST_LEN)

    def nrm(k, shape, scale):
        return jax.random.normal(k, shape, f32) * scale

    dt = jnp.exp(jax.random.uniform(ks[10], (DEPTH, A_HEADS), f32, minval=math.log(1e-3), maxval=math.log(1e-1)))
    return {
        "x_prompt": nrm(ks[0], (BATCH, SEQ, D_MODEL), 1.0),
        "x_sample": nrm(ks[1], (DEC_BATCH, DEC_SEQ, D_MODEL), 1.0),
        "state_conv": nrm(ks[2], (DEPTH, DEC_BATCH, CONV_W - 1, CONV_DIM), 1.0),
        "state_rec": nrm(ks[3], (DEPTH, DEC_BATCH, A_HEADS, A_DK, A_DV), 0.5),
        "cache_win_k": nrm(ks[4], (DEPTH, DEC_BATCH, win_buf, B_HEADS, B_HEAD_DIM), 1.0),
        "cache_win_v": nrm(ks[5], (DEPTH, DEC_BATCH, win_buf, B_HEADS, B_HEAD_DIM), 1.0),
        "norm_mix": 1.0 + nrm(ks[6], (DEPTH, D_MODEL), 0.02),
        "w_in": nrm(ks[7], (DEPTH, D_MODEL, IN_COLS), D_MODEL ** -0.5),
        "w_conv": nrm(ks[8], (DEPTH, CONV_W, CONV_DIM), CONV_W ** -0.5),
        "a_log": jnp.log(jax.random.uniform(ks[9], (DEPTH, A_HEADS), f32, minval=1.0, maxval=16.0)),
        "dt_bias": dt + jnp.log(-jnp.expm1(-dt)),
        "norm_out_a": 1.0 + nrm(ks[11], (DEPTH, A_DV), 0.02),
        "norm_out_b": 1.0 + nrm(ks[12], (DEPTH, B_HEAD_DIM), 0.02),
        "w_out": nrm(ks[13], (DEPTH, MIX_WIDTH, D_MODEL), MIX_WIDTH ** -0.5),
        "norm_ffn": 1.0 + nrm(ks[14], (DEPTH, D_MODEL), 0.02),
        "w_gate": nrm(ks[15], (DEPTH, D_MODEL, D_FF), D_MODEL ** -0.5),
        "w_up": nrm(ks[16], (DEPTH, D_MODEL, D_FF), D_MODEL ** -0.5),
        "w_down": nrm(ks[17], (DEPTH, D_FF, D_MODEL), D_FF ** -0.5),
        "norm_final": 1.0 + nrm(ks[18], (D_MODEL,), 0.02),
    }


def reference(x_prompt, x_sample, state_conv, state_rec, cache_win_k, cache_win_v, norm_mix, w_in, w_conv,
              a_log, dt_bias, norm_out_a, norm_out_b, w_out, norm_ffn, w_gate, w_up, w_down, norm_final):
    xp, xs = x_prompt, x_sample
    bp = xp.shape[0]
    pc, pr, pk, pv = [], [], [], []
    sc, sr, sk, sv = [], [], [], []
    for layer in range(DEPTH):
        weights = (norm_mix[layer], w_in[layer], w_conv[layer], a_log[layer], dt_bias[layer], norm_out_a[layer],
                   norm_out_b[layer], w_out[layer], norm_ffn[layer], w_gate[layer], w_up[layer], w_down[layer])
        zero_conv = jnp.zeros((bp, CONV_W - 1, CONV_DIM), xp.dtype)
        zero_rec = jnp.zeros((bp, A_HEADS, A_DK, A_DV), jnp.float32)
        xp, c, r, k, v = hybrid_block(xp, zero_conv, zero_rec, None, None, *weights)
        pc.append(c); pr.append(r); pk.append(k); pv.append(v)
        xs, c, r, k, v = hybrid_block(xs, state_conv[layer], state_rec[layer], cache_win_k[layer],
                                      cache_win_v[layer], *weights)
        sc.append(c); sr.append(r); sk.append(k); sv.append(v)
    y_prompt = rmsnorm(xp, norm_final)
    y_sample = rmsnorm(xs, norm_final)
    return (y_prompt, y_sample, jnp.stack(pc), jnp.stack(pr), jnp.stack(pk), jnp.stack(pv),
            jnp.stack(sc), jnp.stack(sr), jnp.stack(sk), jnp.stack(sv))
```

```python
import functools

import jax
import jax.numpy as jnp
from jax import lax
from jax.experimental import pallas as pl
from jax.experimental.pallas import tpu as pltpu

F32 = jnp.float32
BF16 = jnp.bfloat16

D_MODEL = 1024
A_HEADS = 4
A_DK = 128
A_DV = 128
A_WIDTH = A_HEADS * A_DV
CONV_W = 4
CONV_DIM = 2 * A_HEADS * A_DK + A_HEADS * A_DV
B_HEADS = 8
B_HEAD_DIM = 64
B_WIDTH = B_HEADS * B_HEAD_DIM
DILATED = ((128, 1), (512, 4), (2048, 16))
STEPS = 128
MAX_WINDOW = 2048
EPS = 1e-6
NEG = -1e30
CHUNK = 64
GATE_PAD = 128
COL_Z = CONV_DIM
COL_G = CONV_DIM + A_WIDTH
COLS_A = COL_G + GATE_PAD
COLS_B = 3 * B_WIDTH
VMEM_LIMIT = 56 * 1024 * 1024


def _dot(a, b):
    return jnp.dot(a.astype(BF16), b.astype(BF16), preferred_element_type=F32)


def _dot_nt(a, b):
    return lax.dot_general(a.astype(BF16), b.astype(BF16), (((1,), (1,)), ((), ())),
                           preferred_element_type=F32)


def _dot_tn(a, b):
    return lax.dot_general(a.astype(BF16), b.astype(BF16), (((0,), (0,)), ((), ())),
                           preferred_element_type=F32)


def _split2(a):
    hi = a.astype(BF16)
    lo = (a - hi.astype(F32)).astype(BF16)
    return hi, lo


def _dot3(a, b):
    ah, al = _split2(a)
    bh, bl = _split2(b)
    d = functools.partial(jnp.dot, preferred_element_type=F32)
    return d(ah, bh) + d(ah, bl) + d(al, bh)


def _sigmoid(x):
    return 1.0 / (1.0 + jnp.exp(-x))


def _silu(x):
    return x * _sigmoid(x)


def _softplus(x):
    return jnp.maximum(x, 0.0) + jnp.log(1.0 + jnp.exp(-jnp.abs(x)))


def _rms(x, w):
    return x * lax.rsqrt(jnp.mean(x * x, axis=-1, keepdims=True) + EPS) * w


def _norm_proj_kernel(x_ref, nw_ref, w_ref, oa_ref, ob_ref):
    h = _rms(x_ref[...], nw_ref[...]).astype(BF16)
    oa_ref[...] = jnp.dot(h, w_ref[:, :COLS_A], preferred_element_type=F32)
    ob_ref[...] = jnp.dot(h, w_ref[:, COLS_A:], preferred_element_type=F32)


def _norm_proj(x2d, norm_w, w_p, tm):
    t = x2d.shape[0]
    once = pl.Buffered(1)
    return pl.pallas_call(
        _norm_proj_kernel,
        grid=(t // tm,),
        in_specs=[pl.BlockSpec((tm, D_MODEL), lambda i: (i, 0)),
                  pl.BlockSpec((1, D_MODEL), lambda i: (0, 0), pipeline_mode=once),
                  pl.BlockSpec((D_MODEL, COLS_A + COLS_B), lambda i: (0, 0), pipeline_mode=once)],
        out_specs=[pl.BlockSpec((tm, COLS_A), lambda i: (i, 0)),
                   pl.BlockSpec((tm, COLS_B), lambda i: (i, 0))],
        out_shape=[jax.ShapeDtypeStruct((t, COLS_A), F32),
                   jax.ShapeDtypeStruct((t, COLS_B), F32)],
        compiler_params=pltpu.CompilerParams(dimension_semantics=("arbitrary",),
                                             vmem_limit_bytes=VMEM_LIMIT),
        name="norm_proj",
    )(x2d, norm_w, w_p)


def _gdn_kernel(xa_ref, z_ref, gt_ref, conv0_ref, s0_ref, wconv_ref, gp_ref, nwa_ref,
                o_ref, snew_ref, ext_ref, s_ref, *, rows):
    n = pl.program_id(1)
    c = CHUNK

    @pl.when(n == 0)
    def _():
        ext_ref[0:8, :] = conv0_ref[...]
        s_ref[...] = s0_ref[...]

    ext_ref[8:8 + rows, :] = xa_ref[...]
    if rows < c:
        ext_ref[8 + rows:8 + c, :] = jnp.zeros((c - rows, CONV_DIM), F32)

    conv = ext_ref[5:5 + c, :] * wconv_ref[0:1, :]
    for i in range(1, CONV_W):
        conv = conv + ext_ref[5 + i:5 + i + c, :] * wconv_ref[i:i + 1, :]
    conv = _silu(conv)
    ext_ref[0:8, :] = ext_ref[c:c + 8, :]

    row = lax.broadcasted_iota(jnp.int32, (c, 1), 0)
    live = row < rows
    if rows < c:
        conv = jnp.where(live, conv, 0.0)

    gt = gt_ref[...]
    if rows < c:
        gt = jnp.concatenate([gt, jnp.zeros((c - rows, GATE_PAD), F32)], axis=0)
    beta_all = _sigmoid(gt)
    g_all = -jnp.exp(gp_ref[0:1, :]) * _softplus(gt + gp_ref[1:2, :])
    lane = lax.broadcasted_iota(jnp.int32, (c, GATE_PAD), 1)
    g_all = jnp.where((lane >= A_HEADS) & (lane < 2 * A_HEADS) & live, g_all, 0.0)
    beta_all = jnp.where(live, beta_all, 0.0)

    ii = lax.broadcasted_iota(jnp.int32, (c, c), 0)
    jj = lax.broadcasted_iota(jnp.int32, (c, c), 1)
    incl = ii >= jj
    strict = ii > jj
    tril = jnp.where(incl, 1.0, 0.0).astype(BF16)
    eye = jnp.where(ii == jj, 1.0, 0.0).astype(F32)

    g_hi, g_lo = _split2(g_all)
    g_lo2 = (g_all - g_hi.astype(F32) - g_lo.astype(F32)).astype(BF16)
    dd = functools.partial(jnp.dot, preferred_element_type=F32)
    gc_all = dd(tril, g_hi) + dd(tril, g_lo) + dd(tril, g_lo2)
    gl_all = gc_all[c - 1:c, :]

    z = z_ref[...]
    nwa = nwa_ref[...]
    for h in range(A_HEADS):
        q = conv[:, h * A_DK:(h + 1) * A_DK]
        k = conv[:, A_HEADS * A_DK + h * A_DK:A_HEADS * A_DK + (h + 1) * A_DK]
        v = conv[:, 2 * A_HEADS * A_DK + h * A_DV:2 * A_HEADS * A_DK + (h + 1) * A_DV]
        q = q * lax.rsqrt(jnp.sum(q * q, axis=-1, keepdims=True) + EPS) * (A_DK ** -0.5)
        k = k * lax.rsqrt(jnp.sum(k * k, axis=-1, keepdims=True) + EPS)
        beta = beta_all[:, h:h + 1]
        g = g_all[:, A_HEADS + h:A_HEADS + h + 1]
        gc = gc_all[:, A_HEADS + h:A_HEADS + h + 1]
        gl = gl_all[:, A_HEADS + h:A_HEADS + h + 1]

        gm = jnp.where(strict, g, 0.0)
        gm_hi, gm_lo = _split2(gm)
        gm_lo2 = (gm - gm_hi.astype(F32) - gm_lo.astype(F32)).astype(BF16)
        e_mat = dd(tril, gm_hi) + dd(tril, gm_lo) + dd(tril, gm_lo2)
        decay = jnp.where(incl, jnp.exp(jnp.where(incl, e_mat, 0.0)), 0.0)

        kb = k * beta
        lower = jnp.where(strict, _dot_nt(kb, k) * decay, 0.0)
        t_inv = eye - lower
        pw = lower
        for _ in range(c.bit_length() - 2):
            pw = _dot3(pw, pw)
            t_inv = t_inv + _dot3(t_inv, pw)

        rhs = jnp.concatenate([v * beta, kb * jnp.exp(gc)], axis=1)
        uw = _dot(t_inv, rhs)
        u = uw[:, :A_DV]
        w = uw[:, A_DV:]
        qk = jnp.where(incl, _dot_nt(q, k) * decay, 0.0)

        s = s_ref[h]
        e = u - _dot(w, s)
        o = _dot(q * jnp.exp(gc), s) + _dot(qk, e)
        s_ref[h] = s * jnp.exp(gl) + _dot_tn(k * jnp.exp(gl - gc), e)

        o = o if rows == c else o[:rows]
        o_ref[:, h * A_DV:(h + 1) * A_DV] = _rms(o, nwa) * _silu(z[:, h * A_DV:(h + 1) * A_DV])

    @pl.when(n == pl.num_programs(1) - 1)
    def _():
        snew_ref[...] = s_ref[...]


def _gdn(proj, conv0, s0, w_conv, gate_par, norm_a):
    b, t, _ = proj.shape
    rows = min(CHUNK, t)
    assert t % rows == 0
    n = t // rows
    kernel = functools.partial(_gdn_kernel, rows=rows)
    return pl.pallas_call(
        kernel,
        grid=(b, n),
        in_specs=[pl.BlockSpec((None, rows, CONV_DIM), lambda i, j: (i, j, 0)),
                  pl.BlockSpec((None, rows, A_WIDTH), lambda i, j: (i, j, COL_Z // A_WIDTH)),
                  pl.BlockSpec((None, rows, GATE_PAD), lambda i, j: (i, j, COL_G // GATE_PAD)),
                  pl.BlockSpec((None, 8, CONV_DIM), lambda i, j: (i, 0, 0)),
                  pl.BlockSpec((None, A_HEADS, A_DK, A_DV), lambda i, j: (i, 0, 0, 0)),
                  pl.BlockSpec((CONV_W, CONV_DIM), lambda i, j: (0, 0)),
                  pl.BlockSpec((8, GATE_PAD), lambda i, j: (0, 0)),
                  pl.BlockSpec((1, A_DV), lambda i, j: (0, 0))],
        out_specs=[pl.BlockSpec((None, rows, A_WIDTH), lambda i, j: (i, j, 0)),
                   pl.BlockSpec((None, A_HEADS, A_DK, A_DV), lambda i, j: (i, 0, 0, 0))],
        out_shape=[jax.ShapeDtypeStruct((b, t, A_WIDTH), F32),
                   jax.ShapeDtypeStruct((b, A_HEADS, A_DK, A_DV), F32)],
        scratch_shapes=[pltpu.VMEM((8 + CHUNK, CONV_DIM), F32),
                        pltpu.VMEM((A_HEADS, A_DK, A_DV), F32)],
        compiler_params=pltpu.CompilerParams(dimension_semantics=("arbitrary", "arbitrary"),
                                             vmem_limit_bytes=VMEM_LIMIT),
        name="gdn",
    )(proj, proj, proj, conv0, s0, w_conv, gate_par, norm_a)


def _slope(h):
    return 2.0 ** (-8.0 * (h + 1) / B_HEADS)


def _attn_prompt_kernel(q_ref, kp_ref, kc_ref, vp_ref, vc_ref, num_ref, ml_ref, *, dil):
    n = pl.program_id(2)
    blk = STEPS
    qq = lax.broadcasted_iota(jnp.int32, (blk, 2 * blk), 0)
    kk = lax.broadcasted_iota(jnp.int32, (blk, 2 * blk), 1)
    j = qq + blk - kk
    valid = (j >= 0) & (j <= STEPS) & (n * blk - blk + kk >= 0)
    dist = (j * dil).astype(F32)
    lane = lax.broadcasted_iota(jnp.int32, (blk, 128), 1)
    ml = jnp.zeros((blk, 128), F32)
    q = q_ref[...]
    kp = kp_ref[...]
    kc = kc_ref[...]
    vp = vp_ref[...]
    vc = vc_ref[...]
    for h in range(B_HEADS):
        hs = slice(h * B_HEAD_DIM, (h + 1) * B_HEAD_DIM)
        kh = jnp.concatenate([kp[:, hs], kc[:, hs]], axis=0)
        vh = jnp.concatenate([vp[:, hs], vc[:, hs]], axis=0)
        s = _dot_nt(q[:, hs], kh) * (B_HEAD_DIM ** -0.5)
        s = s - _slope(h) * dist
        s = jnp.where(valid, s, NEG)
        m = jnp.max(s, axis=-1, keepdims=True)
        p = jnp.exp(s - m)
        l = jnp.sum(p, axis=-1, keepdims=True)
        num_ref[:, hs] = _dot(p, vh)
        ml = jnp.where(lane == h, m, ml)
        ml = jnp.where(lane == B_HEADS + h, l, ml)
    ml_ref[...] = ml


def _attn_prompt_branch(qkv, dil):
    b, s, _ = qkv.shape
    ln = s // dil
    assert ln % STEPS == 0
    nb = ln // STEPS
    x = qkv.reshape(b, ln, dil * 3 * B_WIDTH)
    blk = (None, STEPS, B_WIDTH)
    kernel = functools.partial(_attn_prompt_kernel, dil=dil)
    num, ml = pl.pallas_call(
        kernel,
        grid=(b, dil, nb),
        in_specs=[pl.BlockSpec(blk, lambda i, r, n: (i, n, 3 * r)),
                  pl.BlockSpec(blk, lambda i, r, n: (i, jnp.maximum(n - 1, 0), 3 * r + 1)),
                  pl.BlockSpec(blk, lambda i, r, n: (i, n, 3 * r + 1)),
                  pl.BlockSpec(blk, lambda i, r, n: (i, jnp.maximum(n - 1, 0), 3 * r + 2)),
                  pl.BlockSpec(blk, lambda i, r, n: (i, n, 3 * r + 2))],
        out_specs=[pl.BlockSpec(blk, lambda i, r, n: (i, n, r)),
                   pl.BlockSpec((None, STEPS, 128), lambda i, r, n: (i, n, r))],
        out_shape=[jax.ShapeDtypeStruct((b, ln, dil * B_WIDTH), F32),
                   jax.ShapeDtypeStruct((b, ln, dil * 128), F32)],
        compiler_params=pltpu.CompilerParams(
            dimension_semantics=("arbitrary", "arbitrary", "arbitrary"), vmem_limit_bytes=VMEM_LIMIT),
        name=f"attn_prompt_d{dil}",
    )(x, x, x, x, x)
    return num.reshape(b, s, B_WIDTH), ml.reshape(b, s, 128)


def _combine_kernel(n1_ref, n2_ref, n3_ref, m1_ref, m2_ref, m3_ref, nwb_ref, o_ref):
    nums = (n1_ref[...], n2_ref[...], n3_ref[...])
    mls = (m1_ref[...], m2_ref[...], m3_ref[...])
    nwb = nwb_ref[...]
    for h in range(B_HEADS):
        hs = slice(h * B_HEAD_DIM, (h + 1) * B_HEAD_DIM)
        ms = [ml[:, h:h + 1] for ml in mls]
        ls = [ml[:, B_HEADS + h:B_HEADS + h + 1] for ml in mls]
        mx = jnp.maximum(jnp.maximum(ms[0], ms[1]), ms[2])
        ws = [jnp.exp(m - mx) for m in ms]
        den = ws[0] * ls[0] + ws[1] * ls[1] + ws[2] * ls[2]
        o = (ws[0] * nums[0][:, hs] + ws[1] * nums[1][:, hs] + ws[2] * nums[2][:, hs]) / den
        o_ref[:, hs] = _rms(o, nwb)


def _combine(nums, mls, norm_b, tm):
    t = nums[0].shape[0]
    big = pl.BlockSpec((tm, B_WIDTH), lambda i: (i, 0))
    small = pl.BlockSpec((tm, 128), lambda i: (i, 0))
    return pl.pallas_call(
        _combine_kernel,
        grid=(t // tm,),
        in_specs=[big, big, big, small, small, small, pl.BlockSpec((1, B_HEAD_DIM), lambda i: (0, 0))],
        out_specs=big,
        out_shape=jax.ShapeDtypeStruct((t, B_WIDTH), F32),
        compiler_params=pltpu.CompilerParams(dimension_semantics=("arbitrary",),
                                             vmem_limit_bytes=VMEM_LIMIT),
        name="attn_combine",
    )(*nums, *mls, norm_b)


def _attn_sample_kernel(q_ref, kn_ref, vn_ref, kc_ref, vc_ref, nwb_ref, o_ref, ko_ref, vo_ref, *, n_past, t):
    keep = min(MAX_WINDOW, n_past + t)
    drop = n_past + t - keep
    ko_ref[0:keep - t, :] = kc_ref[drop:n_past, :]
    ko_ref[keep - t:keep, :] = kn_ref[...]
    vo_ref[0:keep - t, :] = vc_ref[drop:n_past, :]
    vo_ref[keep - t:keep, :] = vn_ref[...]

    pad = (-(n_past + t)) % 128
    nk = n_past + t + pad
    zpad = jnp.zeros((pad, B_WIDTH), BF16)
    k_all = jnp.concatenate([kc_ref[...].astype(BF16), kn_ref[...].astype(BF16), zpad], axis=0)
    v_all = jnp.concatenate([vc_ref[...].astype(BF16), vn_ref[...].astype(BF16), zpad], axis=0)

    rq = B_HEADS * t
    t_bits = t.bit_length() - 1
    d_bits = B_HEAD_DIM.bit_length() - 1
    rowi = lax.broadcasted_iota(jnp.int32, (rq, B_WIDTH), 0)
    lanei = lax.broadcasted_iota(jnp.int32, (rq, B_WIDTH), 1)
    own = lax.shift_right_logical(rowi, t_bits) == lax.shift_right_logical(lanei, d_bits)
    q_rep = jnp.concatenate([q_ref[...]] * B_HEADS, axis=0)
    q_blk = jnp.where(own, q_rep, 0.0)
    s = _dot_nt(q_blk, k_all) * (B_HEAD_DIM ** -0.5)

    r1 = lax.broadcasted_iota(jnp.int32, (rq, 1), 0)
    head = lax.shift_right_logical(r1, t_bits)
    slope = jnp.zeros((rq, 1), F32)
    for h in range(B_HEADS):
        slope = jnp.where(head == h, _slope(h), slope)
    key = lax.broadcasted_iota(jnp.int32, (rq, nk), 1)
    delta = n_past + (r1 & (t - 1)) - key
    s = s - slope * delta.astype(F32)
    in_range = (delta >= 0) & (key < n_past + t)

    ms, ls, nums = [], [], []
    for (_, dil) in DILATED:
        valid = in_range & ((delta & (dil - 1)) == 0) & (delta <= STEPS * dil)
        sd = jnp.where(valid, s, NEG)
        m = jnp.max(sd, axis=-1, keepdims=True)
        p = jnp.exp(sd - m)
        ms.append(m)
        ls.append(jnp.sum(p, axis=-1, keepdims=True))
        nums.append(jnp.dot(p.astype(BF16), v_all, preferred_element_type=F32))
    mx = jnp.maximum(jnp.maximum(ms[0], ms[1]), ms[2])
    ws = [jnp.exp(m - mx) for m in ms]
    den = ws[0] * ls[0] + ws[1] * ls[1] + ws[2] * ls[2]
    o = (ws[0] * nums[0] + ws[1] * nums[1] + ws[2] * nums[2]) / den
    o = jnp.where(own, o, 0.0)
    o = o * lax.rsqrt(jnp.sum(o * o, axis=-1, keepdims=True) / B_HEAD_DIM + EPS) * nwb_ref[...]
    acc = o[0:t, :]
    for h in range(1, B_HEADS):
        acc = acc + o[h * t:(h + 1) * t, :]
    o_ref[...] = acc


def _attn_sample(qkv, win_k, win_v, norm_b_tiled):
    b, t, _ = qkv.shape
    n_past = win_k.shape[1]
    keep = min(MAX_WINDOW, n_past + t)
    assert t % 8 == 0 and t & (t - 1) == 0 and n_past % 8 == 0 and (n_past + t - keep) % 8 == 0
    assert all(d & (d - 1) == 0 for (_, d) in DILATED)
    kernel = functools.partial(_attn_sample_kernel, n_past=n_past, t=t)
    new = pl.BlockSpec((None, t, B_WIDTH), lambda i: (i, 0, 0))
    win = pl.BlockSpec((None, n_past, B_WIDTH), lambda i: (i, 0, 0))
    wout = pl.BlockSpec((None, keep, B_WIDTH), lambda i: (i, 0, 0))
    return pl.pallas_call(
        kernel,
        grid=(b,),
        in_specs=[pl.BlockSpec((None, t, B_WIDTH), lambda i: (i, 0, 0)),
                  pl.BlockSpec((None, t, B_WIDTH), lambda i: (i, 0, 1)),
                  pl.BlockSpec((None, t, B_WIDTH), lambda i: (i, 0, 2)),
                  win, win, pl.BlockSpec((1, B_WIDTH), lambda i: (0, 0))],
        out_specs=[new, wout, wout],
        out_shape=[jax.ShapeDtypeStruct((b, t, B_WIDTH), F32),
                   jax.ShapeDtypeStruct((b, keep, B_WIDTH), F32),
                   jax.ShapeDtypeStruct((b, keep, B_WIDTH), F32)],
        compiler_params=pltpu.CompilerParams(dimension_semantics=("arbitrary",),
                                             vmem_limit_bytes=VMEM_LIMIT),
        name="attn_sample",
    )(qkv, qkv, qkv, win_k, win_v, norm_b_tiled)


def _out_ffn_kernel(x_ref, oa_ref, ob_ref, wo_ref, nf_ref, wg_ref, wu_ref, wd_ref, nfin_ref, y_ref, *, ff_chunk):
    mixed = jnp.concatenate([oa_ref[...], ob_ref[...]], axis=-1).astype(BF16)
    x1 = x_ref[...] + jnp.dot(mixed, wo_ref[...], preferred_element_type=F32)
    hf = _rms(x1, nf_ref[...]).astype(BF16)
    x2 = x1
    for c0 in range(0, wg_ref.shape[1], ff_chunk):
        g = jnp.dot(hf, wg_ref[:, c0:c0 + ff_chunk], preferred_element_type=F32)
        u = jnp.dot(hf, wu_ref[:, c0:c0 + ff_chunk], preferred_element_type=F32)
        act = (_silu(g) * u).astype(BF16)
        x2 = x2 + jnp.dot(act, wd_ref[c0:c0 + ff_chunk, :], preferred_element_type=F32)
    y_ref[...] = _rms(x2, nfin_ref[...])


def _out_ffn(x2d, o_a, o_b, w_out, norm_ffn, w_gate, w_up, w_down, norm_final, tm):
    t = x2d.shape[0]
    d_ff = w_gate.shape[1]
    ff_chunk = d_ff // 2 if d_ff % 256 == 0 else d_ff
    once = pl.Buffered(1)
    row = lambda w: pl.BlockSpec((tm, w), lambda i: (i, 0))
    full = lambda a, b: pl.BlockSpec((a, b), lambda i: (0, 0), pipeline_mode=once)
    return pl.pallas_call(
        functools.partial(_out_ffn_kernel, ff_chunk=ff_chunk),
        grid=(t // tm,),
        in_specs=[row(D_MODEL), row(A_WIDTH), row(B_WIDTH),
                  full(D_MODEL, D_MODEL), full(1, D_MODEL),
                  full(D_MODEL, d_ff), full(D_MODEL, d_ff), full(d_ff, D_MODEL), full(1, D_MODEL)],
        out_specs=row(D_MODEL),
        out_shape=jax.ShapeDtypeStruct((t, D_MODEL), F32),
        compiler_params=pltpu.CompilerParams(dimension_semantics=("arbitrary",),
                                             vmem_limit_bytes=VMEM_LIMIT),
        name="out_ffn",
    )(x2d, o_a, o_b, w_out, norm_ffn, w_gate, w_up, w_down, norm_final)


def _layer_params(norm_mix, w_in, w_conv, a_log, dt_bias, norm_out_a, norm_out_b, w_out, norm_ffn,
                  w_gate, w_up, w_down, layer):
    w = w_in[layer]
    n_gate = 2 * A_HEADS
    w_p = jnp.concatenate([w[:, :COL_G + n_gate], jnp.zeros((D_MODEL, GATE_PAD - n_gate), w.dtype),
                           w[:, COL_G + n_gate:]], axis=1).astype(BF16)
    gate_par = jnp.zeros((8, GATE_PAD), F32)
    gate_par = gate_par.at[0, A_HEADS:2 * A_HEADS].set(a_log[layer].astype(F32))
    gate_par = gate_par.at[1, A_HEADS:2 * A_HEADS].set(dt_bias[layer].astype(F32))
    return dict(
        norm_mix=norm_mix[layer].reshape(1, D_MODEL), w_p=w_p, w_conv=w_conv[layer], gate_par=gate_par,
        norm_a=norm_out_a[layer].reshape(1, A_DV), norm_b=norm_out_b[layer].reshape(1, B_HEAD_DIM),
        norm_b_tiled=jnp.tile(norm_out_b[layer], B_HEADS).reshape(1, B_WIDTH),
        w_out=w_out[layer].astype(BF16), norm_ffn=norm_ffn[layer].reshape(1, D_MODEL),
        w_gate=w_gate[layer].astype(BF16), w_up=w_up[layer].astype(BF16), w_down=w_down[layer].astype(BF16))


def _block(x, conv_buf, s0, win_k, win_v, p, norm_final, tm):
    b, t, _ = x.shape
    assert t >= CONV_W - 1
    x2d = x.reshape(b * t, D_MODEL)
    proj_a, proj_b = _norm_proj(x2d, p["norm_mix"], p["w_p"], tm)
    proj_a = proj_a.reshape(b, t, COLS_A)
    proj_b = proj_b.reshape(b, t, COLS_B)
    conv0 = jnp.pad(conv_buf, ((0, 0), (8 - (CONV_W - 1), 0), (0, 0)))
    o_a, s_new = _gdn(proj_a, conv0, s0, p["w_conv"], p["gate_par"], p["norm_a"])
    new_conv = proj_a[:, t - (CONV_W - 1):, :CONV_DIM]
    if win_k is None:
        parts = [_attn_prompt_branch(proj_b, d) for (_, d) in DILATED]
        o_b = _combine([pt[0].reshape(b * t, B_WIDTH) for pt in parts],
                       [pt[1].reshape(b * t, 128) for pt in parts], p["norm_b"], tm)
        keep = min(MAX_WINDOW, t)
        new_k = proj_b[:, t - keep:, B_WIDTH:2 * B_WIDTH]
        new_v = proj_b[:, t - keep:, 2 * B_WIDTH:]
    else:
        o_b, new_k, new_v = _attn_sample(proj_b, win_k, win_v, p["norm_b_tiled"])
        o_b = o_b.reshape(b * t, B_WIDTH)
    y = _out_ffn(x2d, o_a.reshape(b * t, A_WIDTH), o_b, p["w_out"], p["norm_ffn"], p["w_gate"], p["w_up"],
                 p["w_down"], norm_final.reshape(1, D_MODEL), tm)
    shp = (b, new_k.shape[1], B_HEADS, B_HEAD_DIM)
    return y.reshape(b, t, D_MODEL), new_conv, s_new, new_k.reshape(shp), new_v.reshape(shp)


def kernel(x_prompt, x_sample, state_conv, state_rec, cache_win_k, cache_win_v, norm_mix, w_in, w_conv, a_log,
           dt_bias, norm_out_a, norm_out_b, w_out, norm_ffn, w_gate, w_up, w_down, norm_final):
    depth = w_in.shape[0]
    assert depth == 1, "the final norm is fused into the block of a single-layer trunk"
    p = _layer_params(norm_mix, w_in, w_conv, a_log, dt_bias, norm_out_a, norm_out_b, w_out, norm_ffn,
                      w_gate, w_up, w_down, 0)
    bp = x_prompt.shape[0]
    bs, ts, _ = x_sample.shape
    zero_conv = jnp.zeros((bp, CONV_W - 1, CONV_DIM), F32)
    zero_rec = jnp.zeros((bp, A_HEADS, A_DK, A_DV), F32)
    yp, pc, pr, pk, pv = _block(x_prompt, zero_conv, zero_rec, None, None, p, norm_final, 512)
    n_past = cache_win_k.shape[2]
    wk = cache_win_k[0].reshape(bs, n_past, B_WIDTH)
    wv = cache_win_v[0].reshape(bs, n_past, B_WIDTH)
    ys, sc, sr, sk, sv = _block(x_sample, state_conv[0], state_rec[0], wk, wv, p, norm_final, bs * ts)
    return (yp, ys, pc[None], pr[None], pk[None], pv[None], sc[None], sr[None], sk[None], sv[None])
```

```python
import functools

import jax
import jax.numpy as jnp
from jax import lax
from jax.experimental import pallas as pl
from jax.experimental.pallas import tpu as pltpu

F32 = jnp.float32
BF16 = jnp.bfloat16

D_MODEL = 1024
A_HEADS = 4
A_DK = 128
A_DV = 128
A_WIDTH = A_HEADS * A_DV
CONV_W = 4
CONV_DIM = 2 * A_HEADS * A_DK + A_HEADS * A_DV
B_HEADS = 8
B_HEAD_DIM = 64
B_WIDTH = B_HEADS * B_HEAD_DIM
DILATED = ((128, 1), (512, 4), (2048, 16))
STEPS = 128
MAX_WINDOW = 2048
EPS = 1e-6
NEG = -1e30
CHUNK = 64
GATE_PAD = 128
COL_Z = CONV_DIM
COL_G = CONV_DIM + A_WIDTH
COLS_A = COL_G + GATE_PAD
COLS_B = 3 * B_WIDTH
VMEM_LIMIT = 56 * 1024 * 1024


def _dot(a, b):
    return jnp.dot(a.astype(BF16), b.astype(BF16), preferred_element_type=F32)


def _dot_nt(a, b):
    return lax.dot_general(a.astype(BF16), b.astype(BF16), (((1,), (1,)), ((), ())),
                           preferred_element_type=F32)


def _dot_tn(a, b):
    return lax.dot_general(a.astype(BF16), b.astype(BF16), (((0,), (0,)), ((), ())),
                           preferred_element_type=F32)


def _split2(a):
    hi = a.astype(BF16)
    lo = (a - hi.astype(F32)).astype(BF16)
    return hi, lo


def _dot3(a, b):
    ah, al = _split2(a)
    bh, bl = _split2(b)
    d = functools.partial(jnp.dot, preferred_element_type=F32)
    return d(ah, bh) + d(ah, bl) + d(al, bh)


def _sigmoid(x):
    return 1.0 / (1.0 + jnp.exp(-x))


def _silu(x):
    return x * _sigmoid(x)


def _softplus(x):
    return jnp.maximum(x, 0.0) + jnp.log(1.0 + jnp.exp(-jnp.abs(x)))


def _rms(x, w):
    return x * lax.rsqrt(jnp.mean(x * x, axis=-1, keepdims=True) + EPS) * w


def _norm_proj_kernel(x_ref, nw_ref, w_ref, oa_ref, ob_ref):
    h = _rms(x_ref[...], nw_ref[...]).astype(BF16)
    oa_ref[...] = jnp.dot(h, w_ref[:, :COLS_A], preferred_element_type=F32)
    ob_ref[...] = jnp.dot(h, w_ref[:, COLS_A:], preferred_element_type=F32)


def _norm_proj(x2d, norm_w, w_p, tm):
    t = x2d.shape[0]
    once = pl.Buffered(1)
    return pl.pallas_call(
        _norm_proj_kernel,
        grid=(t // tm,),
        in_specs=[pl.BlockSpec((tm, D_MODEL), lambda i: (i, 0)),
                  pl.BlockSpec((1, D_MODEL), lambda i: (0, 0), pipeline_mode=once),
                  pl.BlockSpec((D_MODEL, COLS_A + COLS_B), lambda i: (0, 0), pipeline_mode=once)],
        out_specs=[pl.BlockSpec((tm, COLS_A), lambda i: (i, 0)),
                   pl.BlockSpec((tm, COLS_B), lambda i: (i, 0))],
        out_shape=[jax.ShapeDtypeStruct((t, COLS_A), F32),
                   jax.ShapeDtypeStruct((t, COLS_B), F32)],
        compiler_params=pltpu.CompilerParams(dimension_semantics=("arbitrary",),
                                             vmem_limit_bytes=VMEM_LIMIT),
        name="norm_proj",
    )(x2d, norm_w, w_p)


def _gdn_kernel(xa_ref, z_ref, gt_ref, conv0_ref, s0_ref, wconv_ref, gp_ref, nwa_ref,
                o_ref, snew_ref, ext_ref, s_ref, *, rows):
    n = pl.program_id(1)
    c = CHUNK
    n_chunks = -(-rows // c)
    rp = n_chunks * c

    @pl.when(n == 0)
    def _():
        ext_ref[0:8, :] = conv0_ref[...]
        s_ref[...] = s0_ref[...]

    ext_ref[8:8 + rows, :] = xa_ref[...]
    if rows < rp:
        ext_ref[8 + rows:8 + rp, :] = jnp.zeros((rp - rows, CONV_DIM), F32)

    conv = ext_ref[5:5 + rp, :] * wconv_ref[0:1, :]
    for i in range(1, CONV_W):
        conv = conv + ext_ref[5 + i:5 + i + rp, :] * wconv_ref[i:i + 1, :]
    conv = _silu(conv)
    ext_ref[0:8, :] = ext_ref[rp:rp + 8, :]

    row = lax.broadcasted_iota(jnp.int32, (rp, 1), 0)
    live = row < rows
    if rows < rp:
        conv = jnp.where(live, conv, 0.0)

    gt = gt_ref[...]
    if rows < rp:
        gt = jnp.concatenate([gt, jnp.zeros((rp - rows, GATE_PAD), F32)], axis=0)
    beta_all = _sigmoid(gt)
    g_all = -jnp.exp(gp_ref[0:1, :]) * _softplus(gt + gp_ref[1:2, :])
    lane = lax.broadcasted_iota(jnp.int32, (rp, GATE_PAD), 1)
    g_all = jnp.where((lane >= A_HEADS) & (lane < 2 * A_HEADS) & live, g_all, 0.0)
    beta_all = jnp.where(live, beta_all, 0.0)

    ii = lax.broadcasted_iota(jnp.int32, (c, c), 0)
    jj = lax.broadcasted_iota(jnp.int32, (c, c), 1)
    incl = ii >= jj
    strict = ii > jj
    tril = jnp.where(incl, 1.0, 0.0).astype(BF16)
    eye = jnp.where(ii == jj, 1.0, 0.0).astype(F32)
    dd = functools.partial(jnp.dot, preferred_element_type=F32)

    def exact_tril_dot(x):
        hi, lo = _split2(x)
        lo2 = (x - hi.astype(F32) - lo.astype(F32)).astype(BF16)
        return dd(tril, hi) + dd(tril, lo) + dd(tril, lo2)

    z = z_ref[...]
    nwa = nwa_ref[...]
    chains = [(ci, h) for ci in range(n_chunks) for h in range(A_HEADS)]
    gc_alls = [exact_tril_dot(g_all[ci * c:(ci + 1) * c]) for ci in range(n_chunks)]
    qs, ks, kbs, rhss, gcs, gls = [], [], [], [], [], []
    for ci, h in chains:
        rs = slice(ci * c, (ci + 1) * c)
        q = conv[rs, h * A_DK:(h + 1) * A_DK]
        k = conv[rs, A_HEADS * A_DK + h * A_DK:A_HEADS * A_DK + (h + 1) * A_DK]
        v = conv[rs, 2 * A_HEADS * A_DK + h * A_DV:2 * A_HEADS * A_DK + (h + 1) * A_DV]
        q = q * lax.rsqrt(jnp.sum(q * q, axis=-1, keepdims=True) + EPS) * (A_DK ** -0.5)
        k = k * lax.rsqrt(jnp.sum(k * k, axis=-1, keepdims=True) + EPS)
        beta = beta_all[rs, h:h + 1]
        gc = gc_alls[ci][:, A_HEADS + h:A_HEADS + h + 1]
        kb = k * beta
        qs.append(q)
        ks.append(k)
        kbs.append(kb)
        rhss.append(jnp.concatenate([v * beta, kb * jnp.exp(gc)], axis=1))
        gcs.append(gc)
        gls.append(gc_alls[ci][c - 1:c, A_HEADS + h:A_HEADS + h + 1])
    e_mats = [exact_tril_dot(jnp.where(strict, g_all[ci * c:(ci + 1) * c, A_HEADS + h:A_HEADS + h + 1], 0.0))
              for ci, h in chains]
    decays = [jnp.where(incl, jnp.exp(jnp.where(incl, e, 0.0)), 0.0) for e in e_mats]
    kks = [_dot_nt(kb, k) for kb, k in zip(kbs, ks)]
    qks = [_dot_nt(q, k) for q, k in zip(qs, ks)]
    pws = [jnp.where(strict, a * d, 0.0) for a, d in zip(kks, decays)]
    qks = [jnp.where(incl, a * d, 0.0) for a, d in zip(qks, decays)]
    t_invs = [eye - lw for lw in pws]
    for _ in range(c.bit_length() - 2):
        pws = [_dot3(pw, pw) for pw in pws]
        t_invs = [ti + _dot3(ti, pw) for ti, pw in zip(t_invs, pws)]
    uws = [_dot(ti, rhs) for ti, rhs in zip(t_invs, rhss)]

    states = [s_ref[h] for h in range(A_HEADS)]
    for ci in range(n_chunks):
        idx = [ci * A_HEADS + h for h in range(A_HEADS)]
        ws = [_dot(uws[i][:, A_DV:], states[h]) for h, i in enumerate(idx)]
        os = [_dot(qs[i] * jnp.exp(gcs[i]), states[h]) for h, i in enumerate(idx)]
        es = [uws[i][:, :A_DV] - w for i, w in zip(idx, ws)]
        os = [o + _dot(qks[i], e) for o, i, e in zip(os, idx, es)]
        upd = [_dot_tn(ks[i] * jnp.exp(gls[i] - gcs[i]), e) for i, e in zip(idx, es)]
        states = [s * jnp.exp(gls[i]) + d for s, i, d in zip(states, idx, upd)]
        r0 = ci * c
        r1 = min(rows, r0 + c)
        for h, o in enumerate(os):
            o = o if r1 - r0 == c else o[:r1 - r0]
            o_ref[r0:r1, h * A_DV:(h + 1) * A_DV] = (
                _rms(o, nwa) * _silu(z[r0:r1, h * A_DV:(h + 1) * A_DV]))
    for h in range(A_HEADS):
        s_ref[h] = states[h]

    @pl.when(n == pl.num_programs(1) - 1)
    def _():
        snew_ref[...] = s_ref[...]


GDN_ROWS = 4 * CHUNK


def _gdn(proj, conv0, s0, w_conv, gate_par, norm_a):
    b, t, _ = proj.shape
    rows = min(GDN_ROWS, t)
    assert t % rows == 0 and (rows % CHUNK == 0 or rows == t)
    n = t // rows
    kernel = functools.partial(_gdn_kernel, rows=rows)
    return pl.pallas_call(
        kernel,
        grid=(b, n),
        in_specs=[pl.BlockSpec((None, rows, CONV_DIM), lambda i, j: (i, j, 0)),
                  pl.BlockSpec((None, rows, A_WIDTH), lambda i, j: (i, j, COL_Z // A_WIDTH)),
                  pl.BlockSpec((None, rows, GATE_PAD), lambda i, j: (i, j, COL_G // GATE_PAD)),
                  pl.BlockSpec((None, 8, CONV_DIM), lambda i, j: (i, 0, 0)),
                  pl.BlockSpec((None, A_HEADS, A_DK, A_DV), lambda i, j: (i, 0, 0, 0)),
                  pl.BlockSpec((CONV_W, CONV_DIM), lambda i, j: (0, 0)),
                  pl.BlockSpec((8, GATE_PAD), lambda i, j: (0, 0)),
                  pl.BlockSpec((1, A_DV), lambda i, j: (0, 0))],
        out_specs=[pl.BlockSpec((None, rows, A_WIDTH), lambda i, j: (i, j, 0)),
                   pl.BlockSpec((None, A_HEADS, A_DK, A_DV), lambda i, j: (i, 0, 0, 0))],
        out_shape=[jax.ShapeDtypeStruct((b, t, A_WIDTH), F32),
                   jax.ShapeDtypeStruct((b, A_HEADS, A_DK, A_DV), F32)],
        scratch_shapes=[pltpu.VMEM((8 + -(-rows // CHUNK) * CHUNK, CONV_DIM), F32),
                        pltpu.VMEM((A_HEADS, A_DK, A_DV), F32)],
        compiler_params=pltpu.CompilerParams(dimension_semantics=("arbitrary", "arbitrary"),
                                             vmem_limit_bytes=VMEM_LIMIT),
        name="gdn",
    )(proj, proj, proj, conv0, s0, w_conv, gate_par, norm_a)


def _slope(h):
    return 2.0 ** (-8.0 * (h + 1) / B_HEADS)


def _attn_prompt_kernel(q_ref, kp_ref, kc_ref, vp_ref, vc_ref, num_ref, ml_ref, *, dil):
    n = pl.program_id(2)
    blk = STEPS
    qq = lax.broadcasted_iota(jnp.int32, (blk, 2 * blk), 0)
    kk = lax.broadcasted_iota(jnp.int32, (blk, 2 * blk), 1)
    j = qq + blk - kk
    valid = (j >= 0) & (j <= STEPS) & (n * blk - blk + kk >= 0)
    dist = (j * dil).astype(F32)
    lane = lax.broadcasted_iota(jnp.int32, (blk, 128), 1)
    ml = jnp.zeros((blk, 128), F32)
    q = q_ref[...]
    kp = kp_ref[...]
    kc = kc_ref[...]
    vp = vp_ref[...]
    vc = vc_ref[...]
    heads = [slice(h * B_HEAD_DIM, (h + 1) * B_HEAD_DIM) for h in range(B_HEADS)]
    scores = [_dot_nt(q[:, hs], jnp.concatenate([kp[:, hs], kc[:, hs]], axis=0)) for hs in heads]
    ps = []
    for h, s in enumerate(scores):
        s = s * (B_HEAD_DIM ** -0.5) - _slope(h) * dist
        s = jnp.where(valid, s, NEG)
        m = jnp.max(s, axis=-1, keepdims=True)
        p = jnp.exp(s - m)
        l = jnp.sum(p, axis=-1, keepdims=True)
        ps.append(p)
        ml = jnp.where(lane == h, m, ml)
        ml = jnp.where(lane == B_HEADS + h, l, ml)
    for hs, p in zip(heads, ps):
        num_ref[:, hs] = _dot(p, jnp.concatenate([vp[:, hs], vc[:, hs]], axis=0))
    ml_ref[...] = ml


def _attn_prompt_branch(qkv, dil):
    b, s, _ = qkv.shape
    ln = s // dil
    assert ln % STEPS == 0
    nb = ln // STEPS
    x = qkv.reshape(b, ln, dil * 3 * B_WIDTH)
    blk = (None, STEPS, B_WIDTH)
    kernel = functools.partial(_attn_prompt_kernel, dil=dil)
    num, ml = pl.pallas_call(
        kernel,
        grid=(b, dil, nb),
        in_specs=[pl.BlockSpec(blk, lambda i, r, n: (i, n, 3 * r)),
                  pl.BlockSpec(blk, lambda i, r, n: (i, jnp.maximum(n - 1, 0), 3 * r + 1)),
                  pl.BlockSpec(blk, lambda i, r, n: (i, n, 3 * r + 1)),
                  pl.BlockSpec(blk, lambda i, r, n: (i, jnp.maximum(n - 1, 0), 3 * r + 2)),
                  pl.BlockSpec(blk, lambda i, r, n: (i, n, 3 * r + 2))],
        out_specs=[pl.BlockSpec(blk, lambda i, r, n: (i, n, r)),
                   pl.BlockSpec((None, STEPS, 128), lambda i, r, n: (i, n, r))],
        out_shape=[jax.ShapeDtypeStruct((b, ln, dil * B_WIDTH), F32),
                   jax.ShapeDtypeStruct((b, ln, dil * 128), F32)],
        compiler_params=pltpu.CompilerParams(
            dimension_semantics=("arbitrary", "arbitrary", "arbitrary"), vmem_limit_bytes=VMEM_LIMIT),
        name=f"attn_prompt_d{dil}",
    )(x, x, x, x, x)
    return num.reshape(b, s, B_WIDTH), ml.reshape(b, s, 128)


def _combine_kernel(n1_ref, n2_ref, n3_ref, m1_ref, m2_ref, m3_ref, nwb_ref, o_ref):
    nums = (n1_ref[...], n2_ref[...], n3_ref[...])
    mls = (m1_ref[...], m2_ref[...], m3_ref[...])
    nwb = nwb_ref[...]
    for h in range(B_HEADS):
        hs = slice(h * B_HEAD_DIM, (h + 1) * B_HEAD_DIM)
        ms = [ml[:, h:h + 1] for ml in mls]
        ls = [ml[:, B_HEADS + h:B_HEADS + h + 1] for ml in mls]
        mx = jnp.maximum(jnp.maximum(ms[0], ms[1]), ms[2])
        ws = [jnp.exp(m - mx) for m in ms]
        den = ws[0] * ls[0] + ws[1] * ls[1] + ws[2] * ls[2]
        o = (ws[0] * nums[0][:, hs] + ws[1] * nums[1][:, hs] + ws[2] * nums[2][:, hs]) / den
        o_ref[:, hs] = _rms(o, nwb)


def _combine(nums, mls, norm_b, tm):
    t = nums[0].shape[0]
    big = pl.BlockSpec((tm, B_WIDTH), lambda i: (i, 0))
    small = pl.BlockSpec((tm, 128), lambda i: (i, 0))
    return pl.pallas_call(
        _combine_kernel,
        grid=(t // tm,),
        in_specs=[big, big, big, small, small, small, pl.BlockSpec((1, B_HEAD_DIM), lambda i: (0, 0))],
        out_specs=big,
        out_shape=jax.ShapeDtypeStruct((t, B_WIDTH), F32),
        compiler_params=pltpu.CompilerParams(dimension_semantics=("arbitrary",),
                                             vmem_limit_bytes=VMEM_LIMIT),
        name="attn_combine",
    )(*nums, *mls, norm_b)


def _attn_sample_kernel(q_ref, kn_ref, vn_ref, kc_ref, vc_ref, nwb_ref, o_ref, ko_ref, vo_ref, *, n_past, t):
    keep = min(MAX_WINDOW, n_past + t)
    drop = n_past + t - keep
    ko_ref[0:keep - t, :] = kc_ref[drop:n_past, :]
    ko_ref[keep - t:keep, :] = kn_ref[...]
    vo_ref[0:keep - t, :] = vc_ref[drop:n_past, :]
    vo_ref[keep - t:keep, :] = vn_ref[...]

    pad = (-(n_past + t)) % 128
    nk = n_past + t + pad
    zpad = jnp.zeros((pad, B_WIDTH), BF16)
    k_all = jnp.concatenate([kc_ref[...].astype(BF16), kn_ref[...].astype(BF16), zpad], axis=0)
    v_all = jnp.concatenate([vc_ref[...].astype(BF16), vn_ref[...].astype(BF16), zpad], axis=0)

    rq = B_HEADS * t
    t_bits = t.bit_length() - 1
    d_bits = B_HEAD_DIM.bit_length() - 1
    rowi = lax.broadcasted_iota(jnp.int32, (rq, B_WIDTH), 0)
    lanei = lax.broadcasted_iota(jnp.int32, (rq, B_WIDTH), 1)
    own = lax.shift_right_logical(rowi, t_bits) == lax.shift_right_logical(lanei, d_bits)
    q_rep = jnp.concatenate([q_ref[...]] * B_HEADS, axis=0)
    q_blk = jnp.where(own, q_rep, 0.0)
    s = _dot_nt(q_blk, k_all) * (B_HEAD_DIM ** -0.5)

    r1 = lax.broadcasted_iota(jnp.int32, (rq, 1), 0)
    head = lax.shift_right_logical(r1, t_bits)
    slope = jnp.zeros((rq, 1), F32)
    for h in range(B_HEADS):
        slope = jnp.where(head == h, _slope(h), slope)
    key = lax.broadcasted_iota(jnp.int32, (rq, nk), 1)
    delta = n_past + (r1 & (t - 1)) - key
    s = s - slope * delta.astype(F32)
    in_range = (delta >= 0) & (key < n_past + t)

    ms, ls, nums = [], [], []
    for (_, dil) in DILATED:
        valid = in_range & ((delta & (dil - 1)) == 0) & (delta <= STEPS * dil)
        sd = jnp.where(valid, s, NEG)
        m = jnp.max(sd, axis=-1, keepdims=True)
        p = jnp.exp(sd - m)
        ms.append(m)
        ls.append(jnp.sum(p, axis=-1, keepdims=True))
        nums.append(jnp.dot(p.astype(BF16), v_all, preferred_element_type=F32))
    mx = jnp.maximum(jnp.maximum(ms[0], ms[1]), ms[2])
    ws = [jnp.exp(m - mx) for m in ms]
    den = ws[0] * ls[0] + ws[1] * ls[1] + ws[2] * ls[2]
    o = (ws[0] * nums[0] + ws[1] * nums[1] + ws[2] * nums[2]) / den
    o = jnp.where(own, o, 0.0)
    o = o * lax.rsqrt(jnp.sum(o * o, axis=-1, keepdims=True) / B_HEAD_DIM + EPS) * nwb_ref[...]
    acc = o[0:t, :]
    for h in range(1, B_HEADS):
        acc = acc + o[h * t:(h + 1) * t, :]
    o_ref[...] = acc


def _attn_sample(qkv, win_k, win_v, norm_b_tiled):
    b, t, _ = qkv.shape
    n_past = win_k.shape[1]
    keep = min(MAX_WINDOW, n_past + t)
    assert t % 8 == 0 and t & (t - 1) == 0 and n_past % 8 == 0 and (n_past + t - keep) % 8 == 0
    assert all(d & (d - 1) == 0 for (_, d) in DILATED)
    kernel = functools.partial(_attn_sample_kernel, n_past=n_past, t=t)
    new = pl.BlockSpec((None, t, B_WIDTH), lambda i: (i, 0, 0))
    win = pl.BlockSpec((None, n_past, B_WIDTH), lambda i: (i, 0, 0))
    wout = pl.BlockSpec((None, keep, B_WIDTH), lambda i: (i, 0, 0))
    return pl.pallas_call(
        kernel,
        grid=(b,),
        in_specs=[pl.BlockSpec((None, t, B_WIDTH), lambda i: (i, 0, 0)),
                  pl.BlockSpec((None, t, B_WIDTH), lambda i: (i, 0, 1)),
                  pl.BlockSpec((None, t, B_WIDTH), lambda i: (i, 0, 2)),
                  win, win, pl.BlockSpec((1, B_WIDTH), lambda i: (0, 0))],
        out_specs=[new, wout, wout],
        out_shape=[jax.ShapeDtypeStruct((b, t, B_WIDTH), F32),
                   jax.ShapeDtypeStruct((b, keep, B_WIDTH), F32),
                   jax.ShapeDtypeStruct((b, keep, B_WIDTH), F32)],
        compiler_params=pltpu.CompilerParams(dimension_semantics=("arbitrary",),
                                             vmem_limit_bytes=VMEM_LIMIT),
        name="attn_sample",
    )(qkv, qkv, qkv, win_k, win_v, norm_b_tiled)


def _out_ffn_kernel(x_ref, oa_ref, ob_ref, wo_ref, nf_ref, wg_ref, wu_ref, wd_ref, nfin_ref, y_ref, *, ff_chunk):
    mixed = jnp.concatenate([oa_ref[...], ob_ref[...]], axis=-1).astype(BF16)
    x1 = x_ref[...] + jnp.dot(mixed, wo_ref[...], preferred_element_type=F32)
    hf = _rms(x1, nf_ref[...]).astype(BF16)
    x2 = x1
    for c0 in range(0, wg_ref.shape[1], ff_chunk):
        g = jnp.dot(hf, wg_ref[:, c0:c0 + ff_chunk], preferred_element_type=F32)
        u = jnp.dot(hf, wu_ref[:, c0:c0 + ff_chunk], preferred_element_type=F32)
        act = (_silu(g) * u).astype(BF16)
        x2 = x2 + jnp.dot(act, wd_ref[c0:c0 + ff_chunk, :], preferred_element_type=F32)
    y_ref[...] = _rms(x2, nfin_ref[...])


def _out_ffn(x2d, o_a, o_b, w_out, norm_ffn, w_gate, w_up, w_down, norm_final, tm):
    t = x2d.shape[0]
    d_ff = w_gate.shape[1]
    ff_chunk = d_ff // 2 if d_ff % 256 == 0 else d_ff
    once = pl.Buffered(1)
    row = lambda w: pl.BlockSpec((tm, w), lambda i: (i, 0))
    full = lambda a, b: pl.BlockSpec((a, b), lambda i: (0, 0), pipeline_mode=once)
    return pl.pallas_call(
        functools.partial(_out_ffn_kernel, ff_chunk=ff_chunk),
        grid=(t // tm,),
        in_specs=[row(D_MODEL), row(A_WIDTH), row(B_WIDTH),
                  full(D_MODEL, D_MODEL), full(1, D_MODEL),
                  full(D_MODEL, d_ff), full(D_MODEL, d_ff), full(d_ff, D_MODEL), full(1, D_MODEL)],
        out_specs=row(D_MODEL),
        out_shape=jax.ShapeDtypeStruct((t, D_MODEL), F32),
        compiler_params=pltpu.CompilerParams(dimension_semantics=("arbitrary",),
                                             vmem_limit_bytes=VMEM_LIMIT),
        name="out_ffn",
    )(x2d, o_a, o_b, w_out, norm_ffn, w_gate, w_up, w_down, norm_final)


def _layer_params(norm_mix, w_in, w_conv, a_log, dt_bias, norm_out_a, norm_out_b, w_out, norm_ffn,
                  w_gate, w_up, w_down, layer):
    w = w_in[layer]
    n_gate = 2 * A_HEADS
    w_p = jnp.concatenate([w[:, :COL_G + n_gate], jnp.zeros((D_MODEL, GATE_PAD - n_gate), w.dtype),
                           w[:, COL_G + n_gate:]], axis=1).astype(BF16)
    gate_par = jnp.zeros((8, GATE_PAD), F32)
    gate_par = gate_par.at[0, A_HEADS:2 * A_HEADS].set(a_log[layer].astype(F32))
    gate_par = gate_par.at[1, A_HEADS:2 * A_HEADS].set(dt_bias[layer].astype(F32))
    return dict(
        norm_mix=norm_mix[layer].reshape(1, D_MODEL), w_p=w_p, w_conv=w_conv[layer], gate_par=gate_par,
        norm_a=norm_out_a[layer].reshape(1, A_DV), norm_b=norm_out_b[layer].reshape(1, B_HEAD_DIM),
        norm_b_tiled=jnp.tile(norm_out_b[layer], B_HEADS).reshape(1, B_WIDTH),
        w_out=w_out[layer].astype(BF16), norm_ffn=norm_ffn[layer].reshape(1, D_MODEL),
        w_gate=w_gate[layer].astype(BF16), w_up=w_up[layer].astype(BF16), w_down=w_down[layer].astype(BF16))


def _block(x, conv_buf, s0, win_k, win_v, p, norm_final, tm):
    b, t, _ = x.shape
    assert t >= CONV_W - 1
    x2d = x.reshape(b * t, D_MODEL)
    proj_a, proj_b = _norm_proj(x2d, p["norm_mix"], p["w_p"], tm)
    proj_a = proj_a.reshape(b, t, COLS_A)
    proj_b = proj_b.reshape(b, t, COLS_B)
    conv0 = jnp.pad(conv_buf, ((0, 0), (8 - (CONV_W - 1), 0), (0, 0)))
    o_a, s_new = _gdn(proj_a, conv0, s0, p["w_conv"], p["gate_par"], p["norm_a"])
    new_conv = proj_a[:, t - (CONV_W - 1):, :CONV_DIM]
    if win_k is None:
        parts = [_attn_prompt_branch(proj_b, d) for (_, d) in DILATED]
        o_b = _combine([pt[0].reshape(b * t, B_WIDTH) for pt in parts],
                       [pt[1].reshape(b * t, 128) for pt in parts], p["norm_b"], tm)
        keep = min(MAX_WINDOW, t)
        new_k = proj_b[:, t - keep:, B_WIDTH:2 * B_WIDTH]
        new_v = proj_b[:, t - keep:, 2 * B_WIDTH:]
    else:
        o_b, new_k, new_v = _attn_sample(proj_b, win_k, win_v, p["norm_b_tiled"])
        o_b = o_b.reshape(b * t, B_WIDTH)
    y = _out_ffn(x2d, o_a.reshape(b * t, A_WIDTH), o_b, p["w_out"], p["norm_ffn"], p["w_gate"], p["w_up"],
                 p["w_down"], norm_final.reshape(1, D_MODEL), tm)
    shp = (b, new_k.shape[1], B_HEADS, B_HEAD_DIM)
    return y.reshape(b, t, D_MODEL), new_conv, s_new, new_k.reshape(shp), new_v.reshape(shp)


def kernel(x_prompt, x_sample, state_conv, state_rec, cache_win_k, cache_win_v, norm_mix, w_in, w_conv, a_log,
           dt_bias, norm_out_a, norm_out_b, w_out, norm_ffn, w_gate, w_up, w_down, norm_final):
    depth = w_in.shape[0]
    assert depth == 1, "the final norm is fused into the block of a single-layer trunk"
    p = _layer_params(norm_mix, w_in, w_conv, a_log, dt_bias, norm_out_a, norm_out_b, w_out, norm_ffn,
                      w_gate, w_up, w_down, 0)
    bp = x_prompt.shape[0]
    bs, ts, _ = x_sample.shape
    zero_conv = jnp.zeros((bp, CONV_W - 1, CONV_DIM), F32)
    zero_rec = jnp.zeros((bp, A_HEADS, A_DK, A_DV), F32)
    yp, pc, pr, pk, pv = _block(x_prompt, zero_conv, zero_rec, None, None, p, norm_final, 512)
    n_past = cache_win_k.shape[2]
    wk = cache_win_k[0].reshape(bs, n_past, B_WIDTH)
    wv = cache_win_v[0].reshape(bs, n_past, B_WIDTH)
    ys, sc, sr, sk, sv = _block(x_sample, state_conv[0], state_rec[0], wk, wv, p, norm_final, bs * ts)
    return (yp, ys, pc[None], pr[None], pk[None], pv[None], sc[None], sr[None], sk[None], sv[None])
```

```python
import functools

import jax
import jax.numpy as jnp
from jax import lax
from jax.experimental import pallas as pl
from jax.experimental.pallas import tpu as pltpu

F32 = jnp.float32
BF16 = jnp.bfloat16

D_MODEL = 1024
A_HEADS = 4
A_DK = 128
A_DV = 128
A_WIDTH = A_HEADS * A_DV
CONV_W = 4
CONV_DIM = 2 * A_HEADS * A_DK + A_HEADS * A_DV
B_HEADS = 8
B_HEAD_DIM = 64
B_WIDTH = B_HEADS * B_HEAD_DIM
DILATED = ((128, 1), (512, 4), (2048, 16))
STEPS = 128
MAX_WINDOW = 2048
EPS = 1e-6
NEG = -1e30
CHUNK = 64
GATE_PAD = 128
COL_Z = CONV_DIM
COL_G = CONV_DIM + A_WIDTH
COLS_A = COL_G + GATE_PAD
COLS_B = 3 * B_WIDTH
VMEM_LIMIT = 56 * 1024 * 1024


def _dot(a, b):
    return jnp.dot(a.astype(BF16), b.astype(BF16), preferred_element_type=F32)


def _dot_nt(a, b):
    return lax.dot_general(a.astype(BF16), b.astype(BF16), (((1,), (1,)), ((), ())),
                           preferred_element_type=F32)


def _dot_tn(a, b):
    return lax.dot_general(a.astype(BF16), b.astype(BF16), (((0,), (0,)), ((), ())),
                           preferred_element_type=F32)


def _split2(a):
    hi = a.astype(BF16)
    lo = (a - hi.astype(F32)).astype(BF16)
    return hi, lo


def _dot3(a, b):
    ah, al = _split2(a)
    bh, bl = _split2(b)
    d = functools.partial(jnp.dot, preferred_element_type=F32)
    return d(ah, bh) + d(ah, bl) + d(al, bh)


def _sigmoid(x):
    return 1.0 / (1.0 + jnp.exp(-x))


def _silu(x):
    return x * _sigmoid(x)


def _softplus(x):
    return jnp.maximum(x, 0.0) + jnp.log(1.0 + jnp.exp(-jnp.abs(x)))


def _rms(x, w):
    return x * lax.rsqrt(jnp.mean(x * x, axis=-1, keepdims=True) + EPS) * w


def _norm_proj_kernel(x_ref, nw_ref, w_ref, oa_ref, ob_ref):
    h = _rms(x_ref[...], nw_ref[...]).astype(BF16)
    oa_ref[...] = jnp.dot(h, w_ref[:, :COLS_A], preferred_element_type=F32)
    ob_ref[...] = jnp.dot(h, w_ref[:, COLS_A:], preferred_element_type=F32)


def _norm_proj(x2d, norm_w, w_p, tm):
    t = x2d.shape[0]
    once = pl.Buffered(1)
    return pl.pallas_call(
        _norm_proj_kernel,
        grid=(t // tm,),
        in_specs=[pl.BlockSpec((tm, D_MODEL), lambda i: (i, 0)),
                  pl.BlockSpec((1, D_MODEL), lambda i: (0, 0), pipeline_mode=once),
                  pl.BlockSpec((D_MODEL, COLS_A + COLS_B), lambda i: (0, 0), pipeline_mode=once)],
        out_specs=[pl.BlockSpec((tm, COLS_A), lambda i: (i, 0)),
                   pl.BlockSpec((tm, COLS_B), lambda i: (i, 0))],
        out_shape=[jax.ShapeDtypeStruct((t, COLS_A), F32),
                   jax.ShapeDtypeStruct((t, COLS_B), F32)],
        compiler_params=pltpu.CompilerParams(dimension_semantics=("arbitrary",),
                                             vmem_limit_bytes=VMEM_LIMIT),
        name="norm_proj",
    )(x2d, norm_w, w_p)


def _gdn_kernel(xa_ref, z_ref, gt_ref, conv0_ref, s0_ref, wconv_ref, gp_ref, nwa_ref,
                o_ref, snew_ref, ext_ref, s_ref, *, rows):
    n = pl.program_id(1)
    c = CHUNK
    n_chunks = -(-rows // c)
    rp = n_chunks * c

    @pl.when(n == 0)
    def _():
        ext_ref[0:8, :] = conv0_ref[...]
        s_ref[...] = s0_ref[...]

    ext_ref[8:8 + rows, :] = xa_ref[...]
    if rows < rp:
        ext_ref[8 + rows:8 + rp, :] = jnp.zeros((rp - rows, CONV_DIM), F32)

    conv = ext_ref[5:5 + rp, :] * wconv_ref[0:1, :]
    for i in range(1, CONV_W):
        conv = conv + ext_ref[5 + i:5 + i + rp, :] * wconv_ref[i:i + 1, :]
    conv = _silu(conv)
    ext_ref[0:8, :] = ext_ref[rp:rp + 8, :]

    row = lax.broadcasted_iota(jnp.int32, (rp, 1), 0)
    live = row < rows
    if rows < rp:
        conv = jnp.where(live, conv, 0.0)

    gt = gt_ref[...]
    if rows < rp:
        gt = jnp.concatenate([gt, jnp.zeros((rp - rows, GATE_PAD), F32)], axis=0)
    beta_all = _sigmoid(gt)
    g_all = -jnp.exp(gp_ref[0:1, :]) * _softplus(gt + gp_ref[1:2, :])
    lane = lax.broadcasted_iota(jnp.int32, (rp, GATE_PAD), 1)
    g_all = jnp.where((lane >= A_HEADS) & (lane < 2 * A_HEADS) & live, g_all, 0.0)
    beta_all = jnp.where(live, beta_all, 0.0)

    ii = lax.broadcasted_iota(jnp.int32, (c, c), 0)
    jj = lax.broadcasted_iota(jnp.int32, (c, c), 1)
    incl = ii >= jj
    strict = ii > jj
    tril = jnp.where(incl, 1.0, 0.0).astype(BF16)
    eye = jnp.where(ii == jj, 1.0, 0.0).astype(F32)
    dd = functools.partial(jnp.dot, preferred_element_type=F32)

    def exact_tril_dot(x):
        hi, lo = _split2(x)
        lo2 = (x - hi.astype(F32) - lo.astype(F32)).astype(BF16)
        return dd(tril, hi) + dd(tril, lo) + dd(tril, lo2)

    z = z_ref[...]
    nwa = nwa_ref[...]
    chains = [(ci, h) for ci in range(n_chunks) for h in range(A_HEADS)]
    gc_alls = [exact_tril_dot(g_all[ci * c:(ci + 1) * c]) for ci in range(n_chunks)]
    qs, ks, kbs, rhss, gcs, gls = [], [], [], [], [], []
    for ci, h in chains:
        rs = slice(ci * c, (ci + 1) * c)
        q = conv[rs, h * A_DK:(h + 1) * A_DK]
        k = conv[rs, A_HEADS * A_DK + h * A_DK:A_HEADS * A_DK + (h + 1) * A_DK]
        v = conv[rs, 2 * A_HEADS * A_DK + h * A_DV:2 * A_HEADS * A_DK + (h + 1) * A_DV]
        q = q * lax.rsqrt(jnp.sum(q * q, axis=-1, keepdims=True) + EPS) * (A_DK ** -0.5)
        k = k * lax.rsqrt(jnp.sum(k * k, axis=-1, keepdims=True) + EPS)
        beta = beta_all[rs, h:h + 1]
        gc = gc_alls[ci][:, A_HEADS + h:A_HEADS + h + 1]
        kb = k * beta
        qs.append(q)
        ks.append(k)
        kbs.append(kb)
        rhss.append(jnp.concatenate([v * beta, kb * jnp.exp(gc)], axis=1))
        gcs.append(gc)
        gls.append(gc_alls[ci][c - 1:c, A_HEADS + h:A_HEADS + h + 1])
    e_mats = [exact_tril_dot(jnp.where(strict, g_all[ci * c:(ci + 1) * c, A_HEADS + h:A_HEADS + h + 1], 0.0))
              for ci, h in chains]
    decays = [jnp.where(incl, jnp.exp(jnp.where(incl, e, 0.0)), 0.0) for e in e_mats]
    kks = [_dot_nt(kb, k) for kb, k in zip(kbs, ks)]
    qks = [_dot_nt(q, k) for q, k in zip(qs, ks)]
    pws = [jnp.where(strict, a * d, 0.0) for a, d in zip(kks, decays)]
    qks = [jnp.where(incl, a * d, 0.0) for a, d in zip(qks, decays)]
    t_invs = [eye - lw for lw in pws]
    for _ in range(c.bit_length() - 2):
        pws = [_dot3(pw, pw) for pw in pws]
        t_invs = [ti + _dot3(ti, pw) for ti, pw in zip(t_invs, pws)]
    uws = [_dot(ti, rhs) for ti, rhs in zip(t_invs, rhss)]

    states = [s_ref[h] for h in range(A_HEADS)]
    for ci in range(n_chunks):
        idx = [ci * A_HEADS + h for h in range(A_HEADS)]
        ws = [_dot(uws[i][:, A_DV:], states[h]) for h, i in enumerate(idx)]
        os = [_dot(qs[i] * jnp.exp(gcs[i]), states[h]) for h, i in enumerate(idx)]
        es = [uws[i][:, :A_DV] - w for i, w in zip(idx, ws)]
        os = [o + _dot(qks[i], e) for o, i, e in zip(os, idx, es)]
        upd = [_dot_tn(ks[i] * jnp.exp(gls[i] - gcs[i]), e) for i, e in zip(idx, es)]
        states = [s * jnp.exp(gls[i]) + d for s, i, d in zip(states, idx, upd)]
        r0 = ci * c
        r1 = min(rows, r0 + c)
        for h, o in enumerate(os):
            o = o if r1 - r0 == c else o[:r1 - r0]
            o_ref[r0:r1, h * A_DV:(h + 1) * A_DV] = (
                _rms(o, nwa) * _silu(z[r0:r1, h * A_DV:(h + 1) * A_DV]))
    for h in range(A_HEADS):
        s_ref[h] = states[h]

    @pl.when(n == pl.num_programs(1) - 1)
    def _():
        snew_ref[...] = s_ref[...]


GDN_ROWS = 4 * CHUNK


def _gdn(proj, conv0, s0, w_conv, gate_par, norm_a):
    b, t, _ = proj.shape
    rows = min(GDN_ROWS, t)
    assert t % rows == 0 and (rows % CHUNK == 0 or rows == t)
    n = t // rows
    kernel = functools.partial(_gdn_kernel, rows=rows)
    return pl.pallas_call(
        kernel,
        grid=(b, n),
        in_specs=[pl.BlockSpec((None, rows, CONV_DIM), lambda i, j: (i, j, 0)),
                  pl.BlockSpec((None, rows, A_WIDTH), lambda i, j: (i, j, COL_Z // A_WIDTH)),
                  pl.BlockSpec((None, rows, GATE_PAD), lambda i, j: (i, j, COL_G // GATE_PAD)),
                  pl.BlockSpec((None, 8, CONV_DIM), lambda i, j: (i, 0, 0)),
                  pl.BlockSpec((None, A_HEADS, A_DK, A_DV), lambda i, j: (i, 0, 0, 0)),
                  pl.BlockSpec((CONV_W, CONV_DIM), lambda i, j: (0, 0)),
                  pl.BlockSpec((8, GATE_PAD), lambda i, j: (0, 0)),
                  pl.BlockSpec((1, A_DV), lambda i, j: (0, 0))],
        out_specs=[pl.BlockSpec((None, rows, A_WIDTH), lambda i, j: (i, j, 0)),
                   pl.BlockSpec((None, A_HEADS, A_DK, A_DV), lambda i, j: (i, 0, 0, 0))],
        out_shape=[jax.ShapeDtypeStruct((b, t, A_WIDTH), F32),
                   jax.ShapeDtypeStruct((b, A_HEADS, A_DK, A_DV), F32)],
        scratch_shapes=[pltpu.VMEM((8 + -(-rows // CHUNK) * CHUNK, CONV_DIM), F32),
                        pltpu.VMEM((A_HEADS, A_DK, A_DV), F32)],
        compiler_params=pltpu.CompilerParams(dimension_semantics=("arbitrary", "arbitrary"),
                                             vmem_limit_bytes=VMEM_LIMIT),
        name="gdn",
    )(proj, proj, proj, conv0, s0, w_conv, gate_par, norm_a)


def _slope(h):
    return 2.0 ** (-8.0 * (h + 1) / B_HEADS)


ATTN_SB = MAX_WINDOW
ATTN_GROUP = 4
PAIR = 128 // B_HEAD_DIM


def _attn_fused_kernel(q_ref, kp_ref, kc_ref, vp_ref, vc_ref, nwb_ref, o_ref, kbuf, vbuf, *stats):
    hp = pl.program_id(1)
    n = pl.program_id(2)
    sb = ATTN_SB
    blk = STEPS
    kbuf[0:sb, :] = kp_ref[...]
    kbuf[sb:2 * sb, :] = kc_ref[...]
    vbuf[0:sb, :] = vp_ref[...]
    vbuf[sb:2 * sb, :] = vc_ref[...]

    lane_q = lax.broadcasted_iota(jnp.int32, (blk, 128), 1)
    low_q = lane_q < B_HEAD_DIM
    qq = lax.broadcasted_iota(jnp.int32, (2 * blk, 2 * blk), 0)
    kk = lax.broadcasted_iota(jnp.int32, (2 * blk, 2 * blk), 1)
    steps_back = (qq & (blk - 1)) + blk - kk
    in_band = (steps_back >= 0) & (steps_back <= STEPS)
    slope_lo = jnp.float32(_slope(0))
    slope_hi = jnp.float32(_slope(1))
    for i in range(1, B_HEADS // PAIR):
        slope_lo = jnp.where(hp == i, _slope(PAIR * i), slope_lo)
        slope_hi = jnp.where(hp == i, _slope(PAIR * i + 1), slope_hi)
    slope = jnp.where(qq < blk, slope_lo, slope_hi)

    for bi, (_, dil) in enumerate(DILATED):
        m_s, l_s, num_s = stats[3 * bi:3 * bi + 3]
        bias = jnp.where(in_band, -slope * (steps_back * dil).astype(F32), NEG)
        d_bits = dil.bit_length() - 1

        def rows(start, size):
            if dil == 1:
                return pl.ds(pl.multiple_of(start, blk), size)
            return pl.ds(start, size, stride=dil)

        def group(g, carry):
            tiles = []
            for u in range(ATTN_GROUP):
                it = g * ATTN_GROUP + u
                q0 = lax.shift_right_logical(it, d_bits) * (blk * dil) + (it & (dil - 1))
                k0 = sb + q0 - blk * dil
                q = q_ref[rows(q0, blk), :] * (B_HEAD_DIM ** -0.5)
                q2 = jnp.concatenate([jnp.where(low_q, q, 0.0), jnp.where(low_q, 0.0, q)], axis=0)
                tiles.append((q0, k0, q2.astype(BF16), kbuf[rows(k0, 2 * blk), :].astype(BF16),
                              vbuf[rows(k0, 2 * blk), :].astype(BF16)))
            scores = [lax.dot_general(q2, k, (((1,), (1,)), ((), ())), preferred_element_type=F32)
                      for (_, _, q2, k, _) in tiles]
            probs = []
            for (q0, k0, _, _, _), s in zip(tiles, scores):
                seen = (n > 0) | (k0 + kk * dil >= sb)
                s = jnp.where(seen, s + bias, NEG)
                m = jnp.max(s, axis=-1, keepdims=True)
                p = jnp.exp(s - m)
                probs.append((m, jnp.sum(p, axis=-1, keepdims=True), p.astype(BF16)))
            for (q0, _, _, _, v), (m, l, p) in zip(tiles, probs):
                pv = jnp.dot(p, v, preferred_element_type=F32)
                m_s[rows(q0, blk), :] = jnp.where(low_q, m[:blk], m[blk:])
                l_s[rows(q0, blk), :] = jnp.where(low_q, l[:blk], l[blk:])
                num_s[rows(q0, blk), :] = jnp.where(low_q, pv[:blk], pv[blk:])
            return carry

        lax.fori_loop(0, sb // (blk * ATTN_GROUP), group, 0)

    nwb = nwb_ref[...]

    def combine(i, carry):
        rs = pl.ds(pl.multiple_of(i * blk, blk), blk)
        ms = [stats[3 * bi][rs, :] for bi in range(len(DILATED))]
        ls = [stats[3 * bi + 1][rs, :] for bi in range(len(DILATED))]
        nums = [stats[3 * bi + 2][rs, :] for bi in range(len(DILATED))]
        mx = jnp.maximum(jnp.maximum(ms[0], ms[1]), ms[2])
        ws = [jnp.exp(m - mx) for m in ms]
        den = ws[0] * ls[0] + ws[1] * ls[1] + ws[2] * ls[2]
        o = (ws[0] * nums[0] + ws[1] * nums[1] + ws[2] * nums[2]) / den
        sq = o * o
        ss_lo = jnp.sum(jnp.where(low_q, sq, 0.0), axis=-1, keepdims=True)
        ss_hi = jnp.sum(jnp.where(low_q, 0.0, sq), axis=-1, keepdims=True)
        mean_sq = jnp.where(low_q, ss_lo, ss_hi) / B_HEAD_DIM
        o_ref[rs, :] = o * lax.rsqrt(mean_sq + EPS) * nwb
        return carry

    lax.fori_loop(0, sb // blk, combine, 0, unroll=4)


def _attn_prompt(qkv, norm_b_pair):
    b, s, _ = qkv.shape
    sb = ATTN_SB
    assert s % sb == 0 and (sb // STEPS) % ATTN_GROUP == 0
    assert all(w == STEPS * d and sb % w == 0 and d & (d - 1) == 0 for (w, d) in DILATED)
    n_pairs = B_HEADS // PAIR
    blk = (None, sb, 128)
    prev = lambda c0: pl.BlockSpec(blk, lambda i, p, n: (i, jnp.maximum(n - 1, 0), c0 + p))
    cur = lambda c0: pl.BlockSpec(blk, lambda i, p, n: (i, n, c0 + p))
    return pl.pallas_call(
        _attn_fused_kernel,
        grid=(b, n_pairs, s // sb),
        in_specs=[cur(0), prev(n_pairs), cur(n_pairs), prev(2 * n_pairs), cur(2 * n_pairs),
                  pl.BlockSpec((1, 128), lambda i, p, n: (0, 0))],
        out_specs=pl.BlockSpec(blk, lambda i, p, n: (i, n, p)),
        out_shape=jax.ShapeDtypeStruct((b, s, B_WIDTH), F32),
        scratch_shapes=[pltpu.VMEM((2 * sb, 128), F32), pltpu.VMEM((2 * sb, 128), F32)]
                       + [pltpu.VMEM((sb, 128), F32)] * (3 * len(DILATED)),
        compiler_params=pltpu.CompilerParams(
            dimension_semantics=("arbitrary", "arbitrary", "arbitrary"), vmem_limit_bytes=VMEM_LIMIT),
        name="attn_prompt",
    )(qkv, qkv, qkv, qkv, qkv, norm_b_pair)


def _attn_sample_kernel(q_ref, kn_ref, vn_ref, kt_ref, vt_ref, nwb_ref, o_ref, kto_ref, vto_ref, *, n_past, t):
    kn = kn_ref[...]
    vn = vn_ref[...]
    kt = kt_ref[...]
    vt = vt_ref[...]

    prow = lax.broadcasted_iota(jnp.int32, (t, 128), 0)
    plane = lax.broadcasted_iota(jnp.int32, (t, 128), 1)
    place = jnp.where(plane == 128 - t + prow, 1.0, 0.0).astype(BF16)
    tail_lanes = lax.broadcasted_iota(jnp.int32, (B_WIDTH, 128), 1) >= 128 - t
    tn = functools.partial(lax.dot_general, dimension_numbers=(((0,), (0,)), ((), ())),
                           preferred_element_type=F32)

    def shifted(win, new, out_ref):
        hi, lo = _split2(new)
        lo2 = (new - hi.astype(F32) - lo.astype(F32)).astype(BF16)
        new_t = tn(hi, place) + tn(lo, place) + tn(lo2, place)
        sh = pltpu.roll(win, n_past - t, axis=1)
        out_ref[:, :n_past - 128] = sh[:, :n_past - 128]
        out_ref[:, n_past - 128:] = jnp.where(tail_lanes, new_t, sh[:, n_past - 128:])

    shifted(kt, kn, kto_ref)
    shifted(vt, vn, vto_ref)

    rq = B_HEADS * t
    t_bits = t.bit_length() - 1
    d_bits = B_HEAD_DIM.bit_length() - 1
    rowi = lax.broadcasted_iota(jnp.int32, (rq, B_WIDTH), 0)
    lanei = lax.broadcasted_iota(jnp.int32, (rq, B_WIDTH), 1)
    own = lax.shift_right_logical(rowi, t_bits) == lax.shift_right_logical(lanei, d_bits)
    q_rep = jnp.concatenate([q_ref[...]] * B_HEADS, axis=0)
    q_blk = jnp.where(own, q_rep, 0.0).astype(BF16)
    zpad = jnp.zeros((128 - t, B_WIDTH), BF16)
    kn_pad = jnp.concatenate([kn.astype(BF16), zpad], axis=0)
    vn_pad = jnp.concatenate([vn.astype(BF16), zpad], axis=0)
    vt16 = vt.astype(BF16)
    scale = B_HEAD_DIM ** -0.5
    s_c = jnp.dot(q_blk, kt.astype(BF16), preferred_element_type=F32) * scale
    s_n = lax.dot_general(q_blk, kn_pad, (((1,), (1,)), ((), ())), preferred_element_type=F32) * scale

    r1 = lax.broadcasted_iota(jnp.int32, (rq, 1), 0)
    head = lax.shift_right_logical(r1, t_bits)
    slope = jnp.zeros((rq, 1), F32)
    for h in range(B_HEADS):
        slope = jnp.where(head == h, _slope(h), slope)
    tok = r1 & (t - 1)
    delta_c = n_past + tok - lax.broadcasted_iota(jnp.int32, (rq, n_past), 1)
    key_n = lax.broadcasted_iota(jnp.int32, (rq, 128), 1)
    delta_n = tok - key_n
    s_c = s_c - slope * delta_c.astype(F32)
    s_n = s_n - slope * delta_n.astype(F32)
    live_n = (key_n < t) & (delta_n >= 0)

    ms, ls, nums = [], [], []
    for (_, dil) in DILATED:
        valid_c = ((delta_c & (dil - 1)) == 0) & (delta_c <= STEPS * dil)
        valid_n = live_n & ((delta_n & (dil - 1)) == 0) & (delta_n <= STEPS * dil)
        sd_c = jnp.where(valid_c, s_c, NEG)
        sd_n = jnp.where(valid_n, s_n, NEG)
        m = jnp.maximum(jnp.max(sd_c, axis=-1, keepdims=True), jnp.max(sd_n, axis=-1, keepdims=True))
        p_c = jnp.exp(sd_c - m)
        p_n = jnp.exp(sd_n - m)
        ms.append(m)
        ls.append(jnp.sum(p_c, axis=-1, keepdims=True) + jnp.sum(p_n, axis=-1, keepdims=True))
        nums.append(lax.dot_general(p_c.astype(BF16), vt16, (((1,), (1,)), ((), ())), preferred_element_type=F32)
                    + jnp.dot(p_n.astype(BF16), vn_pad, preferred_element_type=F32))
    mx = jnp.maximum(jnp.maximum(ms[0], ms[1]), ms[2])
    ws = [jnp.exp(m - mx) for m in ms]
    den = ws[0] * ls[0] + ws[1] * ls[1] + ws[2] * ls[2]
    o = (ws[0] * nums[0] + ws[1] * nums[1] + ws[2] * nums[2]) / den
    o = jnp.where(own, o, 0.0)
    o = o * lax.rsqrt(jnp.sum(o * o, axis=-1, keepdims=True) / B_HEAD_DIM + EPS) * nwb_ref[...]
    acc = o[0:t, :]
    for h in range(1, B_HEADS):
        acc = acc + o[h * t:(h + 1) * t, :]
    o_ref[...] = acc


def _attn_sample(qkv, win_kt, win_vt, norm_b_tiled):
    b, t, _ = qkv.shape
    n_past = win_kt.shape[2]
    assert n_past == MAX_WINDOW and n_past % 128 == 0 and t % 8 == 0 and t & (t - 1) == 0 and t <= 128
    assert all(d & (d - 1) == 0 for (_, d) in DILATED)
    kernel = functools.partial(_attn_sample_kernel, n_past=n_past, t=t)
    new = pl.BlockSpec((None, t, B_WIDTH), lambda i: (i, 0, 0))
    win = pl.BlockSpec((None, B_WIDTH, n_past), lambda i: (i, 0, 0))
    return pl.pallas_call(
        kernel,
        grid=(b,),
        in_specs=[pl.BlockSpec((None, t, B_WIDTH), lambda i: (i, 0, 0)),
                  pl.BlockSpec((None, t, B_WIDTH), lambda i: (i, 0, 1)),
                  pl.BlockSpec((None, t, B_WIDTH), lambda i: (i, 0, 2)),
                  win, win, pl.BlockSpec((1, B_WIDTH), lambda i: (0, 0))],
        out_specs=[new, win, win],
        out_shape=[jax.ShapeDtypeStruct((b, t, B_WIDTH), F32),
                   jax.ShapeDtypeStruct((b, B_WIDTH, n_past), F32),
                   jax.ShapeDtypeStruct((b, B_WIDTH, n_past), F32)],
        compiler_params=pltpu.CompilerParams(dimension_semantics=("arbitrary",),
                                             vmem_limit_bytes=VMEM_LIMIT),
        name="attn_sample",
    )(qkv, qkv, qkv, win_kt, win_vt, norm_b_tiled)


def _out_ffn_kernel(x_ref, oa_ref, ob_ref, wo_ref, nf_ref, wg_ref, wu_ref, wd_ref, nfin_ref, y_ref, *, ff_chunk):
    mixed = jnp.concatenate([oa_ref[...], ob_ref[...]], axis=-1).astype(BF16)
    x1 = x_ref[...] + jnp.dot(mixed, wo_ref[...], preferred_element_type=F32)
    hf = _rms(x1, nf_ref[...]).astype(BF16)
    x2 = x1
    for c0 in range(0, wg_ref.shape[1], ff_chunk):
        g = jnp.dot(hf, wg_ref[:, c0:c0 + ff_chunk], preferred_element_type=F32)
        u = jnp.dot(hf, wu_ref[:, c0:c0 + ff_chunk], preferred_element_type=F32)
        act = (_silu(g) * u).astype(BF16)
        x2 = x2 + jnp.dot(act, wd_ref[c0:c0 + ff_chunk, :], preferred_element_type=F32)
    y_ref[...] = _rms(x2, nfin_ref[...])


def _out_ffn(x2d, o_a, o_b, w_out, norm_ffn, w_gate, w_up, w_down, norm_final, tm):
    t = x2d.shape[0]
    d_ff = w_gate.shape[1]
    ff_chunk = d_ff // 2 if d_ff % 256 == 0 else d_ff
    once = pl.Buffered(1)
    row = lambda w: pl.BlockSpec((tm, w), lambda i: (i, 0))
    full = lambda a, b: pl.BlockSpec((a, b), lambda i: (0, 0), pipeline_mode=once)
    return pl.pallas_call(
        functools.partial(_out_ffn_kernel, ff_chunk=ff_chunk),
        grid=(t // tm,),
        in_specs=[row(D_MODEL), row(A_WIDTH), row(B_WIDTH),
                  full(D_MODEL, D_MODEL), full(1, D_MODEL),
                  full(D_MODEL, d_ff), full(D_MODEL, d_ff), full(d_ff, D_MODEL), full(1, D_MODEL)],
        out_specs=row(D_MODEL),
        out_shape=jax.ShapeDtypeStruct((t, D_MODEL), F32),
        compiler_params=pltpu.CompilerParams(dimension_semantics=("arbitrary",),
                                             vmem_limit_bytes=VMEM_LIMIT),
        name="out_ffn",
    )(x2d, o_a, o_b, w_out, norm_ffn, w_gate, w_up, w_down, norm_final)


def _layer_params(norm_mix, w_in, w_conv, a_log, dt_bias, norm_out_a, norm_out_b, w_out, norm_ffn,
                  w_gate, w_up, w_down, layer):
    w = w_in[layer]
    n_gate = 2 * A_HEADS
    w_p = jnp.concatenate([w[:, :COL_G + n_gate], jnp.zeros((D_MODEL, GATE_PAD - n_gate), w.dtype),
                           w[:, COL_G + n_gate:]], axis=1).astype(BF16)
    gate_par = jnp.zeros((8, GATE_PAD), F32)
    gate_par = gate_par.at[0, A_HEADS:2 * A_HEADS].set(a_log[layer].astype(F32))
    gate_par = gate_par.at[1, A_HEADS:2 * A_HEADS].set(dt_bias[layer].astype(F32))
    return dict(
        norm_mix=norm_mix[layer].reshape(1, D_MODEL), w_p=w_p, w_conv=w_conv[layer], gate_par=gate_par,
        norm_a=norm_out_a[layer].reshape(1, A_DV),
        norm_b_pair=jnp.tile(norm_out_b[layer], PAIR).reshape(1, PAIR * B_HEAD_DIM),
        norm_b_tiled=jnp.tile(norm_out_b[layer], B_HEADS).reshape(1, B_WIDTH),
        w_out=w_out[layer].astype(BF16), norm_ffn=norm_ffn[layer].reshape(1, D_MODEL),
        w_gate=w_gate[layer].astype(BF16), w_up=w_up[layer].astype(BF16), w_down=w_down[layer].astype(BF16))


def _block(x, conv_buf, s0, win_k, win_v, p, norm_final, tm):
    b, t, _ = x.shape
    assert t >= CONV_W - 1
    x2d = x.reshape(b * t, D_MODEL)
    proj_a, proj_b = _norm_proj(x2d, p["norm_mix"], p["w_p"], tm)
    proj_a = proj_a.reshape(b, t, COLS_A)
    proj_b = proj_b.reshape(b, t, COLS_B)
    conv0 = jnp.pad(conv_buf, ((0, 0), (8 - (CONV_W - 1), 0), (0, 0)))
    o_a, s_new = _gdn(proj_a, conv0, s0, p["w_conv"], p["gate_par"], p["norm_a"])
    new_conv = proj_a[:, t - (CONV_W - 1):, :CONV_DIM]
    if win_k is None:
        o_b = _attn_prompt(proj_b, p["norm_b_pair"]).reshape(b * t, B_WIDTH)
        keep = min(MAX_WINDOW, t)
        shp = (b, keep, B_HEADS, B_HEAD_DIM)
        new_k = proj_b[:, t - keep:, B_WIDTH:2 * B_WIDTH].reshape(shp)
        new_v = proj_b[:, t - keep:, 2 * B_WIDTH:].reshape(shp)
    else:
        n_past = win_k.shape[1]
        to_t = lambda a: jnp.transpose(a, (0, 2, 3, 1)).reshape(b, B_WIDTH, n_past)
        o_b, new_kt, new_vt = _attn_sample(proj_b, to_t(win_k), to_t(win_v), p["norm_b_tiled"])
        from_t = lambda a: jnp.transpose(a.reshape(b, B_HEADS, B_HEAD_DIM, a.shape[2]), (0, 3, 1, 2))
        new_k, new_v = from_t(new_kt), from_t(new_vt)
        o_b = o_b.reshape(b * t, B_WIDTH)
    y = _out_ffn(x2d, o_a.reshape(b * t, A_WIDTH), o_b, p["w_out"], p["norm_ffn"], p["w_gate"], p["w_up"],
                 p["w_down"], norm_final.reshape(1, D_MODEL), tm)
    return y.reshape(b, t, D_MODEL), new_conv, s_new, new_k, new_v


def kernel(x_prompt, x_sample, state_conv, state_rec, cache_win_k, cache_win_v, norm_mix, w_in, w_conv, a_log,
           dt_bias, norm_out_a, norm_out_b, w_out, norm_ffn, w_gate, w_up, w_down, norm_final):
    depth = w_in.shape[0]
    assert depth == 1, "the final norm is fused into the block of a single-layer trunk"
    p = _layer_params(norm_mix, w_in, w_conv, a_log, dt_bias, norm_out_a, norm_out_b, w_out, norm_ffn,
                      w_gate, w_up, w_down, 0)
    bp = x_prompt.shape[0]
    bs, ts, _ = x_sample.shape
    zero_conv = jnp.zeros((bp, CONV_W - 1, CONV_DIM), F32)
    zero_rec = jnp.zeros((bp, A_HEADS, A_DK, A_DV), F32)
    yp, pc, pr, pk, pv = _block(x_prompt, zero_conv, zero_rec, None, None, p, norm_final, 512)
    ys, sc, sr, sk, sv = _block(x_sample, state_conv[0], state_rec[0], cache_win_k[0], cache_win_v[0], p,
                                norm_final, bs * ts)
    return (yp, ys, pc[None], pr[None], pk[None], pv[None], sc[None], sr[None], sk[None], sv[None])
```

```python
import functools

import jax
import jax.numpy as jnp
from jax import lax
from jax.experimental import pallas as pl
from jax.experimental.pallas import tpu as pltpu

F32 = jnp.float32
BF16 = jnp.bfloat16

D_MODEL = 1024
A_HEADS = 4
A_DK = 128
A_DV = 128
A_WIDTH = A_HEADS * A_DV
CONV_W = 4
CONV_DIM = 2 * A_HEADS * A_DK + A_HEADS * A_DV
B_HEADS = 8
B_HEAD_DIM = 64
B_WIDTH = B_HEADS * B_HEAD_DIM
DILATED = ((128, 1), (512, 4), (2048, 16))
STEPS = 128
MAX_WINDOW = 2048
EPS = 1e-6
NEG = -1e30
CHUNK = 64
GATE_PAD = 128
COL_Z = CONV_DIM
COL_G = CONV_DIM + A_WIDTH
COLS_A = COL_G + GATE_PAD
COLS_B = 3 * B_WIDTH
VMEM_LIMIT = 56 * 1024 * 1024


def _dot(a, b):
    return jnp.dot(a.astype(BF16), b.astype(BF16), preferred_element_type=F32)


def _dot_nt(a, b):
    return lax.dot_general(a.astype(BF16), b.astype(BF16), (((1,), (1,)), ((), ())),
                           preferred_element_type=F32)


def _dot_tn(a, b):
    return lax.dot_general(a.astype(BF16), b.astype(BF16), (((0,), (0,)), ((), ())),
                           preferred_element_type=F32)


def _split2(a):
    hi = a.astype(BF16)
    lo = (a - hi.astype(F32)).astype(BF16)
    return hi, lo


def _dot3(a, b):
    ah, al = _split2(a)
    bh, bl = _split2(b)
    d = functools.partial(jnp.dot, preferred_element_type=F32)
    return d(ah, bh) + d(ah, bl) + d(al, bh)


def _sigmoid(x):
    return 1.0 / (1.0 + jnp.exp(-x))


def _silu(x):
    return x * _sigmoid(x)


def _softplus(x):
    return jnp.maximum(x, 0.0) + jnp.log(1.0 + jnp.exp(-jnp.abs(x)))


def _rms(x, w):
    return x * lax.rsqrt(jnp.mean(x * x, axis=-1, keepdims=True) + EPS) * w


def _norm_proj_kernel(x_ref, nw_ref, w_ref, oa_ref, ob_ref):
    h = _rms(x_ref[...], nw_ref[...]).astype(BF16)
    oa_ref[...] = jnp.dot(h, w_ref[:, :COLS_A], preferred_element_type=F32)
    ob_ref[...] = jnp.dot(h, w_ref[:, COLS_A:], preferred_element_type=F32)


def _norm_proj(x2d, norm_w, w_p, tm):
    t = x2d.shape[0]
    once = pl.Buffered(1)
    return pl.pallas_call(
        _norm_proj_kernel,
        grid=(t // tm,),
        in_specs=[pl.BlockSpec((tm, D_MODEL), lambda i: (i, 0)),
                  pl.BlockSpec((1, D_MODEL), lambda i: (0, 0), pipeline_mode=once),
                  pl.BlockSpec((D_MODEL, COLS_A + COLS_B), lambda i: (0, 0), pipeline_mode=once)],
        out_specs=[pl.BlockSpec((tm, COLS_A), lambda i: (i, 0)),
                   pl.BlockSpec((tm, COLS_B), lambda i: (i, 0))],
        out_shape=[jax.ShapeDtypeStruct((t, COLS_A), F32),
                   jax.ShapeDtypeStruct((t, COLS_B), F32)],
        compiler_params=pltpu.CompilerParams(dimension_semantics=("arbitrary",),
                                             vmem_limit_bytes=VMEM_LIMIT),
        name="norm_proj",
    )(x2d, norm_w, w_p)


def _gdn_kernel(xa_ref, z_ref, gt_ref, conv0_ref, s0_ref, wconv_ref, gp_ref, nwa_ref,
                o_ref, snew_ref, ext_ref, s_ref, *, rows):
    n = pl.program_id(1)
    c = CHUNK
    n_chunks = -(-rows // c)
    rp = n_chunks * c

    @pl.when(n == 0)
    def _():
        ext_ref[0:8, :] = conv0_ref[...]
        s_ref[...] = s0_ref[...]

    ext_ref[8:8 + rows, :] = xa_ref[...]
    if rows < rp:
        ext_ref[8 + rows:8 + rp, :] = jnp.zeros((rp - rows, CONV_DIM), F32)

    conv = ext_ref[5:5 + rp, :] * wconv_ref[0:1, :]
    for i in range(1, CONV_W):
        conv = conv + ext_ref[5 + i:5 + i + rp, :] * wconv_ref[i:i + 1, :]
    conv = _silu(conv)
    ext_ref[0:8, :] = ext_ref[rp:rp + 8, :]

    row = lax.broadcasted_iota(jnp.int32, (rp, 1), 0)
    live = row < rows
    if rows < rp:
        conv = jnp.where(live, conv, 0.0)

    gt = gt_ref[...]
    if rows < rp:
        gt = jnp.concatenate([gt, jnp.zeros((rp - rows, GATE_PAD), F32)], axis=0)
    beta_all = _sigmoid(gt)
    g_all = -jnp.exp(gp_ref[0:1, :]) * _softplus(gt + gp_ref[1:2, :])
    lane = lax.broadcasted_iota(jnp.int32, (rp, GATE_PAD), 1)
    g_all = jnp.where((lane >= A_HEADS) & (lane < 2 * A_HEADS) & live, g_all, 0.0)
    beta_all = jnp.where(live, beta_all, 0.0)

    ii = lax.broadcasted_iota(jnp.int32, (c, c), 0)
    jj = lax.broadcasted_iota(jnp.int32, (c, c), 1)
    incl = ii >= jj
    strict = ii > jj
    tril = jnp.where(incl, 1.0, 0.0).astype(BF16)
    eye = jnp.where(ii == jj, 1.0, 0.0).astype(F32)
    dd = functools.partial(jnp.dot, preferred_element_type=F32)

    def exact_tril_dot(x):
        hi, lo = _split2(x)
        lo2 = (x - hi.astype(F32) - lo.astype(F32)).astype(BF16)
        return dd(tril, hi) + dd(tril, lo) + dd(tril, lo2)

    z = z_ref[...]
    nwa = nwa_ref[...]
    chains = [(ci, h) for ci in range(n_chunks) for h in range(A_HEADS)]
    gc_alls = [exact_tril_dot(g_all[ci * c:(ci + 1) * c]) for ci in range(n_chunks)]
    qs, ks, kbs, rhss, gcs, gls = [], [], [], [], [], []
    for ci, h in chains:
        rs = slice(ci * c, (ci + 1) * c)
        q = conv[rs, h * A_DK:(h + 1) * A_DK]
        k = conv[rs, A_HEADS * A_DK + h * A_DK:A_HEADS * A_DK + (h + 1) * A_DK]
        v = conv[rs, 2 * A_HEADS * A_DK + h * A_DV:2 * A_HEADS * A_DK + (h + 1) * A_DV]
        q = q * lax.rsqrt(jnp.sum(q * q, axis=-1, keepdims=True) + EPS) * (A_DK ** -0.5)
        k = k * lax.rsqrt(jnp.sum(k * k, axis=-1, keepdims=True) + EPS)
        beta = beta_all[rs, h:h + 1]
        gc = gc_alls[ci][:, A_HEADS + h:A_HEADS + h + 1]
        kb = k * beta
        qs.append(q)
        ks.append(k)
        kbs.append(kb)
        rhss.append(jnp.concatenate([v * beta, kb * jnp.exp(gc)], axis=1))
        gcs.append(gc)
        gls.append(gc_alls[ci][c - 1:c, A_HEADS + h:A_HEADS + h + 1])
    e_mats = [exact_tril_dot(jnp.where(strict, g_all[ci * c:(ci + 1) * c, A_HEADS + h:A_HEADS + h + 1], 0.0))
              for ci, h in chains]
    decays = [jnp.where(incl, jnp.exp(jnp.where(incl, e, 0.0)), 0.0) for e in e_mats]
    kks = [_dot_nt(kb, k) for kb, k in zip(kbs, ks)]
    qks = [_dot_nt(q, k) for q, k in zip(qs, ks)]
    pws = [jnp.where(strict, a * d, 0.0) for a, d in zip(kks, decays)]
    qks = [jnp.where(incl, a * d, 0.0) for a, d in zip(qks, decays)]
    t_invs = [eye - lw for lw in pws]
    for _ in range(c.bit_length() - 2):
        pws = [_dot3(pw, pw) for pw in pws]
        t_invs = [ti + _dot3(ti, pw) for ti, pw in zip(t_invs, pws)]
    uws = [_dot(ti, rhs) for ti, rhs in zip(t_invs, rhss)]

    states = [s_ref[h] for h in range(A_HEADS)]
    for ci in range(n_chunks):
        idx = [ci * A_HEADS + h for h in range(A_HEADS)]
        ws = [_dot(uws[i][:, A_DV:], states[h]) for h, i in enumerate(idx)]
        os = [_dot(qs[i] * jnp.exp(gcs[i]), states[h]) for h, i in enumerate(idx)]
        es = [uws[i][:, :A_DV] - w for i, w in zip(idx, ws)]
        os = [o + _dot(qks[i], e) for o, i, e in zip(os, idx, es)]
        upd = [_dot_tn(ks[i] * jnp.exp(gls[i] - gcs[i]), e) for i, e in zip(idx, es)]
        states = [s * jnp.exp(gls[i]) + d for s, i, d in zip(states, idx, upd)]
        r0 = ci * c
        r1 = min(rows, r0 + c)
        for h, o in enumerate(os):
            o = o if r1 - r0 == c else o[:r1 - r0]
            o_ref[r0:r1, h * A_DV:(h + 1) * A_DV] = (
                _rms(o, nwa) * _silu(z[r0:r1, h * A_DV:(h + 1) * A_DV]))
    for h in range(A_HEADS):
        s_ref[h] = states[h]

    @pl.when(n == pl.num_programs(1) - 1)
    def _():
        snew_ref[...] = s_ref[...]


GDN_ROWS = 4 * CHUNK


def _gdn(proj, conv0, s0, w_conv, gate_par, norm_a):
    b, t, _ = proj.shape
    rows = min(GDN_ROWS, t)
    assert t % rows == 0 and (rows % CHUNK == 0 or rows == t)
    n = t // rows
    kernel = functools.partial(_gdn_kernel, rows=rows)
    return pl.pallas_call(
        kernel,
        grid=(b, n),
        in_specs=[pl.BlockSpec((None, rows, CONV_DIM), lambda i, j: (i, j, 0)),
                  pl.BlockSpec((None, rows, A_WIDTH), lambda i, j: (i, j, COL_Z // A_WIDTH)),
                  pl.BlockSpec((None, rows, GATE_PAD), lambda i, j: (i, j, COL_G // GATE_PAD)),
                  pl.BlockSpec((None, 8, CONV_DIM), lambda i, j: (i, 0, 0)),
                  pl.BlockSpec((None, A_HEADS, A_DK, A_DV), lambda i, j: (i, 0, 0, 0)),
                  pl.BlockSpec((CONV_W, CONV_DIM), lambda i, j: (0, 0)),
                  pl.BlockSpec((8, GATE_PAD), lambda i, j: (0, 0)),
                  pl.BlockSpec((1, A_DV), lambda i, j: (0, 0))],
        out_specs=[pl.BlockSpec((None, rows, A_WIDTH), lambda i, j: (i, j, 0)),
                   pl.BlockSpec((None, A_HEADS, A_DK, A_DV), lambda i, j: (i, 0, 0, 0))],
        out_shape=[jax.ShapeDtypeStruct((b, t, A_WIDTH), F32),
                   jax.ShapeDtypeStruct((b, A_HEADS, A_DK, A_DV), F32)],
        scratch_shapes=[pltpu.VMEM((8 + -(-rows // CHUNK) * CHUNK, CONV_DIM), F32),
                        pltpu.VMEM((A_HEADS, A_DK, A_DV), F32)],
        compiler_params=pltpu.CompilerParams(dimension_semantics=("arbitrary", "arbitrary"),
                                             vmem_limit_bytes=VMEM_LIMIT),
        name="gdn",
    )(proj, proj, proj, conv0, s0, w_conv, gate_par, norm_a)


HALF = 128 // CHUNK


def _gdn_pair_kernel(xa_ref, z_ref, gt_ref, conv0_ref, s0_ref, wconv_ref, gp_ref, nwa_ref,
                     o_ref, snew_ref, ext_ref, s_ref, *, rows, nb):
    n = pl.program_id(1)
    c = CHUNK
    n_chunks = -(-rows // c)
    rp = n_chunks * c
    assert A_HEADS % 2 == 0 and HALF == 2 and A_DK == 128 and A_DV == 128

    @pl.when(n == 0)
    def _():
        ext_ref[:, 0:8, :] = conv0_ref[...]
        s_ref[...] = s0_ref[...]

    row = lax.broadcasted_iota(jnp.int32, (rp, 1), 0)
    live = row < rows
    lane_g = lax.broadcasted_iota(jnp.int32, (rp, GATE_PAD), 1)
    gate_lanes = (lane_g >= A_HEADS) & (lane_g < 2 * A_HEADS) & live

    ii = lax.broadcasted_iota(jnp.int32, (c, 128), 0)
    ll = lax.broadcasted_iota(jnp.int32, (c, 128), 1)
    jj = ll & (c - 1)
    low = ll < c
    incl = ii >= jj
    strict = ii > jj
    eye = jnp.where(ii == jj, 1.0, 0.0).astype(F32)
    ti = lax.broadcasted_iota(jnp.int32, (c, c), 0)
    tj = lax.broadcasted_iota(jnp.int32, (c, c), 1)
    tril = jnp.where(ti >= tj, 1.0, 0.0).astype(BF16)
    dd = functools.partial(jnp.dot, preferred_element_type=F32)

    def exact_tril_dot(x):
        hi, lo = _split2(x)
        lo2 = (x - hi.astype(F32) - lo.astype(F32)).astype(BF16)
        return dd(tril, hi) + dd(tril, lo) + dd(tril, lo2)

    def bd(x):
        return jnp.concatenate([jnp.where(low, x, 0.0), jnp.where(low, 0.0, x)], axis=0).astype(BF16)

    def unstack(y):
        return jnp.where(low, y[:c], y[c:])

    nwa = nwa_ref[...]
    chains = []
    for bi in range(nb):
        ext_ref[bi, 8:8 + rows, :] = xa_ref[bi]
        if rows < rp:
            ext_ref[bi, 8 + rows:8 + rp, :] = jnp.zeros((rp - rows, CONV_DIM), F32)
        x_all = ext_ref[bi, 0:8 + rp, :]
        conv = x_all[8:8 + rp] * wconv_ref[CONV_W - 1:CONV_W, :]
        for back in range(1, CONV_W):
            tap = CONV_W - 1 - back
            conv = conv + pltpu.roll(x_all, back, axis=0)[8:8 + rp] * wconv_ref[tap:tap + 1, :]
        conv = _silu(conv)
        ext_ref[bi, 0:8, :] = ext_ref[bi, rp:rp + 8, :]
        if rows < rp:
            conv = jnp.where(live, conv, 0.0)

        gt = gt_ref[bi]
        if rows < rp:
            gt = jnp.concatenate([gt, jnp.zeros((rp - rows, GATE_PAD), F32)], axis=0)
        beta_all = jnp.where(live, _sigmoid(gt), 0.0)
        g_all = jnp.where(gate_lanes, -jnp.exp(gp_ref[0:1, :]) * _softplus(gt + gp_ref[1:2, :]), 0.0)

        for ci in range(n_chunks):
            rs = slice(ci * c, (ci + 1) * c)
            gc_all = exact_tril_dot(g_all[rs])
            for p in range(A_HEADS // 2):
                hd = []
                for h in (2 * p, 2 * p + 1):
                    q = conv[rs, h * A_DK:(h + 1) * A_DK]
                    k = conv[rs, A_HEADS * A_DK + h * A_DK:A_HEADS * A_DK + (h + 1) * A_DK]
                    v = conv[rs, 2 * A_HEADS * A_DK + h * A_DV:2 * A_HEADS * A_DK + (h + 1) * A_DV]
                    q = q * lax.rsqrt(jnp.sum(q * q, axis=-1, keepdims=True) + EPS) * (A_DK ** -0.5)
                    k = k * lax.rsqrt(jnp.sum(k * k, axis=-1, keepdims=True) + EPS)
                    beta = beta_all[rs, h:h + 1]
                    g = g_all[rs, A_HEADS + h:A_HEADS + h + 1]
                    gc = gc_all[:, A_HEADS + h:A_HEADS + h + 1]
                    gl = gc_all[c - 1:c, A_HEADS + h:A_HEADS + h + 1]
                    kb = k * beta
                    hd.append(dict(q=q, k=k, kb=kb, g=g, qg=q * jnp.exp(gc), kd=k * jnp.exp(gl - gc),
                                   dl=jnp.exp(gl),
                                   rhs=jnp.concatenate([v * beta, kb * jnp.exp(gc)], axis=1)))
                chains.append(dict(bi=bi, ci=ci, p=p, hd=hd))

    for ch in chains:
        a, b = ch["hd"]
        ch["e_mat"] = exact_tril_dot(jnp.where(strict, jnp.where(low, a["g"], b["g"]), 0.0))
    for ch in chains:
        a, b = ch["hd"]
        k_st = jnp.concatenate([a["k"], b["k"]], axis=0)
        ch["kk"] = _dot_nt(jnp.concatenate([a["kb"], b["kb"]], axis=0), k_st)
        ch["qk"] = _dot_nt(jnp.concatenate([a["q"], b["q"]], axis=0), k_st)
    for ch in chains:
        decay = jnp.where(incl, jnp.exp(jnp.where(incl, ch["e_mat"], 0.0)), 0.0)
        ch["lower"] = jnp.where(strict, unstack(ch["kk"]) * decay, 0.0)
        ch["qk"] = jnp.where(incl, unstack(ch["qk"]) * decay, 0.0)
        ch["x"] = ch["lower"]
        ch["xb"] = bd(ch["lower"])
        ch["t"] = eye - ch["lower"]
    for _ in range(c.bit_length() - 2):
        for ch in chains:
            ch["x"] = jnp.dot(ch["x"].astype(BF16), ch["xb"], preferred_element_type=F32)
        for ch in chains:
            ch["xb"] = bd(ch["x"])
            ch["t"] = ch["t"] + jnp.dot(ch["t"].astype(BF16), ch["xb"], preferred_element_type=F32)
    for ch in chains:
        lh, ll_ = _split2(ch["lower"])
        th = ch["t"].astype(BF16).astype(F32)
        tl = ch["t"] - th
        bth = bd(th)
        lt = dd(lh, bth) + dd(lh, bd(tl)) + dd(ll_, bth)
        ch["res"] = eye - ch["t"] - lt
    for ch in chains:
        ch["t"] = ch["t"] + jnp.dot(ch["t"].astype(BF16), bd(ch["res"]), preferred_element_type=F32)
    for ch in chains:
        a, b = ch["hd"]
        uw = jnp.dot(bd(ch["t"]), jnp.concatenate([a["rhs"], b["rhs"]], axis=0).astype(BF16),
                     preferred_element_type=F32)
        a["u"], a["w"] = uw[:c, :A_DV], uw[:c, A_DV:]
        b["u"], b["w"] = uw[c:, :A_DV], uw[c:, A_DV:]
        ch["qkb"] = bd(ch["qk"])

    by_key = {(ch["bi"], ch["ci"], ch["p"]): ch for ch in chains}
    states = {(bi, h): s_ref[bi, h] for bi in range(nb) for h in range(A_HEADS)}
    for ci in range(n_chunks):
        cur = [by_key[(bi, ci, p)] for bi in range(nb) for p in range(A_HEADS // 2)]
        for ch in cur:
            for hh, d in enumerate(ch["hd"]):
                ws = _dot(jnp.concatenate([d["w"], d["qg"]], axis=0), states[(ch["bi"], 2 * ch["p"] + hh)])
                d["e"] = d["u"] - ws[:c]
                d["o"] = ws[c:]
        for ch in cur:
            a, b = ch["hd"]
            o2 = jnp.dot(ch["qkb"], jnp.concatenate([a["e"], b["e"]], axis=0).astype(BF16),
                         preferred_element_type=F32)
            a["o"] = a["o"] + o2[:c]
            b["o"] = b["o"] + o2[c:]
        for ch in cur:
            for hh, d in enumerate(ch["hd"]):
                key = (ch["bi"], 2 * ch["p"] + hh)
                states[key] = states[key] * d["dl"] + _dot_tn(d["kd"], d["e"])
        r0 = ci * c
        r1 = min(rows, r0 + c)
        for ch in cur:
            for hh, d in enumerate(ch["hd"]):
                h = 2 * ch["p"] + hh
                o = d["o"] if r1 - r0 == c else d["o"][:r1 - r0]
                o_ref[ch["bi"], r0:r1, h * A_DV:(h + 1) * A_DV] = (
                    _rms(o, nwa) * _silu(z_ref[ch["bi"], r0:r1, h * A_DV:(h + 1) * A_DV]))
    for (bi, h), s in states.items():
        s_ref[bi, h] = s

    @pl.when(n == pl.num_programs(1) - 1)
    def _():
        snew_ref[...] = s_ref[...]


def _gdn_pair(proj, conv0, s0, w_conv, gate_par, norm_a):
    b, t, _ = proj.shape
    rows = min(GDN_ROWS, t)
    assert t % rows == 0 and (rows % CHUNK == 0 or rows == t)
    n = t // rows
    nb = max(1, min(b, GDN_ROWS // (-(-rows // CHUNK) * CHUNK)))
    assert b % nb == 0
    rp = -(-rows // CHUNK) * CHUNK
    kernel = functools.partial(_gdn_pair_kernel, rows=rows, nb=nb)
    return pl.pallas_call(
        kernel,
        grid=(b // nb, n),
        in_specs=[pl.BlockSpec((nb, rows, CONV_DIM), lambda i, j: (i, j, 0)),
                  pl.BlockSpec((nb, rows, A_WIDTH), lambda i, j: (i, j, COL_Z // A_WIDTH)),
                  pl.BlockSpec((nb, rows, GATE_PAD), lambda i, j: (i, j, COL_G // GATE_PAD)),
                  pl.BlockSpec((nb, 8, CONV_DIM), lambda i, j: (i, 0, 0)),
                  pl.BlockSpec((nb, A_HEADS, A_DK, A_DV), lambda i, j: (i, 0, 0, 0)),
                  pl.BlockSpec((CONV_W, CONV_DIM), lambda i, j: (0, 0)),
                  pl.BlockSpec((8, GATE_PAD), lambda i, j: (0, 0)),
                  pl.BlockSpec((1, A_DV), lambda i, j: (0, 0))],
        out_specs=[pl.BlockSpec((nb, rows, A_WIDTH), lambda i, j: (i, j, 0)),
                   pl.BlockSpec((nb, A_HEADS, A_DK, A_DV), lambda i, j: (i, 0, 0, 0))],
        out_shape=[jax.ShapeDtypeStruct((b, t, A_WIDTH), F32),
                   jax.ShapeDtypeStruct((b, A_HEADS, A_DK, A_DV), F32)],
        scratch_shapes=[pltpu.VMEM((nb, 8 + rp, CONV_DIM), F32),
                        pltpu.VMEM((nb, A_HEADS, A_DK, A_DV), F32)],
        compiler_params=pltpu.CompilerParams(dimension_semantics=("arbitrary", "arbitrary"),
                                             vmem_limit_bytes=VMEM_LIMIT),
        name="gdn",
    )(proj, proj, proj, conv0, s0, w_conv, gate_par, norm_a)


def _slope(h):
    return 2.0 ** (-8.0 * (h + 1) / B_HEADS)


ATTN_SB = MAX_WINDOW
ATTN_GROUP = 4
PAIR = 128 // B_HEAD_DIM


def _attn_fused_kernel(q_ref, kp_ref, kc_ref, vp_ref, vc_ref, nwb_ref, o_ref, kbuf, vbuf, *stats):
    hp = pl.program_id(1)
    n = pl.program_id(2)
    sb = ATTN_SB
    blk = STEPS
    kbuf[0:sb, :] = kp_ref[...]
    kbuf[sb:2 * sb, :] = kc_ref[...]
    vbuf[0:sb, :] = vp_ref[...]
    vbuf[sb:2 * sb, :] = vc_ref[...]

    lane_q = lax.broadcasted_iota(jnp.int32, (blk, 128), 1)
    low_q = lane_q < B_HEAD_DIM
    qq = lax.broadcasted_iota(jnp.int32, (2 * blk, 2 * blk), 0)
    kk = lax.broadcasted_iota(jnp.int32, (2 * blk, 2 * blk), 1)
    steps_back = (qq & (blk - 1)) + blk - kk
    in_band = (steps_back >= 0) & (steps_back <= STEPS)
    slope_lo = jnp.float32(_slope(0))
    slope_hi = jnp.float32(_slope(1))
    for i in range(1, B_HEADS // PAIR):
        slope_lo = jnp.where(hp == i, _slope(PAIR * i), slope_lo)
        slope_hi = jnp.where(hp == i, _slope(PAIR * i + 1), slope_hi)
    slope = jnp.where(qq < blk, slope_lo, slope_hi)

    for bi, (_, dil) in enumerate(DILATED):
        m_s, l_s, num_s = stats[3 * bi:3 * bi + 3]
        bias = jnp.where(in_band, -slope * (steps_back * dil).astype(F32), NEG)
        d_bits = dil.bit_length() - 1

        def rows(start, size):
            if dil == 1:
                return pl.ds(pl.multiple_of(start, blk), size)
            return pl.ds(start, size, stride=dil)

        def group(g, carry):
            tiles = []
            for u in range(ATTN_GROUP):
                it = g * ATTN_GROUP + u
                q0 = lax.shift_right_logical(it, d_bits) * (blk * dil) + (it & (dil - 1))
                k0 = sb + q0 - blk * dil
                q = q_ref[rows(q0, blk), :] * (B_HEAD_DIM ** -0.5)
                q2 = jnp.concatenate([jnp.where(low_q, q, 0.0), jnp.where(low_q, 0.0, q)], axis=0)
                tiles.append((q0, k0, q2.astype(BF16), kbuf[rows(k0, 2 * blk), :].astype(BF16),
                              vbuf[rows(k0, 2 * blk), :].astype(BF16)))
            scores = [lax.dot_general(q2, k, (((1,), (1,)), ((), ())), preferred_element_type=F32)
                      for (_, _, q2, k, _) in tiles]
            probs = []
            for (q0, k0, _, _, _), s in zip(tiles, scores):
                seen = (n > 0) | (k0 + kk * dil >= sb)
                s = jnp.where(seen, s + bias, NEG)
                m = jnp.max(s, axis=-1, keepdims=True)
                p = jnp.exp(s - m)
                probs.append((m, jnp.sum(p, axis=-1, keepdims=True), p.astype(BF16)))
            for (q0, _, _, _, v), (m, l, p) in zip(tiles, probs):
                pv = jnp.dot(p, v, preferred_element_type=F32)
                m_s[rows(q0, blk), :] = jnp.where(low_q, m[:blk], m[blk:])
                l_s[rows(q0, blk), :] = jnp.where(low_q, l[:blk], l[blk:])
                num_s[rows(q0, blk), :] = jnp.where(low_q, pv[:blk], pv[blk:])
            return carry

        lax.fori_loop(0, sb // (blk * ATTN_GROUP), group, 0)

    nwb = nwb_ref[...]

    def combine(i, carry):
        rs = pl.ds(pl.multiple_of(i * blk, blk), blk)
        ms = [stats[3 * bi][rs, :] for bi in range(len(DILATED))]
        ls = [stats[3 * bi + 1][rs, :] for bi in range(len(DILATED))]
        nums = [stats[3 * bi + 2][rs, :] for bi in range(len(DILATED))]
        mx = jnp.maximum(jnp.maximum(ms[0], ms[1]), ms[2])
        ws = [jnp.exp(m - mx) for m in ms]
        den = ws[0] * ls[0] + ws[1] * ls[1] + ws[2] * ls[2]
        o = (ws[0] * nums[0] + ws[1] * nums[1] + ws[2] * nums[2]) / den
        sq = o * o
        ss_lo = jnp.sum(jnp.where(low_q, sq, 0.0), axis=-1, keepdims=True)
        ss_hi = jnp.sum(jnp.where(low_q, 0.0, sq), axis=-1, keepdims=True)
        mean_sq = jnp.where(low_q, ss_lo, ss_hi) / B_HEAD_DIM
        o_ref[rs, :] = o * lax.rsqrt(mean_sq + EPS) * nwb
        return carry

    lax.fori_loop(0, sb // blk, combine, 0, unroll=4)


def _attn_prompt(qkv, norm_b_pair):
    b, s, _ = qkv.shape
    sb = ATTN_SB
    assert s % sb == 0 and (sb // STEPS) % ATTN_GROUP == 0
    assert all(w == STEPS * d and sb % w == 0 and d & (d - 1) == 0 for (w, d) in DILATED)
    n_pairs = B_HEADS // PAIR
    blk = (None, sb, 128)
    prev = lambda c0: pl.BlockSpec(blk, lambda i, p, n: (i, jnp.maximum(n - 1, 0), c0 + p))
    cur = lambda c0: pl.BlockSpec(blk, lambda i, p, n: (i, n, c0 + p))
    return pl.pallas_call(
        _attn_fused_kernel,
        grid=(b, n_pairs, s // sb),
        in_specs=[cur(0), prev(n_pairs), cur(n_pairs), prev(2 * n_pairs), cur(2 * n_pairs),
                  pl.BlockSpec((1, 128), lambda i, p, n: (0, 0))],
        out_specs=pl.BlockSpec(blk, lambda i, p, n: (i, n, p)),
        out_shape=jax.ShapeDtypeStruct((b, s, B_WIDTH), F32),
        scratch_shapes=[pltpu.VMEM((2 * sb, 128), F32), pltpu.VMEM((2 * sb, 128), F32)]
                       + [pltpu.VMEM((sb, 128), F32)] * (3 * len(DILATED)),
        compiler_params=pltpu.CompilerParams(
            dimension_semantics=("arbitrary", "arbitrary", "arbitrary"), vmem_limit_bytes=VMEM_LIMIT),
        name="attn_prompt",
    )(qkv, qkv, qkv, qkv, qkv, norm_b_pair)


def _attn_sample_kernel(q_ref, kn_ref, vn_ref, kt_ref, vt_ref, nwb_ref, o_ref, kto_ref, vto_ref, *, n_past, t):
    kn = kn_ref[...]
    vn = vn_ref[...]
    kt = kt_ref[...]
    vt = vt_ref[...]

    prow = lax.broadcasted_iota(jnp.int32, (t, 128), 0)
    plane = lax.broadcasted_iota(jnp.int32, (t, 128), 1)
    place = jnp.where(plane == 128 - t + prow, 1.0, 0.0).astype(BF16)
    tail_lanes = lax.broadcasted_iota(jnp.int32, (B_WIDTH, 128), 1) >= 128 - t
    tn = functools.partial(lax.dot_general, dimension_numbers=(((0,), (0,)), ((), ())),
                           preferred_element_type=F32)

    def shifted(win, new, out_ref):
        hi, lo = _split2(new)
        lo2 = (new - hi.astype(F32) - lo.astype(F32)).astype(BF16)
        new_t = tn(hi, place) + tn(lo, place) + tn(lo2, place)
        sh = pltpu.roll(win, n_past - t, axis=1)
        out_ref[:, :n_past - 128] = sh[:, :n_past - 128]
        out_ref[:, n_past - 128:] = jnp.where(tail_lanes, new_t, sh[:, n_past - 128:])

    shifted(kt, kn, kto_ref)
    shifted(vt, vn, vto_ref)

    rq = B_HEADS * t
    t_bits = t.bit_length() - 1
    d_bits = B_HEAD_DIM.bit_length() - 1
    rowi = lax.broadcasted_iota(jnp.int32, (rq, B_WIDTH), 0)
    lanei = lax.broadcasted_iota(jnp.int32, (rq, B_WIDTH), 1)
    own = lax.shift_right_logical(rowi, t_bits) == lax.shift_right_logical(lanei, d_bits)
    q_rep = jnp.concatenate([q_ref[...]] * B_HEADS, axis=0)
    q_blk = jnp.where(own, q_rep, 0.0).astype(BF16)
    zpad = jnp.zeros((128 - t, B_WIDTH), BF16)
    kn_pad = jnp.concatenate([kn.astype(BF16), zpad], axis=0)
    vn_pad = jnp.concatenate([vn.astype(BF16), zpad], axis=0)
    vt16 = vt.astype(BF16)
    scale = B_HEAD_DIM ** -0.5
    s_c = jnp.dot(q_blk, kt.astype(BF16), preferred_element_type=F32) * scale
    s_n = lax.dot_general(q_blk, kn_pad, (((1,), (1,)), ((), ())), preferred_element_type=F32) * scale

    r1 = lax.broadcasted_iota(jnp.int32, (rq, 1), 0)
    head = lax.shift_right_logical(r1, t_bits)
    slope = jnp.zeros((rq, 1), F32)
    for h in range(B_HEADS):
        slope = jnp.where(head == h, _slope(h), slope)
    tok = r1 & (t - 1)
    delta_c = n_past + tok - lax.broadcasted_iota(jnp.int32, (rq, n_past), 1)
    key_n = lax.broadcasted_iota(jnp.int32, (rq, 128), 1)
    delta_n = tok - key_n
    s_c = s_c - slope * delta_c.astype(F32)
    s_n = s_n - slope * delta_n.astype(F32)
    live_n = (key_n < t) & (delta_n >= 0)

    ms, ls, nums = [], [], []
    for (_, dil) in DILATED:
        valid_c = ((delta_c & (dil - 1)) == 0) & (delta_c <= STEPS * dil)
        valid_n = live_n & ((delta_n & (dil - 1)) == 0) & (delta_n <= STEPS * dil)
        sd_c = jnp.where(valid_c, s_c, NEG)
        sd_n = jnp.where(valid_n, s_n, NEG)
        m = jnp.maximum(jnp.max(sd_c, axis=-1, keepdims=True), jnp.max(sd_n, axis=-1, keepdims=True))
        p_c = jnp.exp(sd_c - m)
        p_n = jnp.exp(sd_n - m)
        ms.append(m)
        ls.append(jnp.sum(p_c, axis=-1, keepdims=True) + jnp.sum(p_n, axis=-1, keepdims=True))
        nums.append(lax.dot_general(p_c.astype(BF16), vt16, (((1,), (1,)), ((), ())), preferred_element_type=F32)
                    + jnp.dot(p_n.astype(BF16), vn_pad, preferred_element_type=F32))
    mx = jnp.maximum(jnp.maximum(ms[0], ms[1]), ms[2])
    ws = [jnp.exp(m - mx) for m in ms]
    den = ws[0] * ls[0] + ws[1] * ls[1] + ws[2] * ls[2]
    o = (ws[0] * nums[0] + ws[1] * nums[1] + ws[2] * nums[2]) / den
    o = jnp.where(own, o, 0.0)
    o = o * lax.rsqrt(jnp.sum(o * o, axis=-1, keepdims=True) / B_HEAD_DIM + EPS) * nwb_ref[...]
    acc = o[0:t, :]
    for h in range(1, B_HEADS):
        acc = acc + o[h * t:(h + 1) * t, :]
    o_ref[...] = acc


def _attn_sample(qkv, win_kt, win_vt, norm_b_tiled):
    b, t, _ = qkv.shape
    n_past = win_kt.shape[2]
    assert n_past == MAX_WINDOW and n_past % 128 == 0 and t % 8 == 0 and t & (t - 1) == 0 and t <= 128
    assert all(d & (d - 1) == 0 for (_, d) in DILATED)
    kernel = functools.partial(_attn_sample_kernel, n_past=n_past, t=t)
    new = pl.BlockSpec((None, t, B_WIDTH), lambda i: (i, 0, 0))
    win = pl.BlockSpec((None, B_WIDTH, n_past), lambda i: (i, 0, 0))
    return pl.pallas_call(
        kernel,
        grid=(b,),
        in_specs=[pl.BlockSpec((None, t, B_WIDTH), lambda i: (i, 0, 0)),
                  pl.BlockSpec((None, t, B_WIDTH), lambda i: (i, 0, 1)),
                  pl.BlockSpec((None, t, B_WIDTH), lambda i: (i, 0, 2)),
                  win, win, pl.BlockSpec((1, B_WIDTH), lambda i: (0, 0))],
        out_specs=[new, win, win],
        out_shape=[jax.ShapeDtypeStruct((b, t, B_WIDTH), F32),
                   jax.ShapeDtypeStruct((b, B_WIDTH, n_past), F32),
                   jax.ShapeDtypeStruct((b, B_WIDTH, n_past), F32)],
        compiler_params=pltpu.CompilerParams(dimension_semantics=("arbitrary",),
                                             vmem_limit_bytes=VMEM_LIMIT),
        name="attn_sample",
    )(qkv, qkv, qkv, win_kt, win_vt, norm_b_tiled)


def _out_ffn_kernel(x_ref, oa_ref, ob_ref, wo_ref, nf_ref, wg_ref, wu_ref, wd_ref, nfin_ref, y_ref, *, ff_chunk):
    mixed = jnp.concatenate([oa_ref[...], ob_ref[...]], axis=-1).astype(BF16)
    x1 = x_ref[...] + jnp.dot(mixed, wo_ref[...], preferred_element_type=F32)
    hf = _rms(x1, nf_ref[...]).astype(BF16)
    x2 = x1
    for c0 in range(0, wg_ref.shape[1], ff_chunk):
        g = jnp.dot(hf, wg_ref[:, c0:c0 + ff_chunk], preferred_element_type=F32)
        u = jnp.dot(hf, wu_ref[:, c0:c0 + ff_chunk], preferred_element_type=F32)
        act = (_silu(g) * u).astype(BF16)
        x2 = x2 + jnp.dot(act, wd_ref[c0:c0 + ff_chunk, :], preferred_element_type=F32)
    y_ref[...] = _rms(x2, nfin_ref[...])


def _out_ffn(x2d, o_a, o_b, w_out, norm_ffn, w_gate, w_up, w_down, norm_final, tm):
    t = x2d.shape[0]
    d_ff = w_gate.shape[1]
    ff_chunk = d_ff // 2 if d_ff % 256 == 0 else d_ff
    once = pl.Buffered(1)
    row = lambda w: pl.BlockSpec((tm, w), lambda i: (i, 0))
    full = lambda a, b: pl.BlockSpec((a, b), lambda i: (0, 0), pipeline_mode=once)
    return pl.pallas_call(
        functools.partial(_out_ffn_kernel, ff_chunk=ff_chunk),
        grid=(t // tm,),
        in_specs=[row(D_MODEL), row(A_WIDTH), row(B_WIDTH),
                  full(D_MODEL, D_MODEL), full(1, D_MODEL),
                  full(D_MODEL, d_ff), full(D_MODEL, d_ff), full(d_ff, D_MODEL), full(1, D_MODEL)],
        out_specs=row(D_MODEL),
        out_shape=jax.ShapeDtypeStruct((t, D_MODEL), F32),
        compiler_params=pltpu.CompilerParams(dimension_semantics=("arbitrary",),
                                             vmem_limit_bytes=VMEM_LIMIT),
        name="out_ffn",
    )(x2d, o_a, o_b, w_out, norm_ffn, w_gate, w_up, w_down, norm_final)


def _layer_params(norm_mix, w_in, w_conv, a_log, dt_bias, norm_out_a, norm_out_b, w_out, norm_ffn,
                  w_gate, w_up, w_down, layer):
    w = w_in[layer]
    n_gate = 2 * A_HEADS
    w_p = jnp.concatenate([w[:, :COL_G + n_gate], jnp.zeros((D_MODEL, GATE_PAD - n_gate), w.dtype),
                           w[:, COL_G + n_gate:]], axis=1).astype(BF16)
    gate_par = jnp.zeros((8, GATE_PAD), F32)
    gate_par = gate_par.at[0, A_HEADS:2 * A_HEADS].set(a_log[layer].astype(F32))
    gate_par = gate_par.at[1, A_HEADS:2 * A_HEADS].set(dt_bias[layer].astype(F32))
    return dict(
        norm_mix=norm_mix[layer].reshape(1, D_MODEL), w_p=w_p, w_conv=w_conv[layer], gate_par=gate_par,
        norm_a=norm_out_a[layer].reshape(1, A_DV),
        norm_b_pair=jnp.tile(norm_out_b[layer], PAIR).reshape(1, PAIR * B_HEAD_DIM),
        norm_b_tiled=jnp.tile(norm_out_b[layer], B_HEADS).reshape(1, B_WIDTH),
        w_out=w_out[layer].astype(BF16), norm_ffn=norm_ffn[layer].reshape(1, D_MODEL),
        w_gate=w_gate[layer].astype(BF16), w_up=w_up[layer].astype(BF16), w_down=w_down[layer].astype(BF16))


def _block(x, conv_buf, s0, win_k, win_v, p, norm_final, tm):
    b, t, _ = x.shape
    assert t >= CONV_W - 1
    x2d = x.reshape(b * t, D_MODEL)
    proj_a, proj_b = _norm_proj(x2d, p["norm_mix"], p["w_p"], tm)
    proj_a = proj_a.reshape(b, t, COLS_A)
    proj_b = proj_b.reshape(b, t, COLS_B)
    conv0 = jnp.pad(conv_buf, ((0, 0), (8 - (CONV_W - 1), 0), (0, 0)))
    o_a, s_new = _gdn_pair(proj_a, conv0, s0, p["w_conv"], p["gate_par"], p["norm_a"])
    new_conv = proj_a[:, t - (CONV_W - 1):, :CONV_DIM]
    if win_k is None:
        o_b = _attn_prompt(proj_b, p["norm_b_pair"]).reshape(b * t, B_WIDTH)
        keep = min(MAX_WINDOW, t)
        shp = (b, keep, B_HEADS, B_HEAD_DIM)
        new_k = proj_b[:, t - keep:, B_WIDTH:2 * B_WIDTH].reshape(shp)
        new_v = proj_b[:, t - keep:, 2 * B_WIDTH:].reshape(shp)
    else:
        n_past = win_k.shape[1]
        to_t = lambda a: jnp.transpose(a, (0, 2, 3, 1)).reshape(b, B_WIDTH, n_past)
        o_b, new_kt, new_vt = _attn_sample(proj_b, to_t(win_k), to_t(win_v), p["norm_b_tiled"])
        from_t = lambda a: jnp.transpose(a.reshape(b, B_HEADS, B_HEAD_DIM, a.shape[2]), (0, 3, 1, 2))
        new_k, new_v = from_t(new_kt), from_t(new_vt)
        o_b = o_b.reshape(b * t, B_WIDTH)
    y = _out_ffn(x2d, o_a.reshape(b * t, A_WIDTH), o_b, p["w_out"], p["norm_ffn"], p["w_gate"], p["w_up"],
                 p["w_down"], norm_final.reshape(1, D_MODEL), tm)
    return y.reshape(b, t, D_MODEL), new_conv, s_new, new_k, new_v


def kernel(x_prompt, x_sample, state_conv, state_rec, cache_win_k, cache_win_v, norm_mix, w_in, w_conv, a_log,
           dt_bias, norm_out_a, norm_out_b, w_out, norm_ffn, w_gate, w_up, w_down, norm_final):
    depth = w_in.shape[0]
    assert depth == 1, "the final norm is fused into the block of a single-layer trunk"
    p = _layer_params(norm_mix, w_in, w_conv, a_log, dt_bias, norm_out_a, norm_out_b, w_out, norm_ffn,
                      w_gate, w_up, w_down, 0)
    bp = x_prompt.shape[0]
    bs, ts, _ = x_sample.shape
    zero_conv = jnp.zeros((bp, CONV_W - 1, CONV_DIM), F32)
    zero_rec = jnp.zeros((bp, A_HEADS, A_DK, A_DV), F32)
    yp, pc, pr, pk, pv = _block(x_prompt, zero_conv, zero_rec, None, None, p, norm_final, 512)
    ys, sc, sr, sk, sv = _block(x_sample, state_conv[0], state_rec[0], cache_win_k[0], cache_win_v[0], p,
                                norm_final, bs * ts)
    return (yp, ys, pc[None], pr[None], pk[None], pv[None], sc[None], sr[None], sk[None], sv[None])
```

```python
import functools

import jax
import jax.numpy as jnp
from jax import lax
from jax.experimental import pallas as pl
from jax.experimental.pallas import tpu as pltpu

F32 = jnp.float32
BF16 = jnp.bfloat16

D_MODEL = 1024
A_HEADS = 4
A_DK = 128
A_DV = 128
A_WIDTH = A_HEADS * A_DV
CONV_W = 4
CONV_DIM = 2 * A_HEADS * A_DK + A_HEADS * A_DV
B_HEADS = 8
B_HEAD_DIM = 64
B_WIDTH = B_HEADS * B_HEAD_DIM
DILATED = ((128, 1), (512, 4), (2048, 16))
STEPS = 128
MAX_WINDOW = 2048
EPS = 1e-6
NEG = -1e30
CHUNK = 64
GATE_PAD = 128
COL_Z = CONV_DIM
COL_G = CONV_DIM + A_WIDTH
COLS_A = COL_G + GATE_PAD
COLS_B = 3 * B_WIDTH
VMEM_LIMIT = 56 * 1024 * 1024


def _dot(a, b):
    return jnp.dot(a.astype(BF16), b.astype(BF16), preferred_element_type=F32)


def _dot_nt(a, b):
    return lax.dot_general(a.astype(BF16), b.astype(BF16), (((1,), (1,)), ((), ())),
                           preferred_element_type=F32)


def _dot_tn(a, b):
    return lax.dot_general(a.astype(BF16), b.astype(BF16), (((0,), (0,)), ((), ())),
                           preferred_element_type=F32)


def _split2(a):
    hi = a.astype(BF16)
    lo = (a - hi.astype(F32)).astype(BF16)
    return hi, lo


def _dot3(a, b):
    ah, al = _split2(a)
    bh, bl = _split2(b)
    d = functools.partial(jnp.dot, preferred_element_type=F32)
    return d(ah, bh) + d(ah, bl) + d(al, bh)


def _sigmoid(x):
    return 1.0 / (1.0 + jnp.exp(-x))


def _silu(x):
    return x * _sigmoid(x)


def _softplus(x):
    return jnp.maximum(x, 0.0) + jnp.log(1.0 + jnp.exp(-jnp.abs(x)))


def _rms(x, w):
    return x * lax.rsqrt(jnp.mean(x * x, axis=-1, keepdims=True) + EPS) * w


def _norm_proj_kernel(x_ref, nw_ref, w_ref, oa_ref, ob_ref):
    h = _rms(x_ref[...], nw_ref[...]).astype(BF16)
    oa_ref[...] = jnp.dot(h, w_ref[:, :COLS_A], preferred_element_type=F32)
    ob_ref[...] = jnp.dot(h, w_ref[:, COLS_A:], preferred_element_type=F32)


def _norm_proj(x2d, norm_w, w_p, tm):
    t = x2d.shape[0]
    once = pl.Buffered(1)
    return pl.pallas_call(
        _norm_proj_kernel,
        grid=(t // tm,),
        in_specs=[pl.BlockSpec((tm, D_MODEL), lambda i: (i, 0)),
                  pl.BlockSpec((1, D_MODEL), lambda i: (0, 0), pipeline_mode=once),
                  pl.BlockSpec((D_MODEL, COLS_A + COLS_B), lambda i: (0, 0), pipeline_mode=once)],
        out_specs=[pl.BlockSpec((tm, COLS_A), lambda i: (i, 0)),
                   pl.BlockSpec((tm, COLS_B), lambda i: (i, 0))],
        out_shape=[jax.ShapeDtypeStruct((t, COLS_A), F32),
                   jax.ShapeDtypeStruct((t, COLS_B), F32)],
        compiler_params=pltpu.CompilerParams(dimension_semantics=("arbitrary",),
                                             vmem_limit_bytes=VMEM_LIMIT),
        name="norm_proj",
    )(x2d, norm_w, w_p)


def _causal_conv_silu(x_all, wconv_ref, n):
    conv = x_all[8:8 + n] * wconv_ref[CONV_W - 1:CONV_W, :]
    for back in range(1, CONV_W):
        tap = CONV_W - 1 - back
        conv = conv + pltpu.roll(x_all, back, axis=0)[8:8 + n] * wconv_ref[tap:tap + 1, :]
    return _silu(conv)


def _l2norm(x):
    return x * lax.rsqrt(jnp.sum(x * x, axis=-1, keepdims=True) + EPS)


def _norm_proj_conv_kernel(x_ref, nw_ref, w_ref, wconv_ref, conv0_ref, oa_ref, ob_ref, tail_ref, carry_ref):
    i = pl.program_id(1)
    tm = x_ref.shape[0]

    @pl.when(i == 0)
    def _():
        carry_ref[...] = conv0_ref[...]

    h = _rms(x_ref[...], nw_ref[...]).astype(BF16)
    raw = jnp.dot(h, w_ref[:, :CONV_DIM], preferred_element_type=F32)
    oa_ref[:, CONV_DIM:] = jnp.dot(h, w_ref[:, CONV_DIM:COLS_A], preferred_element_type=F32)
    ob_ref[...] = jnp.dot(h, w_ref[:, COLS_A:], preferred_element_type=F32)
    conv = _causal_conv_silu(jnp.concatenate([carry_ref[...], raw], axis=0), wconv_ref, tm)
    carry_ref[...] = raw[tm - 8:tm]
    for hd in range(A_HEADS):
        qs = slice(hd * A_DK, (hd + 1) * A_DK)
        ks = slice(A_HEADS * A_DK + hd * A_DK, A_HEADS * A_DK + (hd + 1) * A_DK)
        oa_ref[:, qs] = _l2norm(conv[:, qs]) * (A_DK ** -0.5)
        oa_ref[:, ks] = _l2norm(conv[:, ks])
    oa_ref[:, 2 * A_HEADS * A_DK:CONV_DIM] = conv[:, 2 * A_HEADS * A_DK:]

    @pl.when(i == pl.num_programs(1) - 1)
    def _():
        tail_ref[...] = raw[tm - 8:tm]


def _norm_proj_conv(x, norm_w, w_p, w_conv, conv0, tm):
    b, t, _ = x.shape
    assert t % tm == 0 and tm % 8 == 0
    once = pl.Buffered(1)
    return pl.pallas_call(
        _norm_proj_conv_kernel,
        grid=(b, t // tm),
        in_specs=[pl.BlockSpec((None, tm, D_MODEL), lambda i, j: (i, j, 0)),
                  pl.BlockSpec((1, D_MODEL), lambda i, j: (0, 0), pipeline_mode=once),
                  pl.BlockSpec((D_MODEL, COLS_A + COLS_B), lambda i, j: (0, 0), pipeline_mode=once),
                  pl.BlockSpec((CONV_W, CONV_DIM), lambda i, j: (0, 0), pipeline_mode=once),
                  pl.BlockSpec((None, 8, CONV_DIM), lambda i, j: (i, 0, 0))],
        out_specs=[pl.BlockSpec((None, tm, COLS_A), lambda i, j: (i, j, 0)),
                   pl.BlockSpec((None, tm, COLS_B), lambda i, j: (i, j, 0)),
                   pl.BlockSpec((None, 8, CONV_DIM), lambda i, j: (i, 0, 0))],
        out_shape=[jax.ShapeDtypeStruct((b, t, COLS_A), F32),
                   jax.ShapeDtypeStruct((b, t, COLS_B), F32),
                   jax.ShapeDtypeStruct((b, 8, CONV_DIM), F32)],
        scratch_shapes=[pltpu.VMEM((8, CONV_DIM), F32)],
        compiler_params=pltpu.CompilerParams(dimension_semantics=("arbitrary", "arbitrary"),
                                             vmem_limit_bytes=VMEM_LIMIT),
        name="norm_proj_conv",
    )(x, norm_w, w_p, w_conv, conv0)


def _gdn_kernel(xa_ref, z_ref, gt_ref, conv0_ref, s0_ref, wconv_ref, gp_ref, nwa_ref,
                o_ref, snew_ref, ext_ref, s_ref, *, rows):
    n = pl.program_id(1)
    c = CHUNK
    n_chunks = -(-rows // c)
    rp = n_chunks * c

    @pl.when(n == 0)
    def _():
        ext_ref[0:8, :] = conv0_ref[...]
        s_ref[...] = s0_ref[...]

    ext_ref[8:8 + rows, :] = xa_ref[...]
    if rows < rp:
        ext_ref[8 + rows:8 + rp, :] = jnp.zeros((rp - rows, CONV_DIM), F32)

    conv = ext_ref[5:5 + rp, :] * wconv_ref[0:1, :]
    for i in range(1, CONV_W):
        conv = conv + ext_ref[5 + i:5 + i + rp, :] * wconv_ref[i:i + 1, :]
    conv = _silu(conv)
    ext_ref[0:8, :] = ext_ref[rp:rp + 8, :]

    row = lax.broadcasted_iota(jnp.int32, (rp, 1), 0)
    live = row < rows
    if rows < rp:
        conv = jnp.where(live, conv, 0.0)

    gt = gt_ref[...]
    if rows < rp:
        gt = jnp.concatenate([gt, jnp.zeros((rp - rows, GATE_PAD), F32)], axis=0)
    beta_all = _sigmoid(gt)
    g_all = -jnp.exp(gp_ref[0:1, :]) * _softplus(gt + gp_ref[1:2, :])
    lane = lax.broadcasted_iota(jnp.int32, (rp, GATE_PAD), 1)
    g_all = jnp.where((lane >= A_HEADS) & (lane < 2 * A_HEADS) & live, g_all, 0.0)
    beta_all = jnp.where(live, beta_all, 0.0)

    ii = lax.broadcasted_iota(jnp.int32, (c, c), 0)
    jj = lax.broadcasted_iota(jnp.int32, (c, c), 1)
    incl = ii >= jj
    strict = ii > jj
    tril = jnp.where(incl, 1.0, 0.0).astype(BF16)
    eye = jnp.where(ii == jj, 1.0, 0.0).astype(F32)
    dd = functools.partial(jnp.dot, preferred_element_type=F32)

    def exact_tril_dot(x):
        hi, lo = _split2(x)
        lo2 = (x - hi.astype(F32) - lo.astype(F32)).astype(BF16)
        return dd(tril, hi) + dd(tril, lo) + dd(tril, lo2)

    z = z_ref[...]
    nwa = nwa_ref[...]
    chains = [(ci, h) for ci in range(n_chunks) for h in range(A_HEADS)]
    gc_alls = [exact_tril_dot(g_all[ci * c:(ci + 1) * c]) for ci in range(n_chunks)]
    qs, ks, kbs, rhss, gcs, gls = [], [], [], [], [], []
    for ci, h in chains:
        rs = slice(ci * c, (ci + 1) * c)
        q = conv[rs, h * A_DK:(h + 1) * A_DK]
        k = conv[rs, A_HEADS * A_DK + h * A_DK:A_HEADS * A_DK + (h + 1) * A_DK]
        v = conv[rs, 2 * A_HEADS * A_DK + h * A_DV:2 * A_HEADS * A_DK + (h + 1) * A_DV]
        q = q * lax.rsqrt(jnp.sum(q * q, axis=-1, keepdims=True) + EPS) * (A_DK ** -0.5)
        k = k * lax.rsqrt(jnp.sum(k * k, axis=-1, keepdims=True) + EPS)
        beta = beta_all[rs, h:h + 1]
        gc = gc_alls[ci][:, A_HEADS + h:A_HEADS + h + 1]
        kb = k * beta
        qs.append(q)
        ks.append(k)
        kbs.append(kb)
        rhss.append(jnp.concatenate([v * beta, kb * jnp.exp(gc)], axis=1))
        gcs.append(gc)
        gls.append(gc_alls[ci][c - 1:c, A_HEADS + h:A_HEADS + h + 1])
    e_mats = [exact_tril_dot(jnp.where(strict, g_all[ci * c:(ci + 1) * c, A_HEADS + h:A_HEADS + h + 1], 0.0))
              for ci, h in chains]
    decays = [jnp.where(incl, jnp.exp(jnp.where(incl, e, 0.0)), 0.0) for e in e_mats]
    kks = [_dot_nt(kb, k) for kb, k in zip(kbs, ks)]
    qks = [_dot_nt(q, k) for q, k in zip(qs, ks)]
    pws = [jnp.where(strict, a * d, 0.0) for a, d in zip(kks, decays)]
    qks = [jnp.where(incl, a * d, 0.0) for a, d in zip(qks, decays)]
    t_invs = [eye - lw for lw in pws]
    for _ in range(c.bit_length() - 2):
        pws = [_dot3(pw, pw) for pw in pws]
        t_invs = [ti + _dot3(ti, pw) for ti, pw in zip(t_invs, pws)]
    uws = [_dot(ti, rhs) for ti, rhs in zip(t_invs, rhss)]

    states = [s_ref[h] for h in range(A_HEADS)]
    for ci in range(n_chunks):
        idx = [ci * A_HEADS + h for h in range(A_HEADS)]
        ws = [_dot(uws[i][:, A_DV:], states[h]) for h, i in enumerate(idx)]
        os = [_dot(qs[i] * jnp.exp(gcs[i]), states[h]) for h, i in enumerate(idx)]
        es = [uws[i][:, :A_DV] - w for i, w in zip(idx, ws)]
        os = [o + _dot(qks[i], e) for o, i, e in zip(os, idx, es)]
        upd = [_dot_tn(ks[i] * jnp.exp(gls[i] - gcs[i]), e) for i, e in zip(idx, es)]
        states = [s * jnp.exp(gls[i]) + d for s, i, d in zip(states, idx, upd)]
        r0 = ci * c
        r1 = min(rows, r0 + c)
        for h, o in enumerate(os):
            o = o if r1 - r0 == c else o[:r1 - r0]
            o_ref[r0:r1, h * A_DV:(h + 1) * A_DV] = (
                _rms(o, nwa) * _silu(z[r0:r1, h * A_DV:(h + 1) * A_DV]))
    for h in range(A_HEADS):
        s_ref[h] = states[h]

    @pl.when(n == pl.num_programs(1) - 1)
    def _():
        snew_ref[...] = s_ref[...]


GDN_ROWS = 8 * CHUNK


def _gdn(proj, conv0, s0, w_conv, gate_par, norm_a):
    b, t, _ = proj.shape
    rows = min(GDN_ROWS, t)
    assert t % rows == 0 and (rows % CHUNK == 0 or rows == t)
    n = t // rows
    kernel = functools.partial(_gdn_kernel, rows=rows)
    return pl.pallas_call(
        kernel,
        grid=(b, n),
        in_specs=[pl.BlockSpec((None, rows, CONV_DIM), lambda i, j: (i, j, 0)),
                  pl.BlockSpec((None, rows, A_WIDTH), lambda i, j: (i, j, COL_Z // A_WIDTH)),
                  pl.BlockSpec((None, rows, GATE_PAD), lambda i, j: (i, j, COL_G // GATE_PAD)),
                  pl.BlockSpec((None, 8, CONV_DIM), lambda i, j: (i, 0, 0)),
                  pl.BlockSpec((None, A_HEADS, A_DK, A_DV), lambda i, j: (i, 0, 0, 0)),
                  pl.BlockSpec((CONV_W, CONV_DIM), lambda i, j: (0, 0)),
                  pl.BlockSpec((8, GATE_PAD), lambda i, j: (0, 0)),
                  pl.BlockSpec((1, A_DV), lambda i, j: (0, 0))],
        out_specs=[pl.BlockSpec((None, rows, A_WIDTH), lambda i, j: (i, j, 0)),
                   pl.BlockSpec((None, A_HEADS, A_DK, A_DV), lambda i, j: (i, 0, 0, 0))],
        out_shape=[jax.ShapeDtypeStruct((b, t, A_WIDTH), F32),
                   jax.ShapeDtypeStruct((b, A_HEADS, A_DK, A_DV), F32)],
        scratch_shapes=[pltpu.VMEM((8 + -(-rows // CHUNK) * CHUNK, CONV_DIM), F32),
                        pltpu.VMEM((A_HEADS, A_DK, A_DV), F32)],
        compiler_params=pltpu.CompilerParams(dimension_semantics=("arbitrary", "arbitrary"),
                                             vmem_limit_bytes=VMEM_LIMIT),
        name="gdn",
    )(proj, proj, proj, conv0, s0, w_conv, gate_par, norm_a)


HALF = 128 // CHUNK


def _gdn_pair_kernel(xa_ref, z_ref, gt_ref, conv0_ref, s0_ref, wconv_ref, gp_ref, nwa_ref,
                     o_ref, snew_ref, ext_ref, s_ref, *, rows, nb, conv_done):
    n = pl.program_id(1)
    c = CHUNK
    n_chunks = -(-rows // c)
    rp = n_chunks * c
    assert A_HEADS % 2 == 0 and HALF == 2 and A_DK == 128 and A_DV == 128

    @pl.when(n == 0)
    def _():
        ext_ref[:, 0:8, :] = conv0_ref[...]
        s_ref[...] = s0_ref[...]

    row = lax.broadcasted_iota(jnp.int32, (rp, 1), 0)
    live = row < rows
    lane_g = lax.broadcasted_iota(jnp.int32, (rp, GATE_PAD), 1)
    gate_lanes = (lane_g >= A_HEADS) & (lane_g < 2 * A_HEADS) & live

    ii = lax.broadcasted_iota(jnp.int32, (c, 128), 0)
    ll = lax.broadcasted_iota(jnp.int32, (c, 128), 1)
    jj = ll & (c - 1)
    low = ll < c
    incl = ii >= jj
    strict = ii > jj
    eye = jnp.where(ii == jj, 1.0, 0.0).astype(F32)
    ti = lax.broadcasted_iota(jnp.int32, (c, c), 0)
    tj = lax.broadcasted_iota(jnp.int32, (c, c), 1)
    tril = jnp.where(ti >= tj, 1.0, 0.0).astype(BF16)
    dd = functools.partial(jnp.dot, preferred_element_type=F32)

    def exact_tril_dot(x):
        hi, lo = _split2(x)
        lo2 = (x - hi.astype(F32) - lo.astype(F32)).astype(BF16)
        return dd(tril, hi) + dd(tril, lo) + dd(tril, lo2)

    def bd(x):
        return jnp.concatenate([jnp.where(low, x, 0.0), jnp.where(low, 0.0, x)], axis=0).astype(BF16)

    def unstack(y):
        return jnp.where(low, y[:c], y[c:])

    nwa = nwa_ref[...]
    chains = []
    for bi in range(nb):
        if conv_done:
            assert rows == rp
            conv = xa_ref[bi]
        else:
            ext_ref[bi, 8:8 + rows, :] = xa_ref[bi]
            if rows < rp:
                ext_ref[bi, 8 + rows:8 + rp, :] = jnp.zeros((rp - rows, CONV_DIM), F32)
            conv = _causal_conv_silu(ext_ref[bi, 0:8 + rp, :], wconv_ref, rp)
            ext_ref[bi, 0:8, :] = ext_ref[bi, rp:rp + 8, :]
            if rows < rp:
                conv = jnp.where(live, conv, 0.0)

        gt = gt_ref[bi]
        if rows < rp:
            gt = jnp.concatenate([gt, jnp.zeros((rp - rows, GATE_PAD), F32)], axis=0)
        beta_all = jnp.where(live, _sigmoid(gt), 0.0)
        g_all = jnp.where(gate_lanes, -jnp.exp(gp_ref[0:1, :]) * _softplus(gt + gp_ref[1:2, :]), 0.0)

        for ci in range(n_chunks):
            rs = slice(ci * c, (ci + 1) * c)
            gc_all = exact_tril_dot(g_all[rs])
            for p in range(A_HEADS // 2):
                hd = []
                for h in (2 * p, 2 * p + 1):
                    q = conv[rs, h * A_DK:(h + 1) * A_DK]
                    k = conv[rs, A_HEADS * A_DK + h * A_DK:A_HEADS * A_DK + (h + 1) * A_DK]
                    v = conv[rs, 2 * A_HEADS * A_DK + h * A_DV:2 * A_HEADS * A_DK + (h + 1) * A_DV]
                    if not conv_done:
                        q = _l2norm(q) * (A_DK ** -0.5)
                        k = _l2norm(k)
                    beta = beta_all[rs, h:h + 1]
                    g = g_all[rs, A_HEADS + h:A_HEADS + h + 1]
                    gc = gc_all[:, A_HEADS + h:A_HEADS + h + 1]
                    gl = gc_all[c - 1:c, A_HEADS + h:A_HEADS + h + 1]
                    kb = k * beta
                    hd.append(dict(q=q, k=k, kb=kb, g=g, qg=q * jnp.exp(gc), kd=k * jnp.exp(gl - gc),
                                   dl=jnp.exp(gl),
                                   rhs=jnp.concatenate([v * beta, kb * jnp.exp(gc)], axis=1)))
                chains.append(dict(bi=bi, ci=ci, p=p, hd=hd))

    for ch in chains:
        a, b = ch["hd"]
        ch["e_mat"] = exact_tril_dot(jnp.where(strict, jnp.where(low, a["g"], b["g"]), 0.0))
    for ch in chains:
        a, b = ch["hd"]
        k_st = jnp.concatenate([a["k"], b["k"]], axis=0)
        ch["kk"] = _dot_nt(jnp.concatenate([a["kb"], b["kb"]], axis=0), k_st)
        ch["qk"] = _dot_nt(jnp.concatenate([a["q"], b["q"]], axis=0), k_st)
    for ch in chains:
        decay = jnp.where(incl, jnp.exp(jnp.where(incl, ch["e_mat"], 0.0)), 0.0)
        ch["lower"] = jnp.where(strict, unstack(ch["kk"]) * decay, 0.0)
        ch["qk"] = jnp.where(incl, unstack(ch["qk"]) * decay, 0.0)
        ch["x"] = ch["lower"]
        ch["xb"] = bd(ch["lower"])
        ch["t"] = eye - ch["lower"]
    for _ in range(c.bit_length() - 2):
        for ch in chains:
            ch["x"] = jnp.dot(ch["x"].astype(BF16), ch["xb"], preferred_element_type=F32)
        for ch in chains:
            ch["xb"] = bd(ch["x"])
            ch["t"] = ch["t"] + jnp.dot(ch["t"].astype(BF16), ch["xb"], preferred_element_type=F32)
    for ch in chains:
        lh, ll_ = _split2(ch["lower"])
        th = ch["t"].astype(BF16).astype(F32)
        tl = ch["t"] - th
        bth = bd(th)
        lt = dd(lh, bth) + dd(lh, bd(tl)) + dd(ll_, bth)
        ch["res"] = eye - ch["t"] - lt
    for ch in chains:
        ch["t"] = ch["t"] + jnp.dot(ch["t"].astype(BF16), bd(ch["res"]), preferred_element_type=F32)
    for ch in chains:
        a, b = ch["hd"]
        uw = jnp.dot(bd(ch["t"]), jnp.concatenate([a["rhs"], b["rhs"]], axis=0).astype(BF16),
                     preferred_element_type=F32)
        a["u"], a["w"] = uw[:c, :A_DV], uw[:c, A_DV:]
        b["u"], b["w"] = uw[c:, :A_DV], uw[c:, A_DV:]
        ch["qkb"] = bd(ch["qk"])

    by_key = {(ch["bi"], ch["ci"], ch["p"]): ch for ch in chains}
    states = {(bi, h): s_ref[bi, h] for bi in range(nb) for h in range(A_HEADS)}
    for ci in range(n_chunks):
        cur = [by_key[(bi, ci, p)] for bi in range(nb) for p in range(A_HEADS // 2)]
        for ch in cur:
            for hh, d in enumerate(ch["hd"]):
                ws = _dot(jnp.concatenate([d["w"], d["qg"]], axis=0), states[(ch["bi"], 2 * ch["p"] + hh)])
                d["e"] = d["u"] - ws[:c]
                d["o"] = ws[c:]
        for ch in cur:
            a, b = ch["hd"]
            o2 = jnp.dot(ch["qkb"], jnp.concatenate([a["e"], b["e"]], axis=0).astype(BF16),
                         preferred_element_type=F32)
            a["o"] = a["o"] + o2[:c]
            b["o"] = b["o"] + o2[c:]
        for ch in cur:
            for hh, d in enumerate(ch["hd"]):
                key = (ch["bi"], 2 * ch["p"] + hh)
                states[key] = states[key] * d["dl"] + _dot_tn(d["kd"], d["e"])
        r0 = ci * c
        r1 = min(rows, r0 + c)
        for ch in cur:
            for hh, d in enumerate(ch["hd"]):
                h = 2 * ch["p"] + hh
                o = d["o"] if r1 - r0 == c else d["o"][:r1 - r0]
                o_ref[ch["bi"], r0:r1, h * A_DV:(h + 1) * A_DV] = (
                    _rms(o, nwa) * _silu(z_ref[ch["bi"], r0:r1, h * A_DV:(h + 1) * A_DV]))
    for (bi, h), s in states.items():
        s_ref[bi, h] = s

    @pl.when(n == pl.num_programs(1) - 1)
    def _():
        snew_ref[...] = s_ref[...]


def _gdn_pair(proj, conv0, s0, w_conv, gate_par, norm_a, conv_done):
    b, t, _ = proj.shape
    rows = min(GDN_ROWS, t)
    assert t % rows == 0 and (rows % CHUNK == 0 or rows == t)
    n = t // rows
    nb = max(1, min(b, GDN_ROWS // (-(-rows // CHUNK) * CHUNK)))
    assert b % nb == 0
    rp = 0 if conv_done else -(-rows // CHUNK) * CHUNK
    kernel = functools.partial(_gdn_pair_kernel, rows=rows, nb=nb, conv_done=conv_done)
    return pl.pallas_call(
        kernel,
        grid=(b // nb, n),
        in_specs=[pl.BlockSpec((nb, rows, CONV_DIM), lambda i, j: (i, j, 0)),
                  pl.BlockSpec((nb, rows, A_WIDTH), lambda i, j: (i, j, COL_Z // A_WIDTH)),
                  pl.BlockSpec((nb, rows, GATE_PAD), lambda i, j: (i, j, COL_G // GATE_PAD)),
                  pl.BlockSpec((nb, 8, CONV_DIM), lambda i, j: (i, 0, 0)),
                  pl.BlockSpec((nb, A_HEADS, A_DK, A_DV), lambda i, j: (i, 0, 0, 0)),
                  pl.BlockSpec((CONV_W, CONV_DIM), lambda i, j: (0, 0)),
                  pl.BlockSpec((8, GATE_PAD), lambda i, j: (0, 0)),
                  pl.BlockSpec((1, A_DV), lambda i, j: (0, 0))],
        out_specs=[pl.BlockSpec((nb, rows, A_WIDTH), lambda i, j: (i, j, 0)),
                   pl.BlockSpec((nb, A_HEADS, A_DK, A_DV), lambda i, j: (i, 0, 0, 0))],
        out_shape=[jax.ShapeDtypeStruct((b, t, A_WIDTH), F32),
                   jax.ShapeDtypeStruct((b, A_HEADS, A_DK, A_DV), F32)],
        scratch_shapes=[pltpu.VMEM((nb, 8 + rp, CONV_DIM), F32),
                        pltpu.VMEM((nb, A_HEADS, A_DK, A_DV), F32)],
        compiler_params=pltpu.CompilerParams(dimension_semantics=("arbitrary", "arbitrary"),
                                             vmem_limit_bytes=VMEM_LIMIT),
        name="gdn",
    )(proj, proj, proj, conv0, s0, w_conv, gate_par, norm_a)


def _slope(h):
    return 2.0 ** (-8.0 * (h + 1) / B_HEADS)


ATTN_SB = MAX_WINDOW
ATTN_GROUP = 16
PAIR = 128 // B_HEAD_DIM


def _attn_fused_kernel(q_ref, kp_ref, kc_ref, vp_ref, vc_ref, nwb_ref, o_ref, kbuf, vbuf, *stats):
    hp = pl.program_id(1)
    n = pl.program_id(2)
    sb = ATTN_SB
    blk = STEPS
    kbuf[0:sb, :] = kp_ref[...]
    kbuf[sb:2 * sb, :] = kc_ref[...]
    vbuf[0:sb, :] = vp_ref[...]
    vbuf[sb:2 * sb, :] = vc_ref[...]

    lane_q = lax.broadcasted_iota(jnp.int32, (blk, 128), 1)
    low_q = lane_q < B_HEAD_DIM
    qq = lax.broadcasted_iota(jnp.int32, (2 * blk, 2 * blk), 0)
    kk = lax.broadcasted_iota(jnp.int32, (2 * blk, 2 * blk), 1)
    steps_back = (qq & (blk - 1)) + blk - kk
    in_band = (steps_back >= 0) & (steps_back <= STEPS)
    slope_lo = jnp.float32(_slope(0))
    slope_hi = jnp.float32(_slope(1))
    for i in range(1, B_HEADS // PAIR):
        slope_lo = jnp.where(hp == i, _slope(PAIR * i), slope_lo)
        slope_hi = jnp.where(hp == i, _slope(PAIR * i + 1), slope_hi)
    slope = jnp.where(qq < blk, slope_lo, slope_hi)

    for bi, (_, dil) in enumerate(DILATED):
        m_s, l_s, num_s = stats[3 * bi:3 * bi + 3]
        bias = jnp.where(in_band, -slope * (steps_back * dil).astype(F32), NEG)
        bias_first = jnp.where(kk >= blk, bias, NEG)
        d_bits = dil.bit_length() - 1

        def rows(start, size):
            if dil == 1:
                return pl.ds(pl.multiple_of(start, blk), size)
            return pl.ds(start, size, stride=dil)

        def group(g, carry):
            tiles = []
            for u in range(ATTN_GROUP):
                it = g * ATTN_GROUP + u
                q0 = lax.shift_right_logical(it, d_bits) * (blk * dil) + (it & (dil - 1))
                k0 = sb + q0 - blk * dil
                q = q_ref[rows(q0, blk), :] * (B_HEAD_DIM ** -0.5)
                q2 = jnp.concatenate([jnp.where(low_q, q, 0.0), jnp.where(low_q, 0.0, q)], axis=0)
                tiles.append((q0, k0, q2.astype(BF16), kbuf[rows(k0, 2 * blk), :].astype(BF16),
                              vbuf[rows(k0, 2 * blk), :].astype(BF16)))
            scores = [lax.dot_general(q2, k, (((1,), (1,)), ((), ())), preferred_element_type=F32)
                      for (_, _, q2, k, _) in tiles]
            probs = []
            for (q0, k0, _, _, _), s in zip(tiles, scores):
                s = s + jnp.where((n == 0) & (q0 < blk * dil), bias_first, bias)
                m = jnp.max(s, axis=-1, keepdims=True)
                p = jnp.exp(s - m)
                probs.append((m, jnp.sum(p, axis=-1, keepdims=True), p.astype(BF16)))
            for (q0, _, _, _, v), (m, l, p) in zip(tiles, probs):
                pv = jnp.dot(p, v, preferred_element_type=F32)
                m_s[rows(q0, blk), :] = jnp.where(low_q, m[:blk], m[blk:])
                l_s[rows(q0, blk), :] = jnp.where(low_q, l[:blk], l[blk:])
                num_s[rows(q0, blk), :] = jnp.where(low_q, pv[:blk], pv[blk:])
            return carry

        lax.fori_loop(0, sb // (blk * ATTN_GROUP), group, 0)

    nwb = nwb_ref[...]

    def combine(i, carry):
        rs = pl.ds(pl.multiple_of(i * blk, blk), blk)
        ms = [stats[3 * bi][rs, :] for bi in range(len(DILATED))]
        ls = [stats[3 * bi + 1][rs, :] for bi in range(len(DILATED))]
        nums = [stats[3 * bi + 2][rs, :] for bi in range(len(DILATED))]
        mx = jnp.maximum(jnp.maximum(ms[0], ms[1]), ms[2])
        ws = [jnp.exp(m - mx) for m in ms]
        den = ws[0] * ls[0] + ws[1] * ls[1] + ws[2] * ls[2]
        o = (ws[0] * nums[0] + ws[1] * nums[1] + ws[2] * nums[2]) / den
        sq = o * o
        ss_lo = jnp.sum(jnp.where(low_q, sq, 0.0), axis=-1, keepdims=True)
        ss_hi = jnp.sum(jnp.where(low_q, 0.0, sq), axis=-1, keepdims=True)
        mean_sq = jnp.where(low_q, ss_lo, ss_hi) / B_HEAD_DIM
        o_ref[rs, :] = o * lax.rsqrt(mean_sq + EPS) * nwb
        return carry

    lax.fori_loop(0, sb // blk, combine, 0, unroll=4)


def _attn_prompt(qkv, norm_b_pair):
    b, s, _ = qkv.shape
    sb = ATTN_SB
    assert s % sb == 0 and (sb // STEPS) % ATTN_GROUP == 0
    assert all(w == STEPS * d and sb % w == 0 and d & (d - 1) == 0 for (w, d) in DILATED)
    n_pairs = B_HEADS // PAIR
    blk = (None, sb, 128)
    prev = lambda c0: pl.BlockSpec(blk, lambda i, p, n: (i, jnp.maximum(n - 1, 0), c0 + p))
    cur = lambda c0: pl.BlockSpec(blk, lambda i, p, n: (i, n, c0 + p))
    return pl.pallas_call(
        _attn_fused_kernel,
        grid=(b, n_pairs, s // sb),
        in_specs=[cur(0), prev(n_pairs), cur(n_pairs), prev(2 * n_pairs), cur(2 * n_pairs),
                  pl.BlockSpec((1, 128), lambda i, p, n: (0, 0))],
        out_specs=pl.BlockSpec(blk, lambda i, p, n: (i, n, p)),
        out_shape=jax.ShapeDtypeStruct((b, s, B_WIDTH), F32),
        scratch_shapes=[pltpu.VMEM((2 * sb, 128), F32), pltpu.VMEM((2 * sb, 128), F32)]
                       + [pltpu.VMEM((sb, 128), F32)] * (3 * len(DILATED)),
        compiler_params=pltpu.CompilerParams(
            dimension_semantics=("arbitrary", "arbitrary", "arbitrary"), vmem_limit_bytes=VMEM_LIMIT),
        name="attn_prompt",
    )(qkv, qkv, qkv, qkv, qkv, norm_b_pair)


def _attn_sample_kernel(q_ref, kn_ref, vn_ref, kt_ref, vt_ref, nwb_ref, o_ref, kto_ref, vto_ref, *, n_past, t):
    kn = kn_ref[...]
    vn = vn_ref[...]
    kt = kt_ref[...]
    vt = vt_ref[...]

    prow = lax.broadcasted_iota(jnp.int32, (t, 128), 0)
    plane = lax.broadcasted_iota(jnp.int32, (t, 128), 1)
    place = jnp.where(plane == 128 - t + prow, 1.0, 0.0).astype(BF16)
    tail_lanes = lax.broadcasted_iota(jnp.int32, (B_WIDTH, 128), 1) >= 128 - t
    tn = functools.partial(lax.dot_general, dimension_numbers=(((0,), (0,)), ((), ())),
                           preferred_element_type=F32)

    def shifted(win, new, out_ref):
        hi, lo = _split2(new)
        lo2 = (new - hi.astype(F32) - lo.astype(F32)).astype(BF16)
        new_t = tn(hi, place) + tn(lo, place) + tn(lo2, place)
        sh = pltpu.roll(win, n_past - t, axis=1)
        out_ref[:, :n_past - 128] = sh[:, :n_past - 128]
        out_ref[:, n_past - 128:] = jnp.where(tail_lanes, new_t, sh[:, n_past - 128:])

    shifted(kt, kn, kto_ref)
    shifted(vt, vn, vto_ref)

    rq = B_HEADS * t
    t_bits = t.bit_length() - 1
    d_bits = B_HEAD_DIM.bit_length() - 1
    rowi = lax.broadcasted_iota(jnp.int32, (rq, B_WIDTH), 0)
    lanei = lax.broadcasted_iota(jnp.int32, (rq, B_WIDTH), 1)
    own = lax.shift_right_logical(rowi, t_bits) == lax.shift_right_logical(lanei, d_bits)
    q_rep = jnp.concatenate([q_ref[...]] * B_HEADS, axis=0)
    q_blk = jnp.where(own, q_rep, 0.0).astype(BF16)
    zpad = jnp.zeros((128 - t, B_WIDTH), BF16)
    kn_pad = jnp.concatenate([kn.astype(BF16), zpad], axis=0)
    vn_pad = jnp.concatenate([vn.astype(BF16), zpad], axis=0)
    vt16 = vt.astype(BF16)
    scale = B_HEAD_DIM ** -0.5
    s_c = jnp.dot(q_blk, kt.astype(BF16), preferred_element_type=F32) * scale
    s_n = lax.dot_general(q_blk, kn_pad, (((1,), (1,)), ((), ())), preferred_element_type=F32) * scale

    r1 = lax.broadcasted_iota(jnp.int32, (rq, 1), 0)
    head = lax.shift_right_logical(r1, t_bits)
    slope = jnp.zeros((rq, 1), F32)
    for h in range(B_HEADS):
        slope = jnp.where(head == h, _slope(h), slope)
    tok = r1 & (t - 1)
    delta_c = n_past + tok - lax.broadcasted_iota(jnp.int32, (rq, n_past), 1)
    key_n = lax.broadcasted_iota(jnp.int32, (rq, 128), 1)
    delta_n = tok - key_n
    s_c = s_c - slope * delta_c.astype(F32)
    s_n = s_n - slope * delta_n.astype(F32)
    live_n = (key_n < t) & (delta_n >= 0)

    ms, ls, nums = [], [], []
    for (_, dil) in DILATED:
        valid_c = ((delta_c & (dil - 1)) == 0) & (delta_c <= STEPS * dil)
        valid_n = live_n & ((delta_n & (dil - 1)) == 0) & (delta_n <= STEPS * dil)
        sd_c = jnp.where(valid_c, s_c, NEG)
        sd_n = jnp.where(valid_n, s_n, NEG)
        m = jnp.maximum(jnp.max(sd_c, axis=-1, keepdims=True), jnp.max(sd_n, axis=-1, keepdims=True))
        p_c = jnp.exp(sd_c - m)
        p_n = jnp.exp(sd_n - m)
        ms.append(m)
        ls.append(jnp.sum(p_c, axis=-1, keepdims=True) + jnp.sum(p_n, axis=-1, keepdims=True))
        nums.append(lax.dot_general(p_c.astype(BF16), vt16, (((1,), (1,)), ((), ())), preferred_element_type=F32)
                    + jnp.dot(p_n.astype(BF16), vn_pad, preferred_element_type=F32))
    mx = jnp.maximum(jnp.maximum(ms[0], ms[1]), ms[2])
    ws = [jnp.exp(m - mx) for m in ms]
    den = ws[0] * ls[0] + ws[1] * ls[1] + ws[2] * ls[2]
    o = (ws[0] * nums[0] + ws[1] * nums[1] + ws[2] * nums[2]) / den
    o = jnp.where(own, o, 0.0)
    o = o * lax.rsqrt(jnp.sum(o * o, axis=-1, keepdims=True) / B_HEAD_DIM + EPS) * nwb_ref[...]
    acc = o[0:t, :]
    for h in range(1, B_HEADS):
        acc = acc + o[h * t:(h + 1) * t, :]
    o_ref[...] = acc


def _attn_sample(qkv, win_kt, win_vt, norm_b_tiled):
    b, t, _ = qkv.shape
    n_past = win_kt.shape[2]
    assert n_past == MAX_WINDOW and n_past % 128 == 0 and t % 8 == 0 and t & (t - 1) == 0 and t <= 128
    assert all(d & (d - 1) == 0 for (_, d) in DILATED)
    kernel = functools.partial(_attn_sample_kernel, n_past=n_past, t=t)
    new = pl.BlockSpec((None, t, B_WIDTH), lambda i: (i, 0, 0))
    win = pl.BlockSpec((None, B_WIDTH, n_past), lambda i: (i, 0, 0))
    return pl.pallas_call(
        kernel,
        grid=(b,),
        in_specs=[pl.BlockSpec((None, t, B_WIDTH), lambda i: (i, 0, 0)),
                  pl.BlockSpec((None, t, B_WIDTH), lambda i: (i, 0, 1)),
                  pl.BlockSpec((None, t, B_WIDTH), lambda i: (i, 0, 2)),
                  win, win, pl.BlockSpec((1, B_WIDTH), lambda i: (0, 0))],
        out_specs=[new, win, win],
        out_shape=[jax.ShapeDtypeStruct((b, t, B_WIDTH), F32),
                   jax.ShapeDtypeStruct((b, B_WIDTH, n_past), F32),
                   jax.ShapeDtypeStruct((b, B_WIDTH, n_past), F32)],
        compiler_params=pltpu.CompilerParams(dimension_semantics=("arbitrary",),
                                             vmem_limit_bytes=VMEM_LIMIT),
        name="attn_sample",
    )(qkv, qkv, qkv, win_kt, win_vt, norm_b_tiled)


def _out_ffn_kernel(x_ref, oa_ref, ob_ref, wo_ref, nf_ref, wg_ref, wu_ref, wd_ref, nfin_ref, y_ref, *, ff_chunk):
    mixed = jnp.concatenate([oa_ref[...], ob_ref[...]], axis=-1).astype(BF16)
    x1 = x_ref[...] + jnp.dot(mixed, wo_ref[...], preferred_element_type=F32)
    hf = _rms(x1, nf_ref[...]).astype(BF16)
    x2 = x1
    for c0 in range(0, wg_ref.shape[1], ff_chunk):
        g = jnp.dot(hf, wg_ref[:, c0:c0 + ff_chunk], preferred_element_type=F32)
        u = jnp.dot(hf, wu_ref[:, c0:c0 + ff_chunk], preferred_element_type=F32)
        act = (_silu(g) * u).astype(BF16)
        x2 = x2 + jnp.dot(act, wd_ref[c0:c0 + ff_chunk, :], preferred_element_type=F32)
    y_ref[...] = _rms(x2, nfin_ref[...])


def _out_ffn(x2d, o_a, o_b, w_out, norm_ffn, w_gate, w_up, w_down, norm_final, tm):
    t = x2d.shape[0]
    d_ff = w_gate.shape[1]
    ff_chunk = d_ff // 2 if d_ff % 256 == 0 else d_ff
    once = pl.Buffered(1)
    row = lambda w: pl.BlockSpec((tm, w), lambda i: (i, 0))
    full = lambda a, b: pl.BlockSpec((a, b), lambda i: (0, 0), pipeline_mode=once)
    return pl.pallas_call(
        functools.partial(_out_ffn_kernel, ff_chunk=ff_chunk),
        grid=(t // tm,),
        in_specs=[row(D_MODEL), row(A_WIDTH), row(B_WIDTH),
                  full(D_MODEL, D_MODEL), full(1, D_MODEL),
                  full(D_MODEL, d_ff), full(D_MODEL, d_ff), full(d_ff, D_MODEL), full(1, D_MODEL)],
        out_specs=row(D_MODEL),
        out_shape=jax.ShapeDtypeStruct((t, D_MODEL), F32),
        compiler_params=pltpu.CompilerParams(dimension_semantics=("arbitrary",),
                                             vmem_limit_bytes=VMEM_LIMIT),
        name="out_ffn",
    )(x2d, o_a, o_b, w_out, norm_ffn, w_gate, w_up, w_down, norm_final)


def _layer_params(norm_mix, w_in, w_conv, a_log, dt_bias, norm_out_a, norm_out_b, w_out, norm_ffn,
                  w_gate, w_up, w_down, layer):
    w = w_in[layer]
    n_gate = 2 * A_HEADS
    w_p = jnp.concatenate([w[:, :COL_G + n_gate], jnp.zeros((D_MODEL, GATE_PAD - n_gate), w.dtype),
                           w[:, COL_G + n_gate:]], axis=1).astype(BF16)
    gate_par = jnp.zeros((8, GATE_PAD), F32)
    gate_par = gate_par.at[0, A_HEADS:2 * A_HEADS].set(a_log[layer].astype(F32))
    gate_par = gate_par.at[1, A_HEADS:2 * A_HEADS].set(dt_bias[layer].astype(F32))
    return dict(
        norm_mix=norm_mix[layer].reshape(1, D_MODEL), w_p=w_p, w_conv=w_conv[layer], gate_par=gate_par,
        norm_a=norm_out_a[layer].reshape(1, A_DV),
        norm_b_pair=jnp.tile(norm_out_b[layer], PAIR).reshape(1, PAIR * B_HEAD_DIM),
        norm_b_tiled=jnp.tile(norm_out_b[layer], B_HEADS).reshape(1, B_WIDTH),
        w_out=w_out[layer].astype(BF16), norm_ffn=norm_ffn[layer].reshape(1, D_MODEL),
        w_gate=w_gate[layer].astype(BF16), w_up=w_up[layer].astype(BF16), w_down=w_down[layer].astype(BF16))


def _block(x, conv_buf, s0, win_k, win_v, p, norm_final, tm):
    b, t, _ = x.shape
    assert t >= CONV_W - 1
    x2d = x.reshape(b * t, D_MODEL)
    conv0 = jnp.pad(conv_buf, ((0, 0), (8 - (CONV_W - 1), 0), (0, 0)))
    conv_in_proj = t % tm == 0
    if conv_in_proj:
        proj_a, proj_b, tail = _norm_proj_conv(x, p["norm_mix"], p["w_p"], p["w_conv"], conv0, tm)
        new_conv = tail[:, 8 - (CONV_W - 1):]
    else:
        proj_a, proj_b = _norm_proj(x2d, p["norm_mix"], p["w_p"], tm)
        proj_a = proj_a.reshape(b, t, COLS_A)
        proj_b = proj_b.reshape(b, t, COLS_B)
        new_conv = proj_a[:, t - (CONV_W - 1):, :CONV_DIM]
    o_a, s_new = _gdn_pair(proj_a, conv0, s0, p["w_conv"], p["gate_par"], p["norm_a"], conv_in_proj)
    if win_k is None:
        o_b = _attn_prompt(proj_b, p["norm_b_pair"]).reshape(b * t, B_WIDTH)
        keep = min(MAX_WINDOW, t)
        shp = (b, keep, B_HEADS, B_HEAD_DIM)
        new_k = proj_b[:, t - keep:, B_WIDTH:2 * B_WIDTH].reshape(shp)
        new_v = proj_b[:, t - keep:, 2 * B_WIDTH:].reshape(shp)
    else:
        n_past = win_k.shape[1]
        to_t = lambda a: jnp.transpose(a, (0, 2, 3, 1)).reshape(b, B_WIDTH, n_past)
        o_b, new_kt, new_vt = _attn_sample(proj_b, to_t(win_k), to_t(win_v), p["norm_b_tiled"])
        from_t = lambda a: jnp.transpose(a.reshape(b, B_HEADS, B_HEAD_DIM, a.shape[2]), (0, 3, 1, 2))
        new_k, new_v = from_t(new_kt), from_t(new_vt)
        o_b = o_b.reshape(b * t, B_WIDTH)
    y = _out_ffn(x2d, o_a.reshape(b * t, A_WIDTH), o_b, p["w_out"], p["norm_ffn"], p["w_gate"], p["w_up"],
                 p["w_down"], norm_final.reshape(1, D_MODEL), tm)
    return y.reshape(b, t, D_MODEL), new_conv, s_new, new_k, new_v


def kernel(x_prompt, x_sample, state_conv, state_rec, cache_win_k, cache_win_v, norm_mix, w_in, w_conv, a_log,
           dt_bias, norm_out_a, norm_out_b, w_out, norm_ffn, w_gate, w_up, w_down, norm_final):
    depth = w_in.shape[0]
    assert depth == 1, "the final norm is fused into the block of a single-layer trunk"
    p = _layer_params(norm_mix, w_in, w_conv, a_log, dt_bias, norm_out_a, norm_out_b, w_out, norm_ffn,
                      w_gate, w_up, w_down, 0)
    bp = x_prompt.shape[0]
    bs, ts, _ = x_sample.shape
    zero_conv = jnp.zeros((bp, CONV_W - 1, CONV_DIM), F32)
    zero_rec = jnp.zeros((bp, A_HEADS, A_DK, A_DV), F32)
    yp, pc, pr, pk, pv = _block(x_prompt, zero_conv, zero_rec, None, None, p, norm_final, 512)
    ys, sc, sr, sk, sv = _block(x_sample, state_conv[0], state_rec[0], cache_win_k[0], cache_win_v[0], p,
                                norm_final, bs * ts)
    return (yp, ys, pc[None], pr[None], pk[None], pv[None], sc[None], sr[None], sk[None], sv[None])
```

```python
import functools

import jax
import jax.numpy as jnp
from jax import lax
from jax.experimental import pallas as pl
from jax.experimental.pallas import tpu as pltpu

F32 = jnp.float32
BF16 = jnp.bfloat16

D_MODEL = 1024
A_HEADS = 4
A_DK = 128
A_DV = 128
A_WIDTH = A_HEADS * A_DV
CONV_W = 4
CONV_DIM = 2 * A_HEADS * A_DK + A_HEADS * A_DV
B_HEADS = 8
B_HEAD_DIM = 64
B_WIDTH = B_HEADS * B_HEAD_DIM
DILATED = ((128, 1), (512, 4), (2048, 16))
STEPS = 128
MAX_WINDOW = 2048
EPS = 1e-6
NEG = -1e30
CHUNK = 64
GATE_PAD = 128
COL_Z = CONV_DIM
COL_G = CONV_DIM + A_WIDTH
COLS_A = COL_G + GATE_PAD
COLS_B = 3 * B_WIDTH
VMEM_LIMIT = 56 * 1024 * 1024


def _dot(a, b):
    return jnp.dot(a.astype(BF16), b.astype(BF16), preferred_element_type=F32)


def _dot_nt(a, b):
    return lax.dot_general(a.astype(BF16), b.astype(BF16), (((1,), (1,)), ((), ())),
                           preferred_element_type=F32)


def _dot_tn(a, b):
    return lax.dot_general(a.astype(BF16), b.astype(BF16), (((0,), (0,)), ((), ())),
                           preferred_element_type=F32)


def _split2(a):
    hi = a.astype(BF16)
    lo = (a - hi.astype(F32)).astype(BF16)
    return hi, lo


def _dot3(a, b):
    ah, al = _split2(a)
    bh, bl = _split2(b)
    d = functools.partial(jnp.dot, preferred_element_type=F32)
    return d(ah, bh) + d(ah, bl) + d(al, bh)


def _sigmoid(x):
    return 1.0 / (1.0 + jnp.exp(-x))


def _silu(x):
    return x * _sigmoid(x)


def _softplus(x):
    return jnp.maximum(x, 0.0) + jnp.log(1.0 + jnp.exp(-jnp.abs(x)))


def _rms(x, w):
    return x * lax.rsqrt(jnp.mean(x * x, axis=-1, keepdims=True) + EPS) * w


def _norm_proj_kernel(x_ref, nw_ref, w_ref, oa_ref, ob_ref):
    h = _rms(x_ref[...], nw_ref[...]).astype(BF16)
    oa_ref[...] = jnp.dot(h, w_ref[:, :COLS_A], preferred_element_type=F32)
    ob_ref[...] = jnp.dot(h, w_ref[:, COLS_A:], preferred_element_type=F32)


def _norm_proj(x2d, norm_w, w_p, tm):
    t = x2d.shape[0]
    once = pl.Buffered(1)
    return pl.pallas_call(
        _norm_proj_kernel,
        grid=(t // tm,),
        in_specs=[pl.BlockSpec((tm, D_MODEL), lambda i: (i, 0)),
                  pl.BlockSpec((1, D_MODEL), lambda i: (0, 0), pipeline_mode=once),
                  pl.BlockSpec((D_MODEL, COLS_A + COLS_B), lambda i: (0, 0), pipeline_mode=once)],
        out_specs=[pl.BlockSpec((tm, COLS_A), lambda i: (i, 0)),
                   pl.BlockSpec((tm, COLS_B), lambda i: (i, 0))],
        out_shape=[jax.ShapeDtypeStruct((t, COLS_A), F32),
                   jax.ShapeDtypeStruct((t, COLS_B), F32)],
        compiler_params=pltpu.CompilerParams(dimension_semantics=("arbitrary",),
                                             vmem_limit_bytes=VMEM_LIMIT),
        name="norm_proj",
    )(x2d, norm_w, w_p)


def _causal_conv_silu(x_all, wconv_ref, n):
    conv = x_all[8:8 + n] * wconv_ref[CONV_W - 1:CONV_W, :]
    for back in range(1, CONV_W):
        tap = CONV_W - 1 - back
        conv = conv + pltpu.roll(x_all, back, axis=0)[8:8 + n] * wconv_ref[tap:tap + 1, :]
    return _silu(conv)


def _l2norm(x):
    return x * lax.rsqrt(jnp.sum(x * x, axis=-1, keepdims=True) + EPS)


def _norm_proj_conv_kernel(x_ref, nw_ref, w_ref, wconv_ref, conv0_ref, oa_ref, ob_ref, tail_ref, carry_ref):
    i = pl.program_id(1)
    tm = x_ref.shape[0]

    @pl.when(i == 0)
    def _():
        carry_ref[...] = conv0_ref[...]

    h = _rms(x_ref[...], nw_ref[...]).astype(BF16)
    raw = jnp.dot(h, w_ref[:, :CONV_DIM], preferred_element_type=F32)
    oa_ref[:, CONV_DIM:] = jnp.dot(h, w_ref[:, CONV_DIM:COLS_A], preferred_element_type=F32)
    ob_ref[...] = jnp.dot(h, w_ref[:, COLS_A:], preferred_element_type=F32)
    conv = _causal_conv_silu(jnp.concatenate([carry_ref[...], raw], axis=0), wconv_ref, tm)
    carry_ref[...] = raw[tm - 8:tm]
    for hd in range(A_HEADS):
        qs = slice(hd * A_DK, (hd + 1) * A_DK)
        ks = slice(A_HEADS * A_DK + hd * A_DK, A_HEADS * A_DK + (hd + 1) * A_DK)
        oa_ref[:, qs] = _l2norm(conv[:, qs]) * (A_DK ** -0.5)
        oa_ref[:, ks] = _l2norm(conv[:, ks])
    oa_ref[:, 2 * A_HEADS * A_DK:CONV_DIM] = conv[:, 2 * A_HEADS * A_DK:]

    @pl.when(i == pl.num_programs(1) - 1)
    def _():
        tail_ref[...] = raw[tm - 8:tm]


def _norm_proj_conv(x, norm_w, w_p, w_conv, conv0, tm):
    b, t, _ = x.shape
    assert t % tm == 0 and tm % 8 == 0
    once = pl.Buffered(1)
    return pl.pallas_call(
        _norm_proj_conv_kernel,
        grid=(b, t // tm),
        in_specs=[pl.BlockSpec((None, tm, D_MODEL), lambda i, j: (i, j, 0)),
                  pl.BlockSpec((1, D_MODEL), lambda i, j: (0, 0), pipeline_mode=once),
                  pl.BlockSpec((D_MODEL, COLS_A + COLS_B), lambda i, j: (0, 0), pipeline_mode=once),
                  pl.BlockSpec((CONV_W, CONV_DIM), lambda i, j: (0, 0), pipeline_mode=once),
                  pl.BlockSpec((None, 8, CONV_DIM), lambda i, j: (i, 0, 0))],
        out_specs=[pl.BlockSpec((None, tm, COLS_A), lambda i, j: (i, j, 0)),
                   pl.BlockSpec((None, tm, COLS_B), lambda i, j: (i, j, 0)),
                   pl.BlockSpec((None, 8, CONV_DIM), lambda i, j: (i, 0, 0))],
        out_shape=[jax.ShapeDtypeStruct((b, t, COLS_A), F32),
                   jax.ShapeDtypeStruct((b, t, COLS_B), F32),
                   jax.ShapeDtypeStruct((b, 8, CONV_DIM), F32)],
        scratch_shapes=[pltpu.VMEM((8, CONV_DIM), F32)],
        compiler_params=pltpu.CompilerParams(dimension_semantics=("arbitrary", "arbitrary"),
                                             vmem_limit_bytes=VMEM_LIMIT),
        name="norm_proj_conv",
    )(x, norm_w, w_p, w_conv, conv0)


def _gdn_kernel(xa_ref, z_ref, gt_ref, conv0_ref, s0_ref, wconv_ref, gp_ref, nwa_ref,
                o_ref, snew_ref, ext_ref, s_ref, *, rows):
    n = pl.program_id(1)
    c = CHUNK
    n_chunks = -(-rows // c)
    rp = n_chunks * c

    @pl.when(n == 0)
    def _():
        ext_ref[0:8, :] = conv0_ref[...]
        s_ref[...] = s0_ref[...]

    ext_ref[8:8 + rows, :] = xa_ref[...]
    if rows < rp:
        ext_ref[8 + rows:8 + rp, :] = jnp.zeros((rp - rows, CONV_DIM), F32)

    conv = ext_ref[5:5 + rp, :] * wconv_ref[0:1, :]
    for i in range(1, CONV_W):
        conv = conv + ext_ref[5 + i:5 + i + rp, :] * wconv_ref[i:i + 1, :]
    conv = _silu(conv)
    ext_ref[0:8, :] = ext_ref[rp:rp + 8, :]

    row = lax.broadcasted_iota(jnp.int32, (rp, 1), 0)
    live = row < rows
    if rows < rp:
        conv = jnp.where(live, conv, 0.0)

    gt = gt_ref[...]
    if rows < rp:
        gt = jnp.concatenate([gt, jnp.zeros((rp - rows, GATE_PAD), F32)], axis=0)
    beta_all = _sigmoid(gt)
    g_all = -jnp.exp(gp_ref[0:1, :]) * _softplus(gt + gp_ref[1:2, :])
    lane = lax.broadcasted_iota(jnp.int32, (rp, GATE_PAD), 1)
    g_all = jnp.where((lane >= A_HEADS) & (lane < 2 * A_HEADS) & live, g_all, 0.0)
    beta_all = jnp.where(live, beta_all, 0.0)

    ii = lax.broadcasted_iota(jnp.int32, (c, c), 0)
    jj = lax.broadcasted_iota(jnp.int32, (c, c), 1)
    incl = ii >= jj
    strict = ii > jj
    tril = jnp.where(incl, 1.0, 0.0).astype(BF16)
    eye = jnp.where(ii == jj, 1.0, 0.0).astype(F32)
    dd = functools.partial(jnp.dot, preferred_element_type=F32)

    def exact_tril_dot(x):
        hi, lo = _split2(x)
        lo2 = (x - hi.astype(F32) - lo.astype(F32)).astype(BF16)
        return dd(tril, hi) + dd(tril, lo) + dd(tril, lo2)

    z = z_ref[...]
    nwa = nwa_ref[...]
    chains = [(ci, h) for ci in range(n_chunks) for h in range(A_HEADS)]
    gc_alls = [exact_tril_dot(g_all[ci * c:(ci + 1) * c]) for ci in range(n_chunks)]
    qs, ks, kbs, rhss, gcs, gls = [], [], [], [], [], []
    for ci, h in chains:
        rs = slice(ci * c, (ci + 1) * c)
        q = conv[rs, h * A_DK:(h + 1) * A_DK]
        k = conv[rs, A_HEADS * A_DK + h * A_DK:A_HEADS * A_DK + (h + 1) * A_DK]
        v = conv[rs, 2 * A_HEADS * A_DK + h * A_DV:2 * A_HEADS * A_DK + (h + 1) * A_DV]
        q = q * lax.rsqrt(jnp.sum(q * q, axis=-1, keepdims=True) + EPS) * (A_DK ** -0.5)
        k = k * lax.rsqrt(jnp.sum(k * k, axis=-1, keepdims=True) + EPS)
        beta = beta_all[rs, h:h + 1]
        gc = gc_alls[ci][:, A_HEADS + h:A_HEADS + h + 1]
        kb = k * beta
        qs.append(q)
        ks.append(k)
        kbs.append(kb)
        rhss.append(jnp.concatenate([v * beta, kb * jnp.exp(gc)], axis=1))
        gcs.append(gc)
        gls.append(gc_alls[ci][c - 1:c, A_HEADS + h:A_HEADS + h + 1])
    e_mats = [exact_tril_dot(jnp.where(strict, g_all[ci * c:(ci + 1) * c, A_HEADS + h:A_HEADS + h + 1], 0.0))
              for ci, h in chains]
    decays = [jnp.where(incl, jnp.exp(jnp.where(incl, e, 0.0)), 0.0) for e in e_mats]
    kks = [_dot_nt(kb, k) for kb, k in zip(kbs, ks)]
    qks = [_dot_nt(q, k) for q, k in zip(qs, ks)]
    pws = [jnp.where(strict, a * d, 0.0) for a, d in zip(kks, decays)]
    qks = [jnp.where(incl, a * d, 0.0) for a, d in zip(qks, decays)]
    t_invs = [eye - lw for lw in pws]
    for _ in range(c.bit_length() - 2):
        pws = [_dot3(pw, pw) for pw in pws]
        t_invs = [ti + _dot3(ti, pw) for ti, pw in zip(t_invs, pws)]
    uws = [_dot(ti, rhs) for ti, rhs in zip(t_invs, rhss)]

    states = [s_ref[h] for h in range(A_HEADS)]
    for ci in range(n_chunks):
        idx = [ci * A_HEADS + h for h in range(A_HEADS)]
        ws = [_dot(uws[i][:, A_DV:], states[h]) for h, i in enumerate(idx)]
        os = [_dot(qs[i] * jnp.exp(gcs[i]), states[h]) for h, i in enumerate(idx)]
        es = [uws[i][:, :A_DV] - w for i, w in zip(idx, ws)]
        os = [o + _dot(qks[i], e) for o, i, e in zip(os, idx, es)]
        upd = [_dot_tn(ks[i] * jnp.exp(gls[i] - gcs[i]), e) for i, e in zip(idx, es)]
        states = [s * jnp.exp(gls[i]) + d for s, i, d in zip(states, idx, upd)]
        r0 = ci * c
        r1 = min(rows, r0 + c)
        for h, o in enumerate(os):
            o = o if r1 - r0 == c else o[:r1 - r0]
            o_ref[r0:r1, h * A_DV:(h + 1) * A_DV] = (
                _rms(o, nwa) * _silu(z[r0:r1, h * A_DV:(h + 1) * A_DV]))
    for h in range(A_HEADS):
        s_ref[h] = states[h]

    @pl.when(n == pl.num_programs(1) - 1)
    def _():
        snew_ref[...] = s_ref[...]


GDN_ROWS = 8 * CHUNK


def _gdn(proj, conv0, s0, w_conv, gate_par, norm_a):
    b, t, _ = proj.shape
    rows = min(GDN_ROWS, t)
    assert t % rows == 0 and (rows % CHUNK == 0 or rows == t)
    n = t // rows
    kernel = functools.partial(_gdn_kernel, rows=rows)
    return pl.pallas_call(
        kernel,
        grid=(b, n),
        in_specs=[pl.BlockSpec((None, rows, CONV_DIM), lambda i, j: (i, j, 0)),
                  pl.BlockSpec((None, rows, A_WIDTH), lambda i, j: (i, j, COL_Z // A_WIDTH)),
                  pl.BlockSpec((None, rows, GATE_PAD), lambda i, j: (i, j, COL_G // GATE_PAD)),
                  pl.BlockSpec((None, 8, CONV_DIM), lambda i, j: (i, 0, 0)),
                  pl.BlockSpec((None, A_HEADS, A_DK, A_DV), lambda i, j: (i, 0, 0, 0)),
                  pl.BlockSpec((CONV_W, CONV_DIM), lambda i, j: (0, 0)),
                  pl.BlockSpec((8, GATE_PAD), lambda i, j: (0, 0)),
                  pl.BlockSpec((1, A_DV), lambda i, j: (0, 0))],
        out_specs=[pl.BlockSpec((None, rows, A_WIDTH), lambda i, j: (i, j, 0)),
                   pl.BlockSpec((None, A_HEADS, A_DK, A_DV), lambda i, j: (i, 0, 0, 0))],
        out_shape=[jax.ShapeDtypeStruct((b, t, A_WIDTH), F32),
                   jax.ShapeDtypeStruct((b, A_HEADS, A_DK, A_DV), F32)],
        scratch_shapes=[pltpu.VMEM((8 + -(-rows // CHUNK) * CHUNK, CONV_DIM), F32),
                        pltpu.VMEM((A_HEADS, A_DK, A_DV), F32)],
        compiler_params=pltpu.CompilerParams(dimension_semantics=("arbitrary", "arbitrary"),
                                             vmem_limit_bytes=VMEM_LIMIT),
        name="gdn",
    )(proj, proj, proj, conv0, s0, w_conv, gate_par, norm_a)


HALF = 128 // CHUNK


def _gdn_pair_kernel(xa_ref, z_ref, gt_ref, conv0_ref, s0_ref, wconv_ref, gp_ref, nwa_ref,
                     o_ref, snew_ref, ext_ref, s_ref, *, rows, nb, conv_done, side_work=None):
    n = pl.program_id(1)
    c = CHUNK
    n_chunks = -(-rows // c)
    rp = n_chunks * c
    assert A_HEADS % 2 == 0 and HALF == 2 and A_DK == 128 and A_DV == 128

    @pl.when(n == 0)
    def _():
        ext_ref[:, 0:8, :] = conv0_ref[...]
        s_ref[...] = s0_ref[...]

    if side_work is not None:
        side_work()

    row = lax.broadcasted_iota(jnp.int32, (rp, 1), 0)
    live = row < rows
    lane_g = lax.broadcasted_iota(jnp.int32, (rp, GATE_PAD), 1)
    gate_lanes = (lane_g >= A_HEADS) & (lane_g < 2 * A_HEADS) & live

    ii = lax.broadcasted_iota(jnp.int32, (c, 128), 0)
    ll = lax.broadcasted_iota(jnp.int32, (c, 128), 1)
    jj = ll & (c - 1)
    low = ll < c
    incl = ii >= jj
    strict = ii > jj
    eye = jnp.where(ii == jj, 1.0, 0.0).astype(F32)
    ti = lax.broadcasted_iota(jnp.int32, (c, c), 0)
    tj = lax.broadcasted_iota(jnp.int32, (c, c), 1)
    tril = jnp.where(ti >= tj, 1.0, 0.0).astype(BF16)
    dd = functools.partial(jnp.dot, preferred_element_type=F32)

    def exact_tril_dot(x):
        hi, lo = _split2(x)
        lo2 = (x - hi.astype(F32) - lo.astype(F32)).astype(BF16)
        return dd(tril, hi) + dd(tril, lo) + dd(tril, lo2)

    def bd(x):
        return jnp.concatenate([jnp.where(low, x, 0.0), jnp.where(low, 0.0, x)], axis=0).astype(BF16)

    def unstack(y):
        return jnp.where(low, y[:c], y[c:])

    nwa = nwa_ref[...]
    chains = []
    for bi in range(nb):
        if conv_done:
            assert rows == rp
            conv = xa_ref[bi]
        else:
            ext_ref[bi, 8:8 + rows, :] = xa_ref[bi]
            if rows < rp:
                ext_ref[bi, 8 + rows:8 + rp, :] = jnp.zeros((rp - rows, CONV_DIM), F32)
            conv = _causal_conv_silu(ext_ref[bi, 0:8 + rp, :], wconv_ref, rp)
            ext_ref[bi, 0:8, :] = ext_ref[bi, rp:rp + 8, :]
            if rows < rp:
                conv = jnp.where(live, conv, 0.0)

        gt = gt_ref[bi]
        if rows < rp:
            gt = jnp.concatenate([gt, jnp.zeros((rp - rows, GATE_PAD), F32)], axis=0)
        beta_all = jnp.where(live, _sigmoid(gt), 0.0)
        g_all = jnp.where(gate_lanes, -jnp.exp(gp_ref[0:1, :]) * _softplus(gt + gp_ref[1:2, :]), 0.0)

        for ci in range(n_chunks):
            rs = slice(ci * c, (ci + 1) * c)
            gc_all = exact_tril_dot(g_all[rs])
            for p in range(A_HEADS // 2):
                hd = []
                for h in (2 * p, 2 * p + 1):
                    q = conv[rs, h * A_DK:(h + 1) * A_DK]
                    k = conv[rs, A_HEADS * A_DK + h * A_DK:A_HEADS * A_DK + (h + 1) * A_DK]
                    v = conv[rs, 2 * A_HEADS * A_DK + h * A_DV:2 * A_HEADS * A_DK + (h + 1) * A_DV]
                    if not conv_done:
                        q = _l2norm(q) * (A_DK ** -0.5)
                        k = _l2norm(k)
                    beta = beta_all[rs, h:h + 1]
                    g = g_all[rs, A_HEADS + h:A_HEADS + h + 1]
                    gc = gc_all[:, A_HEADS + h:A_HEADS + h + 1]
                    gl = gc_all[c - 1:c, A_HEADS + h:A_HEADS + h + 1]
                    kb = k * beta
                    hd.append(dict(q=q, k=k, kb=kb, g=g, qg=q * jnp.exp(gc), kd=k * jnp.exp(gl - gc),
                                   dl=jnp.exp(gl),
                                   rhs=jnp.concatenate([v * beta, kb * jnp.exp(gc)], axis=1)))
                chains.append(dict(bi=bi, ci=ci, p=p, hd=hd))

    for ch in chains:
        a, b = ch["hd"]
        ch["e_mat"] = exact_tril_dot(jnp.where(strict, jnp.where(low, a["g"], b["g"]), 0.0))
    for ch in chains:
        a, b = ch["hd"]
        k_st = jnp.concatenate([a["k"], b["k"]], axis=0)
        ch["kk"] = _dot_nt(jnp.concatenate([a["kb"], b["kb"]], axis=0), k_st)
        ch["qk"] = _dot_nt(jnp.concatenate([a["q"], b["q"]], axis=0), k_st)
    for ch in chains:
        decay = jnp.where(incl, jnp.exp(jnp.where(incl, ch["e_mat"], 0.0)), 0.0)
        ch["lower"] = jnp.where(strict, unstack(ch["kk"]) * decay, 0.0)
        ch["qk"] = jnp.where(incl, unstack(ch["qk"]) * decay, 0.0)
        ch["x"] = ch["lower"]
        ch["xb"] = bd(ch["lower"])
        ch["t"] = eye - ch["lower"]
    for _ in range(c.bit_length() - 2):
        for ch in chains:
            ch["x"] = jnp.dot(ch["x"].astype(BF16), ch["xb"], preferred_element_type=F32)
        for ch in chains:
            ch["xb"] = bd(ch["x"])
            ch["t"] = ch["t"] + jnp.dot(ch["t"].astype(BF16), ch["xb"], preferred_element_type=F32)
    for ch in chains:
        lh, ll_ = _split2(ch["lower"])
        th = ch["t"].astype(BF16).astype(F32)
        tl = ch["t"] - th
        bth = bd(th)
        lt = dd(lh, bth) + dd(lh, bd(tl)) + dd(ll_, bth)
        ch["res"] = eye - ch["t"] - lt
    for ch in chains:
        ch["t"] = ch["t"] + jnp.dot(ch["t"].astype(BF16), bd(ch["res"]), preferred_element_type=F32)
    for ch in chains:
        a, b = ch["hd"]
        uw = jnp.dot(bd(ch["t"]), jnp.concatenate([a["rhs"], b["rhs"]], axis=0).astype(BF16),
                     preferred_element_type=F32)
        a["u"], a["w"] = uw[:c, :A_DV], uw[:c, A_DV:]
        b["u"], b["w"] = uw[c:, :A_DV], uw[c:, A_DV:]
        ch["qkb"] = bd(ch["qk"])

    by_key = {(ch["bi"], ch["ci"], ch["p"]): ch for ch in chains}
    states = {(bi, h): s_ref[bi, h] for bi in range(nb) for h in range(A_HEADS)}
    for ci in range(n_chunks):
        cur = [by_key[(bi, ci, p)] for bi in range(nb) for p in range(A_HEADS // 2)]
        for ch in cur:
            for hh, d in enumerate(ch["hd"]):
                ws = _dot(jnp.concatenate([d["w"], d["qg"]], axis=0), states[(ch["bi"], 2 * ch["p"] + hh)])
                d["e"] = d["u"] - ws[:c]
                d["o"] = ws[c:]
        for ch in cur:
            a, b = ch["hd"]
            o2 = jnp.dot(ch["qkb"], jnp.concatenate([a["e"], b["e"]], axis=0).astype(BF16),
                         preferred_element_type=F32)
            a["o"] = a["o"] + o2[:c]
            b["o"] = b["o"] + o2[c:]
        for ch in cur:
            for hh, d in enumerate(ch["hd"]):
                key = (ch["bi"], 2 * ch["p"] + hh)
                states[key] = states[key] * d["dl"] + _dot_tn(d["kd"], d["e"])
        r0 = ci * c
        r1 = min(rows, r0 + c)
        for ch in cur:
            for hh, d in enumerate(ch["hd"]):
                h = 2 * ch["p"] + hh
                o = d["o"] if r1 - r0 == c else d["o"][:r1 - r0]
                o_ref[ch["bi"], r0:r1, h * A_DV:(h + 1) * A_DV] = (
                    _rms(o, nwa) * _silu(z_ref[ch["bi"], r0:r1, h * A_DV:(h + 1) * A_DV]))
    for (bi, h), s in states.items():
        s_ref[bi, h] = s

    @pl.when(n == pl.num_programs(1) - 1)
    def _():
        snew_ref[...] = s_ref[...]


def _gdn_pair(proj, conv0, s0, w_conv, gate_par, norm_a, conv_done, side_of_grid=None):
    b, t, _ = proj.shape
    rows = min(GDN_ROWS, t)
    assert t % rows == 0 and (rows % CHUNK == 0 or rows == t)
    n = t // rows
    nb = max(1, min(b, GDN_ROWS // (-(-rows // CHUNK) * CHUNK)))
    assert b % nb == 0
    rp = 0 if conv_done else -(-rows // CHUNK) * CHUNK
    in_specs = [pl.BlockSpec((nb, rows, CONV_DIM), lambda i, j: (i, j, 0)),
                pl.BlockSpec((nb, rows, A_WIDTH), lambda i, j: (i, j, COL_Z // A_WIDTH)),
                pl.BlockSpec((nb, rows, GATE_PAD), lambda i, j: (i, j, COL_G // GATE_PAD)),
                pl.BlockSpec((nb, 8, CONV_DIM), lambda i, j: (i, 0, 0)),
                pl.BlockSpec((nb, A_HEADS, A_DK, A_DV), lambda i, j: (i, 0, 0, 0)),
                pl.BlockSpec((CONV_W, CONV_DIM), lambda i, j: (0, 0)),
                pl.BlockSpec((8, GATE_PAD), lambda i, j: (0, 0)),
                pl.BlockSpec((1, A_DV), lambda i, j: (0, 0))]
    out_specs = [pl.BlockSpec((nb, rows, A_WIDTH), lambda i, j: (i, j, 0)),
                 pl.BlockSpec((nb, A_HEADS, A_DK, A_DV), lambda i, j: (i, 0, 0, 0))]
    out_shape = [jax.ShapeDtypeStruct((b, t, A_WIDTH), F32),
                 jax.ShapeDtypeStruct((b, A_HEADS, A_DK, A_DV), F32)]
    inputs = (proj, proj, proj, conv0, s0, w_conv, gate_par, norm_a)
    n_in, n_out = len(in_specs), len(out_specs)
    side = side_of_grid((b // nb, n)) if side_of_grid is not None else None

    def kernel(*refs):
        n_side_in = len(side["in_specs"]) if side else 0
        gin, sin = refs[:n_in], refs[n_in:n_in + n_side_in]
        gout = refs[n_in + n_side_in:n_in + n_side_in + n_out]
        sout = refs[n_in + n_side_in + n_out:len(refs) - 2]
        side_work = (lambda: side["kernel"](*sin, *sout)) if side else None
        _gdn_pair_kernel(*gin, *gout, *refs[len(refs) - 2:], rows=rows, nb=nb, conv_done=conv_done,
                         side_work=side_work)

    if side:
        inputs = inputs + tuple(side["inputs"])
        in_specs = in_specs + list(side["in_specs"])
        out_specs = out_specs + list(side["out_specs"])
        out_shape = out_shape + list(side["out_shape"])
    outs = pl.pallas_call(
        kernel,
        grid=(b // nb, n),
        in_specs=in_specs,
        out_specs=out_specs,
        out_shape=out_shape,
        scratch_shapes=[pltpu.VMEM((nb, 8 + rp, CONV_DIM), F32),
                        pltpu.VMEM((nb, A_HEADS, A_DK, A_DV), F32)],
        compiler_params=pltpu.CompilerParams(dimension_semantics=("arbitrary", "arbitrary"),
                                             vmem_limit_bytes=VMEM_LIMIT),
        name="gdn_side" if side else "gdn",
    )(*inputs)
    return outs[:n_out], outs[n_out:]


def _slope(h):
    return 2.0 ** (-8.0 * (h + 1) / B_HEADS)


ATTN_SB = MAX_WINDOW
ATTN_GROUP = 16
PAIR = 128 // B_HEAD_DIM


def _attn_fused_kernel(q_ref, kp_ref, kc_ref, vp_ref, vc_ref, nwb_ref, o_ref, kbuf, vbuf, *stats):
    hp = pl.program_id(1)
    n = pl.program_id(2)
    sb = ATTN_SB
    blk = STEPS
    kbuf[0:sb, :] = kp_ref[...]
    kbuf[sb:2 * sb, :] = kc_ref[...]
    vbuf[0:sb, :] = vp_ref[...]
    vbuf[sb:2 * sb, :] = vc_ref[...]

    lane_q = lax.broadcasted_iota(jnp.int32, (blk, 128), 1)
    low_q = lane_q < B_HEAD_DIM
    qq = lax.broadcasted_iota(jnp.int32, (2 * blk, 2 * blk), 0)
    kk = lax.broadcasted_iota(jnp.int32, (2 * blk, 2 * blk), 1)
    steps_back = (qq & (blk - 1)) + blk - kk
    in_band = (steps_back >= 0) & (steps_back <= STEPS)
    slope_lo = jnp.float32(_slope(0))
    slope_hi = jnp.float32(_slope(1))
    for i in range(1, B_HEADS // PAIR):
        slope_lo = jnp.where(hp == i, _slope(PAIR * i), slope_lo)
        slope_hi = jnp.where(hp == i, _slope(PAIR * i + 1), slope_hi)
    slope = jnp.where(qq < blk, slope_lo, slope_hi)

    for bi, (_, dil) in enumerate(DILATED):
        m_s, l_s, num_s = stats[3 * bi:3 * bi + 3]
        bias = jnp.where(in_band, -slope * (steps_back * dil).astype(F32), NEG)
        bias_first = jnp.where(kk >= blk, bias, NEG)
        d_bits = dil.bit_length() - 1

        def rows(start, size):
            if dil == 1:
                return pl.ds(pl.multiple_of(start, blk), size)
            return pl.ds(start, size, stride=dil)

        def group(g, carry):
            tiles = []
            for u in range(ATTN_GROUP):
                it = g * ATTN_GROUP + u
                q0 = lax.shift_right_logical(it, d_bits) * (blk * dil) + (it & (dil - 1))
                k0 = sb + q0 - blk * dil
                q = q_ref[rows(q0, blk), :] * (B_HEAD_DIM ** -0.5)
                q2 = jnp.concatenate([jnp.where(low_q, q, 0.0), jnp.where(low_q, 0.0, q)], axis=0)
                tiles.append((q0, k0, q2.astype(BF16), kbuf[rows(k0, 2 * blk), :].astype(BF16),
                              vbuf[rows(k0, 2 * blk), :].astype(BF16)))
            scores = [lax.dot_general(q2, k, (((1,), (1,)), ((), ())), preferred_element_type=F32)
                      for (_, _, q2, k, _) in tiles]
            probs = []
            for (q0, k0, _, _, _), s in zip(tiles, scores):
                s = s + jnp.where((n == 0) & (q0 < blk * dil), bias_first, bias)
                m = jnp.max(s, axis=-1, keepdims=True)
                p = jnp.exp(s - m)
                probs.append((m, jnp.sum(p, axis=-1, keepdims=True), p.astype(BF16)))
            for (q0, _, _, _, v), (m, l, p) in zip(tiles, probs):
                pv = jnp.dot(p, v, preferred_element_type=F32)
                m_s[rows(q0, blk), :] = jnp.where(low_q, m[:blk], m[blk:])
                l_s[rows(q0, blk), :] = jnp.where(low_q, l[:blk], l[blk:])
                num_s[rows(q0, blk), :] = jnp.where(low_q, pv[:blk], pv[blk:])
            return carry

        lax.fori_loop(0, sb // (blk * ATTN_GROUP), group, 0)

    nwb = nwb_ref[...]

    def combine(i, carry):
        rs = pl.ds(pl.multiple_of(i * blk, blk), blk)
        ms = [stats[3 * bi][rs, :] for bi in range(len(DILATED))]
        ls = [stats[3 * bi + 1][rs, :] for bi in range(len(DILATED))]
        nums = [stats[3 * bi + 2][rs, :] for bi in range(len(DILATED))]
        mx = jnp.maximum(jnp.maximum(ms[0], ms[1]), ms[2])
        ws = [jnp.exp(m - mx) for m in ms]
        den = ws[0] * ls[0] + ws[1] * ls[1] + ws[2] * ls[2]
        o = (ws[0] * nums[0] + ws[1] * nums[1] + ws[2] * nums[2]) / den
        sq = o * o
        ss_lo = jnp.sum(jnp.where(low_q, sq, 0.0), axis=-1, keepdims=True)
        ss_hi = jnp.sum(jnp.where(low_q, 0.0, sq), axis=-1, keepdims=True)
        mean_sq = jnp.where(low_q, ss_lo, ss_hi) / B_HEAD_DIM
        o_ref[rs, :] = o * lax.rsqrt(mean_sq + EPS) * nwb
        return carry

    lax.fori_loop(0, sb // blk, combine, 0, unroll=4)


def _attn_prompt(qkv, norm_b_pair):
    b, s, _ = qkv.shape
    sb = ATTN_SB
    assert s % sb == 0 and (sb // STEPS) % ATTN_GROUP == 0
    assert all(w == STEPS * d and sb % w == 0 and d & (d - 1) == 0 for (w, d) in DILATED)
    n_pairs = B_HEADS // PAIR
    blk = (None, sb, 128)
    prev = lambda c0: pl.BlockSpec(blk, lambda i, p, n: (i, jnp.maximum(n - 1, 0), c0 + p))
    cur = lambda c0: pl.BlockSpec(blk, lambda i, p, n: (i, n, c0 + p))
    return pl.pallas_call(
        _attn_fused_kernel,
        grid=(b, n_pairs, s // sb),
        in_specs=[cur(0), prev(n_pairs), cur(n_pairs), prev(2 * n_pairs), cur(2 * n_pairs),
                  pl.BlockSpec((1, 128), lambda i, p, n: (0, 0))],
        out_specs=pl.BlockSpec(blk, lambda i, p, n: (i, n, p)),
        out_shape=jax.ShapeDtypeStruct((b, s, B_WIDTH), F32),
        scratch_shapes=[pltpu.VMEM((2 * sb, 128), F32), pltpu.VMEM((2 * sb, 128), F32)]
                       + [pltpu.VMEM((sb, 128), F32)] * (3 * len(DILATED)),
        compiler_params=pltpu.CompilerParams(
            dimension_semantics=("arbitrary", "arbitrary", "arbitrary"), vmem_limit_bytes=VMEM_LIMIT),
        name="attn_prompt",
    )(qkv, qkv, qkv, qkv, qkv, norm_b_pair)


def _attn_sample_kernel(q_ref, kn_ref, vn_ref, kt_ref, vt_ref, nwb_ref, o_ref, kto_ref, vto_ref, *, n_past, t):
    kn = kn_ref[...]
    vn = vn_ref[...]
    kt = kt_ref[...]
    vt = vt_ref[...]

    prow = lax.broadcasted_iota(jnp.int32, (t, 128), 0)
    plane = lax.broadcasted_iota(jnp.int32, (t, 128), 1)
    place = jnp.where(plane == 128 - t + prow, 1.0, 0.0).astype(BF16)
    tail_lanes = lax.broadcasted_iota(jnp.int32, (B_WIDTH, 128), 1) >= 128 - t
    tn = functools.partial(lax.dot_general, dimension_numbers=(((0,), (0,)), ((), ())),
                           preferred_element_type=F32)

    def shifted(win, new, out_ref):
        hi, lo = _split2(new)
        lo2 = (new - hi.astype(F32) - lo.astype(F32)).astype(BF16)
        new_t = tn(hi, place) + tn(lo, place) + tn(lo2, place)
        sh = pltpu.roll(win, n_past - t, axis=1)
        out_ref[:, :n_past - 128] = sh[:, :n_past - 128]
        out_ref[:, n_past - 128:] = jnp.where(tail_lanes, new_t, sh[:, n_past - 128:])

    shifted(kt, kn, kto_ref)
    shifted(vt, vn, vto_ref)

    rq = B_HEADS * t
    t_bits = t.bit_length() - 1
    d_bits = B_HEAD_DIM.bit_length() - 1
    rowi = lax.broadcasted_iota(jnp.int32, (rq, B_WIDTH), 0)
    lanei = lax.broadcasted_iota(jnp.int32, (rq, B_WIDTH), 1)
    own = lax.shift_right_logical(rowi, t_bits) == lax.shift_right_logical(lanei, d_bits)
    q_rep = jnp.concatenate([q_ref[...]] * B_HEADS, axis=0)
    q_blk = jnp.where(own, q_rep, 0.0).astype(BF16)
    zpad = jnp.zeros((128 - t, B_WIDTH), BF16)
    kn_pad = jnp.concatenate([kn.astype(BF16), zpad], axis=0)
    vn_pad = jnp.concatenate([vn.astype(BF16), zpad], axis=0)
    vt16 = vt.astype(BF16)
    scale = B_HEAD_DIM ** -0.5
    s_c = jnp.dot(q_blk, kt.astype(BF16), preferred_element_type=F32) * scale
    s_n = lax.dot_general(q_blk, kn_pad, (((1,), (1,)), ((), ())), preferred_element_type=F32) * scale

    r1 = lax.broadcasted_iota(jnp.int32, (rq, 1), 0)
    head = lax.shift_right_logical(r1, t_bits)
    slope = jnp.zeros((rq, 1), F32)
    for h in range(B_HEADS):
        slope = jnp.where(head == h, _slope(h), slope)
    tok = r1 & (t - 1)
    delta_c = n_past + tok - lax.broadcasted_iota(jnp.int32, (rq, n_past), 1)
    key_n = lax.broadcasted_iota(jnp.int32, (rq, 128), 1)
    delta_n = tok - key_n
    s_c = s_c - slope * delta_c.astype(F32)
    s_n = s_n - slope * delta_n.astype(F32)
    live_n = (key_n < t) & (delta_n >= 0)

    ms, ls, nums = [], [], []
    for (_, dil) in DILATED:
        valid_c = ((delta_c & (dil - 1)) == 0) & (delta_c <= STEPS * dil)
        valid_n = live_n & ((delta_n & (dil - 1)) == 0) & (delta_n <= STEPS * dil)
        sd_c = jnp.where(valid_c, s_c, NEG)
        sd_n = jnp.where(valid_n, s_n, NEG)
        m = jnp.maximum(jnp.max(sd_c, axis=-1, keepdims=True), jnp.max(sd_n, axis=-1, keepdims=True))
        p_c = jnp.exp(sd_c - m)
        p_n = jnp.exp(sd_n - m)
        ms.append(m)
        ls.append(jnp.sum(p_c, axis=-1, keepdims=True) + jnp.sum(p_n, axis=-1, keepdims=True))
        nums.append(lax.dot_general(p_c.astype(BF16), vt16, (((1,), (1,)), ((), ())), preferred_element_type=F32)
                    + jnp.dot(p_n.astype(BF16), vn_pad, preferred_element_type=F32))
    mx = jnp.maximum(jnp.maximum(ms[0], ms[1]), ms[2])
    ws = [jnp.exp(m - mx) for m in ms]
    den = ws[0] * ls[0] + ws[1] * ls[1] + ws[2] * ls[2]
    o = (ws[0] * nums[0] + ws[1] * nums[1] + ws[2] * nums[2]) / den
    o = jnp.where(own, o, 0.0)
    o = o * lax.rsqrt(jnp.sum(o * o, axis=-1, keepdims=True) / B_HEAD_DIM + EPS) * nwb_ref[...]
    acc = o[0:t, :]
    for h in range(1, B_HEADS):
        acc = acc + o[h * t:(h + 1) * t, :]
    o_ref[...] = acc


def _attn_sample_parts(qkv, win_kt, win_vt, norm_b_tiled, seq_of_step):
    b, t, _ = qkv.shape
    n_past = win_kt.shape[2]
    assert n_past == MAX_WINDOW and n_past % 128 == 0 and t % 8 == 0 and t & (t - 1) == 0 and t <= 128
    assert all(d & (d - 1) == 0 for (_, d) in DILATED)
    col = lambda c: pl.BlockSpec((None, t, B_WIDTH), lambda *g: (seq_of_step(*g), 0, c))
    win = pl.BlockSpec((None, B_WIDTH, n_past), lambda *g: (seq_of_step(*g), 0, 0))
    return dict(
        kernel=functools.partial(_attn_sample_kernel, n_past=n_past, t=t),
        inputs=(qkv, qkv, qkv, win_kt, win_vt, norm_b_tiled),
        in_specs=[col(0), col(1), col(2), win, win, pl.BlockSpec((1, B_WIDTH), lambda *g: (0, 0))],
        out_specs=[col(0), win, win],
        out_shape=[jax.ShapeDtypeStruct((b, t, B_WIDTH), F32),
                   jax.ShapeDtypeStruct((b, B_WIDTH, n_past), F32),
                   jax.ShapeDtypeStruct((b, B_WIDTH, n_past), F32)])


def _attn_sample(qkv, win_kt, win_vt, norm_b_tiled):
    parts = _attn_sample_parts(qkv, win_kt, win_vt, norm_b_tiled, lambda i: i)
    return pl.pallas_call(
        parts["kernel"],
        grid=(qkv.shape[0],),
        in_specs=parts["in_specs"],
        out_specs=parts["out_specs"],
        out_shape=parts["out_shape"],
        compiler_params=pltpu.CompilerParams(dimension_semantics=("arbitrary",),
                                             vmem_limit_bytes=VMEM_LIMIT),
        name="attn_sample",
    )(*parts["inputs"])


def _out_ffn_kernel(x_ref, oa_ref, ob_ref, wo_ref, nf_ref, wg_ref, wu_ref, wd_ref, nfin_ref, y_ref, *, ff_chunk):
    mixed = jnp.concatenate([oa_ref[...], ob_ref[...]], axis=-1).astype(BF16)
    x1 = x_ref[...] + jnp.dot(mixed, wo_ref[...], preferred_element_type=F32)
    hf = _rms(x1, nf_ref[...]).astype(BF16)
    x2 = x1
    for c0 in range(0, wg_ref.shape[1], ff_chunk):
        g = jnp.dot(hf, wg_ref[:, c0:c0 + ff_chunk], preferred_element_type=F32)
        u = jnp.dot(hf, wu_ref[:, c0:c0 + ff_chunk], preferred_element_type=F32)
        act = (_silu(g) * u).astype(BF16)
        x2 = x2 + jnp.dot(act, wd_ref[c0:c0 + ff_chunk, :], preferred_element_type=F32)
    y_ref[...] = _rms(x2, nfin_ref[...])


def _out_ffn(x2d, o_a, o_b, w_out, norm_ffn, w_gate, w_up, w_down, norm_final, tm):
    t = x2d.shape[0]
    d_ff = w_gate.shape[1]
    ff_chunk = d_ff // 2 if d_ff % 256 == 0 else d_ff
    once = pl.Buffered(1)
    row = lambda w: pl.BlockSpec((tm, w), lambda i: (i, 0))
    full = lambda a, b: pl.BlockSpec((a, b), lambda i: (0, 0), pipeline_mode=once)
    return pl.pallas_call(
        functools.partial(_out_ffn_kernel, ff_chunk=ff_chunk),
        grid=(t // tm,),
        in_specs=[row(D_MODEL), row(A_WIDTH), row(B_WIDTH),
                  full(D_MODEL, D_MODEL), full(1, D_MODEL),
                  full(D_MODEL, d_ff), full(D_MODEL, d_ff), full(d_ff, D_MODEL), full(1, D_MODEL)],
        out_specs=row(D_MODEL),
        out_shape=jax.ShapeDtypeStruct((t, D_MODEL), F32),
        compiler_params=pltpu.CompilerParams(dimension_semantics=("arbitrary",),
                                             vmem_limit_bytes=VMEM_LIMIT),
        name="out_ffn",
    )(x2d, o_a, o_b, w_out, norm_ffn, w_gate, w_up, w_down, norm_final)


def _layer_params(norm_mix, w_in, w_conv, a_log, dt_bias, norm_out_a, norm_out_b, w_out, norm_ffn,
                  w_gate, w_up, w_down, layer):
    w = w_in[layer]
    n_gate = 2 * A_HEADS
    w_p = jnp.concatenate([w[:, :COL_G + n_gate], jnp.zeros((D_MODEL, GATE_PAD - n_gate), w.dtype),
                           w[:, COL_G + n_gate:]], axis=1).astype(BF16)
    gate_par = jnp.zeros((8, GATE_PAD), F32)
    gate_par = gate_par.at[0, A_HEADS:2 * A_HEADS].set(a_log[layer].astype(F32))
    gate_par = gate_par.at[1, A_HEADS:2 * A_HEADS].set(dt_bias[layer].astype(F32))
    return dict(
        norm_mix=norm_mix[layer].reshape(1, D_MODEL), w_p=w_p, w_conv=w_conv[layer], gate_par=gate_par,
        norm_a=norm_out_a[layer].reshape(1, A_DV),
        norm_b_pair=jnp.tile(norm_out_b[layer], PAIR).reshape(1, PAIR * B_HEAD_DIM),
        norm_b_tiled=jnp.tile(norm_out_b[layer], B_HEADS).reshape(1, B_WIDTH),
        w_out=w_out[layer].astype(BF16), norm_ffn=norm_ffn[layer].reshape(1, D_MODEL),
        w_gate=w_gate[layer].astype(BF16), w_up=w_up[layer].astype(BF16), w_down=w_down[layer].astype(BF16))


def _project(x, conv_buf, p, tm):
    b, t, _ = x.shape
    assert t >= CONV_W - 1
    conv0 = jnp.pad(conv_buf, ((0, 0), (8 - (CONV_W - 1), 0), (0, 0)))
    conv_done = t % tm == 0
    if conv_done:
        proj_a, proj_b, tail = _norm_proj_conv(x, p["norm_mix"], p["w_p"], p["w_conv"], conv0, tm)
        new_conv = tail[:, 8 - (CONV_W - 1):]
    else:
        proj_a, proj_b = _norm_proj(x.reshape(b * t, D_MODEL), p["norm_mix"], p["w_p"], tm)
        proj_a = proj_a.reshape(b, t, COLS_A)
        proj_b = proj_b.reshape(b, t, COLS_B)
        new_conv = proj_a[:, t - (CONV_W - 1):, :CONV_DIM]
    return proj_a, proj_b, new_conv, conv0, conv_done


def _finish(x, o_a, o_b, p, norm_final, tm):
    b, t, _ = x.shape
    y = _out_ffn(x.reshape(b * t, D_MODEL), o_a.reshape(b * t, A_WIDTH), o_b.reshape(b * t, B_WIDTH), p["w_out"],
                 p["norm_ffn"], p["w_gate"], p["w_up"], p["w_down"], norm_final.reshape(1, D_MODEL), tm)
    return y.reshape(b, t, D_MODEL)


def kernel(x_prompt, x_sample, state_conv, state_rec, cache_win_k, cache_win_v, norm_mix, w_in, w_conv, a_log,
           dt_bias, norm_out_a, norm_out_b, w_out, norm_ffn, w_gate, w_up, w_down, norm_final):
    depth = w_in.shape[0]
    assert depth == 1, "the final norm is fused into the block of a single-layer trunk"
    p = _layer_params(norm_mix, w_in, w_conv, a_log, dt_bias, norm_out_a, norm_out_b, w_out, norm_ffn,
                      w_gate, w_up, w_down, 0)
    bp, tp, _ = x_prompt.shape
    bs, ts, _ = x_sample.shape
    tm_p, tm_s = 512, bs * ts
    gdn_args = (p["w_conv"], p["gate_par"], p["norm_a"])

    sa, sb, sc, s_conv0, s_done = _project(x_sample, state_conv[0], p, tm_s)
    n_past = cache_win_k.shape[2]
    to_t = lambda a: jnp.transpose(a, (0, 2, 3, 1)).reshape(bs, B_WIDTH, n_past)
    from_t = lambda a: jnp.transpose(a.reshape(bs, B_HEADS, B_HEAD_DIM, a.shape[2]), (0, 3, 1, 2))
    win_kt, win_vt = to_t(cache_win_k[0]), to_t(cache_win_v[0])

    def sample_attention_of_grid(grid):
        if grid[0] * grid[1] != bs:
            return None
        return _attn_sample_parts(sb, win_kt, win_vt, p["norm_b_tiled"], lambda i, j: i * grid[1] + j)

    pa, pb, pc, p_conv0, p_done = _project(x_prompt, jnp.zeros((bp, CONV_W - 1, CONV_DIM), F32), p, tm_p)
    zero_rec = jnp.zeros((bp, A_HEADS, A_DK, A_DV), F32)
    (o_a_p, pr), side = _gdn_pair(pa, p_conv0, zero_rec, *gdn_args, p_done, sample_attention_of_grid)
    o_b_s, new_kt, new_vt = side if side else _attn_sample(sb, win_kt, win_vt, p["norm_b_tiled"])
    (o_a_s, sr), _ = _gdn_pair(sa, s_conv0, state_rec[0], *gdn_args, s_done)

    o_b_p = _attn_prompt(pb, p["norm_b_pair"])
    keep = min(MAX_WINDOW, tp)
    shp = (bp, keep, B_HEADS, B_HEAD_DIM)
    pk = pb[:, tp - keep:, B_WIDTH:2 * B_WIDTH].reshape(shp)
    pv = pb[:, tp - keep:, 2 * B_WIDTH:].reshape(shp)

    yp = _finish(x_prompt, o_a_p, o_b_p, p, norm_final, tm_p)
    ys = _finish(x_sample, o_a_s, o_b_s, p, norm_final, tm_s)
    return (yp, ys, pc[None], pr[None], pk[None], pv[None], sc[None], sr[None], from_t(new_kt)[None],
            from_t(new_vt)[None])
```

```python
import functools

import jax
import jax.numpy as jnp
from jax import lax
from jax.experimental import pallas as pl
from jax.experimental.pallas import tpu as pltpu

F32 = jnp.float32
BF16 = jnp.bfloat16

D_MODEL = 1024
A_HEADS = 4
A_DK = 128
A_DV = 128
A_WIDTH = A_HEADS * A_DV
CONV_W = 4
CONV_DIM = 2 * A_HEADS * A_DK + A_HEADS * A_DV
B_HEADS = 8
B_HEAD_DIM = 64
B_WIDTH = B_HEADS * B_HEAD_DIM
DILATED = ((128, 1), (512, 4), (2048, 16))
STEPS = 128
MAX_WINDOW = 2048
EPS = 1e-6
NEG = -1e30
CHUNK = 64
GATE_PAD = 128
COL_Z = CONV_DIM
COL_G = CONV_DIM + A_WIDTH
COLS_A = COL_G + GATE_PAD
COLS_B = 3 * B_WIDTH
VMEM_LIMIT = 56 * 1024 * 1024


def _dot(a, b):
    return jnp.dot(a.astype(BF16), b.astype(BF16), preferred_element_type=F32)


def _dot_nt(a, b):
    return lax.dot_general(a.astype(BF16), b.astype(BF16), (((1,), (1,)), ((), ())),
                           preferred_element_type=F32)


def _dot_tn(a, b):
    return lax.dot_general(a.astype(BF16), b.astype(BF16), (((0,), (0,)), ((), ())),
                           preferred_element_type=F32)


def _split2(a):
    hi = a.astype(BF16)
    lo = (a - hi.astype(F32)).astype(BF16)
    return hi, lo


def _dot3(a, b):
    ah, al = _split2(a)
    bh, bl = _split2(b)
    d = functools.partial(jnp.dot, preferred_element_type=F32)
    return d(ah, bh) + d(ah, bl) + d(al, bh)


def _sigmoid(x):
    return 1.0 / (1.0 + jnp.exp(-x))


def _silu(x):
    return x * _sigmoid(x)


def _softplus(x):
    return jnp.maximum(x, 0.0) + jnp.log(1.0 + jnp.exp(-jnp.abs(x)))


def _rms(x, w):
    return x * lax.rsqrt(jnp.mean(x * x, axis=-1, keepdims=True) + EPS) * w


def _norm_proj_kernel(x_ref, nw_ref, wa_ref, wg_ref, wb_ref, oa_ref, ob_ref):
    h = _rms(x_ref[...], nw_ref[...]).astype(BF16)
    oa_ref[:, :COL_G] = jnp.dot(h, wa_ref[...], preferred_element_type=F32)
    oa_ref[:, COL_G:] = jnp.dot(h, wg_ref[...], preferred_element_type=F32)
    ob_ref[...] = jnp.dot(h, wb_ref[...], preferred_element_type=F32)


def _weight_specs():
    zero = lambda *g: (0, 0)
    return [pl.BlockSpec((D_MODEL, c), zero, pipeline_mode=pl.Buffered(1)) for c in (COL_G, GATE_PAD, COLS_B)]


def _norm_proj(x2d, norm_w, w_p, tm):
    t = x2d.shape[0]
    once = pl.Buffered(1)
    return pl.pallas_call(
        _norm_proj_kernel,
        grid=(t // tm,),
        in_specs=[pl.BlockSpec((tm, D_MODEL), lambda i: (i, 0)),
                  pl.BlockSpec((1, D_MODEL), lambda i: (0, 0), pipeline_mode=once)] + _weight_specs(),
        out_specs=[pl.BlockSpec((tm, COLS_A), lambda i: (i, 0)),
                   pl.BlockSpec((tm, COLS_B), lambda i: (i, 0))],
        out_shape=[jax.ShapeDtypeStruct((t, COLS_A), F32),
                   jax.ShapeDtypeStruct((t, COLS_B), F32)],
        compiler_params=pltpu.CompilerParams(dimension_semantics=("arbitrary",),
                                             vmem_limit_bytes=VMEM_LIMIT),
        name="norm_proj",
    )(x2d, norm_w, *w_p)


def _causal_conv_silu(x_all, wconv_ref, n):
    conv = x_all[8:8 + n] * wconv_ref[CONV_W - 1:CONV_W, :]
    for back in range(1, CONV_W):
        tap = CONV_W - 1 - back
        conv = conv + pltpu.roll(x_all, back, axis=0)[8:8 + n] * wconv_ref[tap:tap + 1, :]
    return _silu(conv)


def _l2norm(x):
    return x * lax.rsqrt(jnp.sum(x * x, axis=-1, keepdims=True) + EPS)


def _norm_proj_conv_kernel(x_ref, nw_ref, wa_ref, wg_ref, wb_ref, wconv_ref, conv0_ref, oa_ref, ob_ref, tail_ref,
                           carry_ref):
    i = pl.program_id(1)
    tm = x_ref.shape[0]

    @pl.when(i == 0)
    def _():
        carry_ref[...] = conv0_ref[...]

    h = _rms(x_ref[...], nw_ref[...]).astype(BF16)
    raw = jnp.dot(h, wa_ref[:, :CONV_DIM], preferred_element_type=F32)
    oa_ref[:, CONV_DIM:COL_G] = jnp.dot(h, wa_ref[:, CONV_DIM:], preferred_element_type=F32)
    oa_ref[:, COL_G:] = jnp.dot(h, wg_ref[...], preferred_element_type=F32)
    ob_ref[...] = jnp.dot(h, wb_ref[...], preferred_element_type=F32)
    conv = _causal_conv_silu(jnp.concatenate([carry_ref[...], raw], axis=0), wconv_ref, tm)
    carry_ref[...] = raw[tm - 8:tm]
    for hd in range(A_HEADS):
        qs = slice(hd * A_DK, (hd + 1) * A_DK)
        ks = slice(A_HEADS * A_DK + hd * A_DK, A_HEADS * A_DK + (hd + 1) * A_DK)
        oa_ref[:, qs] = _l2norm(conv[:, qs]) * (A_DK ** -0.5)
        oa_ref[:, ks] = _l2norm(conv[:, ks])
    oa_ref[:, 2 * A_HEADS * A_DK:CONV_DIM] = conv[:, 2 * A_HEADS * A_DK:]

    @pl.when(i == pl.num_programs(1) - 1)
    def _():
        tail_ref[...] = raw[tm - 8:tm]


def _norm_proj_conv(x, norm_w, w_p, w_conv, conv0, tm):
    b, t, _ = x.shape
    assert t % tm == 0 and tm % 8 == 0
    once = pl.Buffered(1)
    return pl.pallas_call(
        _norm_proj_conv_kernel,
        grid=(b, t // tm),
        in_specs=[pl.BlockSpec((None, tm, D_MODEL), lambda i, j: (i, j, 0)),
                  pl.BlockSpec((1, D_MODEL), lambda i, j: (0, 0), pipeline_mode=once)] + _weight_specs() + [
                  pl.BlockSpec((CONV_W, CONV_DIM), lambda i, j: (0, 0), pipeline_mode=once),
                  pl.BlockSpec((None, 8, CONV_DIM), lambda i, j: (i, 0, 0))],
        out_specs=[pl.BlockSpec((None, tm, COLS_A), lambda i, j: (i, j, 0)),
                   pl.BlockSpec((None, tm, COLS_B), lambda i, j: (i, j, 0)),
                   pl.BlockSpec((None, 8, CONV_DIM), lambda i, j: (i, 0, 0))],
        out_shape=[jax.ShapeDtypeStruct((b, t, COLS_A), F32),
                   jax.ShapeDtypeStruct((b, t, COLS_B), F32),
                   jax.ShapeDtypeStruct((b, 8, CONV_DIM), F32)],
        scratch_shapes=[pltpu.VMEM((8, CONV_DIM), F32)],
        compiler_params=pltpu.CompilerParams(dimension_semantics=("arbitrary", "arbitrary"),
                                             vmem_limit_bytes=VMEM_LIMIT),
        name="norm_proj_conv",
    )(x, norm_w, *w_p, w_conv, conv0)


def _gdn_kernel(xa_ref, z_ref, gt_ref, conv0_ref, s0_ref, wconv_ref, gp_ref, nwa_ref,
                o_ref, snew_ref, ext_ref, s_ref, *, rows):
    n = pl.program_id(1)
    c = CHUNK
    n_chunks = -(-rows // c)
    rp = n_chunks * c

    @pl.when(n == 0)
    def _():
        ext_ref[0:8, :] = conv0_ref[...]
        s_ref[...] = s0_ref[...]

    ext_ref[8:8 + rows, :] = xa_ref[...]
    if rows < rp:
        ext_ref[8 + rows:8 + rp, :] = jnp.zeros((rp - rows, CONV_DIM), F32)

    conv = ext_ref[5:5 + rp, :] * wconv_ref[0:1, :]
    for i in range(1, CONV_W):
        conv = conv + ext_ref[5 + i:5 + i + rp, :] * wconv_ref[i:i + 1, :]
    conv = _silu(conv)
    ext_ref[0:8, :] = ext_ref[rp:rp + 8, :]

    row = lax.broadcasted_iota(jnp.int32, (rp, 1), 0)
    live = row < rows
    if rows < rp:
        conv = jnp.where(live, conv, 0.0)

    gt = gt_ref[...]
    if rows < rp:
        gt = jnp.concatenate([gt, jnp.zeros((rp - rows, GATE_PAD), F32)], axis=0)
    beta_all = _sigmoid(gt)
    g_all = -jnp.exp(gp_ref[0:1, :]) * _softplus(gt + gp_ref[1:2, :])
    lane = lax.broadcasted_iota(jnp.int32, (rp, GATE_PAD), 1)
    g_all = jnp.where((lane >= A_HEADS) & (lane < 2 * A_HEADS) & live, g_all, 0.0)
    beta_all = jnp.where(live, beta_all, 0.0)

    ii = lax.broadcasted_iota(jnp.int32, (c, c), 0)
    jj = lax.broadcasted_iota(jnp.int32, (c, c), 1)
    incl = ii >= jj
    strict = ii > jj
    tril = jnp.where(incl, 1.0, 0.0).astype(BF16)
    eye = jnp.where(ii == jj, 1.0, 0.0).astype(F32)
    dd = functools.partial(jnp.dot, preferred_element_type=F32)

    def exact_tril_dot(x):
        hi, lo = _split2(x)
        lo2 = (x - hi.astype(F32) - lo.astype(F32)).astype(BF16)
        return dd(tril, hi) + dd(tril, lo) + dd(tril, lo2)

    z = z_ref[...]
    nwa = nwa_ref[...]
    chains = [(ci, h) for ci in range(n_chunks) for h in range(A_HEADS)]
    gc_alls = [exact_tril_dot(g_all[ci * c:(ci + 1) * c]) for ci in range(n_chunks)]
    qs, ks, kbs, rhss, gcs, gls = [], [], [], [], [], []
    for ci, h in chains:
        rs = slice(ci * c, (ci + 1) * c)
        q = conv[rs, h * A_DK:(h + 1) * A_DK]
        k = conv[rs, A_HEADS * A_DK + h * A_DK:A_HEADS * A_DK + (h + 1) * A_DK]
        v = conv[rs, 2 * A_HEADS * A_DK + h * A_DV:2 * A_HEADS * A_DK + (h + 1) * A_DV]
        q = q * lax.rsqrt(jnp.sum(q * q, axis=-1, keepdims=True) + EPS) * (A_DK ** -0.5)
        k = k * lax.rsqrt(jnp.sum(k * k, axis=-1, keepdims=True) + EPS)
        beta = beta_all[rs, h:h + 1]
        gc = gc_alls[ci][:, A_HEADS + h:A_HEADS + h + 1]
        kb = k * beta
        qs.append(q)
        ks.append(k)
        kbs.append(kb)
        rhss.append(jnp.concatenate([v * beta, kb * jnp.exp(gc)], axis=1))
        gcs.append(gc)
        gls.append(gc_alls[ci][c - 1:c, A_HEADS + h:A_HEADS + h + 1])
    e_mats = [exact_tril_dot(jnp.where(strict, g_all[ci * c:(ci + 1) * c, A_HEADS + h:A_HEADS + h + 1], 0.0))
              for ci, h in chains]
    decays = [jnp.where(incl, jnp.exp(jnp.where(incl, e, 0.0)), 0.0) for e in e_mats]
    kks = [_dot_nt(kb, k) for kb, k in zip(kbs, ks)]
    qks = [_dot_nt(q, k) for q, k in zip(qs, ks)]
    pws = [jnp.where(strict, a * d, 0.0) for a, d in zip(kks, decays)]
    qks = [jnp.where(incl, a * d, 0.0) for a, d in zip(qks, decays)]
    t_invs = [eye - lw for lw in pws]
    for _ in range(c.bit_length() - 2):
        pws = [_dot3(pw, pw) for pw in pws]
        t_invs = [ti + _dot3(ti, pw) for ti, pw in zip(t_invs, pws)]
    uws = [_dot(ti, rhs) for ti, rhs in zip(t_invs, rhss)]

    states = [s_ref[h] for h in range(A_HEADS)]
    for ci in range(n_chunks):
        idx = [ci * A_HEADS + h for h in range(A_HEADS)]
        ws = [_dot(uws[i][:, A_DV:], states[h]) for h, i in enumerate(idx)]
        os = [_dot(qs[i] * jnp.exp(gcs[i]), states[h]) for h, i in enumerate(idx)]
        es = [uws[i][:, :A_DV] - w for i, w in zip(idx, ws)]
        os = [o + _dot(qks[i], e) for o, i, e in zip(os, idx, es)]
        upd = [_dot_tn(ks[i] * jnp.exp(gls[i] - gcs[i]), e) for i, e in zip(idx, es)]
        states = [s * jnp.exp(gls[i]) + d for s, i, d in zip(states, idx, upd)]
        r0 = ci * c
        r1 = min(rows, r0 + c)
        for h, o in enumerate(os):
            o = o if r1 - r0 == c else o[:r1 - r0]
            o_ref[r0:r1, h * A_DV:(h + 1) * A_DV] = (
                _rms(o, nwa) * _silu(z[r0:r1, h * A_DV:(h + 1) * A_DV]))
    for h in range(A_HEADS):
        s_ref[h] = states[h]

    @pl.when(n == pl.num_programs(1) - 1)
    def _():
        snew_ref[...] = s_ref[...]


GDN_ROWS = 8 * CHUNK


def _gdn(proj, conv0, s0, w_conv, gate_par, norm_a):
    b, t, _ = proj.shape
    rows = min(GDN_ROWS, t)
    assert t % rows == 0 and (rows % CHUNK == 0 or rows == t)
    n = t // rows
    kernel = functools.partial(_gdn_kernel, rows=rows)
    return pl.pallas_call(
        kernel,
        grid=(b, n),
        in_specs=[pl.BlockSpec((None, rows, CONV_DIM), lambda i, j: (i, j, 0)),
                  pl.BlockSpec((None, rows, A_WIDTH), lambda i, j: (i, j, COL_Z // A_WIDTH)),
                  pl.BlockSpec((None, rows, GATE_PAD), lambda i, j: (i, j, COL_G // GATE_PAD)),
                  pl.BlockSpec((None, 8, CONV_DIM), lambda i, j: (i, 0, 0)),
                  pl.BlockSpec((None, A_HEADS, A_DK, A_DV), lambda i, j: (i, 0, 0, 0)),
                  pl.BlockSpec((CONV_W, CONV_DIM), lambda i, j: (0, 0)),
                  pl.BlockSpec((8, GATE_PAD), lambda i, j: (0, 0)),
                  pl.BlockSpec((1, A_DV), lambda i, j: (0, 0))],
        out_specs=[pl.BlockSpec((None, rows, A_WIDTH), lambda i, j: (i, j, 0)),
                   pl.BlockSpec((None, A_HEADS, A_DK, A_DV), lambda i, j: (i, 0, 0, 0))],
        out_shape=[jax.ShapeDtypeStruct((b, t, A_WIDTH), F32),
                   jax.ShapeDtypeStruct((b, A_HEADS, A_DK, A_DV), F32)],
        scratch_shapes=[pltpu.VMEM((8 + -(-rows // CHUNK) * CHUNK, CONV_DIM), F32),
                        pltpu.VMEM((A_HEADS, A_DK, A_DV), F32)],
        compiler_params=pltpu.CompilerParams(dimension_semantics=("arbitrary", "arbitrary"),
                                             vmem_limit_bytes=VMEM_LIMIT),
        name="gdn",
    )(proj, proj, proj, conv0, s0, w_conv, gate_par, norm_a)


HALF = 128 // CHUNK


def _gdn_pair_kernel(xa_ref, z_ref, gt_ref, conv0_ref, s0_ref, wconv_ref, gp_ref, nwa_ref,
                     o_ref, snew_ref, ext_ref, s_ref, *, rows, nb, conv_done, side_work=None):
    n = pl.program_id(1)
    c = CHUNK
    n_chunks = -(-rows // c)
    rp = n_chunks * c
    assert A_HEADS % 2 == 0 and HALF == 2 and A_DK == 128 and A_DV == 128

    @pl.when(n == 0)
    def _():
        ext_ref[:, 0:8, :] = conv0_ref[...]
        s_ref[...] = s0_ref[...]

    if side_work is not None:
        side_work()

    row = lax.broadcasted_iota(jnp.int32, (rp, 1), 0)
    live = row < rows
    lane_g = lax.broadcasted_iota(jnp.int32, (rp, GATE_PAD), 1)
    gate_lanes = (lane_g >= A_HEADS) & (lane_g < 2 * A_HEADS) & live

    ii = lax.broadcasted_iota(jnp.int32, (c, 128), 0)
    ll = lax.broadcasted_iota(jnp.int32, (c, 128), 1)
    jj = ll & (c - 1)
    low = ll < c
    incl = ii >= jj
    strict = ii > jj
    eye = jnp.where(ii == jj, 1.0, 0.0).astype(F32)
    ti = lax.broadcasted_iota(jnp.int32, (c, c), 0)
    tj = lax.broadcasted_iota(jnp.int32, (c, c), 1)
    tril = jnp.where(ti >= tj, 1.0, 0.0).astype(BF16)
    dd = functools.partial(jnp.dot, preferred_element_type=F32)

    def exact_tril_dot(x):
        hi, lo = _split2(x)
        lo2 = (x - hi.astype(F32) - lo.astype(F32)).astype(BF16)
        return dd(tril, hi) + dd(tril, lo) + dd(tril, lo2)

    def bd(x):
        return jnp.concatenate([jnp.where(low, x, 0.0), jnp.where(low, 0.0, x)], axis=0).astype(BF16)

    def unstack(y):
        return jnp.where(low, y[:c], y[c:])

    nwa = nwa_ref[...]
    chains = []
    for bi in range(nb):
        if conv_done:
            assert rows == rp
            conv = xa_ref[bi]
        else:
            ext_ref[bi, 8:8 + rows, :] = xa_ref[bi]
            if rows < rp:
                ext_ref[bi, 8 + rows:8 + rp, :] = jnp.zeros((rp - rows, CONV_DIM), F32)
            conv = _causal_conv_silu(ext_ref[bi, 0:8 + rp, :], wconv_ref, rp)
            ext_ref[bi, 0:8, :] = ext_ref[bi, rp:rp + 8, :]
            if rows < rp:
                conv = jnp.where(live, conv, 0.0)

        gt = gt_ref[bi]
        if rows < rp:
            gt = jnp.concatenate([gt, jnp.zeros((rp - rows, GATE_PAD), F32)], axis=0)
        beta_all = jnp.where(live, _sigmoid(gt), 0.0)
        g_all = jnp.where(gate_lanes, -jnp.exp(gp_ref[0:1, :]) * _softplus(gt + gp_ref[1:2, :]), 0.0)

        for ci in range(n_chunks):
            rs = slice(ci * c, (ci + 1) * c)
            gc_all = exact_tril_dot(g_all[rs])
            for p in range(A_HEADS // 2):
                hd = []
                for h in (2 * p, 2 * p + 1):
                    q = conv[rs, h * A_DK:(h + 1) * A_DK]
                    k = conv[rs, A_HEADS * A_DK + h * A_DK:A_HEADS * A_DK + (h + 1) * A_DK]
                    v = conv[rs, 2 * A_HEADS * A_DK + h * A_DV:2 * A_HEADS * A_DK + (h + 1) * A_DV]
                    if not conv_done:
                        q = _l2norm(q) * (A_DK ** -0.5)
                        k = _l2norm(k)
                    beta = beta_all[rs, h:h + 1]
                    g = g_all[rs, A_HEADS + h:A_HEADS + h + 1]
                    gc = gc_all[:, A_HEADS + h:A_HEADS + h + 1]
                    gl = gc_all[c - 1:c, A_HEADS + h:A_HEADS + h + 1]
                    kb = k * beta
                    hd.append(dict(q=q, k=k, kb=kb, g=g, qg=q * jnp.exp(gc), kd=k * jnp.exp(gl - gc),
                                   dl=jnp.exp(gl),
                                   rhs=jnp.concatenate([v * beta, kb * jnp.exp(gc)], axis=1)))
                chains.append(dict(bi=bi, ci=ci, p=p, hd=hd))

    for ch in chains:
        a, b = ch["hd"]
        ch["e_mat"] = exact_tril_dot(jnp.where(strict, jnp.where(low, a["g"], b["g"]), 0.0))
    for ch in chains:
        a, b = ch["hd"]
        k_st = jnp.concatenate([a["k"], b["k"]], axis=0)
        ch["kk"] = _dot_nt(jnp.concatenate([a["kb"], b["kb"]], axis=0), k_st)
        ch["qk"] = _dot_nt(jnp.concatenate([a["q"], b["q"]], axis=0), k_st)
    for ch in chains:
        decay = jnp.where(incl, jnp.exp(jnp.where(incl, ch["e_mat"], 0.0)), 0.0)
        ch["lower"] = jnp.where(strict, unstack(ch["kk"]) * decay, 0.0)
        ch["qk"] = jnp.where(incl, unstack(ch["qk"]) * decay, 0.0)
        ch["x"] = ch["lower"]
        ch["xb"] = bd(ch["lower"])
        ch["t"] = eye - ch["lower"]
    for _ in range(c.bit_length() - 2):
        for ch in chains:
            ch["x"] = jnp.dot(ch["x"].astype(BF16), ch["xb"], preferred_element_type=F32)
        for ch in chains:
            ch["xb"] = bd(ch["x"])
            ch["t"] = ch["t"] + jnp.dot(ch["t"].astype(BF16), ch["xb"], preferred_element_type=F32)
    for ch in chains:
        lh, ll_ = _split2(ch["lower"])
        th = ch["t"].astype(BF16).astype(F32)
        tl = ch["t"] - th
        bth = bd(th)
        lt = dd(lh, bth) + dd(lh, bd(tl)) + dd(ll_, bth)
        ch["res"] = eye - ch["t"] - lt
    for ch in chains:
        ch["t"] = ch["t"] + jnp.dot(ch["t"].astype(BF16), bd(ch["res"]), preferred_element_type=F32)
    for ch in chains:
        a, b = ch["hd"]
        uw = jnp.dot(bd(ch["t"]), jnp.concatenate([a["rhs"], b["rhs"]], axis=0).astype(BF16),
                     preferred_element_type=F32)
        a["u"], a["w"] = uw[:c, :A_DV], uw[:c, A_DV:]
        b["u"], b["w"] = uw[c:, :A_DV], uw[c:, A_DV:]
        ch["qkb"] = bd(ch["qk"])

    by_key = {(ch["bi"], ch["ci"], ch["p"]): ch for ch in chains}
    states = {(bi, h): s_ref[bi, h] for bi in range(nb) for h in range(A_HEADS)}
    for ci in range(n_chunks):
        cur = [by_key[(bi, ci, p)] for bi in range(nb) for p in range(A_HEADS // 2)]
        for ch in cur:
            for hh, d in enumerate(ch["hd"]):
                ws = _dot(jnp.concatenate([d["w"], d["qg"]], axis=0), states[(ch["bi"], 2 * ch["p"] + hh)])
                d["e"] = d["u"] - ws[:c]
                d["o"] = ws[c:]
        for ch in cur:
            a, b = ch["hd"]
            o2 = jnp.dot(ch["qkb"], jnp.concatenate([a["e"], b["e"]], axis=0).astype(BF16),
                         preferred_element_type=F32)
            a["o"] = a["o"] + o2[:c]
            b["o"] = b["o"] + o2[c:]
        for ch in cur:
            for hh, d in enumerate(ch["hd"]):
                key = (ch["bi"], 2 * ch["p"] + hh)
                states[key] = states[key] * d["dl"] + _dot_tn(d["kd"], d["e"])
        r0 = ci * c
        r1 = min(rows, r0 + c)
        for ch in cur:
            for hh, d in enumerate(ch["hd"]):
                h = 2 * ch["p"] + hh
                o = d["o"] if r1 - r0 == c else d["o"][:r1 - r0]
                o_ref[ch["bi"], r0:r1, h * A_DV:(h + 1) * A_DV] = (
                    _rms(o, nwa) * _silu(z_ref[ch["bi"], r0:r1, h * A_DV:(h + 1) * A_DV]))
    for (bi, h), s in states.items():
        s_ref[bi, h] = s

    @pl.when(n == pl.num_programs(1) - 1)
    def _():
        snew_ref[...] = s_ref[...]


def _gdn_pair(proj, conv0, s0, w_conv, gate_par, norm_a, conv_done, side_of_grid=None):
    b, t, _ = proj.shape
    rows = min(GDN_ROWS, t)
    assert t % rows == 0 and (rows % CHUNK == 0 or rows == t)
    n = t // rows
    nb = max(1, min(b, GDN_ROWS // (-(-rows // CHUNK) * CHUNK)))
    assert b % nb == 0
    rp = 0 if conv_done else -(-rows // CHUNK) * CHUNK
    in_specs = [pl.BlockSpec((nb, rows, CONV_DIM), lambda i, j: (i, j, 0)),
                pl.BlockSpec((nb, rows, A_WIDTH), lambda i, j: (i, j, COL_Z // A_WIDTH)),
                pl.BlockSpec((nb, rows, GATE_PAD), lambda i, j: (i, j, COL_G // GATE_PAD)),
                pl.BlockSpec((nb, 8, CONV_DIM), lambda i, j: (i, 0, 0)),
                pl.BlockSpec((nb, A_HEADS, A_DK, A_DV), lambda i, j: (i, 0, 0, 0)),
                pl.BlockSpec((CONV_W, CONV_DIM), lambda i, j: (0, 0)),
                pl.BlockSpec((8, GATE_PAD), lambda i, j: (0, 0)),
                pl.BlockSpec((1, A_DV), lambda i, j: (0, 0))]
    out_specs = [pl.BlockSpec((nb, rows, A_WIDTH), lambda i, j: (i, j, 0)),
                 pl.BlockSpec((nb, A_HEADS, A_DK, A_DV), lambda i, j: (i, 0, 0, 0))]
    out_shape = [jax.ShapeDtypeStruct((b, t, A_WIDTH), F32),
                 jax.ShapeDtypeStruct((b, A_HEADS, A_DK, A_DV), F32)]
    inputs = (proj, proj, proj, conv0, s0, w_conv, gate_par, norm_a)
    n_in, n_out = len(in_specs), len(out_specs)
    side = side_of_grid((b // nb, n)) if side_of_grid is not None else None

    def kernel(*refs):
        n_side_in = len(side["in_specs"]) if side else 0
        gin, sin = refs[:n_in], refs[n_in:n_in + n_side_in]
        gout = refs[n_in + n_side_in:n_in + n_side_in + n_out]
        sout = refs[n_in + n_side_in + n_out:len(refs) - 2]
        side_work = (lambda: side["kernel"](*sin, *sout)) if side else None
        _gdn_pair_kernel(*gin, *gout, *refs[len(refs) - 2:], rows=rows, nb=nb, conv_done=conv_done,
                         side_work=side_work)

    if side:
        inputs = inputs + tuple(side["inputs"])
        in_specs = in_specs + list(side["in_specs"])
        out_specs = out_specs + list(side["out_specs"])
        out_shape = out_shape + list(side["out_shape"])
    outs = pl.pallas_call(
        kernel,
        grid=(b // nb, n),
        in_specs=in_specs,
        out_specs=out_specs,
        out_shape=out_shape,
        scratch_shapes=[pltpu.VMEM((nb, 8 + rp, CONV_DIM), F32),
                        pltpu.VMEM((nb, A_HEADS, A_DK, A_DV), F32)],
        compiler_params=pltpu.CompilerParams(dimension_semantics=("arbitrary", "arbitrary"),
                                             vmem_limit_bytes=VMEM_LIMIT),
        name="gdn_side" if side else "gdn",
    )(*inputs)
    return outs[:n_out], outs[n_out:]


def _slope(h):
    return 2.0 ** (-8.0 * (h + 1) / B_HEADS)


ATTN_SB = MAX_WINDOW
ATTN_GROUP = 16
PAIR = 128 // B_HEAD_DIM


def _attn_fused_kernel(q_ref, kp_ref, kc_ref, vp_ref, vc_ref, nwb_ref, o_ref, kt_ref, vt_ref, kbuf, vbuf, *stats):
    hp = pl.program_id(1)
    n = pl.program_id(2)
    sb = ATTN_SB
    blk = STEPS
    kbuf[0:sb, :] = kp_ref[...]
    kbuf[sb:2 * sb, :] = kc_ref[...]
    vbuf[0:sb, :] = vp_ref[...]
    vbuf[sb:2 * sb, :] = vc_ref[...]

    lane_q = lax.broadcasted_iota(jnp.int32, (blk, 128), 1)
    low_q = lane_q < B_HEAD_DIM
    qq = lax.broadcasted_iota(jnp.int32, (2 * blk, 2 * blk), 0)
    kk = lax.broadcasted_iota(jnp.int32, (2 * blk, 2 * blk), 1)
    steps_back = (qq & (blk - 1)) + blk - kk
    in_band = (steps_back >= 0) & (steps_back <= STEPS)
    slope_lo = jnp.float32(_slope(0))
    slope_hi = jnp.float32(_slope(1))
    for i in range(1, B_HEADS // PAIR):
        slope_lo = jnp.where(hp == i, _slope(PAIR * i), slope_lo)
        slope_hi = jnp.where(hp == i, _slope(PAIR * i + 1), slope_hi)
    slope = jnp.where(qq < blk, slope_lo, slope_hi)

    for bi, (_, dil) in enumerate(DILATED):
        m_s, l_s, num_s = stats[3 * bi:3 * bi + 3]
        bias = jnp.where(in_band, -slope * (steps_back * dil).astype(F32), NEG)
        bias_first = jnp.where(kk >= blk, bias, NEG)
        d_bits = dil.bit_length() - 1

        def rows(start, size):
            if dil == 1:
                return pl.ds(pl.multiple_of(start, blk), size)
            return pl.ds(start, size, stride=dil)

        def group(g, carry):
            tiles = []
            for u in range(ATTN_GROUP):
                it = g * ATTN_GROUP + u
                q0 = lax.shift_right_logical(it, d_bits) * (blk * dil) + (it & (dil - 1))
                k0 = sb + q0 - blk * dil
                q = q_ref[rows(q0, blk), :] * (B_HEAD_DIM ** -0.5)
                q2 = jnp.concatenate([jnp.where(low_q, q, 0.0), jnp.where(low_q, 0.0, q)], axis=0)
                tiles.append((q0, k0, q2.astype(BF16), kbuf[rows(k0, 2 * blk), :].astype(BF16),
                              vbuf[rows(k0, 2 * blk), :].astype(BF16)))
            scores = [lax.dot_general(q2, k, (((1,), (1,)), ((), ())), preferred_element_type=F32)
                      for (_, _, q2, k, _) in tiles]
            probs = []
            for (q0, k0, _, _, _), s in zip(tiles, scores):
                s = s + jnp.where((n == 0) & (q0 < blk * dil), bias_first, bias)
                m = jnp.max(s, axis=-1, keepdims=True)
                p = jnp.exp(s - m)
                probs.append((m, jnp.sum(p, axis=-1, keepdims=True), p.astype(BF16)))
            for (q0, _, _, _, v), (m, l, p) in zip(tiles, probs):
                pv = jnp.dot(p, v, preferred_element_type=F32)
                m_s[rows(q0, blk), :] = jnp.where(low_q, m[:blk], m[blk:])
                l_s[rows(q0, blk), :] = jnp.where(low_q, l[:blk], l[blk:])
                num_s[rows(q0, blk), :] = jnp.where(low_q, pv[:blk], pv[blk:])
            return carry

        lax.fori_loop(0, sb // (blk * ATTN_GROUP), group, 0)

    nwb = nwb_ref[...]

    def combine(i, carry):
        rs = pl.ds(pl.multiple_of(i * blk, blk), blk)
        ms = [stats[3 * bi][rs, :] for bi in range(len(DILATED))]
        ls = [stats[3 * bi + 1][rs, :] for bi in range(len(DILATED))]
        nums = [stats[3 * bi + 2][rs, :] for bi in range(len(DILATED))]
        mx = jnp.maximum(jnp.maximum(ms[0], ms[1]), ms[2])
        ws = [jnp.exp(m - mx) for m in ms]
        den = ws[0] * ls[0] + ws[1] * ls[1] + ws[2] * ls[2]
        o = (ws[0] * nums[0] + ws[1] * nums[1] + ws[2] * nums[2]) / den
        sq = o * o
        ss_lo = jnp.sum(jnp.where(low_q, sq, 0.0), axis=-1, keepdims=True)
        ss_hi = jnp.sum(jnp.where(low_q, 0.0, sq), axis=-1, keepdims=True)
        mean_sq = jnp.where(low_q, ss_lo, ss_hi) / B_HEAD_DIM
        o_ref[rs, :] = o * lax.rsqrt(mean_sq + EPS) * nwb
        return carry

    lax.fori_loop(0, sb // blk, combine, 0, unroll=4)

    @pl.when(n == pl.num_programs(2) - 1)
    def _():
        kt_ref[...] = kc_ref[...].T
        vt_ref[...] = vc_ref[...].T


def _attn_prompt(qkv, norm_b_pair):
    b, s, _ = qkv.shape
    sb = ATTN_SB
    assert s % sb == 0 and (sb // STEPS) % ATTN_GROUP == 0 and sb == min(MAX_WINDOW, s)
    assert all(w == STEPS * d and sb % w == 0 and d & (d - 1) == 0 for (w, d) in DILATED)
    n_pairs = B_HEADS // PAIR
    blk = (None, sb, 128)
    prev = lambda c0: pl.BlockSpec(blk, lambda i, p, n: (i, jnp.maximum(n - 1, 0), c0 + p))
    cur = lambda c0: pl.BlockSpec(blk, lambda i, p, n: (i, n, c0 + p))
    return pl.pallas_call(
        _attn_fused_kernel,
        grid=(b, n_pairs, s // sb),
        in_specs=[cur(0), prev(n_pairs), cur(n_pairs), prev(2 * n_pairs), cur(2 * n_pairs),
                  pl.BlockSpec((1, 128), lambda i, p, n: (0, 0))],
        out_specs=[pl.BlockSpec(blk, lambda i, p, n: (i, n, p)),
                   pl.BlockSpec((None, 128, sb), lambda i, p, n: (i, p, 0)),
                   pl.BlockSpec((None, 128, sb), lambda i, p, n: (i, p, 0))],
        out_shape=[jax.ShapeDtypeStruct((b, s, B_WIDTH), F32),
                   jax.ShapeDtypeStruct((b, B_WIDTH, sb), F32),
                   jax.ShapeDtypeStruct((b, B_WIDTH, sb), F32)],
        scratch_shapes=[pltpu.VMEM((2 * sb, 128), F32), pltpu.VMEM((2 * sb, 128), F32)]
                       + [pltpu.VMEM((sb, 128), F32)] * (3 * len(DILATED)),
        compiler_params=pltpu.CompilerParams(
            dimension_semantics=("arbitrary", "arbitrary", "arbitrary"), vmem_limit_bytes=VMEM_LIMIT),
        name="attn_prompt",
    )(qkv, qkv, qkv, qkv, qkv, norm_b_pair)


def _attn_sample_kernel(q_ref, kn_ref, vn_ref, kt_ref, vt_ref, nwb_ref, o_ref, kto_ref, vto_ref, *, n_past, t):
    kn = kn_ref[...]
    vn = vn_ref[...]
    kt = kt_ref[...]
    vt = vt_ref[...]

    prow = lax.broadcasted_iota(jnp.int32, (t, 128), 0)
    plane = lax.broadcasted_iota(jnp.int32, (t, 128), 1)
    place = jnp.where(plane == 128 - t + prow, 1.0, 0.0).astype(BF16)
    tail_lanes = lax.broadcasted_iota(jnp.int32, (B_WIDTH, 128), 1) >= 128 - t
    tn = functools.partial(lax.dot_general, dimension_numbers=(((0,), (0,)), ((), ())),
                           preferred_element_type=F32)

    def shifted(win, new, out_ref):
        hi, lo = _split2(new)
        lo2 = (new - hi.astype(F32) - lo.astype(F32)).astype(BF16)
        new_t = tn(hi, place) + tn(lo, place) + tn(lo2, place)
        sh = pltpu.roll(win, n_past - t, axis=1)
        out_ref[:, :n_past - 128] = sh[:, :n_past - 128]
        out_ref[:, n_past - 128:] = jnp.where(tail_lanes, new_t, sh[:, n_past - 128:])

    shifted(kt, kn, kto_ref)
    shifted(vt, vn, vto_ref)

    rq = B_HEADS * t
    t_bits = t.bit_length() - 1
    d_bits = B_HEAD_DIM.bit_length() - 1
    rowi = lax.broadcasted_iota(jnp.int32, (rq, B_WIDTH), 0)
    lanei = lax.broadcasted_iota(jnp.int32, (rq, B_WIDTH), 1)
    own = lax.shift_right_logical(rowi, t_bits) == lax.shift_right_logical(lanei, d_bits)
    q_rep = jnp.concatenate([q_ref[...]] * B_HEADS, axis=0)
    q_blk = jnp.where(own, q_rep, 0.0).astype(BF16)
    zpad = jnp.zeros((128 - t, B_WIDTH), BF16)
    kn_pad = jnp.concatenate([kn.astype(BF16), zpad], axis=0)
    vn_pad = jnp.concatenate([vn.astype(BF16), zpad], axis=0)
    vt16 = vt.astype(BF16)
    scale = B_HEAD_DIM ** -0.5
    s_c = jnp.dot(q_blk, kt.astype(BF16), preferred_element_type=F32) * scale
    s_n = lax.dot_general(q_blk, kn_pad, (((1,), (1,)), ((), ())), preferred_element_type=F32) * scale

    r1 = lax.broadcasted_iota(jnp.int32, (rq, 1), 0)
    head = lax.shift_right_logical(r1, t_bits)
    slope = jnp.zeros((rq, 1), F32)
    for h in range(B_HEADS):
        slope = jnp.where(head == h, _slope(h), slope)
    tok = r1 & (t - 1)
    delta_c = n_past + tok - lax.broadcasted_iota(jnp.int32, (rq, n_past), 1)
    key_n = lax.broadcasted_iota(jnp.int32, (rq, 128), 1)
    delta_n = tok - key_n
    s_c = s_c - slope * delta_c.astype(F32)
    s_n = s_n - slope * delta_n.astype(F32)
    live_n = (key_n < t) & (delta_n >= 0)

    ms, ls, nums = [], [], []
    for (_, dil) in DILATED:
        valid_c = ((delta_c & (dil - 1)) == 0) & (delta_c <= STEPS * dil)
        valid_n = live_n & ((delta_n & (dil - 1)) == 0) & (delta_n <= STEPS * dil)
        sd_c = jnp.where(valid_c, s_c, NEG)
        sd_n = jnp.where(valid_n, s_n, NEG)
        m = jnp.maximum(jnp.max(sd_c, axis=-1, keepdims=True), jnp.max(sd_n, axis=-1, keepdims=True))
        p_c = jnp.exp(sd_c - m)
        p_n = jnp.exp(sd_n - m)
        ms.append(m)
        ls.append(jnp.sum(p_c, axis=-1, keepdims=True) + jnp.sum(p_n, axis=-1, keepdims=True))
        nums.append(lax.dot_general(p_c.astype(BF16), vt16, (((1,), (1,)), ((), ())), preferred_element_type=F32)
                    + jnp.dot(p_n.astype(BF16), vn_pad, preferred_element_type=F32))
    mx = jnp.maximum(jnp.maximum(ms[0], ms[1]), ms[2])
    ws = [jnp.exp(m - mx) for m in ms]
    den = ws[0] * ls[0] + ws[1] * ls[1] + ws[2] * ls[2]
    o = (ws[0] * nums[0] + ws[1] * nums[1] + ws[2] * nums[2]) / den
    o = jnp.where(own, o, 0.0)
    o = o * lax.rsqrt(jnp.sum(o * o, axis=-1, keepdims=True) / B_HEAD_DIM + EPS) * nwb_ref[...]
    acc = o[0:t, :]
    for h in range(1, B_HEADS):
        acc = acc + o[h * t:(h + 1) * t, :]
    o_ref[...] = acc


def _attn_sample_parts(qkv, win_kt, win_vt, norm_b_tiled, seq_of_step):
    b, t, _ = qkv.shape
    n_past = win_kt.shape[2]
    assert n_past == MAX_WINDOW and n_past % 128 == 0 and t % 8 == 0 and t & (t - 1) == 0 and t <= 128
    assert all(d & (d - 1) == 0 for (_, d) in DILATED)
    col = lambda c: pl.BlockSpec((None, t, B_WIDTH), lambda *g: (seq_of_step(*g), 0, c))
    win = pl.BlockSpec((None, B_WIDTH, n_past), lambda *g: (seq_of_step(*g), 0, 0))
    return dict(
        kernel=functools.partial(_attn_sample_kernel, n_past=n_past, t=t),
        inputs=(qkv, qkv, qkv, win_kt, win_vt, norm_b_tiled),
        in_specs=[col(0), col(1), col(2), win, win, pl.BlockSpec((1, B_WIDTH), lambda *g: (0, 0))],
        out_specs=[col(0), win, win],
        out_shape=[jax.ShapeDtypeStruct((b, t, B_WIDTH), F32),
                   jax.ShapeDtypeStruct((b, B_WIDTH, n_past), F32),
                   jax.ShapeDtypeStruct((b, B_WIDTH, n_past), F32)])


def _attn_sample(qkv, win_kt, win_vt, norm_b_tiled):
    parts = _attn_sample_parts(qkv, win_kt, win_vt, norm_b_tiled, lambda i: i)
    return pl.pallas_call(
        parts["kernel"],
        grid=(qkv.shape[0],),
        in_specs=parts["in_specs"],
        out_specs=parts["out_specs"],
        out_shape=parts["out_shape"],
        compiler_params=pltpu.CompilerParams(dimension_semantics=("arbitrary",),
                                             vmem_limit_bytes=VMEM_LIMIT),
        name="attn_sample",
    )(*parts["inputs"])


def _out_ffn_kernel(x_ref, oa_ref, ob_ref, wo_ref, nf_ref, wg_ref, wu_ref, wd_ref, nfin_ref, y_ref, *, ff_chunk):
    mixed = jnp.concatenate([oa_ref[...], ob_ref[...]], axis=-1).astype(BF16)
    x1 = x_ref[...] + jnp.dot(mixed, wo_ref[...], preferred_element_type=F32)
    hf = _rms(x1, nf_ref[...]).astype(BF16)
    x2 = x1
    for c0 in range(0, wg_ref.shape[1], ff_chunk):
        g = jnp.dot(hf, wg_ref[:, c0:c0 + ff_chunk], preferred_element_type=F32)
        u = jnp.dot(hf, wu_ref[:, c0:c0 + ff_chunk], preferred_element_type=F32)
        act = (_silu(g) * u).astype(BF16)
        x2 = x2 + jnp.dot(act, wd_ref[c0:c0 + ff_chunk, :], preferred_element_type=F32)
    y_ref[...] = _rms(x2, nfin_ref[...])


def _out_ffn(x2d, o_a, o_b, w_out, norm_ffn, w_gate, w_up, w_down, norm_final, tm):
    t = x2d.shape[0]
    d_ff = w_gate.shape[1]
    ff_chunk = d_ff // 2 if d_ff % 256 == 0 else d_ff
    once = pl.Buffered(1)
    row = lambda w: pl.BlockSpec((tm, w), lambda i: (i, 0))
    full = lambda a, b: pl.BlockSpec((a, b), lambda i: (0, 0), pipeline_mode=once)
    return pl.pallas_call(
        functools.partial(_out_ffn_kernel, ff_chunk=ff_chunk),
        grid=(t // tm,),
        in_specs=[row(D_MODEL), row(A_WIDTH), row(B_WIDTH),
                  full(D_MODEL, D_MODEL), full(1, D_MODEL),
                  full(D_MODEL, d_ff), full(D_MODEL, d_ff), full(d_ff, D_MODEL), full(1, D_MODEL)],
        out_specs=row(D_MODEL),
        out_shape=jax.ShapeDtypeStruct((t, D_MODEL), F32),
        compiler_params=pltpu.CompilerParams(dimension_semantics=("arbitrary",),
                                             vmem_limit_bytes=VMEM_LIMIT),
        name="out_ffn",
    )(x2d, o_a, o_b, w_out, norm_ffn, w_gate, w_up, w_down, norm_final)


def _layer_params(norm_mix, w_in, w_conv, a_log, dt_bias, norm_out_a, norm_out_b, w_out, norm_ffn,
                  w_gate, w_up, w_down, layer):
    w = w_in[layer]
    n_gate = 2 * A_HEADS
    w_p = (w[:, :COL_G].astype(BF16),
           jnp.pad(w[:, COL_G:COL_G + n_gate], ((0, 0), (0, GATE_PAD - n_gate))).astype(BF16),
           w[:, COL_G + n_gate:].astype(BF16))
    gate_par = jnp.zeros((8, GATE_PAD), F32)
    gate_par = gate_par.at[0, A_HEADS:2 * A_HEADS].set(a_log[layer].astype(F32))
    gate_par = gate_par.at[1, A_HEADS:2 * A_HEADS].set(dt_bias[layer].astype(F32))
    return dict(
        norm_mix=norm_mix[layer].reshape(1, D_MODEL), w_p=w_p, w_conv=w_conv[layer], gate_par=gate_par,
        norm_a=norm_out_a[layer].reshape(1, A_DV),
        norm_b_pair=jnp.tile(norm_out_b[layer], PAIR).reshape(1, PAIR * B_HEAD_DIM),
        norm_b_tiled=jnp.tile(norm_out_b[layer], B_HEADS).reshape(1, B_WIDTH),
        w_out=w_out[layer].astype(BF16), norm_ffn=norm_ffn[layer].reshape(1, D_MODEL),
        w_gate=w_gate[layer].astype(BF16), w_up=w_up[layer].astype(BF16), w_down=w_down[layer].astype(BF16))


def _project(x, conv_buf, p, tm):
    b, t, _ = x.shape
    assert t >= CONV_W - 1
    conv0 = jnp.pad(conv_buf, ((0, 0), (8 - (CONV_W - 1), 0), (0, 0)))
    conv_done = t % tm == 0
    if conv_done:
        proj_a, proj_b, tail = _norm_proj_conv(x, p["norm_mix"], p["w_p"], p["w_conv"], conv0, tm)
        new_conv = tail[:, 8 - (CONV_W - 1):]
    else:
        proj_a, proj_b = _norm_proj(x.reshape(b * t, D_MODEL), p["norm_mix"], p["w_p"], tm)
        proj_a = proj_a.reshape(b, t, COLS_A)
        proj_b = proj_b.reshape(b, t, COLS_B)
        new_conv = proj_a[:, t - (CONV_W - 1):, :CONV_DIM]
    return proj_a, proj_b, new_conv, conv0, conv_done


def _finish(x, o_a, o_b, p, norm_final, tm):
    b, t, _ = x.shape
    y = _out_ffn(x.reshape(b * t, D_MODEL), o_a.reshape(b * t, A_WIDTH), o_b.reshape(b * t, B_WIDTH), p["w_out"],
                 p["norm_ffn"], p["w_gate"], p["w_up"], p["w_down"], norm_final.reshape(1, D_MODEL), tm)
    return y.reshape(b, t, D_MODEL)


def kernel(x_prompt, x_sample, state_conv, state_rec, cache_win_k, cache_win_v, norm_mix, w_in, w_conv, a_log,
           dt_bias, norm_out_a, norm_out_b, w_out, norm_ffn, w_gate, w_up, w_down, norm_final):
    depth = w_in.shape[0]
    assert depth == 1, "the final norm is fused into the block of a single-layer trunk"
    p = _layer_params(norm_mix, w_in, w_conv, a_log, dt_bias, norm_out_a, norm_out_b, w_out, norm_ffn,
                      w_gate, w_up, w_down, 0)
    bp, tp, _ = x_prompt.shape
    bs, ts, _ = x_sample.shape
    tm_p, tm_s = 512, bs * ts
    gdn_args = (p["w_conv"], p["gate_par"], p["norm_a"])

    sa, sb, sc, s_conv0, s_done = _project(x_sample, state_conv[0], p, tm_s)
    n_past = cache_win_k.shape[2]
    to_t = lambda a: jnp.transpose(a, (0, 2, 3, 1)).reshape(bs, B_WIDTH, n_past)
    from_t = lambda a: jnp.transpose(a.reshape(a.shape[0], B_HEADS, B_HEAD_DIM, a.shape[2]), (0, 3, 1, 2))
    win_kt, win_vt = to_t(cache_win_k[0]), to_t(cache_win_v[0])

    def sample_attention_of_grid(grid):
        if grid[0] * grid[1] != bs:
            return None
        return _attn_sample_parts(sb, win_kt, win_vt, p["norm_b_tiled"], lambda i, j: i * grid[1] + j)

    pa, pb, pc, p_conv0, p_done = _project(x_prompt, jnp.zeros((bp, CONV_W - 1, CONV_DIM), F32), p, tm_p)
    zero_rec = jnp.zeros((bp, A_HEADS, A_DK, A_DV), F32)
    (o_a_p, pr), side = _gdn_pair(pa, p_conv0, zero_rec, *gdn_args, p_done, sample_attention_of_grid)
    o_b_s, new_kt, new_vt = side if side else _attn_sample(sb, win_kt, win_vt, p["norm_b_tiled"])
    (o_a_s, sr), _ = _gdn_pair(sa, s_conv0, state_rec[0], *gdn_args, s_done)

    o_b_p, pkt, pvt = _attn_prompt(pb, p["norm_b_pair"])

    yp = _finish(x_prompt, o_a_p, o_b_p, p, norm_final, tm_p)
    ys = _finish(x_sample, o_a_s, o_b_s, p, norm_final, tm_s)
    return (yp, ys, pc[None], pr[None], from_t(pkt)[None], from_t(pvt)[None], sc[None], sr[None],
            from_t(new_kt)[None], from_t(new_vt)[None])
```

```python
import functools

import jax
import jax.numpy as jnp
from jax import lax
from jax.experimental import pallas as pl
from jax.experimental.pallas import tpu as pltpu

F32 = jnp.float32
BF16 = jnp.bfloat16

D_MODEL = 1024
A_HEADS = 4
A_DK = 128
A_DV = 128
A_WIDTH = A_HEADS * A_DV
CONV_W = 4
CONV_DIM = 2 * A_HEADS * A_DK + A_HEADS * A_DV
B_HEADS = 8
B_HEAD_DIM = 64
B_WIDTH = B_HEADS * B_HEAD_DIM
DILATED = ((128, 1), (512, 4), (2048, 16))
STEPS = 128
MAX_WINDOW = 2048
EPS = 1e-6
NEG = -1e30
CHUNK = 64
GATE_PAD = 128
COL_Z = CONV_DIM
COL_G = CONV_DIM + A_WIDTH
COLS_A = COL_G + GATE_PAD
COLS_B = 3 * B_WIDTH
VMEM_LIMIT = 56 * 1024 * 1024


def _dot(a, b):
    return jnp.dot(a.astype(BF16), b.astype(BF16), preferred_element_type=F32)


def _dot_nt(a, b):
    return lax.dot_general(a.astype(BF16), b.astype(BF16), (((1,), (1,)), ((), ())),
                           preferred_element_type=F32)


def _dot_tn(a, b):
    return lax.dot_general(a.astype(BF16), b.astype(BF16), (((0,), (0,)), ((), ())),
                           preferred_element_type=F32)


def _split2(a):
    hi = a.astype(BF16)
    lo = (a - hi.astype(F32)).astype(BF16)
    return hi, lo


def _dot3(a, b):
    ah, al = _split2(a)
    bh, bl = _split2(b)
    d = functools.partial(jnp.dot, preferred_element_type=F32)
    return d(ah, bh) + d(ah, bl) + d(al, bh)


def _sigmoid(x):
    return 1.0 / (1.0 + jnp.exp(-x))


def _silu(x):
    return x * _sigmoid(x)


def _softplus(x):
    return jnp.maximum(x, 0.0) + jnp.log(1.0 + jnp.exp(-jnp.abs(x)))


def _rms(x, w):
    return x * lax.rsqrt(jnp.mean(x * x, axis=-1, keepdims=True) + EPS) * w


def _norm_proj_kernel(x_ref, nw_ref, wa_ref, wg_ref, wb_ref, oa_ref, ob_ref):
    h = _rms(x_ref[...], nw_ref[...]).astype(BF16)
    oa_ref[:, :COL_G] = jnp.dot(h, wa_ref[...], preferred_element_type=F32)
    oa_ref[:, COL_G:] = jnp.dot(h, wg_ref[...], preferred_element_type=F32)
    ob_ref[...] = jnp.dot(h, wb_ref[...], preferred_element_type=F32)


def _weight_specs():
    zero = lambda *g: (0, 0)
    return [pl.BlockSpec((D_MODEL, c), zero, pipeline_mode=pl.Buffered(1)) for c in (COL_G, GATE_PAD, COLS_B)]


def _norm_proj(x2d, norm_w, w_p, tm):
    t = x2d.shape[0]
    once = pl.Buffered(1)
    return pl.pallas_call(
        _norm_proj_kernel,
        grid=(t // tm,),
        in_specs=[pl.BlockSpec((tm, D_MODEL), lambda i: (i, 0)),
                  pl.BlockSpec((1, D_MODEL), lambda i: (0, 0), pipeline_mode=once)] + _weight_specs(),
        out_specs=[pl.BlockSpec((tm, COLS_A), lambda i: (i, 0)),
                   pl.BlockSpec((tm, COLS_B), lambda i: (i, 0))],
        out_shape=[jax.ShapeDtypeStruct((t, COLS_A), F32),
                   jax.ShapeDtypeStruct((t, COLS_B), F32)],
        compiler_params=pltpu.CompilerParams(dimension_semantics=("arbitrary",),
                                             vmem_limit_bytes=VMEM_LIMIT),
        name="norm_proj",
    )(x2d, norm_w, *w_p)


def _causal_conv_silu(x_all, wconv_ref, n):
    conv = x_all[8:8 + n] * wconv_ref[CONV_W - 1:CONV_W, :]
    for back in range(1, CONV_W):
        tap = CONV_W - 1 - back
        conv = conv + pltpu.roll(x_all, back, axis=0)[8:8 + n] * wconv_ref[tap:tap + 1, :]
    return _silu(conv)


def _l2norm(x):
    return x * lax.rsqrt(jnp.sum(x * x, axis=-1, keepdims=True) + EPS)


def _norm_proj_conv_kernel(x_ref, nw_ref, wa_ref, wg_ref, wb_ref, wconv_ref, conv0_ref, oa_ref, ob_ref, tail_ref,
                           carry_ref, h_ref, raw_ref):
    i = pl.program_id(1)
    tm = x_ref.shape[0]
    wq = A_HEADS * A_DK
    assert CONV_DIM == 3 * wq and COLS_B == 3 * B_WIDTH
    dd = functools.partial(jnp.dot, preferred_element_type=F32)

    @pl.when(i == 0)
    def _():
        carry_ref[...] = conv0_ref[...]

    def conv_cols(g):
        cs = slice(g * wq, (g + 1) * wq)
        conv = _causal_conv_silu(jnp.concatenate([carry_ref[:, cs], raw_ref[:, cs]], axis=0),
                                 wconv_ref.at[:, cs], tm)
        carry_ref[:, cs] = raw_ref[tm - 8:tm, cs]
        if g == 2:
            oa_ref[:, cs] = conv
            return
        for hd in range(A_HEADS):
            hs = slice(hd * A_DK, (hd + 1) * A_DK)
            out = _l2norm(conv[:, hs])
            oa_ref[:, g * wq + hd * A_DK:g * wq + (hd + 1) * A_DK] = out * (A_DK ** -0.5) if g == 0 else out

    def phase(k):
        if k == 0:
            h_ref[...] = _rms(x_ref[...], nw_ref[...]).astype(BF16)
        h = h_ref[...]
        if k < 3:
            raw_ref[:, k * wq:(k + 1) * wq] = dd(h, wa_ref[:, k * wq:(k + 1) * wq])
        if k == 0:
            oa_ref[:, CONV_DIM:COL_G] = dd(h, wa_ref[:, CONV_DIM:])
        else:
            ob_ref[:, (k - 1) * B_WIDTH:k * B_WIDTH] = dd(h, wb_ref[:, (k - 1) * B_WIDTH:k * B_WIDTH])
            conv_cols(k - 1)
        if k == 3:
            oa_ref[:, COL_G:] = dd(h, wg_ref[...])

    for k in range(4):
        pl.when(i >= 0)(functools.partial(phase, k))

    @pl.when(i == pl.num_programs(1) - 1)
    def _():
        tail_ref[...] = raw_ref[tm - 8:tm, :]


def _norm_proj_conv(x, norm_w, w_p, w_conv, conv0, tm):
    b, t, _ = x.shape
    assert t % tm == 0 and tm % 8 == 0
    once = pl.Buffered(1)
    return pl.pallas_call(
        _norm_proj_conv_kernel,
        grid=(b, t // tm),
        in_specs=[pl.BlockSpec((None, tm, D_MODEL), lambda i, j: (i, j, 0)),
                  pl.BlockSpec((1, D_MODEL), lambda i, j: (0, 0), pipeline_mode=once)] + _weight_specs() + [
                  pl.BlockSpec((CONV_W, CONV_DIM), lambda i, j: (0, 0), pipeline_mode=once),
                  pl.BlockSpec((None, 8, CONV_DIM), lambda i, j: (i, 0, 0))],
        out_specs=[pl.BlockSpec((None, tm, COLS_A), lambda i, j: (i, j, 0)),
                   pl.BlockSpec((None, tm, COLS_B), lambda i, j: (i, j, 0)),
                   pl.BlockSpec((None, 8, CONV_DIM), lambda i, j: (i, 0, 0))],
        out_shape=[jax.ShapeDtypeStruct((b, t, COLS_A), F32),
                   jax.ShapeDtypeStruct((b, t, COLS_B), F32),
                   jax.ShapeDtypeStruct((b, 8, CONV_DIM), F32)],
        scratch_shapes=[pltpu.VMEM((8, CONV_DIM), F32), pltpu.VMEM((tm, D_MODEL), BF16),
                        pltpu.VMEM((tm, CONV_DIM), F32)],
        compiler_params=pltpu.CompilerParams(dimension_semantics=("arbitrary", "arbitrary"),
                                             vmem_limit_bytes=VMEM_LIMIT),
        name="norm_proj_conv",
    )(x, norm_w, *w_p, w_conv, conv0)


def _gdn_kernel(xa_ref, z_ref, gt_ref, conv0_ref, s0_ref, wconv_ref, gp_ref, nwa_ref,
                o_ref, snew_ref, ext_ref, s_ref, *, rows):
    n = pl.program_id(1)
    c = CHUNK
    n_chunks = -(-rows // c)
    rp = n_chunks * c

    @pl.when(n == 0)
    def _():
        ext_ref[0:8, :] = conv0_ref[...]
        s_ref[...] = s0_ref[...]

    ext_ref[8:8 + rows, :] = xa_ref[...]
    if rows < rp:
        ext_ref[8 + rows:8 + rp, :] = jnp.zeros((rp - rows, CONV_DIM), F32)

    conv = ext_ref[5:5 + rp, :] * wconv_ref[0:1, :]
    for i in range(1, CONV_W):
        conv = conv + ext_ref[5 + i:5 + i + rp, :] * wconv_ref[i:i + 1, :]
    conv = _silu(conv)
    ext_ref[0:8, :] = ext_ref[rp:rp + 8, :]

    row = lax.broadcasted_iota(jnp.int32, (rp, 1), 0)
    live = row < rows
    if rows < rp:
        conv = jnp.where(live, conv, 0.0)

    gt = gt_ref[...]
    if rows < rp:
        gt = jnp.concatenate([gt, jnp.zeros((rp - rows, GATE_PAD), F32)], axis=0)
    beta_all = _sigmoid(gt)
    g_all = -jnp.exp(gp_ref[0:1, :]) * _softplus(gt + gp_ref[1:2, :])
    lane = lax.broadcasted_iota(jnp.int32, (rp, GATE_PAD), 1)
    g_all = jnp.where((lane >= A_HEADS) & (lane < 2 * A_HEADS) & live, g_all, 0.0)
    beta_all = jnp.where(live, beta_all, 0.0)

    ii = lax.broadcasted_iota(jnp.int32, (c, c), 0)
    jj = lax.broadcasted_iota(jnp.int32, (c, c), 1)
    incl = ii >= jj
    strict = ii > jj
    tril = jnp.where(incl, 1.0, 0.0).astype(BF16)
    eye = jnp.where(ii == jj, 1.0, 0.0).astype(F32)
    dd = functools.partial(jnp.dot, preferred_element_type=F32)

    def exact_tril_dot(x):
        hi, lo = _split2(x)
        lo2 = (x - hi.astype(F32) - lo.astype(F32)).astype(BF16)
        return dd(tril, hi) + dd(tril, lo) + dd(tril, lo2)

    z = z_ref[...]
    nwa = nwa_ref[...]
    chains = [(ci, h) for ci in range(n_chunks) for h in range(A_HEADS)]
    gc_alls = [exact_tril_dot(g_all[ci * c:(ci + 1) * c]) for ci in range(n_chunks)]
    qs, ks, kbs, rhss, gcs, gls = [], [], [], [], [], []
    for ci, h in chains:
        rs = slice(ci * c, (ci + 1) * c)
        q = conv[rs, h * A_DK:(h + 1) * A_DK]
        k = conv[rs, A_HEADS * A_DK + h * A_DK:A_HEADS * A_DK + (h + 1) * A_DK]
        v = conv[rs, 2 * A_HEADS * A_DK + h * A_DV:2 * A_HEADS * A_DK + (h + 1) * A_DV]
        q = q * lax.rsqrt(jnp.sum(q * q, axis=-1, keepdims=True) + EPS) * (A_DK ** -0.5)
        k = k * lax.rsqrt(jnp.sum(k * k, axis=-1, keepdims=True) + EPS)
        beta = beta_all[rs, h:h + 1]
        gc = gc_alls[ci][:, A_HEADS + h:A_HEADS + h + 1]
        kb = k * beta
        qs.append(q)
        ks.append(k)
        kbs.append(kb)
        rhss.append(jnp.concatenate([v * beta, kb * jnp.exp(gc)], axis=1))
        gcs.append(gc)
        gls.append(gc_alls[ci][c - 1:c, A_HEADS + h:A_HEADS + h + 1])
    e_mats = [exact_tril_dot(jnp.where(strict, g_all[ci * c:(ci + 1) * c, A_HEADS + h:A_HEADS + h + 1], 0.0))
              for ci, h in chains]
    decays = [jnp.where(incl, jnp.exp(jnp.where(incl, e, 0.0)), 0.0) for e in e_mats]
    kks = [_dot_nt(kb, k) for kb, k in zip(kbs, ks)]
    qks = [_dot_nt(q, k) for q, k in zip(qs, ks)]
    pws = [jnp.where(strict, a * d, 0.0) for a, d in zip(kks, decays)]
    qks = [jnp.where(incl, a * d, 0.0) for a, d in zip(qks, decays)]
    t_invs = [eye - lw for lw in pws]
    for _ in range(c.bit_length() - 2):
        pws = [_dot3(pw, pw) for pw in pws]
        t_invs = [ti + _dot3(ti, pw) for ti, pw in zip(t_invs, pws)]
    uws = [_dot(ti, rhs) for ti, rhs in zip(t_invs, rhss)]

    states = [s_ref[h] for h in range(A_HEADS)]
    for ci in range(n_chunks):
        idx = [ci * A_HEADS + h for h in range(A_HEADS)]
        ws = [_dot(uws[i][:, A_DV:], states[h]) for h, i in enumerate(idx)]
        os = [_dot(qs[i] * jnp.exp(gcs[i]), states[h]) for h, i in enumerate(idx)]
        es = [uws[i][:, :A_DV] - w for i, w in zip(idx, ws)]
        os = [o + _dot(qks[i], e) for o, i, e in zip(os, idx, es)]
        upd = [_dot_tn(ks[i] * jnp.exp(gls[i] - gcs[i]), e) for i, e in zip(idx, es)]
        states = [s * jnp.exp(gls[i]) + d for s, i, d in zip(states, idx, upd)]
        r0 = ci * c
        r1 = min(rows, r0 + c)
        for h, o in enumerate(os):
            o = o if r1 - r0 == c else o[:r1 - r0]
            o_ref[r0:r1, h * A_DV:(h + 1) * A_DV] = (
                _rms(o, nwa) * _silu(z[r0:r1, h * A_DV:(h + 1) * A_DV]))
    for h in range(A_HEADS):
        s_ref[h] = states[h]

    @pl.when(n == pl.num_programs(1) - 1)
    def _():
        snew_ref[...] = s_ref[...]


GDN_ROWS = 8 * CHUNK


def _gdn(proj, conv0, s0, w_conv, gate_par, norm_a):
    b, t, _ = proj.shape
    rows = min(GDN_ROWS, t)
    assert t % rows == 0 and (rows % CHUNK == 0 or rows == t)
    n = t // rows
    kernel = functools.partial(_gdn_kernel, rows=rows)
    return pl.pallas_call(
        kernel,
        grid=(b, n),
        in_specs=[pl.BlockSpec((None, rows, CONV_DIM), lambda i, j: (i, j, 0)),
                  pl.BlockSpec((None, rows, A_WIDTH), lambda i, j: (i, j, COL_Z // A_WIDTH)),
                  pl.BlockSpec((None, rows, GATE_PAD), lambda i, j: (i, j, COL_G // GATE_PAD)),
                  pl.BlockSpec((None, 8, CONV_DIM), lambda i, j: (i, 0, 0)),
                  pl.BlockSpec((None, A_HEADS, A_DK, A_DV), lambda i, j: (i, 0, 0, 0)),
                  pl.BlockSpec((CONV_W, CONV_DIM), lambda i, j: (0, 0)),
                  pl.BlockSpec((8, GATE_PAD), lambda i, j: (0, 0)),
                  pl.BlockSpec((1, A_DV), lambda i, j: (0, 0))],
        out_specs=[pl.BlockSpec((None, rows, A_WIDTH), lambda i, j: (i, j, 0)),
                   pl.BlockSpec((None, A_HEADS, A_DK, A_DV), lambda i, j: (i, 0, 0, 0))],
        out_shape=[jax.ShapeDtypeStruct((b, t, A_WIDTH), F32),
                   jax.ShapeDtypeStruct((b, A_HEADS, A_DK, A_DV), F32)],
        scratch_shapes=[pltpu.VMEM((8 + -(-rows // CHUNK) * CHUNK, CONV_DIM), F32),
                        pltpu.VMEM((A_HEADS, A_DK, A_DV), F32)],
        compiler_params=pltpu.CompilerParams(dimension_semantics=("arbitrary", "arbitrary"),
                                             vmem_limit_bytes=VMEM_LIMIT),
        name="gdn",
    )(proj, proj, proj, conv0, s0, w_conv, gate_par, norm_a)


HALF = 128 // CHUNK


def _gdn_pair_kernel(xa_ref, z_ref, gt_ref, conv0_ref, s0_ref, wconv_ref, gp_ref, nwa_ref,
                     o_ref, snew_ref, ext_ref, s_ref, *, rows, nb, conv_done, side_work=None):
    n = pl.program_id(1)
    c = CHUNK
    n_chunks = -(-rows // c)
    rp = n_chunks * c
    assert A_HEADS % 2 == 0 and HALF == 2 and A_DK == 128 and A_DV == 128

    @pl.when(n == 0)
    def _():
        ext_ref[:, 0:8, :] = conv0_ref[...]
        s_ref[...] = s0_ref[...]

    if side_work is not None:
        side_work()

    row = lax.broadcasted_iota(jnp.int32, (rp, 1), 0)
    live = row < rows
    lane_g = lax.broadcasted_iota(jnp.int32, (rp, GATE_PAD), 1)
    gate_lanes = (lane_g >= A_HEADS) & (lane_g < 2 * A_HEADS) & live

    ii = lax.broadcasted_iota(jnp.int32, (c, 128), 0)
    ll = lax.broadcasted_iota(jnp.int32, (c, 128), 1)
    jj = ll & (c - 1)
    low = ll < c
    incl = ii >= jj
    strict = ii > jj
    eye = jnp.where(ii == jj, 1.0, 0.0).astype(F32)
    ti = lax.broadcasted_iota(jnp.int32, (c, c), 0)
    tj = lax.broadcasted_iota(jnp.int32, (c, c), 1)
    tril = jnp.where(ti >= tj, 1.0, 0.0).astype(BF16)
    dd = functools.partial(jnp.dot, preferred_element_type=F32)

    def exact_tril_dot(x):
        hi, lo = _split2(x)
        lo2 = (x - hi.astype(F32) - lo.astype(F32)).astype(BF16)
        return dd(tril, hi) + dd(tril, lo) + dd(tril, lo2)

    def bd(x):
        return jnp.concatenate([jnp.where(low, x, 0.0), jnp.where(low, 0.0, x)], axis=0).astype(BF16)

    def unstack(y):
        return jnp.where(low, y[:c], y[c:])

    nwa = nwa_ref[...]
    chains = []
    for bi in range(nb):
        if conv_done:
            assert rows == rp
            conv = xa_ref[bi]
        else:
            ext_ref[bi, 8:8 + rows, :] = xa_ref[bi]
            if rows < rp:
                ext_ref[bi, 8 + rows:8 + rp, :] = jnp.zeros((rp - rows, CONV_DIM), F32)
            conv = _causal_conv_silu(ext_ref[bi, 0:8 + rp, :], wconv_ref, rp)
            ext_ref[bi, 0:8, :] = ext_ref[bi, rp:rp + 8, :]
            if rows < rp:
                conv = jnp.where(live, conv, 0.0)

        gt = gt_ref[bi]
        if rows < rp:
            gt = jnp.concatenate([gt, jnp.zeros((rp - rows, GATE_PAD), F32)], axis=0)
        beta_all = jnp.where(live, _sigmoid(gt), 0.0)
        g_all = jnp.where(gate_lanes, -jnp.exp(gp_ref[0:1, :]) * _softplus(gt + gp_ref[1:2, :]), 0.0)

        for ci in range(n_chunks):
            rs = slice(ci * c, (ci + 1) * c)
            gc_all = exact_tril_dot(g_all[rs])
            for p in range(A_HEADS // 2):
                hd = []
                for h in (2 * p, 2 * p + 1):
                    q = conv[rs, h * A_DK:(h + 1) * A_DK]
                    k = conv[rs, A_HEADS * A_DK + h * A_DK:A_HEADS * A_DK + (h + 1) * A_DK]
                    v = conv[rs, 2 * A_HEADS * A_DK + h * A_DV:2 * A_HEADS * A_DK + (h + 1) * A_DV]
                    if not conv_done:
                        q = _l2norm(q) * (A_DK ** -0.5)
                        k = _l2norm(k)
                    beta = beta_all[rs, h:h + 1]
                    g = g_all[rs, A_HEADS + h:A_HEADS + h + 1]
                    gc = gc_all[:, A_HEADS + h:A_HEADS + h + 1]
                    gl = gc_all[c - 1:c, A_HEADS + h:A_HEADS + h + 1]
                    kb = k * beta
                    hd.append(dict(q=q, k=k, kb=kb, g=g, qg=q * jnp.exp(gc), kd=k * jnp.exp(gl - gc),
                                   dl=jnp.exp(gl),
                                   rhs=jnp.concatenate([v * beta, kb * jnp.exp(gc)], axis=1)))
                chains.append(dict(bi=bi, ci=ci, p=p, hd=hd))

    for ch in chains:
        a, b = ch["hd"]
        ch["e_mat"] = exact_tril_dot(jnp.where(strict, jnp.where(low, a["g"], b["g"]), 0.0))
    for ch in chains:
        a, b = ch["hd"]
        k_st = jnp.concatenate([a["k"], b["k"]], axis=0)
        ch["kk"] = _dot_nt(jnp.concatenate([a["kb"], b["kb"]], axis=0), k_st)
        ch["qk"] = _dot_nt(jnp.concatenate([a["q"], b["q"]], axis=0), k_st)
    for ch in chains:
        decay = jnp.where(incl, jnp.exp(jnp.where(incl, ch["e_mat"], 0.0)), 0.0)
        ch["lower"] = jnp.where(strict, unstack(ch["kk"]) * decay, 0.0)
        ch["qk"] = jnp.where(incl, unstack(ch["qk"]) * decay, 0.0)
        ch["x"] = ch["lower"]
        ch["xb"] = bd(ch["lower"])
        ch["t"] = eye - ch["lower"]
    for _ in range(c.bit_length() - 2):
        for ch in chains:
            ch["x"] = jnp.dot(ch["x"].astype(BF16), ch["xb"], preferred_element_type=F32)
        for ch in chains:
            ch["xb"] = bd(ch["x"])
            ch["t"] = ch["t"] + jnp.dot(ch["t"].astype(BF16), ch["xb"], preferred_element_type=F32)
    for ch in chains:
        lh, ll_ = _split2(ch["lower"])
        th = ch["t"].astype(BF16).astype(F32)
        tl = ch["t"] - th
        bth = bd(th)
        lt = dd(lh, bth) + dd(lh, bd(tl)) + dd(ll_, bth)
        ch["res"] = eye - ch["t"] - lt
    for ch in chains:
        ch["t"] = ch["t"] + jnp.dot(ch["t"].astype(BF16), bd(ch["res"]), preferred_element_type=F32)
    for ch in chains:
        a, b = ch["hd"]
        uw = jnp.dot(bd(ch["t"]), jnp.concatenate([a["rhs"], b["rhs"]], axis=0).astype(BF16),
                     preferred_element_type=F32)
        a["u"], a["w"] = uw[:c, :A_DV], uw[:c, A_DV:]
        b["u"], b["w"] = uw[c:, :A_DV], uw[c:, A_DV:]
        ch["qkb"] = bd(ch["qk"])

    by_key = {(ch["bi"], ch["ci"], ch["p"]): ch for ch in chains}
    states = {(bi, h): s_ref[bi, h] for bi in range(nb) for h in range(A_HEADS)}
    for ci in range(n_chunks):
        cur = [by_key[(bi, ci, p)] for bi in range(nb) for p in range(A_HEADS // 2)]
        for ch in cur:
            for hh, d in enumerate(ch["hd"]):
                ws = _dot(jnp.concatenate([d["w"], d["qg"]], axis=0), states[(ch["bi"], 2 * ch["p"] + hh)])
                d["e"] = d["u"] - ws[:c]
                d["o"] = ws[c:]
        for ch in cur:
            a, b = ch["hd"]
            o2 = jnp.dot(ch["qkb"], jnp.concatenate([a["e"], b["e"]], axis=0).astype(BF16),
                         preferred_element_type=F32)
            a["o"] = a["o"] + o2[:c]
            b["o"] = b["o"] + o2[c:]
        for ch in cur:
            for hh, d in enumerate(ch["hd"]):
                key = (ch["bi"], 2 * ch["p"] + hh)
                states[key] = states[key] * d["dl"] + _dot_tn(d["kd"], d["e"])
        r0 = ci * c
        r1 = min(rows, r0 + c)
        for ch in cur:
            for hh, d in enumerate(ch["hd"]):
                h = 2 * ch["p"] + hh
                o = d["o"] if r1 - r0 == c else d["o"][:r1 - r0]
                o_ref[ch["bi"], r0:r1, h * A_DV:(h + 1) * A_DV] = (
                    _rms(o, nwa) * _silu(z_ref[ch["bi"], r0:r1, h * A_DV:(h + 1) * A_DV]))
    for (bi, h), s in states.items():
        s_ref[bi, h] = s

    @pl.when(n == pl.num_programs(1) - 1)
    def _():
        snew_ref[...] = s_ref[...]


def _gdn_pair(proj, conv0, s0, w_conv, gate_par, norm_a, conv_done, side_of_grid=None):
    b, t, _ = proj.shape
    rows = min(GDN_ROWS, t)
    assert t % rows == 0 and (rows % CHUNK == 0 or rows == t)
    n = t // rows
    nb = max(1, min(b, GDN_ROWS // (-(-rows // CHUNK) * CHUNK)))
    assert b % nb == 0
    rp = 0 if conv_done else -(-rows // CHUNK) * CHUNK
    in_specs = [pl.BlockSpec((nb, rows, CONV_DIM), lambda i, j: (i, j, 0)),
                pl.BlockSpec((nb, rows, A_WIDTH), lambda i, j: (i, j, COL_Z // A_WIDTH)),
                pl.BlockSpec((nb, rows, GATE_PAD), lambda i, j: (i, j, COL_G // GATE_PAD)),
                pl.BlockSpec((nb, 8, CONV_DIM), lambda i, j: (i, 0, 0)),
                pl.BlockSpec((nb, A_HEADS, A_DK, A_DV), lambda i, j: (i, 0, 0, 0)),
                pl.BlockSpec((CONV_W, CONV_DIM), lambda i, j: (0, 0)),
                pl.BlockSpec((8, GATE_PAD), lambda i, j: (0, 0)),
                pl.BlockSpec((1, A_DV), lambda i, j: (0, 0))]
    out_specs = [pl.BlockSpec((nb, rows, A_WIDTH), lambda i, j: (i, j, 0)),
                 pl.BlockSpec((nb, A_HEADS, A_DK, A_DV), lambda i, j: (i, 0, 0, 0))]
    out_shape = [jax.ShapeDtypeStruct((b, t, A_WIDTH), F32),
                 jax.ShapeDtypeStruct((b, A_HEADS, A_DK, A_DV), F32)]
    inputs = (proj, proj, proj, conv0, s0, w_conv, gate_par, norm_a)
    n_in, n_out = len(in_specs), len(out_specs)
    side = side_of_grid((b // nb, n)) if side_of_grid is not None else None

    def kernel(*refs):
        n_side_in = len(side["in_specs"]) if side else 0
        gin, sin = refs[:n_in], refs[n_in:n_in + n_side_in]
        gout = refs[n_in + n_side_in:n_in + n_side_in + n_out]
        sout = refs[n_in + n_side_in + n_out:len(refs) - 2]
        side_work = (lambda: side["kernel"](*sin, *sout)) if side else None
        _gdn_pair_kernel(*gin, *gout, *refs[len(refs) - 2:], rows=rows, nb=nb, conv_done=conv_done,
                         side_work=side_work)

    if side:
        inputs = inputs + tuple(side["inputs"])
        in_specs = in_specs + list(side["in_specs"])
        out_specs = out_specs + list(side["out_specs"])
        out_shape = out_shape + list(side["out_shape"])
    outs = pl.pallas_call(
        kernel,
        grid=(b // nb, n),
        in_specs=in_specs,
        out_specs=out_specs,
        out_shape=out_shape,
        scratch_shapes=[pltpu.VMEM((nb, 8 + rp, CONV_DIM), F32),
                        pltpu.VMEM((nb, A_HEADS, A_DK, A_DV), F32)],
        compiler_params=pltpu.CompilerParams(dimension_semantics=("arbitrary", "arbitrary"),
                                             vmem_limit_bytes=VMEM_LIMIT),
        name="gdn_side" if side else "gdn",
    )(*inputs)
    return outs[:n_out], outs[n_out:]


def _slope(h):
    return 2.0 ** (-8.0 * (h + 1) / B_HEADS)


ATTN_SB = MAX_WINDOW
ATTN_GROUP = 16
PAIR = 128 // B_HEAD_DIM


def _attn_fused_kernel(q_ref, kp_ref, kc_ref, vp_ref, vc_ref, nwb_ref, o_ref, kt_ref, vt_ref, kbuf, vbuf, *stats):
    hp = pl.program_id(1)
    n = pl.program_id(2)
    sb = ATTN_SB
    blk = STEPS
    kbuf[0:sb, :] = kp_ref[...]
    kbuf[sb:2 * sb, :] = kc_ref[...]
    vbuf[0:sb, :] = vp_ref[...]
    vbuf[sb:2 * sb, :] = vc_ref[...]

    lane_q = lax.broadcasted_iota(jnp.int32, (blk, 128), 1)
    low_q = lane_q < B_HEAD_DIM
    qq = lax.broadcasted_iota(jnp.int32, (2 * blk, 2 * blk), 0)
    kk = lax.broadcasted_iota(jnp.int32, (2 * blk, 2 * blk), 1)
    steps_back = (qq & (blk - 1)) + blk - kk
    in_band = (steps_back >= 0) & (steps_back <= STEPS)
    slope_lo = jnp.float32(_slope(0))
    slope_hi = jnp.float32(_slope(1))
    for i in range(1, B_HEADS // PAIR):
        slope_lo = jnp.where(hp == i, _slope(PAIR * i), slope_lo)
        slope_hi = jnp.where(hp == i, _slope(PAIR * i + 1), slope_hi)
    slope = jnp.where(qq < blk, slope_lo, slope_hi)

    for bi, (_, dil) in enumerate(DILATED):
        m_s, l_s, num_s = stats[3 * bi:3 * bi + 3]
        bias = jnp.where(in_band, -slope * (steps_back * dil).astype(F32), NEG)
        bias_first = jnp.where(kk >= blk, bias, NEG)
        d_bits = dil.bit_length() - 1

        def rows(start, size):
            if dil == 1:
                return pl.ds(pl.multiple_of(start, blk), size)
            return pl.ds(start, size, stride=dil)

        def group(g, carry):
            tiles = []
            for u in range(ATTN_GROUP):
                it = g * ATTN_GROUP + u
                q0 = lax.shift_right_logical(it, d_bits) * (blk * dil) + (it & (dil - 1))
                k0 = sb + q0 - blk * dil
                q = q_ref[rows(q0, blk), :] * (B_HEAD_DIM ** -0.5)
                q2 = jnp.concatenate([jnp.where(low_q, q, 0.0), jnp.where(low_q, 0.0, q)], axis=0)
                tiles.append((q0, k0, q2.astype(BF16), kbuf[rows(k0, 2 * blk), :].astype(BF16),
                              vbuf[rows(k0, 2 * blk), :].astype(BF16)))
            scores = [lax.dot_general(q2, k, (((1,), (1,)), ((), ())), preferred_element_type=F32)
                      for (_, _, q2, k, _) in tiles]
            probs = []
            for (q0, k0, _, _, _), s in zip(tiles, scores):
                s = s + jnp.where((n == 0) & (q0 < blk * dil), bias_first, bias)
                m = jnp.max(s, axis=-1, keepdims=True)
                p = jnp.exp(s - m)
                probs.append((m, jnp.sum(p, axis=-1, keepdims=True), p.astype(BF16)))
            for (q0, _, _, _, v), (m, l, p) in zip(tiles, probs):
                pv = jnp.dot(p, v, preferred_element_type=F32)
                m_s[rows(q0, blk), :] = jnp.where(low_q, m[:blk], m[blk:])
                l_s[rows(q0, blk), :] = jnp.where(low_q, l[:blk], l[blk:])
                num_s[rows(q0, blk), :] = jnp.where(low_q, pv[:blk], pv[blk:])
            return carry

        lax.fori_loop(0, sb // (blk * ATTN_GROUP), group, 0)

    nwb = nwb_ref[...]

    def combine(i, carry):
        rs = pl.ds(pl.multiple_of(i * blk, blk), blk)
        ms = [stats[3 * bi][rs, :] for bi in range(len(DILATED))]
        ls = [stats[3 * bi + 1][rs, :] for bi in range(len(DILATED))]
        nums = [stats[3 * bi + 2][rs, :] for bi in range(len(DILATED))]
        mx = jnp.maximum(jnp.maximum(ms[0], ms[1]), ms[2])
        ws = [jnp.exp(m - mx) for m in ms]
        den = ws[0] * ls[0] + ws[1] * ls[1] + ws[2] * ls[2]
        o = (ws[0] * nums[0] + ws[1] * nums[1] + ws[2] * nums[2]) / den
        sq = o * o
        ss_lo = jnp.sum(jnp.where(low_q, sq, 0.0), axis=-1, keepdims=True)
        ss_hi = jnp.sum(jnp.where(low_q, 0.0, sq), axis=-1, keepdims=True)
        mean_sq = jnp.where(low_q, ss_lo, ss_hi) / B_HEAD_DIM
        o_ref[rs, :] = o * lax.rsqrt(mean_sq + EPS) * nwb
        return carry

    lax.fori_loop(0, sb // blk, combine, 0, unroll=4)

    @pl.when(n == pl.num_programs(2) - 1)
    def _():
        kt_ref[...] = kc_ref[...].T
        vt_ref[...] = vc_ref[...].T


def _attn_prompt(qkv, norm_b_pair):
    b, s, _ = qkv.shape
    sb = ATTN_SB
    assert s % sb == 0 and (sb // STEPS) % ATTN_GROUP == 0 and sb == min(MAX_WINDOW, s)
    assert all(w == STEPS * d and sb % w == 0 and d & (d - 1) == 0 for (w, d) in DILATED)
    n_pairs = B_HEADS // PAIR
    blk = (None, sb, 128)
    prev = lambda c0: pl.BlockSpec(blk, lambda i, p, n: (i, jnp.maximum(n - 1, 0), c0 + p))
    cur = lambda c0: pl.BlockSpec(blk, lambda i, p, n: (i, n, c0 + p))
    return pl.pallas_call(
        _attn_fused_kernel,
        grid=(b, n_pairs, s // sb),
        in_specs=[cur(0), prev(n_pairs), cur(n_pairs), prev(2 * n_pairs), cur(2 * n_pairs),
                  pl.BlockSpec((1, 128), lambda i, p, n: (0, 0))],
        out_specs=[pl.BlockSpec(blk, lambda i, p, n: (i, n, p)),
                   pl.BlockSpec((None, 128, sb), lambda i, p, n: (i, p, 0)),
                   pl.BlockSpec((None, 128, sb), lambda i, p, n: (i, p, 0))],
        out_shape=[jax.ShapeDtypeStruct((b, s, B_WIDTH), F32),
                   jax.ShapeDtypeStruct((b, B_WIDTH, sb), F32),
                   jax.ShapeDtypeStruct((b, B_WIDTH, sb), F32)],
        scratch_shapes=[pltpu.VMEM((2 * sb, 128), F32), pltpu.VMEM((2 * sb, 128), F32)]
                       + [pltpu.VMEM((sb, 128), F32)] * (3 * len(DILATED)),
        compiler_params=pltpu.CompilerParams(
            dimension_semantics=("arbitrary", "arbitrary", "arbitrary"), vmem_limit_bytes=VMEM_LIMIT),
        name="attn_prompt",
    )(qkv, qkv, qkv, qkv, qkv, norm_b_pair)


def _attn_sample_kernel(q_ref, kn_ref, vn_ref, kt_ref, vt_ref, nwb_ref, o_ref, kto_ref, vto_ref, *, n_past, t):
    kn = kn_ref[...]
    vn = vn_ref[...]
    kt = kt_ref[...]
    vt = vt_ref[...]

    prow = lax.broadcasted_iota(jnp.int32, (t, 128), 0)
    plane = lax.broadcasted_iota(jnp.int32, (t, 128), 1)
    place = jnp.where(plane == 128 - t + prow, 1.0, 0.0).astype(BF16)
    tail_lanes = lax.broadcasted_iota(jnp.int32, (B_WIDTH, 128), 1) >= 128 - t
    tn = functools.partial(lax.dot_general, dimension_numbers=(((0,), (0,)), ((), ())),
                           preferred_element_type=F32)

    def shifted(win, new, out_ref):
        hi, lo = _split2(new)
        lo2 = (new - hi.astype(F32) - lo.astype(F32)).astype(BF16)
        new_t = tn(hi, place) + tn(lo, place) + tn(lo2, place)
        sh = pltpu.roll(win, n_past - t, axis=1)
        out_ref[:, :n_past - 128] = sh[:, :n_past - 128]
        out_ref[:, n_past - 128:] = jnp.where(tail_lanes, new_t, sh[:, n_past - 128:])

    shifted(kt, kn, kto_ref)
    shifted(vt, vn, vto_ref)

    rq = B_HEADS * t
    t_bits = t.bit_length() - 1
    d_bits = B_HEAD_DIM.bit_length() - 1
    rowi = lax.broadcasted_iota(jnp.int32, (rq, B_WIDTH), 0)
    lanei = lax.broadcasted_iota(jnp.int32, (rq, B_WIDTH), 1)
    own = lax.shift_right_logical(rowi, t_bits) == lax.shift_right_logical(lanei, d_bits)
    q_rep = jnp.concatenate([q_ref[...]] * B_HEADS, axis=0)
    q_blk = jnp.where(own, q_rep, 0.0).astype(BF16)
    zpad = jnp.zeros((128 - t, B_WIDTH), BF16)
    kn_pad = jnp.concatenate([kn.astype(BF16), zpad], axis=0)
    vn_pad = jnp.concatenate([vn.astype(BF16), zpad], axis=0)
    vt16 = vt.astype(BF16)
    scale = B_HEAD_DIM ** -0.5
    s_c = jnp.dot(q_blk, kt.astype(BF16), preferred_element_type=F32) * scale
    s_n = lax.dot_general(q_blk, kn_pad, (((1,), (1,)), ((), ())), preferred_element_type=F32) * scale

    r1 = lax.broadcasted_iota(jnp.int32, (rq, 1), 0)
    head = lax.shift_right_logical(r1, t_bits)
    slope = jnp.zeros((rq, 1), F32)
    for h in range(B_HEADS):
        slope = jnp.where(head == h, _slope(h), slope)
    tok = r1 & (t - 1)
    delta_c = n_past + tok - lax.broadcasted_iota(jnp.int32, (rq, n_past), 1)
    key_n = lax.broadcasted_iota(jnp.int32, (rq, 128), 1)
    delta_n = tok - key_n
    s_c = s_c - slope * delta_c.astype(F32)
    s_n = s_n - slope * delta_n.astype(F32)
    live_n = (key_n < t) & (delta_n >= 0)

    ms, ls, nums = [], [], []
    for (_, dil) in DILATED:
        valid_c = ((delta_c & (dil - 1)) == 0) & (delta_c <= STEPS * dil)
        valid_n = live_n & ((delta_n & (dil - 1)) == 0) & (delta_n <= STEPS * dil)
        sd_c = jnp.where(valid_c, s_c, NEG)
        sd_n = jnp.where(valid_n, s_n, NEG)
        m = jnp.maximum(jnp.max(sd_c, axis=-1, keepdims=True), jnp.max(sd_n, axis=-1, keepdims=True))
        p_c = jnp.exp(sd_c - m)
        p_n = jnp.exp(sd_n - m)
        ms.append(m)
        ls.append(jnp.sum(p_c, axis=-1, keepdims=True) + jnp.sum(p_n, axis=-1, keepdims=True))
        nums.append(lax.dot_general(p_c.astype(BF16), vt16, (((1,), (1,)), ((), ())), preferred_element_type=F32)
                    + jnp.dot(p_n.astype(BF16), vn_pad, preferred_element_type=F32))
    mx = jnp.maximum(jnp.maximum(ms[0], ms[1]), ms[2])
    ws = [jnp.exp(m - mx) for m in ms]
    den = ws[0] * ls[0] + ws[1] * ls[1] + ws[2] * ls[2]
    o = (ws[0] * nums[0] + ws[1] * nums[1] + ws[2] * nums[2]) / den
    o = jnp.where(own, o, 0.0)
    o = o * lax.rsqrt(jnp.sum(o * o, axis=-1, keepdims=True) / B_HEAD_DIM + EPS) * nwb_ref[...]
    acc = o[0:t, :]
    for h in range(1, B_HEADS):
        acc = acc + o[h * t:(h + 1) * t, :]
    o_ref[...] = acc


def _attn_sample_parts(qkv, win_kt, win_vt, norm_b_tiled, seq_of_step):
    b, t, _ = qkv.shape
    n_past = win_kt.shape[2]
    assert n_past == MAX_WINDOW and n_past % 128 == 0 and t % 8 == 0 and t & (t - 1) == 0 and t <= 128
    assert all(d & (d - 1) == 0 for (_, d) in DILATED)
    col = lambda c: pl.BlockSpec((None, t, B_WIDTH), lambda *g: (seq_of_step(*g), 0, c))
    win = pl.BlockSpec((None, B_WIDTH, n_past), lambda *g: (seq_of_step(*g), 0, 0))
    return dict(
        kernel=functools.partial(_attn_sample_kernel, n_past=n_past, t=t),
        inputs=(qkv, qkv, qkv, win_kt, win_vt, norm_b_tiled),
        in_specs=[col(0), col(1), col(2), win, win, pl.BlockSpec((1, B_WIDTH), lambda *g: (0, 0))],
        out_specs=[col(0), win, win],
        out_shape=[jax.ShapeDtypeStruct((b, t, B_WIDTH), F32),
                   jax.ShapeDtypeStruct((b, B_WIDTH, n_past), F32),
                   jax.ShapeDtypeStruct((b, B_WIDTH, n_past), F32)])


def _attn_sample(qkv, win_kt, win_vt, norm_b_tiled):
    parts = _attn_sample_parts(qkv, win_kt, win_vt, norm_b_tiled, lambda i: i)
    return pl.pallas_call(
        parts["kernel"],
        grid=(qkv.shape[0],),
        in_specs=parts["in_specs"],
        out_specs=parts["out_specs"],
        out_shape=parts["out_shape"],
        compiler_params=pltpu.CompilerParams(dimension_semantics=("arbitrary",),
                                             vmem_limit_bytes=VMEM_LIMIT),
        name="attn_sample",
    )(*parts["inputs"])


def _out_ffn_kernel(x_ref, oa_ref, ob_ref, wo_ref, nf_ref, wg_ref, wu_ref, wd_ref, nfin_ref, y_ref, *, ff_chunk):
    mixed = jnp.concatenate([oa_ref[...], ob_ref[...]], axis=-1).astype(BF16)
    x1 = x_ref[...] + jnp.dot(mixed, wo_ref[...], preferred_element_type=F32)
    hf = _rms(x1, nf_ref[...]).astype(BF16)
    x2 = x1
    for c0 in range(0, wg_ref.shape[1], ff_chunk):
        g = jnp.dot(hf, wg_ref[:, c0:c0 + ff_chunk], preferred_element_type=F32)
        u = jnp.dot(hf, wu_ref[:, c0:c0 + ff_chunk], preferred_element_type=F32)
        act = (_silu(g) * u).astype(BF16)
        x2 = x2 + jnp.dot(act, wd_ref[c0:c0 + ff_chunk, :], preferred_element_type=F32)
    y_ref[...] = _rms(x2, nfin_ref[...])


def _out_ffn(x2d, o_a, o_b, w_out, norm_ffn, w_gate, w_up, w_down, norm_final, tm):
    t = x2d.shape[0]
    d_ff = w_gate.shape[1]
    ff_chunk = d_ff // 2 if d_ff % 256 == 0 else d_ff
    once = pl.Buffered(1)
    row = lambda w: pl.BlockSpec((tm, w), lambda i: (i, 0))
    full = lambda a, b: pl.BlockSpec((a, b), lambda i: (0, 0), pipeline_mode=once)
    return pl.pallas_call(
        functools.partial(_out_ffn_kernel, ff_chunk=ff_chunk),
        grid=(t // tm,),
        in_specs=[row(D_MODEL), row(A_WIDTH), row(B_WIDTH),
                  full(D_MODEL, D_MODEL), full(1, D_MODEL),
                  full(D_MODEL, d_ff), full(D_MODEL, d_ff), full(d_ff, D_MODEL), full(1, D_MODEL)],
        out_specs=row(D_MODEL),
        out_shape=jax.ShapeDtypeStruct((t, D_MODEL), F32),
        compiler_params=pltpu.CompilerParams(dimension_semantics=("arbitrary",),
                                             vmem_limit_bytes=VMEM_LIMIT),
        name="out_ffn",
    )(x2d, o_a, o_b, w_out, norm_ffn, w_gate, w_up, w_down, norm_final)


def _layer_params(norm_mix, w_in, w_conv, a_log, dt_bias, norm_out_a, norm_out_b, w_out, norm_ffn,
                  w_gate, w_up, w_down, layer):
    w = w_in[layer]
    n_gate = 2 * A_HEADS
    w_p = (w[:, :COL_G].astype(BF16),
           jnp.pad(w[:, COL_G:COL_G + n_gate], ((0, 0), (0, GATE_PAD - n_gate))).astype(BF16),
           w[:, COL_G + n_gate:].astype(BF16))
    gate_par = jnp.zeros((8, GATE_PAD), F32)
    gate_par = gate_par.at[0, A_HEADS:2 * A_HEADS].set(a_log[layer].astype(F32))
    gate_par = gate_par.at[1, A_HEADS:2 * A_HEADS].set(dt_bias[layer].astype(F32))
    return dict(
        norm_mix=norm_mix[layer].reshape(1, D_MODEL), w_p=w_p, w_conv=w_conv[layer], gate_par=gate_par,
        norm_a=norm_out_a[layer].reshape(1, A_DV),
        norm_b_pair=jnp.tile(norm_out_b[layer], PAIR).reshape(1, PAIR * B_HEAD_DIM),
        norm_b_tiled=jnp.tile(norm_out_b[layer], B_HEADS).reshape(1, B_WIDTH),
        w_out=w_out[layer].astype(BF16), norm_ffn=norm_ffn[layer].reshape(1, D_MODEL),
        w_gate=w_gate[layer].astype(BF16), w_up=w_up[layer].astype(BF16), w_down=w_down[layer].astype(BF16))


def _project(x, conv_buf, p, tm):
    b, t, _ = x.shape
    assert t >= CONV_W - 1
    conv0 = jnp.pad(conv_buf, ((0, 0), (8 - (CONV_W - 1), 0), (0, 0)))
    conv_done = t % tm == 0
    if conv_done:
        proj_a, proj_b, tail = _norm_proj_conv(x, p["norm_mix"], p["w_p"], p["w_conv"], conv0, tm)
        new_conv = tail[:, 8 - (CONV_W - 1):]
    else:
        proj_a, proj_b = _norm_proj(x.reshape(b * t, D_MODEL), p["norm_mix"], p["w_p"], tm)
        proj_a = proj_a.reshape(b, t, COLS_A)
        proj_b = proj_b.reshape(b, t, COLS_B)
        new_conv = proj_a[:, t - (CONV_W - 1):, :CONV_DIM]
    return proj_a, proj_b, new_conv, conv0, conv_done


def _finish(x, o_a, o_b, p, norm_final, tm):
    b, t, _ = x.shape
    y = _out_ffn(x.reshape(b * t, D_MODEL), o_a.reshape(b * t, A_WIDTH), o_b.reshape(b * t, B_WIDTH), p["w_out"],
                 p["norm_ffn"], p["w_gate"], p["w_up"], p["w_down"], norm_final.reshape(1, D_MODEL), tm)
    return y.reshape(b, t, D_MODEL)


def kernel(x_prompt, x_sample, state_conv, state_rec, cache_win_k, cache_win_v, norm_mix, w_in, w_conv, a_log,
           dt_bias, norm_out_a, norm_out_b, w_out, norm_ffn, w_gate, w_up, w_down, norm_final):
    depth = w_in.shape[0]
    assert depth == 1, "the final norm is fused into the block of a single-layer trunk"
    p = _layer_params(norm_mix, w_in, w_conv, a_log, dt_bias, norm_out_a, norm_out_b, w_out, norm_ffn,
                      w_gate, w_up, w_down, 0)
    bp, tp, _ = x_prompt.shape
    bs, ts, _ = x_sample.shape
    tm_p, tm_s = 512, bs * ts
    gdn_args = (p["w_conv"], p["gate_par"], p["norm_a"])

    sa, sb, sc, s_conv0, s_done = _project(x_sample, state_conv[0], p, tm_s)
    n_past = cache_win_k.shape[2]
    to_t = lambda a: jnp.transpose(a, (0, 2, 3, 1)).reshape(bs, B_WIDTH, n_past)
    from_t = lambda a: jnp.transpose(a.reshape(a.shape[0], B_HEADS, B_HEAD_DIM, a.shape[2]), (0, 3, 1, 2))
    win_kt, win_vt = to_t(cache_win_k[0]), to_t(cache_win_v[0])

    def sample_attention_of_grid(grid):
        if grid[0] * grid[1] != bs:
            return None
        return _attn_sample_parts(sb, win_kt, win_vt, p["norm_b_tiled"], lambda i, j: i * grid[1] + j)

    pa, pb, pc, p_conv0, p_done = _project(x_prompt, jnp.zeros((bp, CONV_W - 1, CONV_DIM), F32), p, tm_p)
    zero_rec = jnp.zeros((bp, A_HEADS, A_DK, A_DV), F32)
    (o_a_p, pr), side = _gdn_pair(pa, p_conv0, zero_rec, *gdn_args, p_done, sample_attention_of_grid)
    o_b_s, new_kt, new_vt = side if side else _attn_sample(sb, win_kt, win_vt, p["norm_b_tiled"])
    (o_a_s, sr), _ = _gdn_pair(sa, s_conv0, state_rec[0], *gdn_args, s_done)

    o_b_p, pkt, pvt = _attn_prompt(pb, p["norm_b_pair"])

    yp = _finish(x_prompt, o_a_p, o_b_p, p, norm_final, tm_p)
    ys = _finish(x_sample, o_a_s, o_b_s, p, norm_final, tm_s)
    return (yp, ys, pc[None], pr[None], from_t(pkt)[None], from_t(pvt)[None], sc[None], sr[None],
            from_t(new_kt)[None], from_t(new_vt)[None])
```

```python
import functools

import jax
import jax.numpy as jnp
from jax import lax
from jax.experimental import pallas as pl
from jax.experimental.pallas import tpu as pltpu

F32 = jnp.float32
BF16 = jnp.bfloat16

D_MODEL = 1024
A_HEADS = 4
A_DK = 128
A_DV = 128
A_WIDTH = A_HEADS * A_DV
CONV_W = 4
CONV_DIM = 2 * A_HEADS * A_DK + A_HEADS * A_DV
B_HEADS = 8
B_HEAD_DIM = 64
B_WIDTH = B_HEADS * B_HEAD_DIM
DILATED = ((128, 1), (512, 4), (2048, 16))
STEPS = 128
MAX_WINDOW = 2048
EPS = 1e-6
NEG = -1e30
CHUNK = 64
GATE_PAD = 128
COL_Z = CONV_DIM
COL_G = CONV_DIM + A_WIDTH
COLS_A = COL_G + GATE_PAD
COLS_B = 3 * B_WIDTH
VMEM_LIMIT = 56 * 1024 * 1024


def _dot(a, b):
    return jnp.dot(a.astype(BF16), b.astype(BF16), preferred_element_type=F32)


def _dot_nt(a, b):
    return lax.dot_general(a.astype(BF16), b.astype(BF16), (((1,), (1,)), ((), ())),
                           preferred_element_type=F32)


def _dot_tn(a, b):
    return lax.dot_general(a.astype(BF16), b.astype(BF16), (((0,), (0,)), ((), ())),
                           preferred_element_type=F32)


def _split2(a):
    hi = a.astype(BF16)
    lo = (a - hi.astype(F32)).astype(BF16)
    return hi, lo


def _dot3(a, b):
    ah, al = _split2(a)
    bh, bl = _split2(b)
    d = functools.partial(jnp.dot, preferred_element_type=F32)
    return d(ah, bh) + d(ah, bl) + d(al, bh)


def _sigmoid(x):
    return 1.0 / (1.0 + jnp.exp(-x))


def _silu(x):
    return x * _sigmoid(x)


def _softplus(x):
    return jnp.maximum(x, 0.0) + jnp.log(1.0 + jnp.exp(-jnp.abs(x)))


def _rms(x, w):
    return x * lax.rsqrt(jnp.mean(x * x, axis=-1, keepdims=True) + EPS) * w


def _norm_proj_kernel(x_ref, nw_ref, wa_ref, wg_ref, wb_ref, oa_ref, ob_ref):
    h = _rms(x_ref[...], nw_ref[...]).astype(BF16)
    oa_ref[:, :COL_G] = jnp.dot(h, wa_ref[...], preferred_element_type=F32)
    oa_ref[:, COL_G:] = jnp.dot(h, wg_ref[...], preferred_element_type=F32)
    ob_ref[...] = jnp.dot(h, wb_ref[...], preferred_element_type=F32)


def _weight_specs():
    zero = lambda *g: (0, 0)
    return [pl.BlockSpec((D_MODEL, c), zero, pipeline_mode=pl.Buffered(1)) for c in (COL_G, GATE_PAD, COLS_B)]


def _norm_proj(x2d, norm_w, w_p, tm):
    t = x2d.shape[0]
    once = pl.Buffered(1)
    return pl.pallas_call(
        _norm_proj_kernel,
        grid=(t // tm,),
        in_specs=[pl.BlockSpec((tm, D_MODEL), lambda i: (i, 0)),
                  pl.BlockSpec((1, D_MODEL), lambda i: (0, 0), pipeline_mode=once)] + _weight_specs(),
        out_specs=[pl.BlockSpec((tm, COLS_A), lambda i: (i, 0)),
                   pl.BlockSpec((tm, COLS_B), lambda i: (i, 0))],
        out_shape=[jax.ShapeDtypeStruct((t, COLS_A), F32),
                   jax.ShapeDtypeStruct((t, COLS_B), F32)],
        compiler_params=pltpu.CompilerParams(dimension_semantics=("arbitrary",),
                                             vmem_limit_bytes=VMEM_LIMIT),
        name="norm_proj",
    )(x2d, norm_w, *w_p)


def _causal_conv_silu(x_all, wconv_ref, n):
    conv = x_all[8:8 + n] * wconv_ref[CONV_W - 1:CONV_W, :]
    for back in range(1, CONV_W):
        tap = CONV_W - 1 - back
        conv = conv + pltpu.roll(x_all, back, axis=0)[8:8 + n] * wconv_ref[tap:tap + 1, :]
    return _silu(conv)


def _l2norm(x):
    return x * lax.rsqrt(jnp.sum(x * x, axis=-1, keepdims=True) + EPS)


def _norm_proj_conv_kernel(x_ref, nw_ref, wa_ref, wg_ref, wb_ref, wconv_ref, conv0_ref, oa_ref, ob_ref, tail_ref,
                           carry_ref, h_ref, raw_ref):
    i = pl.program_id(1)
    tm = x_ref.shape[0]
    wq = A_HEADS * A_DK
    assert CONV_DIM == 3 * wq and COLS_B == 3 * B_WIDTH
    dd = functools.partial(jnp.dot, preferred_element_type=F32)

    @pl.when(i == 0)
    def _():
        carry_ref[...] = conv0_ref[...]

    def conv_cols(g):
        cs = slice(g * wq, (g + 1) * wq)
        conv = _causal_conv_silu(jnp.concatenate([carry_ref[:, cs], raw_ref[:, cs]], axis=0),
                                 wconv_ref.at[:, cs], tm)
        carry_ref[:, cs] = raw_ref[tm - 8:tm, cs]
        if g == 2:
            oa_ref[:, cs] = conv
            return
        for hd in range(A_HEADS):
            hs = slice(hd * A_DK, (hd + 1) * A_DK)
            out = _l2norm(conv[:, hs])
            oa_ref[:, g * wq + hd * A_DK:g * wq + (hd + 1) * A_DK] = out * (A_DK ** -0.5) if g == 0 else out

    def phase(k):
        if k == 0:
            h_ref[...] = _rms(x_ref[...], nw_ref[...]).astype(BF16)
        h = h_ref[...]
        if k < 3:
            raw_ref[:, k * wq:(k + 1) * wq] = dd(h, wa_ref[:, k * wq:(k + 1) * wq])
        if k == 0:
            oa_ref[:, CONV_DIM:COL_G] = dd(h, wa_ref[:, CONV_DIM:])
        else:
            ob_ref[:, (k - 1) * B_WIDTH:k * B_WIDTH] = dd(h, wb_ref[:, (k - 1) * B_WIDTH:k * B_WIDTH])
            conv_cols(k - 1)
        if k == 3:
            oa_ref[:, COL_G:] = dd(h, wg_ref[...])

    for k in range(4):
        pl.when(i >= 0)(functools.partial(phase, k))

    @pl.when(i == pl.num_programs(1) - 1)
    def _():
        tail_ref[...] = raw_ref[tm - 8:tm, :]


def _norm_proj_conv(x, norm_w, w_p, w_conv, conv0, tm):
    b, t, _ = x.shape
    assert t % tm == 0 and tm % 8 == 0
    once = pl.Buffered(1)
    return pl.pallas_call(
        _norm_proj_conv_kernel,
        grid=(b, t // tm),
        in_specs=[pl.BlockSpec((None, tm, D_MODEL), lambda i, j: (i, j, 0)),
                  pl.BlockSpec((1, D_MODEL), lambda i, j: (0, 0), pipeline_mode=once)] + _weight_specs() + [
                  pl.BlockSpec((CONV_W, CONV_DIM), lambda i, j: (0, 0), pipeline_mode=once),
                  pl.BlockSpec((None, 8, CONV_DIM), lambda i, j: (i, 0, 0))],
        out_specs=[pl.BlockSpec((None, tm, COLS_A), lambda i, j: (i, j, 0)),
                   pl.BlockSpec((None, tm, COLS_B), lambda i, j: (i, j, 0)),
                   pl.BlockSpec((None, 8, CONV_DIM), lambda i, j: (i, 0, 0))],
        out_shape=[jax.ShapeDtypeStruct((b, t, COLS_A), F32),
                   jax.ShapeDtypeStruct((b, t, COLS_B), F32),
                   jax.ShapeDtypeStruct((b, 8, CONV_DIM), F32)],
        scratch_shapes=[pltpu.VMEM((8, CONV_DIM), F32), pltpu.VMEM((tm, D_MODEL), BF16),
                        pltpu.VMEM((tm, CONV_DIM), F32)],
        compiler_params=pltpu.CompilerParams(dimension_semantics=("arbitrary", "arbitrary"),
                                             vmem_limit_bytes=VMEM_LIMIT),
        name="norm_proj_conv",
    )(x, norm_w, *w_p, w_conv, conv0)


def _gdn_kernel(xa_ref, z_ref, gt_ref, conv0_ref, s0_ref, wconv_ref, gp_ref, nwa_ref,
                o_ref, snew_ref, ext_ref, s_ref, *, rows):
    n = pl.program_id(1)
    c = CHUNK
    n_chunks = -(-rows // c)
    rp = n_chunks * c

    @pl.when(n == 0)
    def _():
        ext_ref[0:8, :] = conv0_ref[...]
        s_ref[...] = s0_ref[...]

    ext_ref[8:8 + rows, :] = xa_ref[...]
    if rows < rp:
        ext_ref[8 + rows:8 + rp, :] = jnp.zeros((rp - rows, CONV_DIM), F32)

    conv = ext_ref[5:5 + rp, :] * wconv_ref[0:1, :]
    for i in range(1, CONV_W):
        conv = conv + ext_ref[5 + i:5 + i + rp, :] * wconv_ref[i:i + 1, :]
    conv = _silu(conv)
    ext_ref[0:8, :] = ext_ref[rp:rp + 8, :]

    row = lax.broadcasted_iota(jnp.int32, (rp, 1), 0)
    live = row < rows
    if rows < rp:
        conv = jnp.where(live, conv, 0.0)

    gt = gt_ref[...]
    if rows < rp:
        gt = jnp.concatenate([gt, jnp.zeros((rp - rows, GATE_PAD), F32)], axis=0)
    beta_all = _sigmoid(gt)
    g_all = -jnp.exp(gp_ref[0:1, :]) * _softplus(gt + gp_ref[1:2, :])
    lane = lax.broadcasted_iota(jnp.int32, (rp, GATE_PAD), 1)
    g_all = jnp.where((lane >= A_HEADS) & (lane < 2 * A_HEADS) & live, g_all, 0.0)
    beta_all = jnp.where(live, beta_all, 0.0)

    ii = lax.broadcasted_iota(jnp.int32, (c, c), 0)
    jj = lax.broadcasted_iota(jnp.int32, (c, c), 1)
    incl = ii >= jj
    strict = ii > jj
    tril = jnp.where(incl, 1.0, 0.0).astype(BF16)
    eye = jnp.where(ii == jj, 1.0, 0.0).astype(F32)
    dd = functools.partial(jnp.dot, preferred_element_type=F32)

    def exact_tril_dot(x):
        hi, lo = _split2(x)
        lo2 = (x - hi.astype(F32) - lo.astype(F32)).astype(BF16)
        return dd(tril, hi) + dd(tril, lo) + dd(tril, lo2)

    z = z_ref[...]
    nwa = nwa_ref[...]
    chains = [(ci, h) for ci in range(n_chunks) for h in range(A_HEADS)]
    gc_alls = [exact_tril_dot(g_all[ci * c:(ci + 1) * c]) for ci in range(n_chunks)]
    qs, ks, kbs, rhss, gcs, gls = [], [], [], [], [], []
    for ci, h in chains:
        rs = slice(ci * c, (ci + 1) * c)
        q = conv[rs, h * A_DK:(h + 1) * A_DK]
        k = conv[rs, A_HEADS * A_DK + h * A_DK:A_HEADS * A_DK + (h + 1) * A_DK]
        v = conv[rs, 2 * A_HEADS * A_DK + h * A_DV:2 * A_HEADS * A_DK + (h + 1) * A_DV]
        q = q * lax.rsqrt(jnp.sum(q * q, axis=-1, keepdims=True) + EPS) * (A_DK ** -0.5)
        k = k * lax.rsqrt(jnp.sum(k * k, axis=-1, keepdims=True) + EPS)
        beta = beta_all[rs, h:h + 1]
        gc = gc_alls[ci][:, A_HEADS + h:A_HEADS + h + 1]
        kb = k * beta
        qs.append(q)
        ks.append(k)
        kbs.append(kb)
        rhss.append(jnp.concatenate([v * beta, kb * jnp.exp(gc)], axis=1))
        gcs.append(gc)
        gls.append(gc_alls[ci][c - 1:c, A_HEADS + h:A_HEADS + h + 1])
    e_mats = [exact_tril_dot(jnp.where(strict, g_all[ci * c:(ci + 1) * c, A_HEADS + h:A_HEADS + h + 1], 0.0))
              for ci, h in chains]
    decays = [jnp.where(incl, jnp.exp(jnp.where(incl, e, 0.0)), 0.0) for e in e_mats]
    kks = [_dot_nt(kb, k) for kb, k in zip(kbs, ks)]
    qks = [_dot_nt(q, k) for q, k in zip(qs, ks)]
    pws = [jnp.where(strict, a * d, 0.0) for a, d in zip(kks, decays)]
    qks = [jnp.where(incl, a * d, 0.0) for a, d in zip(qks, decays)]
    t_invs = [eye - lw for lw in pws]
    for _ in range(c.bit_length() - 2):
        pws = [_dot3(pw, pw) for pw in pws]
        t_invs = [ti + _dot3(ti, pw) for ti, pw in zip(t_invs, pws)]
    uws = [_dot(ti, rhs) for ti, rhs in zip(t_invs, rhss)]

    states = [s_ref[h] for h in range(A_HEADS)]
    for ci in range(n_chunks):
        idx = [ci * A_HEADS + h for h in range(A_HEADS)]
        ws = [_dot(uws[i][:, A_DV:], states[h]) for h, i in enumerate(idx)]
        os = [_dot(qs[i] * jnp.exp(gcs[i]), states[h]) for h, i in enumerate(idx)]
        es = [uws[i][:, :A_DV] - w for i, w in zip(idx, ws)]
        os = [o + _dot(qks[i], e) for o, i, e in zip(os, idx, es)]
        upd = [_dot_tn(ks[i] * jnp.exp(gls[i] - gcs[i]), e) for i, e in zip(idx, es)]
        states = [s * jnp.exp(gls[i]) + d for s, i, d in zip(states, idx, upd)]
        r0 = ci * c
        r1 = min(rows, r0 + c)
        for h, o in enumerate(os):
            o = o if r1 - r0 == c else o[:r1 - r0]
            o_ref[r0:r1, h * A_DV:(h + 1) * A_DV] = (
                _rms(o, nwa) * _silu(z[r0:r1, h * A_DV:(h + 1) * A_DV]))
    for h in range(A_HEADS):
        s_ref[h] = states[h]

    @pl.when(n == pl.num_programs(1) - 1)
    def _():
        snew_ref[...] = s_ref[...]


GDN_ROWS = 8 * CHUNK


def _gdn(proj, conv0, s0, w_conv, gate_par, norm_a):
    b, t, _ = proj.shape
    rows = min(GDN_ROWS, t)
    assert t % rows == 0 and (rows % CHUNK == 0 or rows == t)
    n = t // rows
    kernel = functools.partial(_gdn_kernel, rows=rows)
    return pl.pallas_call(
        kernel,
        grid=(b, n),
        in_specs=[pl.BlockSpec((None, rows, CONV_DIM), lambda i, j: (i, j, 0)),
                  pl.BlockSpec((None, rows, A_WIDTH), lambda i, j: (i, j, COL_Z // A_WIDTH)),
                  pl.BlockSpec((None, rows, GATE_PAD), lambda i, j: (i, j, COL_G // GATE_PAD)),
                  pl.BlockSpec((None, 8, CONV_DIM), lambda i, j: (i, 0, 0)),
                  pl.BlockSpec((None, A_HEADS, A_DK, A_DV), lambda i, j: (i, 0, 0, 0)),
                  pl.BlockSpec((CONV_W, CONV_DIM), lambda i, j: (0, 0)),
                  pl.BlockSpec((8, GATE_PAD), lambda i, j: (0, 0)),
                  pl.BlockSpec((1, A_DV), lambda i, j: (0, 0))],
        out_specs=[pl.BlockSpec((None, rows, A_WIDTH), lambda i, j: (i, j, 0)),
                   pl.BlockSpec((None, A_HEADS, A_DK, A_DV), lambda i, j: (i, 0, 0, 0))],
        out_shape=[jax.ShapeDtypeStruct((b, t, A_WIDTH), F32),
                   jax.ShapeDtypeStruct((b, A_HEADS, A_DK, A_DV), F32)],
        scratch_shapes=[pltpu.VMEM((8 + -(-rows // CHUNK) * CHUNK, CONV_DIM), F32),
                        pltpu.VMEM((A_HEADS, A_DK, A_DV), F32)],
        compiler_params=pltpu.CompilerParams(dimension_semantics=("arbitrary", "arbitrary"),
                                             vmem_limit_bytes=VMEM_LIMIT),
        name="gdn",
    )(proj, proj, proj, conv0, s0, w_conv, gate_par, norm_a)


HALF = 128 // CHUNK


def _gdn_pair_kernel(xa_ref, z_ref, gt_ref, conv0_ref, s0_ref, wconv_ref, gp_ref, nwa_ref,
                     o_ref, snew_ref, ext_ref, s_ref, *, rows, nb, conv_done, side_work=None):
    n = pl.program_id(1)
    c = CHUNK
    n_chunks = -(-rows // c)
    rp = n_chunks * c
    assert A_HEADS % 2 == 0 and HALF == 2 and A_DK == 128 and A_DV == 128

    @pl.when(n == 0)
    def _():
        ext_ref[:, 0:8, :] = conv0_ref[...]
        s_ref[...] = s0_ref[...]

    if side_work is not None:
        side_work()

    row = lax.broadcasted_iota(jnp.int32, (rp, 1), 0)
    live = row < rows
    lane_g = lax.broadcasted_iota(jnp.int32, (rp, GATE_PAD), 1)
    gate_lanes = (lane_g >= A_HEADS) & (lane_g < 2 * A_HEADS) & live

    ii = lax.broadcasted_iota(jnp.int32, (c, 128), 0)
    ll = lax.broadcasted_iota(jnp.int32, (c, 128), 1)
    jj = ll & (c - 1)
    low = ll < c
    incl = ii >= jj
    strict = ii > jj
    eye = jnp.where(ii == jj, 1.0, 0.0).astype(F32)
    ti = lax.broadcasted_iota(jnp.int32, (c, c), 0)
    tj = lax.broadcasted_iota(jnp.int32, (c, c), 1)
    tril = jnp.where(ti >= tj, 1.0, 0.0).astype(BF16)
    dd = functools.partial(jnp.dot, preferred_element_type=F32)

    def exact_tril_dot(x):
        hi, lo = _split2(x)
        lo2 = (x - hi.astype(F32) - lo.astype(F32)).astype(BF16)
        return dd(tril, hi) + dd(tril, lo) + dd(tril, lo2)

    def bd(x):
        return jnp.concatenate([jnp.where(low, x, 0.0), jnp.where(low, 0.0, x)], axis=0).astype(BF16)

    def unstack(y):
        return jnp.where(low, y[:c], y[c:])

    nwa = nwa_ref[...]
    chains = []
    for bi in range(nb):
        if conv_done:
            assert rows == rp
            conv = xa_ref[bi]
        else:
            ext_ref[bi, 8:8 + rows, :] = xa_ref[bi]
            if rows < rp:
                ext_ref[bi, 8 + rows:8 + rp, :] = jnp.zeros((rp - rows, CONV_DIM), F32)
            conv = _causal_conv_silu(ext_ref[bi, 0:8 + rp, :], wconv_ref, rp)
            ext_ref[bi, 0:8, :] = ext_ref[bi, rp:rp + 8, :]
            if rows < rp:
                conv = jnp.where(live, conv, 0.0)

        gt = gt_ref[bi]
        if rows < rp:
            gt = jnp.concatenate([gt, jnp.zeros((rp - rows, GATE_PAD), F32)], axis=0)
        beta_all = jnp.where(live, _sigmoid(gt), 0.0)
        g_all = jnp.where(gate_lanes, -jnp.exp(gp_ref[0:1, :]) * _softplus(gt + gp_ref[1:2, :]), 0.0)

        for ci in range(n_chunks):
            rs = slice(ci * c, (ci + 1) * c)
            gc_all = exact_tril_dot(g_all[rs])
            for p in range(A_HEADS // 2):
                hd = []
                for h in (2 * p, 2 * p + 1):
                    q = conv[rs, h * A_DK:(h + 1) * A_DK]
                    k = conv[rs, A_HEADS * A_DK + h * A_DK:A_HEADS * A_DK + (h + 1) * A_DK]
                    v = conv[rs, 2 * A_HEADS * A_DK + h * A_DV:2 * A_HEADS * A_DK + (h + 1) * A_DV]
                    if not conv_done:
                        q = _l2norm(q) * (A_DK ** -0.5)
                        k = _l2norm(k)
                    beta = beta_all[rs, h:h + 1]
                    g = g_all[rs, A_HEADS + h:A_HEADS + h + 1]
                    gc = gc_all[:, A_HEADS + h:A_HEADS + h + 1]
                    gl = gc_all[c - 1:c, A_HEADS + h:A_HEADS + h + 1]
                    kb = k * beta
                    hd.append(dict(q=q, k=k, kb=kb, g=g, qg=q * jnp.exp(gc), kd=k * jnp.exp(gl - gc),
                                   dl=jnp.exp(gl),
                                   rhs=jnp.concatenate([v * beta, kb * jnp.exp(gc)], axis=1)))
                chains.append(dict(bi=bi, ci=ci, p=p, hd=hd))

    for ch in chains:
        a, b = ch["hd"]
        ch["e_mat"] = exact_tril_dot(jnp.where(strict, jnp.where(low, a["g"], b["g"]), 0.0))
    for ch in chains:
        a, b = ch["hd"]
        k_st = jnp.concatenate([a["k"], b["k"]], axis=0)
        ch["kk"] = _dot_nt(jnp.concatenate([a["kb"], b["kb"]], axis=0), k_st)
        ch["qk"] = _dot_nt(jnp.concatenate([a["q"], b["q"]], axis=0), k_st)
    for ch in chains:
        decay = jnp.where(incl, jnp.exp(jnp.where(incl, ch["e_mat"], 0.0)), 0.0)
        ch["lower"] = jnp.where(strict, unstack(ch["kk"]) * decay, 0.0)
        ch["qk"] = jnp.where(incl, unstack(ch["qk"]) * decay, 0.0)
        ch["x"] = ch["lower"]
        ch["xb"] = bd(ch["lower"])
        ch["t"] = eye - ch["lower"]
    for _ in range(c.bit_length() - 2):
        for ch in chains:
            ch["x"] = jnp.dot(ch["x"].astype(BF16), ch["xb"], preferred_element_type=F32)
        for ch in chains:
            ch["xb"] = bd(ch["x"])
            ch["t"] = ch["t"] + jnp.dot(ch["t"].astype(BF16), ch["xb"], preferred_element_type=F32)
    for ch in chains:
        lh, ll_ = _split2(ch["lower"])
        th = ch["t"].astype(BF16).astype(F32)
        tl = ch["t"] - th
        bth = bd(th)
        lt = dd(lh, bth) + dd(lh, bd(tl)) + dd(ll_, bth)
        ch["res"] = eye - ch["t"] - lt
    for ch in chains:
        ch["t"] = ch["t"] + jnp.dot(ch["t"].astype(BF16), bd(ch["res"]), preferred_element_type=F32)
    for ch in chains:
        a, b = ch["hd"]
        uw = jnp.dot(bd(ch["t"]), jnp.concatenate([a["rhs"], b["rhs"]], axis=0).astype(BF16),
                     preferred_element_type=F32)
        a["u"], a["w"] = uw[:c, :A_DV], uw[:c, A_DV:]
        b["u"], b["w"] = uw[c:, :A_DV], uw[c:, A_DV:]
        ch["qkb"] = bd(ch["qk"])

    by_key = {(ch["bi"], ch["ci"], ch["p"]): ch for ch in chains}
    states = {(bi, h): s_ref[bi, h] for bi in range(nb) for h in range(A_HEADS)}
    for ci in range(n_chunks):
        cur = [by_key[(bi, ci, p)] for bi in range(nb) for p in range(A_HEADS // 2)]
        for ch in cur:
            for hh, d in enumerate(ch["hd"]):
                ws = _dot(jnp.concatenate([d["w"], d["qg"]], axis=0), states[(ch["bi"], 2 * ch["p"] + hh)])
                d["e"] = d["u"] - ws[:c]
                d["o"] = ws[c:]
        for ch in cur:
            a, b = ch["hd"]
            o2 = jnp.dot(ch["qkb"], jnp.concatenate([a["e"], b["e"]], axis=0).astype(BF16),
                         preferred_element_type=F32)
            a["o"] = a["o"] + o2[:c]
            b["o"] = b["o"] + o2[c:]
        for ch in cur:
            for hh, d in enumerate(ch["hd"]):
                key = (ch["bi"], 2 * ch["p"] + hh)
                states[key] = states[key] * d["dl"] + _dot_tn(d["kd"], d["e"])
        r0 = ci * c
        r1 = min(rows, r0 + c)
        for ch in cur:
            for hh, d in enumerate(ch["hd"]):
                h = 2 * ch["p"] + hh
                o = d["o"] if r1 - r0 == c else d["o"][:r1 - r0]
                o_ref[ch["bi"], r0:r1, h * A_DV:(h + 1) * A_DV] = (
                    _rms(o, nwa) * _silu(z_ref[ch["bi"], r0:r1, h * A_DV:(h + 1) * A_DV]))
    for (bi, h), s in states.items():
        s_ref[bi, h] = s

    @pl.when(n == pl.num_programs(1) - 1)
    def _():
        snew_ref[...] = s_ref[...]


def _gdn_pair(proj, conv0, s0, w_conv, gate_par, norm_a, conv_done, side_of_grid=None):
    b, t, _ = proj.shape
    rows = min(GDN_ROWS, t)
    assert t % rows == 0 and (rows % CHUNK == 0 or rows == t)
    n = t // rows
    nb = max(1, min(b, GDN_ROWS // (-(-rows // CHUNK) * CHUNK)))
    assert b % nb == 0
    rp = 0 if conv_done else -(-rows // CHUNK) * CHUNK
    in_specs = [pl.BlockSpec((nb, rows, CONV_DIM), lambda i, j: (i, j, 0)),
                pl.BlockSpec((nb, rows, A_WIDTH), lambda i, j: (i, j, COL_Z // A_WIDTH)),
                pl.BlockSpec((nb, rows, GATE_PAD), lambda i, j: (i, j, COL_G // GATE_PAD)),
                pl.BlockSpec((nb, 8, CONV_DIM), lambda i, j: (i, 0, 0)),
                pl.BlockSpec((nb, A_HEADS, A_DK, A_DV), lambda i, j: (i, 0, 0, 0)),
                pl.BlockSpec((CONV_W, CONV_DIM), lambda i, j: (0, 0)),
                pl.BlockSpec((8, GATE_PAD), lambda i, j: (0, 0)),
                pl.BlockSpec((1, A_DV), lambda i, j: (0, 0))]
    out_specs = [pl.BlockSpec((nb, rows, A_WIDTH), lambda i, j: (i, j, 0)),
                 pl.BlockSpec((nb, A_HEADS, A_DK, A_DV), lambda i, j: (i, 0, 0, 0))]
    out_shape = [jax.ShapeDtypeStruct((b, t, A_WIDTH), F32),
                 jax.ShapeDtypeStruct((b, A_HEADS, A_DK, A_DV), F32)]
    inputs = (proj, proj, proj, conv0, s0, w_conv, gate_par, norm_a)
    n_in, n_out = len(in_specs), len(out_specs)
    side = side_of_grid((b // nb, n)) if side_of_grid is not None else None

    def kernel(*refs):
        n_side_in = len(side["in_specs"]) if side else 0
        gin, sin = refs[:n_in], refs[n_in:n_in + n_side_in]
        gout = refs[n_in + n_side_in:n_in + n_side_in + n_out]
        sout = refs[n_in + n_side_in + n_out:len(refs) - 2]
        side_work = (lambda: side["kernel"](*sin, *sout)) if side else None
        _gdn_pair_kernel(*gin, *gout, *refs[len(refs) - 2:], rows=rows, nb=nb, conv_done=conv_done,
                         side_work=side_work)

    if side:
        inputs = inputs + tuple(side["inputs"])
        in_specs = in_specs + list(side["in_specs"])
        out_specs = out_specs + list(side["out_specs"])
        out_shape = out_shape + list(side["out_shape"])
    outs = pl.pallas_call(
        kernel,
        grid=(b // nb, n),
        in_specs=in_specs,
        out_specs=out_specs,
        out_shape=out_shape,
        scratch_shapes=[pltpu.VMEM((nb, 8 + rp, CONV_DIM), F32),
                        pltpu.VMEM((nb, A_HEADS, A_DK, A_DV), F32)],
        compiler_params=pltpu.CompilerParams(dimension_semantics=("arbitrary", "arbitrary"),
                                             vmem_limit_bytes=VMEM_LIMIT),
        name="gdn_side" if side else "gdn",
    )(*inputs)
    return outs[:n_out], outs[n_out:]


def _slope(h):
    return 2.0 ** (-8.0 * (h + 1) / B_HEADS)


ATTN_SB = MAX_WINDOW
ATTN_SKEW = 2
PAIR = 128 // B_HEAD_DIM


def _attn_fused_kernel(q_ref, kp_ref, kc_ref, vp_ref, vc_ref, nwb_ref, o_ref, kt_ref, vt_ref, *stats):
    hp = pl.program_id(1)
    n = pl.program_id(2)
    sb = ATTN_SB
    blk = STEPS

    lane_q = lax.broadcasted_iota(jnp.int32, (blk, 128), 1)
    low_q = lane_q < B_HEAD_DIM
    qq = lax.broadcasted_iota(jnp.int32, (2 * blk, 2 * blk), 0)
    kk = lax.broadcasted_iota(jnp.int32, (2 * blk, 2 * blk), 1)
    steps_back = (qq & (blk - 1)) + blk - kk
    in_band = (steps_back >= 0) & (steps_back <= STEPS)
    slope_lo = jnp.float32(_slope(0))
    slope_hi = jnp.float32(_slope(1))
    for i in range(1, B_HEADS // PAIR):
        slope_lo = jnp.where(hp == i, _slope(PAIR * i), slope_lo)
        slope_hi = jnp.where(hp == i, _slope(PAIR * i + 1), slope_hi)
    slope = jnp.where(qq < blk, slope_lo, slope_hi)

    biases = []
    for (_, dil) in DILATED:
        bias = jnp.where(in_band, -slope * (steps_back * dil).astype(F32), NEG)
        biases.append((bias, jnp.where(kk >= blk, bias, NEG)))
    n_blk = sb // blk
    work = [(bi, it) for bi in range(len(DILATED)) for it in range(n_blk)]

    def rows(dil, start, size):
        return pl.ds(start, size) if dil == 1 else pl.ds(start, size, stride=dil)

    def key_tile(prev_ref, cur_ref, dil, k0):
        if k0 >= sb:
            return cur_ref[rows(dil, k0 - sb, 2 * blk), :].astype(BF16)
        assert k0 + blk * dil >= sb > k0 + (blk - 1) * dil
        return jnp.concatenate([prev_ref[rows(dil, k0, blk), :].astype(BF16),
                                cur_ref[rows(dil, k0 + blk * dil - sb, blk), :].astype(BF16)], axis=0)

    def issue_scores(bi, it):
        dil = DILATED[bi][1]
        q0 = (it // dil) * (blk * dil) + it % dil
        k0 = sb + q0 - blk * dil
        q = q_ref[rows(dil, q0, blk), :] * (B_HEAD_DIM ** -0.5)
        q2 = jnp.concatenate([jnp.where(low_q, q, 0.0), jnp.where(low_q, 0.0, q)], axis=0).astype(BF16)
        k = key_tile(kp_ref, kc_ref, dil, k0)
        return dict(bi=bi, dil=dil, q0=q0, k0=k0,
                    s=lax.dot_general(q2, k, (((1,), (1,)), ((), ())), preferred_element_type=F32))

    def softmax(w):
        bias, bias_first = biases[w["bi"]]
        s = w.pop("s") + (jnp.where(n == 0, bias_first, bias) if w["q0"] < blk * w["dil"] else bias)
        m = jnp.max(s, axis=-1, keepdims=True)
        p = jnp.exp(s - m)
        w.update(m=m, l=jnp.sum(p, axis=-1, keepdims=True), p=p.astype(BF16))

    def issue_values(w):
        dil, q0 = w["dil"], w["q0"]
        m_s, l_s, num_s = stats[3 * w["bi"]:3 * w["bi"] + 3]
        v = key_tile(vp_ref, vc_ref, dil, w["k0"])
        pv = jnp.dot(w["p"], v, preferred_element_type=F32)
        m_s[rows(dil, q0, blk), :] = jnp.where(low_q, w["m"][:blk], w["m"][blk:])
        l_s[rows(dil, q0, blk), :] = jnp.where(low_q, w["l"][:blk], w["l"][blk:])
        num_s[rows(dil, q0, blk), :] = jnp.where(low_q, pv[:blk], pv[blk:])

    inflight = {}
    for t in range(len(work) + 2 * ATTN_SKEW):
        if t < len(work):
            inflight[t] = issue_scores(*work[t])
        if 0 <= t - ATTN_SKEW < len(work):
            softmax(inflight[t - ATTN_SKEW])
        if 0 <= t - 2 * ATTN_SKEW < len(work):
            issue_values(inflight.pop(t - 2 * ATTN_SKEW))

    nwb = nwb_ref[...]

    def combine(i, carry):
        rs = pl.ds(pl.multiple_of(i * blk, blk), blk)
        ms = [stats[3 * bi][rs, :] for bi in range(len(DILATED))]
        ls = [stats[3 * bi + 1][rs, :] for bi in range(len(DILATED))]
        nums = [stats[3 * bi + 2][rs, :] for bi in range(len(DILATED))]
        mx = jnp.maximum(jnp.maximum(ms[0], ms[1]), ms[2])
        ws = [jnp.exp(m - mx) for m in ms]
        den = ws[0] * ls[0] + ws[1] * ls[1] + ws[2] * ls[2]
        o = (ws[0] * nums[0] + ws[1] * nums[1] + ws[2] * nums[2]) / den
        sq = o * o
        ss_lo = jnp.sum(jnp.where(low_q, sq, 0.0), axis=-1, keepdims=True)
        ss_hi = jnp.sum(jnp.where(low_q, 0.0, sq), axis=-1, keepdims=True)
        mean_sq = jnp.where(low_q, ss_lo, ss_hi) / B_HEAD_DIM
        o_ref[rs, :] = o * lax.rsqrt(mean_sq + EPS) * nwb
        return carry

    lax.fori_loop(0, sb // blk, combine, 0, unroll=4)

    @pl.when(n == pl.num_programs(2) - 1)
    def _():
        kt_ref[...] = kc_ref[...].T
        vt_ref[...] = vc_ref[...].T


def _attn_prompt(qkv, norm_b_pair):
    b, s, _ = qkv.shape
    sb = ATTN_SB
    assert s % sb == 0 and sb == min(MAX_WINDOW, s)
    assert all(w == STEPS * d and sb % w == 0 and d & (d - 1) == 0 for (w, d) in DILATED)
    n_pairs = B_HEADS // PAIR
    blk = (None, sb, 128)
    prev = lambda c0: pl.BlockSpec(blk, lambda i, p, n: (i, jnp.maximum(n - 1, 0), c0 + p))
    cur = lambda c0: pl.BlockSpec(blk, lambda i, p, n: (i, n, c0 + p))
    return pl.pallas_call(
        _attn_fused_kernel,
        grid=(b, n_pairs, s // sb),
        in_specs=[cur(0), prev(n_pairs), cur(n_pairs), prev(2 * n_pairs), cur(2 * n_pairs),
                  pl.BlockSpec((1, 128), lambda i, p, n: (0, 0))],
        out_specs=[pl.BlockSpec(blk, lambda i, p, n: (i, n, p)),
                   pl.BlockSpec((None, 128, sb), lambda i, p, n: (i, p, 0)),
                   pl.BlockSpec((None, 128, sb), lambda i, p, n: (i, p, 0))],
        out_shape=[jax.ShapeDtypeStruct((b, s, B_WIDTH), F32),
                   jax.ShapeDtypeStruct((b, B_WIDTH, sb), F32),
                   jax.ShapeDtypeStruct((b, B_WIDTH, sb), F32)],
        scratch_shapes=[pltpu.VMEM((sb, 128), F32)] * (3 * len(DILATED)),
        compiler_params=pltpu.CompilerParams(
            dimension_semantics=("arbitrary", "arbitrary", "arbitrary"), vmem_limit_bytes=VMEM_LIMIT),
        name="attn_prompt",
    )(qkv, qkv, qkv, qkv, qkv, norm_b_pair)


def _attn_sample_kernel(q_ref, kn_ref, vn_ref, kt_ref, vt_ref, nwb_ref, o_ref, kto_ref, vto_ref, *, n_past, t):
    kn = kn_ref[...]
    vn = vn_ref[...]
    kt = kt_ref[...]
    vt = vt_ref[...]

    prow = lax.broadcasted_iota(jnp.int32, (t, 128), 0)
    plane = lax.broadcasted_iota(jnp.int32, (t, 128), 1)
    place = jnp.where(plane == 128 - t + prow, 1.0, 0.0).astype(BF16)
    tail_lanes = lax.broadcasted_iota(jnp.int32, (B_WIDTH, 128), 1) >= 128 - t
    tn = functools.partial(lax.dot_general, dimension_numbers=(((0,), (0,)), ((), ())),
                           preferred_element_type=F32)

    def shifted(win, new, out_ref):
        hi, lo = _split2(new)
        lo2 = (new - hi.astype(F32) - lo.astype(F32)).astype(BF16)
        new_t = tn(hi, place) + tn(lo, place) + tn(lo2, place)
        sh = pltpu.roll(win, n_past - t, axis=1)
        out_ref[:, :n_past - 128] = sh[:, :n_past - 128]
        out_ref[:, n_past - 128:] = jnp.where(tail_lanes, new_t, sh[:, n_past - 128:])

    shifted(kt, kn, kto_ref)
    shifted(vt, vn, vto_ref)

    rq = B_HEADS * t
    t_bits = t.bit_length() - 1
    d_bits = B_HEAD_DIM.bit_length() - 1
    rowi = lax.broadcasted_iota(jnp.int32, (rq, B_WIDTH), 0)
    lanei = lax.broadcasted_iota(jnp.int32, (rq, B_WIDTH), 1)
    own = lax.shift_right_logical(rowi, t_bits) == lax.shift_right_logical(lanei, d_bits)
    q_rep = jnp.concatenate([q_ref[...]] * B_HEADS, axis=0)
    q_blk = jnp.where(own, q_rep, 0.0).astype(BF16)
    zpad = jnp.zeros((128 - t, B_WIDTH), BF16)
    kn_pad = jnp.concatenate([kn.astype(BF16), zpad], axis=0)
    vn_pad = jnp.concatenate([vn.astype(BF16), zpad], axis=0)
    vt16 = vt.astype(BF16)
    scale = B_HEAD_DIM ** -0.5
    s_c = jnp.dot(q_blk, kt.astype(BF16), preferred_element_type=F32) * scale
    s_n = lax.dot_general(q_blk, kn_pad, (((1,), (1,)), ((), ())), preferred_element_type=F32) * scale

    r1 = lax.broadcasted_iota(jnp.int32, (rq, 1), 0)
    head = lax.shift_right_logical(r1, t_bits)
    slope = jnp.zeros((rq, 1), F32)
    for h in range(B_HEADS):
        slope = jnp.where(head == h, _slope(h), slope)
    tok = r1 & (t - 1)
    delta_c = n_past + tok - lax.broadcasted_iota(jnp.int32, (rq, n_past), 1)
    key_n = lax.broadcasted_iota(jnp.int32, (rq, 128), 1)
    delta_n = tok - key_n
    s_c = s_c - slope * delta_c.astype(F32)
    s_n = s_n - slope * delta_n.astype(F32)
    live_n = (key_n < t) & (delta_n >= 0)

    ms, ls, pcs, pns = [], [], [], []
    for (_, dil) in DILATED:
        valid_c = ((delta_c & (dil - 1)) == 0) & (delta_c <= STEPS * dil)
        valid_n = live_n & ((delta_n & (dil - 1)) == 0) & (delta_n <= STEPS * dil)
        sd_c = jnp.where(valid_c, s_c, NEG)
        sd_n = jnp.where(valid_n, s_n, NEG)
        m = jnp.maximum(jnp.max(sd_c, axis=-1, keepdims=True), jnp.max(sd_n, axis=-1, keepdims=True))
        p_c = jnp.exp(sd_c - m)
        p_n = jnp.exp(sd_n - m)
        ms.append(m)
        ls.append(jnp.sum(p_c, axis=-1, keepdims=True) + jnp.sum(p_n, axis=-1, keepdims=True))
        pcs.append(p_c.astype(BF16))
        pns.append(p_n.astype(BF16))
    num_all = (lax.dot_general(jnp.concatenate(pcs, axis=0), vt16, (((1,), (1,)), ((), ())),
                               preferred_element_type=F32)
               + jnp.dot(jnp.concatenate(pns, axis=0), vn_pad, preferred_element_type=F32))
    nums = [num_all[i * rq:(i + 1) * rq] for i in range(len(DILATED))]
    mx = jnp.maximum(jnp.maximum(ms[0], ms[1]), ms[2])
    ws = [jnp.exp(m - mx) for m in ms]
    den = ws[0] * ls[0] + ws[1] * ls[1] + ws[2] * ls[2]
    o = (ws[0] * nums[0] + ws[1] * nums[1] + ws[2] * nums[2]) / den
    o = jnp.where(own, o, 0.0)
    o = o * lax.rsqrt(jnp.sum(o * o, axis=-1, keepdims=True) / B_HEAD_DIM + EPS) * nwb_ref[...]
    acc = o[0:t, :]
    for h in range(1, B_HEADS):
        acc = acc + o[h * t:(h + 1) * t, :]
    o_ref[...] = acc


def _attn_sample_parts(qkv, win_kt, win_vt, norm_b_tiled, seq_of_step):
    b, t, _ = qkv.shape
    n_past = win_kt.shape[2]
    assert n_past == MAX_WINDOW and n_past % 128 == 0 and t % 8 == 0 and t & (t - 1) == 0 and t <= 128
    assert all(d & (d - 1) == 0 for (_, d) in DILATED)
    col = lambda c: pl.BlockSpec((None, t, B_WIDTH), lambda *g: (seq_of_step(*g), 0, c))
    win = pl.BlockSpec((None, B_WIDTH, n_past), lambda *g: (seq_of_step(*g), 0, 0))
    return dict(
        kernel=functools.partial(_attn_sample_kernel, n_past=n_past, t=t),
        inputs=(qkv, qkv, qkv, win_kt, win_vt, norm_b_tiled),
        in_specs=[col(0), col(1), col(2), win, win, pl.BlockSpec((1, B_WIDTH), lambda *g: (0, 0))],
        out_specs=[col(0), win, win],
        out_shape=[jax.ShapeDtypeStruct((b, t, B_WIDTH), F32),
                   jax.ShapeDtypeStruct((b, B_WIDTH, n_past), F32),
                   jax.ShapeDtypeStruct((b, B_WIDTH, n_past), F32)])


def _attn_sample(qkv, win_kt, win_vt, norm_b_tiled):
    parts = _attn_sample_parts(qkv, win_kt, win_vt, norm_b_tiled, lambda i: i)
    return pl.pallas_call(
        parts["kernel"],
        grid=(qkv.shape[0],),
        in_specs=parts["in_specs"],
        out_specs=parts["out_specs"],
        out_shape=parts["out_shape"],
        compiler_params=pltpu.CompilerParams(dimension_semantics=("arbitrary",),
                                             vmem_limit_bytes=VMEM_LIMIT),
        name="attn_sample",
    )(*parts["inputs"])


def _out_ffn_kernel(x_ref, oa_ref, ob_ref, wo_ref, nf_ref, wg_ref, wu_ref, wd_ref, nfin_ref, y_ref, *, ff_chunk):
    mixed = jnp.concatenate([oa_ref[...], ob_ref[...]], axis=-1).astype(BF16)
    x1 = x_ref[...] + jnp.dot(mixed, wo_ref[...], preferred_element_type=F32)
    hf = _rms(x1, nf_ref[...]).astype(BF16)
    x2 = x1
    for c0 in range(0, wg_ref.shape[1], ff_chunk):
        g = jnp.dot(hf, wg_ref[:, c0:c0 + ff_chunk], preferred_element_type=F32)
        u = jnp.dot(hf, wu_ref[:, c0:c0 + ff_chunk], preferred_element_type=F32)
        act = (_silu(g) * u).astype(BF16)
        x2 = x2 + jnp.dot(act, wd_ref[c0:c0 + ff_chunk, :], preferred_element_type=F32)
    y_ref[...] = _rms(x2, nfin_ref[...])


def _out_ffn(x2d, o_a, o_b, w_out, norm_ffn, w_gate, w_up, w_down, norm_final, tm):
    t = x2d.shape[0]
    d_ff = w_gate.shape[1]
    ff_chunk = d_ff // 2 if d_ff % 256 == 0 else d_ff
    once = pl.Buffered(1)
    row = lambda w: pl.BlockSpec((tm, w), lambda i: (i, 0))
    full = lambda a, b: pl.BlockSpec((a, b), lambda i: (0, 0), pipeline_mode=once)
    return pl.pallas_call(
        functools.partial(_out_ffn_kernel, ff_chunk=ff_chunk),
        grid=(t // tm,),
        in_specs=[row(D_MODEL), row(A_WIDTH), row(B_WIDTH),
                  full(D_MODEL, D_MODEL), full(1, D_MODEL),
                  full(D_MODEL, d_ff), full(D_MODEL, d_ff), full(d_ff, D_MODEL), full(1, D_MODEL)],
        out_specs=row(D_MODEL),
        out_shape=jax.ShapeDtypeStruct((t, D_MODEL), F32),
        compiler_params=pltpu.CompilerParams(dimension_semantics=("arbitrary",),
                                             vmem_limit_bytes=VMEM_LIMIT),
        name="out_ffn",
    )(x2d, o_a, o_b, w_out, norm_ffn, w_gate, w_up, w_down, norm_final)


def _layer_params(norm_mix, w_in, w_conv, a_log, dt_bias, norm_out_a, norm_out_b, w_out, norm_ffn,
                  w_gate, w_up, w_down, layer):
    w = w_in[layer]
    n_gate = 2 * A_HEADS
    w_p = (w[:, :COL_G].astype(BF16),
           jnp.pad(w[:, COL_G:COL_G + n_gate], ((0, 0), (0, GATE_PAD - n_gate))).astype(BF16),
           w[:, COL_G + n_gate:].astype(BF16))
    gate_par = jnp.zeros((8, GATE_PAD), F32)
    gate_par = gate_par.at[0, A_HEADS:2 * A_HEADS].set(a_log[layer].astype(F32))
    gate_par = gate_par.at[1, A_HEADS:2 * A_HEADS].set(dt_bias[layer].astype(F32))
    return dict(
        norm_mix=norm_mix[layer].reshape(1, D_MODEL), w_p=w_p, w_conv=w_conv[layer], gate_par=gate_par,
        norm_a=norm_out_a[layer].reshape(1, A_DV),
        norm_b_pair=jnp.tile(norm_out_b[layer], PAIR).reshape(1, PAIR * B_HEAD_DIM),
        norm_b_tiled=jnp.tile(norm_out_b[layer], B_HEADS).reshape(1, B_WIDTH),
        w_out=w_out[layer].astype(BF16), norm_ffn=norm_ffn[layer].reshape(1, D_MODEL),
        w_gate=w_gate[layer].astype(BF16), w_up=w_up[layer].astype(BF16), w_down=w_down[layer].astype(BF16))


def _project(x, conv_buf, p, tm):
    b, t, _ = x.shape
    assert t >= CONV_W - 1
    conv0 = jnp.pad(conv_buf, ((0, 0), (8 - (CONV_W - 1), 0), (0, 0)))
    conv_done = t % tm == 0
    if conv_done:
        proj_a, proj_b, tail = _norm_proj_conv(x, p["norm_mix"], p["w_p"], p["w_conv"], conv0, tm)
        new_conv = tail[:, 8 - (CONV_W - 1):]
    else:
        proj_a, proj_b = _norm_proj(x.reshape(b * t, D_MODEL), p["norm_mix"], p["w_p"], tm)
        proj_a = proj_a.reshape(b, t, COLS_A)
        proj_b = proj_b.reshape(b, t, COLS_B)
        new_conv = proj_a[:, t - (CONV_W - 1):, :CONV_DIM]
    return proj_a, proj_b, new_conv, conv0, conv_done


def _finish(x, o_a, o_b, p, norm_final, tm):
    b, t, _ = x.shape
    y = _out_ffn(x.reshape(b * t, D_MODEL), o_a.reshape(b * t, A_WIDTH), o_b.reshape(b * t, B_WIDTH), p["w_out"],
                 p["norm_ffn"], p["w_gate"], p["w_up"], p["w_down"], norm_final.reshape(1, D_MODEL), tm)
    return y.reshape(b, t, D_MODEL)


def kernel(x_prompt, x_sample, state_conv, state_rec, cache_win_k, cache_win_v, norm_mix, w_in, w_conv, a_log,
           dt_bias, norm_out_a, norm_out_b, w_out, norm_ffn, w_gate, w_up, w_down, norm_final):
    depth = w_in.shape[0]
    assert depth == 1, "the final norm is fused into the block of a single-layer trunk"
    p = _layer_params(norm_mix, w_in, w_conv, a_log, dt_bias, norm_out_a, norm_out_b, w_out, norm_ffn,
                      w_gate, w_up, w_down, 0)
    bp, tp, _ = x_prompt.shape
    bs, ts, _ = x_sample.shape
    tm_p, tm_s = 512, bs * ts
    gdn_args = (p["w_conv"], p["gate_par"], p["norm_a"])

    sa, sb, sc, s_conv0, s_done = _project(x_sample, state_conv[0], p, tm_s)
    n_past = cache_win_k.shape[2]
    to_t = lambda a: jnp.transpose(a, (0, 2, 3, 1)).reshape(bs, B_WIDTH, n_past)
    from_t = lambda a: jnp.transpose(a.reshape(a.shape[0], B_HEADS, B_HEAD_DIM, a.shape[2]), (0, 3, 1, 2))
    win_kt, win_vt = to_t(cache_win_k[0]), to_t(cache_win_v[0])

    def sample_attention_of_grid(grid):
        if grid[0] * grid[1] != bs:
            return None
        return _attn_sample_parts(sb, win_kt, win_vt, p["norm_b_tiled"], lambda i, j: i * grid[1] + j)

    pa, pb, pc, p_conv0, p_done = _project(x_prompt, jnp.zeros((bp, CONV_W - 1, CONV_DIM), F32), p, tm_p)
    zero_rec = jnp.zeros((bp, A_HEADS, A_DK, A_DV), F32)
    (o_a_p, pr), side = _gdn_pair(pa, p_conv0, zero_rec, *gdn_args, p_done, sample_attention_of_grid)
    o_b_s, new_kt, new_vt = side if side else _attn_sample(sb, win_kt, win_vt, p["norm_b_tiled"])
    (o_a_s, sr), _ = _gdn_pair(sa, s_conv0, state_rec[0], *gdn_args, s_done)

    o_b_p, pkt, pvt = _attn_prompt(pb, p["norm_b_pair"])

    yp = _finish(x_prompt, o_a_p, o_b_p, p, norm_final, tm_p)
    ys = _finish(x_sample, o_a_s, o_b_s, p, norm_final, tm_s)
    return (yp, ys, pc[None], pr[None], from_t(pkt)[None], from_t(pvt)[None], sc[None], sr[None],
            from_t(new_kt)[None], from_t(new_vt)[None])
```

```python
import functools

import jax
import jax.numpy as jnp
from jax import lax
from jax.experimental import pallas as pl
from jax.experimental.pallas import tpu as pltpu

F32 = jnp.float32
BF16 = jnp.bfloat16

D_MODEL = 1024
A_HEADS = 4
A_DK = 128
A_DV = 128
A_WIDTH = A_HEADS * A_DV
CONV_W = 4
CONV_DIM = 2 * A_HEADS * A_DK + A_HEADS * A_DV
B_HEADS = 8
B_HEAD_DIM = 64
B_WIDTH = B_HEADS * B_HEAD_DIM
DILATED = ((128, 1), (512, 4), (2048, 16))
STEPS = 128
MAX_WINDOW = 2048
EPS = 1e-6
NEG = -1e30
CHUNK = 64
GATE_PAD = 128
COL_Z = CONV_DIM
COL_G = CONV_DIM + A_WIDTH
COLS_A = COL_G + GATE_PAD
COLS_B = 3 * B_WIDTH
VMEM_LIMIT = 56 * 1024 * 1024


def _dot(a, b):
    return jnp.dot(a.astype(BF16), b.astype(BF16), preferred_element_type=F32)


def _dot_nt(a, b):
    return lax.dot_general(a.astype(BF16), b.astype(BF16), (((1,), (1,)), ((), ())),
                           preferred_element_type=F32)


def _dot_tn(a, b):
    return lax.dot_general(a.astype(BF16), b.astype(BF16), (((0,), (0,)), ((), ())),
                           preferred_element_type=F32)


def _split2(a):
    hi = a.astype(BF16)
    lo = (a - hi.astype(F32)).astype(BF16)
    return hi, lo


def _sigmoid(x):
    return 1.0 / (1.0 + jnp.exp(-x))


def _silu(x):
    return x * _sigmoid(x)


def _softplus(x):
    return jnp.maximum(x, 0.0) + jnp.log(1.0 + jnp.exp(-jnp.abs(x)))


def _rms(x, w):
    return x * lax.rsqrt(jnp.mean(x * x, axis=-1, keepdims=True) + EPS) * w


def _norm_proj_kernel(x_ref, nw_ref, wa_ref, wg_ref, wb_ref, oa_ref, ob_ref):
    h = _rms(x_ref[...], nw_ref[...]).astype(BF16)
    oa_ref[:, :COL_G] = jnp.dot(h, wa_ref[...], preferred_element_type=F32)
    oa_ref[:, COL_G:] = jnp.dot(h, wg_ref[...], preferred_element_type=F32)
    ob_ref[...] = jnp.dot(h, wb_ref[...], preferred_element_type=F32)


def _weight_specs():
    zero = lambda *g: (0, 0)
    return [pl.BlockSpec((D_MODEL, c), zero, pipeline_mode=pl.Buffered(1)) for c in (COL_G, GATE_PAD, COLS_B)]


def _norm_proj(x2d, norm_w, w_p, tm):
    t = x2d.shape[0]
    once = pl.Buffered(1)
    return pl.pallas_call(
        _norm_proj_kernel,
        grid=(t // tm,),
        in_specs=[pl.BlockSpec((tm, D_MODEL), lambda i: (i, 0)),
                  pl.BlockSpec((1, D_MODEL), lambda i: (0, 0), pipeline_mode=once)] + _weight_specs(),
        out_specs=[pl.BlockSpec((tm, COLS_A), lambda i: (i, 0)),
                   pl.BlockSpec((tm, COLS_B), lambda i: (i, 0))],
        out_shape=[jax.ShapeDtypeStruct((t, COLS_A), F32),
                   jax.ShapeDtypeStruct((t, COLS_B), F32)],
        compiler_params=pltpu.CompilerParams(dimension_semantics=("arbitrary",),
                                             vmem_limit_bytes=VMEM_LIMIT),
        name="norm_proj",
    )(x2d, norm_w, *w_p)


def _causal_conv_silu(x_all, wconv_ref, n):
    conv = x_all[8:8 + n] * wconv_ref[CONV_W - 1:CONV_W, :]
    for back in range(1, CONV_W):
        tap = CONV_W - 1 - back
        conv = conv + pltpu.roll(x_all, back, axis=0)[8:8 + n] * wconv_ref[tap:tap + 1, :]
    return _silu(conv)


def _l2norm(x):
    return x * lax.rsqrt(jnp.sum(x * x, axis=-1, keepdims=True) + EPS)


def _norm_proj_conv_kernel(x_ref, nw_ref, wa_ref, wg_ref, wb_ref, wconv_ref, conv0_ref, oa_ref, ob_ref, tail_ref,
                           carry_ref, h_ref, raw_ref):
    i = pl.program_id(1)
    tm = x_ref.shape[0]
    wq = A_HEADS * A_DK
    assert CONV_DIM == 3 * wq and COLS_B == 3 * B_WIDTH
    dd = functools.partial(jnp.dot, preferred_element_type=F32)

    @pl.when(i == 0)
    def _():
        carry_ref[...] = conv0_ref[...]

    def conv_cols(g):
        cs = slice(g * wq, (g + 1) * wq)
        conv = _causal_conv_silu(jnp.concatenate([carry_ref[:, cs], raw_ref[:, cs]], axis=0),
                                 wconv_ref.at[:, cs], tm)
        carry_ref[:, cs] = raw_ref[tm - 8:tm, cs]
        if g == 2:
            oa_ref[:, cs] = conv
            return
        for hd in range(A_HEADS):
            hs = slice(hd * A_DK, (hd + 1) * A_DK)
            out = _l2norm(conv[:, hs])
            oa_ref[:, g * wq + hd * A_DK:g * wq + (hd + 1) * A_DK] = out * (A_DK ** -0.5) if g == 0 else out

    def phase(k):
        if k == 0:
            h_ref[...] = _rms(x_ref[...], nw_ref[...]).astype(BF16)
        h = h_ref[...]
        if k < 3:
            raw_ref[:, k * wq:(k + 1) * wq] = dd(h, wa_ref[:, k * wq:(k + 1) * wq])
        if k == 0:
            oa_ref[:, CONV_DIM:COL_G] = dd(h, wa_ref[:, CONV_DIM:])
        else:
            ob_ref[:, (k - 1) * B_WIDTH:k * B_WIDTH] = dd(h, wb_ref[:, (k - 1) * B_WIDTH:k * B_WIDTH])
            conv_cols(k - 1)
        if k == 3:
            oa_ref[:, COL_G:] = dd(h, wg_ref[...])

    for k in range(4):
        pl.when(i >= 0)(functools.partial(phase, k))

    @pl.when(i == pl.num_programs(1) - 1)
    def _():
        tail_ref[...] = raw_ref[tm - 8:tm, :]


def _norm_proj_conv(x, norm_w, w_p, w_conv, conv0, tm):
    b, t, _ = x.shape
    assert t % tm == 0 and tm % 8 == 0
    once = pl.Buffered(1)
    return pl.pallas_call(
        _norm_proj_conv_kernel,
        grid=(b, t // tm),
        in_specs=[pl.BlockSpec((None, tm, D_MODEL), lambda i, j: (i, j, 0)),
                  pl.BlockSpec((1, D_MODEL), lambda i, j: (0, 0), pipeline_mode=once)] + _weight_specs() + [
                  pl.BlockSpec((CONV_W, CONV_DIM), lambda i, j: (0, 0), pipeline_mode=once),
                  pl.BlockSpec((None, 8, CONV_DIM), lambda i, j: (i, 0, 0))],
        out_specs=[pl.BlockSpec((None, tm, COLS_A), lambda i, j: (i, j, 0)),
                   pl.BlockSpec((None, tm, COLS_B), lambda i, j: (i, j, 0)),
                   pl.BlockSpec((None, 8, CONV_DIM), lambda i, j: (i, 0, 0))],
        out_shape=[jax.ShapeDtypeStruct((b, t, COLS_A), F32),
                   jax.ShapeDtypeStruct((b, t, COLS_B), F32),
                   jax.ShapeDtypeStruct((b, 8, CONV_DIM), F32)],
        scratch_shapes=[pltpu.VMEM((8, CONV_DIM), F32), pltpu.VMEM((tm, D_MODEL), BF16),
                        pltpu.VMEM((tm, CONV_DIM), F32)],
        compiler_params=pltpu.CompilerParams(dimension_semantics=("arbitrary", "arbitrary"),
                                             vmem_limit_bytes=VMEM_LIMIT),
        name="norm_proj_conv",
    )(x, norm_w, *w_p, w_conv, conv0)


GDN_ROWS = 8 * CHUNK


HALF = 128 // CHUNK


def _gdn_pair_kernel(xa_ref, z_ref, gt_ref, conv0_ref, s0_ref, wconv_ref, gp_ref, nwa_ref,
                     o_ref, snew_ref, ext_ref, s_ref, *, rows, nb, conv_done, side_work=None):
    n = pl.program_id(1)
    c = CHUNK
    n_chunks = -(-rows // c)
    rp = n_chunks * c
    assert A_HEADS % 2 == 0 and HALF == 2 and A_DK == 128 and A_DV == 128

    @pl.when(n == 0)
    def _():
        ext_ref[:, 0:8, :] = conv0_ref[...]
        s_ref[...] = s0_ref[...]

    if side_work is not None:
        side_work()

    row = lax.broadcasted_iota(jnp.int32, (rp, 1), 0)
    live = row < rows
    lane_g = lax.broadcasted_iota(jnp.int32, (rp, GATE_PAD), 1)
    gate_lanes = (lane_g >= A_HEADS) & (lane_g < 2 * A_HEADS) & live

    ii = lax.broadcasted_iota(jnp.int32, (c, 128), 0)
    ll = lax.broadcasted_iota(jnp.int32, (c, 128), 1)
    jj = ll & (c - 1)
    low = ll < c
    incl = ii >= jj
    strict = ii > jj
    eye = jnp.where(ii == jj, 1.0, 0.0).astype(F32)
    ti = lax.broadcasted_iota(jnp.int32, (c, c), 0)
    tj = lax.broadcasted_iota(jnp.int32, (c, c), 1)
    tril = jnp.where(ti >= tj, 1.0, 0.0).astype(BF16)
    dd = functools.partial(jnp.dot, preferred_element_type=F32)

    def exact_tril_dot(x):
        hi, lo = _split2(x)
        lo2 = (x - hi.astype(F32) - lo.astype(F32)).astype(BF16)
        return dd(tril, hi) + dd(tril, lo) + dd(tril, lo2)

    def bd(x):
        return jnp.concatenate([jnp.where(low, x, 0.0), jnp.where(low, 0.0, x)], axis=0).astype(BF16)

    def unstack(y):
        return jnp.where(low, y[:c], y[c:])

    nwa = nwa_ref[...]
    chains = []
    for bi in range(nb):
        if conv_done:
            assert rows == rp
            conv = xa_ref[bi]
        else:
            ext_ref[bi, 8:8 + rows, :] = xa_ref[bi]
            if rows < rp:
                ext_ref[bi, 8 + rows:8 + rp, :] = jnp.zeros((rp - rows, CONV_DIM), F32)
            conv = _causal_conv_silu(ext_ref[bi, 0:8 + rp, :], wconv_ref, rp)
            ext_ref[bi, 0:8, :] = ext_ref[bi, rp:rp + 8, :]
            if rows < rp:
                conv = jnp.where(live, conv, 0.0)

        gt = gt_ref[bi]
        if rows < rp:
            gt = jnp.concatenate([gt, jnp.zeros((rp - rows, GATE_PAD), F32)], axis=0)
        beta_all = jnp.where(live, _sigmoid(gt), 0.0)
        g_all = jnp.where(gate_lanes, -jnp.exp(gp_ref[0:1, :]) * _softplus(gt + gp_ref[1:2, :]), 0.0)

        for ci in range(n_chunks):
            rs = slice(ci * c, (ci + 1) * c)
            gc_all = exact_tril_dot(g_all[rs])
            for p in range(A_HEADS // 2):
                hd = []
                for h in (2 * p, 2 * p + 1):
                    q = conv[rs, h * A_DK:(h + 1) * A_DK]
                    k = conv[rs, A_HEADS * A_DK + h * A_DK:A_HEADS * A_DK + (h + 1) * A_DK]
                    v = conv[rs, 2 * A_HEADS * A_DK + h * A_DV:2 * A_HEADS * A_DK + (h + 1) * A_DV]
                    if not conv_done:
                        q = _l2norm(q) * (A_DK ** -0.5)
                        k = _l2norm(k)
                    beta = beta_all[rs, h:h + 1]
                    g = g_all[rs, A_HEADS + h:A_HEADS + h + 1]
                    gc = gc_all[:, A_HEADS + h:A_HEADS + h + 1]
                    gl = gc_all[c - 1:c, A_HEADS + h:A_HEADS + h + 1]
                    kb = k * beta
                    hd.append(dict(q=q, k=k, kb=kb, g=g, qg=q * jnp.exp(gc), kd=k * jnp.exp(gl - gc),
                                   dl=jnp.exp(gl),
                                   rhs=jnp.concatenate([v * beta, kb * jnp.exp(gc)], axis=1)))
                chains.append(dict(bi=bi, ci=ci, p=p, hd=hd))

    for ch in chains:
        a, b = ch["hd"]
        ch["e_mat"] = exact_tril_dot(jnp.where(strict, jnp.where(low, a["g"], b["g"]), 0.0))
    for ch in chains:
        a, b = ch["hd"]
        k_st = jnp.concatenate([a["k"], b["k"]], axis=0)
        ch["kk"] = _dot_nt(jnp.concatenate([a["kb"], b["kb"]], axis=0), k_st)
        ch["qk"] = _dot_nt(jnp.concatenate([a["q"], b["q"]], axis=0), k_st)
    for ch in chains:
        decay = jnp.where(incl, jnp.exp(jnp.where(incl, ch["e_mat"], 0.0)), 0.0)
        ch["lower"] = jnp.where(strict, unstack(ch["kk"]) * decay, 0.0)
        ch["qk"] = jnp.where(incl, unstack(ch["qk"]) * decay, 0.0)
        ch["x"] = ch["lower"]
        ch["xb"] = bd(ch["lower"])
        ch["t"] = eye - ch["lower"]
    for _ in range(c.bit_length() - 2):
        for ch in chains:
            ch["x"] = jnp.dot(ch["x"].astype(BF16), ch["xb"], preferred_element_type=F32)
        for ch in chains:
            ch["xb"] = bd(ch["x"])
            ch["t"] = ch["t"] + jnp.dot(ch["t"].astype(BF16), ch["xb"], preferred_element_type=F32)
    for ch in chains:
        lh, ll_ = _split2(ch["lower"])
        th = ch["t"].astype(BF16).astype(F32)
        tl = ch["t"] - th
        bth = bd(th)
        lt = dd(lh, bth) + dd(lh, bd(tl)) + dd(ll_, bth)
        ch["res"] = eye - ch["t"] - lt
    for ch in chains:
        ch["t"] = ch["t"] + jnp.dot(ch["t"].astype(BF16), bd(ch["res"]), preferred_element_type=F32)
    for ch in chains:
        a, b = ch["hd"]
        uw = jnp.dot(bd(ch["t"]), jnp.concatenate([a["rhs"], b["rhs"]], axis=0).astype(BF16),
                     preferred_element_type=F32)
        a["u"], a["w"] = uw[:c, :A_DV], uw[:c, A_DV:]
        b["u"], b["w"] = uw[c:, :A_DV], uw[c:, A_DV:]
        ch["qkb"] = bd(ch["qk"])

    by_key = {(ch["bi"], ch["ci"], ch["p"]): ch for ch in chains}
    states = {(bi, h): s_ref[bi, h] for bi in range(nb) for h in range(A_HEADS)}
    for ci in range(n_chunks):
        cur = [by_key[(bi, ci, p)] for bi in range(nb) for p in range(A_HEADS // 2)]
        for ch in cur:
            for hh, d in enumerate(ch["hd"]):
                ws = _dot(jnp.concatenate([d["w"], d["qg"]], axis=0), states[(ch["bi"], 2 * ch["p"] + hh)])
                d["e"] = d["u"] - ws[:c]
                d["o"] = ws[c:]
        for ch in cur:
            a, b = ch["hd"]
            o2 = jnp.dot(ch["qkb"], jnp.concatenate([a["e"], b["e"]], axis=0).astype(BF16),
                         preferred_element_type=F32)
            a["o"] = a["o"] + o2[:c]
            b["o"] = b["o"] + o2[c:]
        for ch in cur:
            for hh, d in enumerate(ch["hd"]):
                key = (ch["bi"], 2 * ch["p"] + hh)
                states[key] = states[key] * d["dl"] + _dot_tn(d["kd"], d["e"])
        r0 = ci * c
        r1 = min(rows, r0 + c)
        for ch in cur:
            for hh, d in enumerate(ch["hd"]):
                h = 2 * ch["p"] + hh
                o = d["o"] if r1 - r0 == c else d["o"][:r1 - r0]
                o_ref[ch["bi"], r0:r1, h * A_DV:(h + 1) * A_DV] = (
                    _rms(o, nwa) * _silu(z_ref[ch["bi"], r0:r1, h * A_DV:(h + 1) * A_DV]))
    for (bi, h), s in states.items():
        s_ref[bi, h] = s

    @pl.when(n == pl.num_programs(1) - 1)
    def _():
        snew_ref[...] = s_ref[...]


def _gdn_pair(proj, conv0, s0, w_conv, gate_par, norm_a, conv_done, side_of_grid=None):
    b, t, _ = proj.shape
    rows = min(GDN_ROWS, t)
    assert t % rows == 0 and (rows % CHUNK == 0 or rows == t)
    n = t // rows
    nb = max(1, min(b, GDN_ROWS // (-(-rows // CHUNK) * CHUNK)))
    assert b % nb == 0
    rp = 0 if conv_done else -(-rows // CHUNK) * CHUNK
    in_specs = [pl.BlockSpec((nb, rows, CONV_DIM), lambda i, j: (i, j, 0)),
                pl.BlockSpec((nb, rows, A_WIDTH), lambda i, j: (i, j, COL_Z // A_WIDTH)),
                pl.BlockSpec((nb, rows, GATE_PAD), lambda i, j: (i, j, COL_G // GATE_PAD)),
                pl.BlockSpec((nb, 8, CONV_DIM), lambda i, j: (i, 0, 0)),
                pl.BlockSpec((nb, A_HEADS, A_DK, A_DV), lambda i, j: (i, 0, 0, 0)),
                pl.BlockSpec((CONV_W, CONV_DIM), lambda i, j: (0, 0)),
                pl.BlockSpec((8, GATE_PAD), lambda i, j: (0, 0)),
                pl.BlockSpec((1, A_DV), lambda i, j: (0, 0))]
    out_specs = [pl.BlockSpec((nb, rows, A_WIDTH), lambda i, j: (i, j, 0)),
                 pl.BlockSpec((nb, A_HEADS, A_DK, A_DV), lambda i, j: (i, 0, 0, 0))]
    out_shape = [jax.ShapeDtypeStruct((b, t, A_WIDTH), F32),
                 jax.ShapeDtypeStruct((b, A_HEADS, A_DK, A_DV), F32)]
    inputs = (proj, proj, proj, conv0, s0, w_conv, gate_par, norm_a)
    n_in, n_out = len(in_specs), len(out_specs)
    side = side_of_grid((b // nb, n)) if side_of_grid is not None else None

    def kernel(*refs):
        n_side_in = len(side["in_specs"]) if side else 0
        gin, sin = refs[:n_in], refs[n_in:n_in + n_side_in]
        gout = refs[n_in + n_side_in:n_in + n_side_in + n_out]
        sout = refs[n_in + n_side_in + n_out:len(refs) - 2]
        side_work = (lambda: side["kernel"](*sin, *sout)) if side else None
        _gdn_pair_kernel(*gin, *gout, *refs[len(refs) - 2:], rows=rows, nb=nb, conv_done=conv_done,
                         side_work=side_work)

    if side:
        inputs = inputs + tuple(side["inputs"])
        in_specs = in_specs + list(side["in_specs"])
        out_specs = out_specs + list(side["out_specs"])
        out_shape = out_shape + list(side["out_shape"])
    outs = pl.pallas_call(
        kernel,
        grid=(b // nb, n),
        in_specs=in_specs,
        out_specs=out_specs,
        out_shape=out_shape,
        scratch_shapes=[pltpu.VMEM((nb, 8 + rp, CONV_DIM), F32),
                        pltpu.VMEM((nb, A_HEADS, A_DK, A_DV), F32)],
        compiler_params=pltpu.CompilerParams(dimension_semantics=("arbitrary", "arbitrary"),
                                             vmem_limit_bytes=VMEM_LIMIT),
        name="gdn_side" if side else "gdn",
    )(*inputs)
    return outs[:n_out], outs[n_out:]


def _slope(h):
    return 2.0 ** (-8.0 * (h + 1) / B_HEADS)


ATTN_SB = MAX_WINDOW
ATTN_SKEW = 2
PAIR = 128 // B_HEAD_DIM


def _attn_fused_kernel(q_ref, kp_ref, kc_ref, vp_ref, vc_ref, nwb_ref, o_ref, kt_ref, vt_ref, *stats):
    hp = pl.program_id(1)
    n = pl.program_id(2)
    sb = ATTN_SB
    blk = STEPS

    lane_q = lax.broadcasted_iota(jnp.int32, (blk, 128), 1)
    low_q = lane_q < B_HEAD_DIM
    qq = lax.broadcasted_iota(jnp.int32, (2 * blk, 2 * blk), 0)
    kk = lax.broadcasted_iota(jnp.int32, (2 * blk, 2 * blk), 1)
    steps_back = (qq & (blk - 1)) + blk - kk
    in_band = (steps_back >= 0) & (steps_back <= STEPS)
    slope_lo = jnp.float32(_slope(0))
    slope_hi = jnp.float32(_slope(1))
    for i in range(1, B_HEADS // PAIR):
        slope_lo = jnp.where(hp == i, _slope(PAIR * i), slope_lo)
        slope_hi = jnp.where(hp == i, _slope(PAIR * i + 1), slope_hi)
    slope = jnp.where(qq < blk, slope_lo, slope_hi)

    biases = []
    for (_, dil) in DILATED:
        bias = jnp.where(in_band, -slope * (steps_back * dil).astype(F32), NEG)
        biases.append((bias, jnp.where(kk >= blk, bias, NEG)))
    n_blk = sb // blk
    work = [(bi, it) for bi in range(len(DILATED)) for it in range(n_blk)]

    def rows(dil, start, size):
        return pl.ds(start, size) if dil == 1 else pl.ds(start, size, stride=dil)

    def key_tile(prev_ref, cur_ref, dil, k0):
        if k0 >= sb:
            return cur_ref[rows(dil, k0 - sb, 2 * blk), :].astype(BF16)
        assert k0 + blk * dil >= sb > k0 + (blk - 1) * dil
        return jnp.concatenate([prev_ref[rows(dil, k0, blk), :].astype(BF16),
                                cur_ref[rows(dil, k0 + blk * dil - sb, blk), :].astype(BF16)], axis=0)

    def issue_scores(bi, it):
        dil = DILATED[bi][1]
        q0 = (it // dil) * (blk * dil) + it % dil
        k0 = sb + q0 - blk * dil
        q = q_ref[rows(dil, q0, blk), :] * (B_HEAD_DIM ** -0.5)
        q2 = jnp.concatenate([jnp.where(low_q, q, 0.0), jnp.where(low_q, 0.0, q)], axis=0).astype(BF16)
        k = key_tile(kp_ref, kc_ref, dil, k0)
        return dict(bi=bi, dil=dil, q0=q0, k0=k0,
                    s=lax.dot_general(q2, k, (((1,), (1,)), ((), ())), preferred_element_type=F32))

    def softmax(w):
        bias, bias_first = biases[w["bi"]]
        s = w.pop("s") + (jnp.where(n == 0, bias_first, bias) if w["q0"] < blk * w["dil"] else bias)
        m = jnp.max(s, axis=-1, keepdims=True)
        p = jnp.exp(s - m)
        w.update(m=m, l=jnp.sum(p, axis=-1, keepdims=True), p=p.astype(BF16))

    def issue_values(w):
        dil, q0 = w["dil"], w["q0"]
        m_s, l_s, num_s = stats[3 * w["bi"]:3 * w["bi"] + 3]
        v = key_tile(vp_ref, vc_ref, dil, w["k0"])
        pv = jnp.dot(w["p"], v, preferred_element_type=F32)
        m_s[rows(dil, q0, blk), :] = jnp.where(low_q, w["m"][:blk], w["m"][blk:])
        l_s[rows(dil, q0, blk), :] = jnp.where(low_q, w["l"][:blk], w["l"][blk:])
        num_s[rows(dil, q0, blk), :] = jnp.where(low_q, pv[:blk], pv[blk:])

    inflight = {}
    for t in range(len(work) + 2 * ATTN_SKEW):
        if t < len(work):
            inflight[t] = issue_scores(*work[t])
        if 0 <= t - ATTN_SKEW < len(work):
            softmax(inflight[t - ATTN_SKEW])
        if 0 <= t - 2 * ATTN_SKEW < len(work):
            issue_values(inflight.pop(t - 2 * ATTN_SKEW))

    nwb = nwb_ref[...]

    def combine(i, carry):
        rs = pl.ds(pl.multiple_of(i * blk, blk), blk)
        ms = [stats[3 * bi][rs, :] for bi in range(len(DILATED))]
        ls = [stats[3 * bi + 1][rs, :] for bi in range(len(DILATED))]
        nums = [stats[3 * bi + 2][rs, :] for bi in range(len(DILATED))]
        mx = jnp.maximum(jnp.maximum(ms[0], ms[1]), ms[2])
        ws = [jnp.exp(m - mx) for m in ms]
        den = ws[0] * ls[0] + ws[1] * ls[1] + ws[2] * ls[2]
        o = (ws[0] * nums[0] + ws[1] * nums[1] + ws[2] * nums[2]) / den
        sq = o * o
        ss_lo = jnp.sum(jnp.where(low_q, sq, 0.0), axis=-1, keepdims=True)
        ss_hi = jnp.sum(jnp.where(low_q, 0.0, sq), axis=-1, keepdims=True)
        mean_sq = jnp.where(low_q, ss_lo, ss_hi) / B_HEAD_DIM
        o_ref[rs, :] = o * lax.rsqrt(mean_sq + EPS) * nwb
        return carry

    lax.fori_loop(0, sb // blk, combine, 0, unroll=4)

    @pl.when(n == pl.num_programs(2) - 1)
    def _():
        kt_ref[...] = kc_ref[...].T
        vt_ref[...] = vc_ref[...].T


def _attn_prompt(qkv, norm_b_pair):
    b, s, _ = qkv.shape
    sb = ATTN_SB
    assert s % sb == 0 and sb == min(MAX_WINDOW, s)
    assert all(w == STEPS * d and sb % w == 0 and d & (d - 1) == 0 for (w, d) in DILATED)
    n_pairs = B_HEADS // PAIR
    blk = (None, sb, 128)
    prev = lambda c0: pl.BlockSpec(blk, lambda i, p, n: (i, jnp.maximum(n - 1, 0), c0 + p))
    cur = lambda c0: pl.BlockSpec(blk, lambda i, p, n: (i, n, c0 + p))
    return pl.pallas_call(
        _attn_fused_kernel,
        grid=(b, n_pairs, s // sb),
        in_specs=[cur(0), prev(n_pairs), cur(n_pairs), prev(2 * n_pairs), cur(2 * n_pairs),
                  pl.BlockSpec((1, 128), lambda i, p, n: (0, 0))],
        out_specs=[pl.BlockSpec(blk, lambda i, p, n: (i, n, p)),
                   pl.BlockSpec((None, 128, sb), lambda i, p, n: (i, p, 0)),
                   pl.BlockSpec((None, 128, sb), lambda i, p, n: (i, p, 0))],
        out_shape=[jax.ShapeDtypeStruct((b, s, B_WIDTH), F32),
                   jax.ShapeDtypeStruct((b, B_WIDTH, sb), F32),
                   jax.ShapeDtypeStruct((b, B_WIDTH, sb), F32)],
        scratch_shapes=[pltpu.VMEM((sb, 128), F32)] * (3 * len(DILATED)),
        compiler_params=pltpu.CompilerParams(
            dimension_semantics=("arbitrary", "arbitrary", "arbitrary"), vmem_limit_bytes=VMEM_LIMIT),
        name="attn_prompt",
    )(qkv, qkv, qkv, qkv, qkv, norm_b_pair)


def _attn_sample_kernel(q_ref, kn_ref, vn_ref, kt_ref, vt_ref, nwb_ref, o_ref, kto_ref, vto_ref, *, n_past, t):
    kn = kn_ref[...]
    vn = vn_ref[...]
    kt = kt_ref[...]
    vt = vt_ref[...]

    prow = lax.broadcasted_iota(jnp.int32, (t, 128), 0)
    plane = lax.broadcasted_iota(jnp.int32, (t, 128), 1)
    place = jnp.where(plane == 128 - t + prow, 1.0, 0.0).astype(BF16)
    tail_lanes = lax.broadcasted_iota(jnp.int32, (B_WIDTH, 128), 1) >= 128 - t
    tn = functools.partial(lax.dot_general, dimension_numbers=(((0,), (0,)), ((), ())),
                           preferred_element_type=F32)

    def shifted(win, new, out_ref):
        hi, lo = _split2(new)
        lo2 = (new - hi.astype(F32) - lo.astype(F32)).astype(BF16)
        new_t = tn(hi, place) + tn(lo, place) + tn(lo2, place)
        sh = pltpu.roll(win, n_past - t, axis=1)
        out_ref[:, :n_past - 128] = sh[:, :n_past - 128]
        out_ref[:, n_past - 128:] = jnp.where(tail_lanes, new_t, sh[:, n_past - 128:])

    shifted(kt, kn, kto_ref)
    shifted(vt, vn, vto_ref)

    rq = B_HEADS * t
    t_bits = t.bit_length() - 1
    d_bits = B_HEAD_DIM.bit_length() - 1
    rowi = lax.broadcasted_iota(jnp.int32, (rq, B_WIDTH), 0)
    lanei = lax.broadcasted_iota(jnp.int32, (rq, B_WIDTH), 1)
    own = lax.shift_right_logical(rowi, t_bits) == lax.shift_right_logical(lanei, d_bits)
    q_rep = jnp.concatenate([q_ref[...]] * B_HEADS, axis=0)
    q_blk = jnp.where(own, q_rep, 0.0).astype(BF16)
    zpad = jnp.zeros((128 - t, B_WIDTH), BF16)
    kn_pad = jnp.concatenate([kn.astype(BF16), zpad], axis=0)
    vn_pad = jnp.concatenate([vn.astype(BF16), zpad], axis=0)
    vt16 = vt.astype(BF16)
    scale = B_HEAD_DIM ** -0.5
    s_c = jnp.dot(q_blk, kt.astype(BF16), preferred_element_type=F32) * scale
    s_n = lax.dot_general(q_blk, kn_pad, (((1,), (1,)), ((), ())), preferred_element_type=F32) * scale

    r1 = lax.broadcasted_iota(jnp.int32, (rq, 1), 0)
    head = lax.shift_right_logical(r1, t_bits)
    slope = jnp.zeros((rq, 1), F32)
    for h in range(B_HEADS):
        slope = jnp.where(head == h, _slope(h), slope)
    tok = r1 & (t - 1)
    delta_c = n_past + tok - lax.broadcasted_iota(jnp.int32, (rq, n_past), 1)
    key_n = lax.broadcasted_iota(jnp.int32, (rq, 128), 1)
    delta_n = tok - key_n
    s_c = s_c - slope * delta_c.astype(F32)
    s_n = s_n - slope * delta_n.astype(F32)
    live_n = (key_n < t) & (delta_n >= 0)

    ms, ls, pcs, pns = [], [], [], []
    for (_, dil) in DILATED:
        valid_c = ((delta_c & (dil - 1)) == 0) & (delta_c <= STEPS * dil)
        valid_n = live_n & ((delta_n & (dil - 1)) == 0) & (delta_n <= STEPS * dil)
        sd_c = jnp.where(valid_c, s_c, NEG)
        sd_n = jnp.where(valid_n, s_n, NEG)
        m = jnp.maximum(jnp.max(sd_c, axis=-1, keepdims=True), jnp.max(sd_n, axis=-1, keepdims=True))
        p_c = jnp.exp(sd_c - m)
        p_n = jnp.exp(sd_n - m)
        ms.append(m)
        ls.append(jnp.sum(p_c, axis=-1, keepdims=True) + jnp.sum(p_n, axis=-1, keepdims=True))
        pcs.append(p_c.astype(BF16))
        pns.append(p_n.astype(BF16))
    num_all = (lax.dot_general(jnp.concatenate(pcs, axis=0), vt16, (((1,), (1,)), ((), ())),
                               preferred_element_type=F32)
               + jnp.dot(jnp.concatenate(pns, axis=0), vn_pad, preferred_element_type=F32))
    nums = [num_all[i * rq:(i + 1) * rq] for i in range(len(DILATED))]
    mx = jnp.maximum(jnp.maximum(ms[0], ms[1]), ms[2])
    ws = [jnp.exp(m - mx) for m in ms]
    den = ws[0] * ls[0] + ws[1] * ls[1] + ws[2] * ls[2]
    o = (ws[0] * nums[0] + ws[1] * nums[1] + ws[2] * nums[2]) / den
    o = jnp.where(own, o, 0.0)
    o = o * lax.rsqrt(jnp.sum(o * o, axis=-1, keepdims=True) / B_HEAD_DIM + EPS) * nwb_ref[...]
    acc = o[0:t, :]
    for h in range(1, B_HEADS):
        acc = acc + o[h * t:(h + 1) * t, :]
    o_ref[...] = acc


def _attn_sample_parts(qkv, win_kt, win_vt, norm_b_tiled, seq_of_step):
    b, t, _ = qkv.shape
    n_past = win_kt.shape[2]
    assert n_past == MAX_WINDOW and n_past % 128 == 0 and t % 8 == 0 and t & (t - 1) == 0 and t <= 128
    assert all(d & (d - 1) == 0 for (_, d) in DILATED)
    col = lambda c: pl.BlockSpec((None, t, B_WIDTH), lambda *g: (seq_of_step(*g), 0, c))
    win = pl.BlockSpec((None, B_WIDTH, n_past), lambda *g: (seq_of_step(*g), 0, 0))
    return dict(
        kernel=functools.partial(_attn_sample_kernel, n_past=n_past, t=t),
        inputs=(qkv, qkv, qkv, win_kt, win_vt, norm_b_tiled),
        in_specs=[col(0), col(1), col(2), win, win, pl.BlockSpec((1, B_WIDTH), lambda *g: (0, 0))],
        out_specs=[col(0), win, win],
        out_shape=[jax.ShapeDtypeStruct((b, t, B_WIDTH), F32),
                   jax.ShapeDtypeStruct((b, B_WIDTH, n_past), F32),
                   jax.ShapeDtypeStruct((b, B_WIDTH, n_past), F32)])


def _attn_sample(qkv, win_kt, win_vt, norm_b_tiled):
    parts = _attn_sample_parts(qkv, win_kt, win_vt, norm_b_tiled, lambda i: i)
    return pl.pallas_call(
        parts["kernel"],
        grid=(qkv.shape[0],),
        in_specs=parts["in_specs"],
        out_specs=parts["out_specs"],
        out_shape=parts["out_shape"],
        compiler_params=pltpu.CompilerParams(dimension_semantics=("arbitrary",),
                                             vmem_limit_bytes=VMEM_LIMIT),
        name="attn_sample",
    )(*parts["inputs"])


def _out_ffn_kernel(x_ref, oa_ref, ob_ref, wo_ref, nf_ref, wg_ref, wu_ref, wd_ref, nfin_ref, y_ref, *, ff_chunk):
    mixed = jnp.concatenate([oa_ref[...], ob_ref[...]], axis=-1).astype(BF16)
    x1 = x_ref[...] + jnp.dot(mixed, wo_ref[...], preferred_element_type=F32)
    hf = _rms(x1, nf_ref[...]).astype(BF16)
    x2 = x1
    for c0 in range(0, wg_ref.shape[1], ff_chunk):
        g = jnp.dot(hf, wg_ref[:, c0:c0 + ff_chunk], preferred_element_type=F32)
        u = jnp.dot(hf, wu_ref[:, c0:c0 + ff_chunk], preferred_element_type=F32)
        act = (_silu(g) * u).astype(BF16)
        x2 = x2 + jnp.dot(act, wd_ref[c0:c0 + ff_chunk, :], preferred_element_type=F32)
    y_ref[...] = _rms(x2, nfin_ref[...])


def _out_ffn(x2d, o_a, o_b, w_out, norm_ffn, w_gate, w_up, w_down, norm_final, tm):
    t = x2d.shape[0]
    d_ff = w_gate.shape[1]
    ff_chunk = d_ff
    once = pl.Buffered(1)
    row = lambda w: pl.BlockSpec((tm, w), lambda i: (i, 0))
    full = lambda a, b: pl.BlockSpec((a, b), lambda i: (0, 0), pipeline_mode=once)
    return pl.pallas_call(
        functools.partial(_out_ffn_kernel, ff_chunk=ff_chunk),
        grid=(t // tm,),
        in_specs=[row(D_MODEL), row(A_WIDTH), row(B_WIDTH),
                  full(D_MODEL, D_MODEL), full(1, D_MODEL),
                  full(D_MODEL, d_ff), full(D_MODEL, d_ff), full(d_ff, D_MODEL), full(1, D_MODEL)],
        out_specs=row(D_MODEL),
        out_shape=jax.ShapeDtypeStruct((t, D_MODEL), F32),
        compiler_params=pltpu.CompilerParams(dimension_semantics=("arbitrary",),
                                             vmem_limit_bytes=VMEM_LIMIT),
        name="out_ffn",
    )(x2d, o_a, o_b, w_out, norm_ffn, w_gate, w_up, w_down, norm_final)


def _layer_params(norm_mix, w_in, w_conv, a_log, dt_bias, norm_out_a, norm_out_b, w_out, norm_ffn,
                  w_gate, w_up, w_down, layer):
    w = w_in[layer]
    n_gate = 2 * A_HEADS
    w_p = (w[:, :COL_G].astype(BF16),
           jnp.pad(w[:, COL_G:COL_G + n_gate], ((0, 0), (0, GATE_PAD - n_gate))).astype(BF16),
           w[:, COL_G + n_gate:].astype(BF16))
    gate_par = jnp.zeros((8, GATE_PAD), F32)
    gate_par = gate_par.at[0, A_HEADS:2 * A_HEADS].set(a_log[layer].astype(F32))
    gate_par = gate_par.at[1, A_HEADS:2 * A_HEADS].set(dt_bias[layer].astype(F32))
    return dict(
        norm_mix=norm_mix[layer].reshape(1, D_MODEL), w_p=w_p, w_conv=w_conv[layer], gate_par=gate_par,
        norm_a=norm_out_a[layer].reshape(1, A_DV),
        norm_b_pair=jnp.tile(norm_out_b[layer], PAIR).reshape(1, PAIR * B_HEAD_DIM),
        norm_b_tiled=jnp.tile(norm_out_b[layer], B_HEADS).reshape(1, B_WIDTH),
        w_out=w_out[layer].astype(BF16), norm_ffn=norm_ffn[layer].reshape(1, D_MODEL),
        w_gate=w_gate[layer].astype(BF16), w_up=w_up[layer].astype(BF16), w_down=w_down[layer].astype(BF16))


def _project(x, conv_buf, p, tm):
    b, t, _ = x.shape
    assert t >= CONV_W - 1
    conv0 = jnp.pad(conv_buf, ((0, 0), (8 - (CONV_W - 1), 0), (0, 0)))
    conv_done = t % tm == 0
    if conv_done:
        proj_a, proj_b, tail = _norm_proj_conv(x, p["norm_mix"], p["w_p"], p["w_conv"], conv0, tm)
        new_conv = tail[:, 8 - (CONV_W - 1):]
    else:
        proj_a, proj_b = _norm_proj(x.reshape(b * t, D_MODEL), p["norm_mix"], p["w_p"], tm)
        proj_a = proj_a.reshape(b, t, COLS_A)
        proj_b = proj_b.reshape(b, t, COLS_B)
        new_conv = proj_a[:, t - (CONV_W - 1):, :CONV_DIM]
    return proj_a, proj_b, new_conv, conv0, conv_done


def _finish(x, o_a, o_b, p, norm_final, tm):
    b, t, _ = x.shape
    y = _out_ffn(x.reshape(b * t, D_MODEL), o_a.reshape(b * t, A_WIDTH), o_b.reshape(b * t, B_WIDTH), p["w_out"],
                 p["norm_ffn"], p["w_gate"], p["w_up"], p["w_down"], norm_final.reshape(1, D_MODEL), tm)
    return y.reshape(b, t, D_MODEL)


def kernel(x_prompt, x_sample, state_conv, state_rec, cache_win_k, cache_win_v, norm_mix, w_in, w_conv, a_log,
           dt_bias, norm_out_a, norm_out_b, w_out, norm_ffn, w_gate, w_up, w_down, norm_final):
    depth = w_in.shape[0]
    assert depth == 1, "the final norm is fused into the block of a single-layer trunk"
    p = _layer_params(norm_mix, w_in, w_conv, a_log, dt_bias, norm_out_a, norm_out_b, w_out, norm_ffn,
                      w_gate, w_up, w_down, 0)
    bp, tp, _ = x_prompt.shape
    bs, ts, _ = x_sample.shape
    tm_p, tm_s = 512, bs * ts
    gdn_args = (p["w_conv"], p["gate_par"], p["norm_a"])

    sa, sb, sc, s_conv0, s_done = _project(x_sample, state_conv[0], p, tm_s)
    n_past = cache_win_k.shape[2]
    to_t = lambda a: jnp.transpose(a, (0, 2, 3, 1)).reshape(bs, B_WIDTH, n_past)
    from_t = lambda a: jnp.transpose(a.reshape(a.shape[0], B_HEADS, B_HEAD_DIM, a.shape[2]), (0, 3, 1, 2))
    win_kt, win_vt = to_t(cache_win_k[0]), to_t(cache_win_v[0])

    def sample_attention_of_grid(grid):
        if grid[0] * grid[1] != bs:
            return None
        return _attn_sample_parts(sb, win_kt, win_vt, p["norm_b_tiled"], lambda i, j: i * grid[1] + j)

    pa, pb, pc, p_conv0, p_done = _project(x_prompt, jnp.zeros((bp, CONV_W - 1, CONV_DIM), F32), p, tm_p)
    zero_rec = jnp.zeros((bp, A_HEADS, A_DK, A_DV), F32)
    (o_a_p, pr), side = _gdn_pair(pa, p_conv0, zero_rec, *gdn_args, p_done, sample_attention_of_grid)
    o_b_s, new_kt, new_vt = side if side else _attn_sample(sb, win_kt, win_vt, p["norm_b_tiled"])
    (o_a_s, sr), _ = _gdn_pair(sa, s_conv0, state_rec[0], *gdn_args, s_done)

    o_b_p, pkt, pvt = _attn_prompt(pb, p["norm_b_pair"])

    yp = _finish(x_prompt, o_a_p, o_b_p, p, norm_final, tm_p)
    ys = _finish(x_sample, o_a_s, o_b_s, p, norm_final, tm_s)
    return (yp, ys, pc[None], pr[None], from_t(pkt)[None], from_t(pvt)[None], sc[None], sr[None],
            from_t(new_kt)[None], from_t(new_vt)[None])
```

```python
import functools

import jax
import jax.numpy as jnp
from jax import lax
from jax.experimental import pallas as pl
from jax.experimental.pallas import tpu as pltpu

F32 = jnp.float32
BF16 = jnp.bfloat16

D_MODEL = 1024
A_HEADS = 4
A_DK = 128
A_DV = 128
A_WIDTH = A_HEADS * A_DV
CONV_W = 4
CONV_DIM = 2 * A_HEADS * A_DK + A_HEADS * A_DV
B_HEADS = 8
B_HEAD_DIM = 64
B_WIDTH = B_HEADS * B_HEAD_DIM
DILATED = ((128, 1), (512, 4), (2048, 16))
STEPS = 128
MAX_WINDOW = 2048
EPS = 1e-6
NEG = -1e30
CHUNK = 64
GATE_PAD = 128
COL_Z = CONV_DIM
COL_G = CONV_DIM + A_WIDTH
COLS_A = COL_G + GATE_PAD
COLS_B = 3 * B_WIDTH
VMEM_LIMIT = 56 * 1024 * 1024


def _dot(a, b):
    return jnp.dot(a.astype(BF16), b.astype(BF16), preferred_element_type=F32)


def _dot_nt(a, b):
    return lax.dot_general(a.astype(BF16), b.astype(BF16), (((1,), (1,)), ((), ())),
                           preferred_element_type=F32)


def _dot_tn(a, b):
    return lax.dot_general(a.astype(BF16), b.astype(BF16), (((0,), (0,)), ((), ())),
                           preferred_element_type=F32)


def _split2(a):
    hi = a.astype(BF16)
    lo = (a - hi.astype(F32)).astype(BF16)
    return hi, lo


def _sigmoid(x):
    return 1.0 / (1.0 + jnp.exp(-x))


def _silu(x):
    return x * _sigmoid(x)


def _softplus(x):
    return jnp.maximum(x, 0.0) + jnp.log(1.0 + jnp.exp(-jnp.abs(x)))


def _rms(x, w):
    return x * lax.rsqrt(jnp.mean(x * x, axis=-1, keepdims=True) + EPS) * w


def _norm_proj_kernel(x_ref, nw_ref, wa_ref, wg_ref, wb_ref, oa_ref, ob_ref):
    h = _rms(x_ref[...], nw_ref[...]).astype(BF16)
    oa_ref[:, :COL_G] = jnp.dot(h, wa_ref[...], preferred_element_type=F32)
    oa_ref[:, COL_G:] = jnp.dot(h, wg_ref[...], preferred_element_type=F32)
    ob_ref[...] = jnp.dot(h, wb_ref[...], preferred_element_type=F32)


def _weight_specs():
    zero = lambda *g: (0, 0)
    return [pl.BlockSpec((D_MODEL, c), zero, pipeline_mode=pl.Buffered(1)) for c in (COL_G, GATE_PAD, COLS_B)]


def _norm_proj(x2d, norm_w, w_p, tm):
    t = x2d.shape[0]
    once = pl.Buffered(1)
    return pl.pallas_call(
        _norm_proj_kernel,
        grid=(t // tm,),
        in_specs=[pl.BlockSpec((tm, D_MODEL), lambda i: (i, 0)),
                  pl.BlockSpec((1, D_MODEL), lambda i: (0, 0), pipeline_mode=once)] + _weight_specs(),
        out_specs=[pl.BlockSpec((tm, COLS_A), lambda i: (i, 0)),
                   pl.BlockSpec((tm, COLS_B), lambda i: (i, 0))],
        out_shape=[jax.ShapeDtypeStruct((t, COLS_A), F32),
                   jax.ShapeDtypeStruct((t, COLS_B), F32)],
        compiler_params=pltpu.CompilerParams(dimension_semantics=("arbitrary",),
                                             vmem_limit_bytes=VMEM_LIMIT),
        name="norm_proj",
    )(x2d, norm_w, *w_p)


def _causal_conv_silu(x_all, wconv_ref, n):
    conv = x_all[8:8 + n] * wconv_ref[CONV_W - 1:CONV_W, :]
    for back in range(1, CONV_W):
        tap = CONV_W - 1 - back
        conv = conv + pltpu.roll(x_all, back, axis=0)[8:8 + n] * wconv_ref[tap:tap + 1, :]
    return _silu(conv)


def _l2norm(x):
    return x * lax.rsqrt(jnp.sum(x * x, axis=-1, keepdims=True) + EPS)


def _norm_proj_conv_kernel(x_ref, nw_ref, wa_ref, wg_ref, wb_ref, wconv_ref, conv0_ref, oa_ref, ob_ref, tail_ref,
                           carry_ref, h_ref, raw_ref):
    i = pl.program_id(1)
    tm = x_ref.shape[0]
    wq = A_HEADS * A_DK
    assert CONV_DIM == 3 * wq and COLS_B == 3 * B_WIDTH
    dd = functools.partial(jnp.dot, preferred_element_type=F32)

    @pl.when(i == 0)
    def _():
        carry_ref[...] = conv0_ref[...]

    def conv_cols(g):
        cs = slice(g * wq, (g + 1) * wq)
        conv = _causal_conv_silu(jnp.concatenate([carry_ref[:, cs], raw_ref[:, cs]], axis=0),
                                 wconv_ref.at[:, cs], tm)
        carry_ref[:, cs] = raw_ref[tm - 8:tm, cs]
        if g == 2:
            oa_ref[:, cs] = conv
            return
        for hd in range(A_HEADS):
            hs = slice(hd * A_DK, (hd + 1) * A_DK)
            out = _l2norm(conv[:, hs])
            oa_ref[:, g * wq + hd * A_DK:g * wq + (hd + 1) * A_DK] = out * (A_DK ** -0.5) if g == 0 else out

    def phase(k):
        if k == 0:
            h_ref[...] = _rms(x_ref[...], nw_ref[...]).astype(BF16)
        h = h_ref[...]
        if k < 3:
            raw_ref[:, k * wq:(k + 1) * wq] = dd(h, wa_ref[:, k * wq:(k + 1) * wq])
        if k == 0:
            oa_ref[:, CONV_DIM:COL_G] = dd(h, wa_ref[:, CONV_DIM:])
        else:
            ob_ref[:, (k - 1) * B_WIDTH:k * B_WIDTH] = dd(h, wb_ref[:, (k - 1) * B_WIDTH:k * B_WIDTH])
            conv_cols(k - 1)
        if k == 3:
            oa_ref[:, COL_G:] = dd(h, wg_ref[...])

    for k in range(4):
        pl.when(i >= 0)(functools.partial(phase, k))

    @pl.when(i == pl.num_programs(1) - 1)
    def _():
        tail_ref[...] = raw_ref[tm - 8:tm, :]


def _norm_proj_conv(x, norm_w, w_p, w_conv, conv0, tm):
    b, t, _ = x.shape
    assert t % tm == 0 and tm % 8 == 0
    once = pl.Buffered(1)
    return pl.pallas_call(
        _norm_proj_conv_kernel,
        grid=(b, t // tm),
        in_specs=[pl.BlockSpec((None, tm, D_MODEL), lambda i, j: (i, j, 0)),
                  pl.BlockSpec((1, D_MODEL), lambda i, j: (0, 0), pipeline_mode=once)] + _weight_specs() + [
                  pl.BlockSpec((CONV_W, CONV_DIM), lambda i, j: (0, 0), pipeline_mode=once),
                  pl.BlockSpec((None, 8, CONV_DIM), lambda i, j: (i, 0, 0))],
        out_specs=[pl.BlockSpec((None, tm, COLS_A), lambda i, j: (i, j, 0)),
                   pl.BlockSpec((None, tm, COLS_B), lambda i, j: (i, j, 0)),
                   pl.BlockSpec((None, 8, CONV_DIM), lambda i, j: (i, 0, 0))],
        out_shape=[jax.ShapeDtypeStruct((b, t, COLS_A), F32),
                   jax.ShapeDtypeStruct((b, t, COLS_B), F32),
                   jax.ShapeDtypeStruct((b, 8, CONV_DIM), F32)],
        scratch_shapes=[pltpu.VMEM((8, CONV_DIM), F32), pltpu.VMEM((tm, D_MODEL), BF16),
                        pltpu.VMEM((tm, CONV_DIM), F32)],
        compiler_params=pltpu.CompilerParams(dimension_semantics=("arbitrary", "arbitrary"),
                                             vmem_limit_bytes=VMEM_LIMIT),
        name="norm_proj_conv",
    )(x, norm_w, *w_p, w_conv, conv0)


GDN_ROWS = 8 * CHUNK


HALF = 128 // CHUNK


def _gdn_pair_kernel(xa_ref, z_ref, gt_ref, conv0_ref, s0_ref, wconv_ref, gp_ref, nwa_ref,
                     o_ref, snew_ref, ext_ref, s_ref, *, rows, nb, conv_done, side_work=None):
    n = pl.program_id(1)
    c = CHUNK
    n_chunks = -(-rows // c)
    rp = n_chunks * c
    assert A_HEADS % 2 == 0 and HALF == 2 and A_DK == 128 and A_DV == 128

    @pl.when(n == 0)
    def _():
        ext_ref[:, 0:8, :] = conv0_ref[...]
        s_ref[...] = s0_ref[...]

    if side_work is not None:
        side_work()

    row = lax.broadcasted_iota(jnp.int32, (rp, 1), 0)
    live = row < rows
    lane_g = lax.broadcasted_iota(jnp.int32, (rp, GATE_PAD), 1)
    gate_lanes = (lane_g >= A_HEADS) & (lane_g < 2 * A_HEADS) & live

    ii = lax.broadcasted_iota(jnp.int32, (c, 128), 0)
    ll = lax.broadcasted_iota(jnp.int32, (c, 128), 1)
    jj = ll & (c - 1)
    low = ll < c
    incl = ii >= jj
    strict = ii > jj
    eye = jnp.where(ii == jj, 1.0, 0.0).astype(F32)
    ti = lax.broadcasted_iota(jnp.int32, (c, c), 0)
    tj = lax.broadcasted_iota(jnp.int32, (c, c), 1)
    tril = jnp.where(ti >= tj, 1.0, 0.0).astype(BF16)
    dd = functools.partial(jnp.dot, preferred_element_type=F32)

    def exact_tril_dot(x):
        hi, lo = _split2(x)
        lo2 = (x - hi.astype(F32) - lo.astype(F32)).astype(BF16)
        return dd(tril, hi) + dd(tril, lo) + dd(tril, lo2)

    def bd(x):
        return jnp.concatenate([jnp.where(low, x, 0.0), jnp.where(low, 0.0, x)], axis=0).astype(BF16)

    def unstack(y):
        return jnp.where(low, y[:c], y[c:])

    nwa = nwa_ref[...]
    chains = []
    for bi in range(nb):
        if conv_done:
            assert rows == rp
            conv = xa_ref[bi]
        else:
            ext_ref[bi, 8:8 + rows, :] = xa_ref[bi]
            if rows < rp:
                ext_ref[bi, 8 + rows:8 + rp, :] = jnp.zeros((rp - rows, CONV_DIM), F32)
            conv = _causal_conv_silu(ext_ref[bi, 0:8 + rp, :], wconv_ref, rp)
            ext_ref[bi, 0:8, :] = ext_ref[bi, rp:rp + 8, :]
            if rows < rp:
                conv = jnp.where(live, conv, 0.0)

        gt = gt_ref[bi]
        if rows < rp:
            gt = jnp.concatenate([gt, jnp.zeros((rp - rows, GATE_PAD), F32)], axis=0)
        beta_all = jnp.where(live, _sigmoid(gt), 0.0)
        g_all = jnp.where(gate_lanes, -jnp.exp(gp_ref[0:1, :]) * _softplus(gt + gp_ref[1:2, :]), 0.0)

        for ci in range(n_chunks):
            rs = slice(ci * c, (ci + 1) * c)
            gc_all = exact_tril_dot(g_all[rs])
            for p in range(A_HEADS // 2):
                hd = []
                for h in (2 * p, 2 * p + 1):
                    q = conv[rs, h * A_DK:(h + 1) * A_DK]
                    k = conv[rs, A_HEADS * A_DK + h * A_DK:A_HEADS * A_DK + (h + 1) * A_DK]
                    v = conv[rs, 2 * A_HEADS * A_DK + h * A_DV:2 * A_HEADS * A_DK + (h + 1) * A_DV]
                    if not conv_done:
                        q = _l2norm(q) * (A_DK ** -0.5)
                        k = _l2norm(k)
                    beta = beta_all[rs, h:h + 1]
                    g = g_all[rs, A_HEADS + h:A_HEADS + h + 1]
                    gc = gc_all[:, A_HEADS + h:A_HEADS + h + 1]
                    gl = gc_all[c - 1:c, A_HEADS + h:A_HEADS + h + 1]
                    kb = k * beta
                    hd.append(dict(q=q, k=k, kb=kb, g=g, qg=q * jnp.exp(gc), kd=k * jnp.exp(gl - gc),
                                   dl=jnp.exp(gl),
                                   rhs=jnp.concatenate([v * beta, kb * jnp.exp(gc)], axis=1)))
                chains.append(dict(bi=bi, ci=ci, p=p, hd=hd))

    for ch in chains:
        a, b = ch["hd"]
        ch["e_mat"] = exact_tril_dot(jnp.where(strict, jnp.where(low, a["g"], b["g"]), 0.0))
    for ch in chains:
        a, b = ch["hd"]
        k_st = jnp.concatenate([a["k"], b["k"]], axis=0)
        ch["kk"] = _dot_nt(jnp.concatenate([a["kb"], b["kb"]], axis=0), k_st)
        ch["qk"] = _dot_nt(jnp.concatenate([a["q"], b["q"]], axis=0), k_st)
    for ch in chains:
        decay = jnp.where(incl, jnp.exp(jnp.where(incl, ch["e_mat"], 0.0)), 0.0)
        ch["lower"] = jnp.where(strict, unstack(ch["kk"]) * decay, 0.0)
        ch["qk"] = jnp.where(incl, unstack(ch["qk"]) * decay, 0.0)
    def joining(level):
        return strict & ((ii >> (level + 1)) == (jj >> (level + 1))) & ((ii >> level) != (jj >> level))

    for ch in chains:
        ch["t"] = eye - jnp.where(joining(0), ch["lower"], 0.0)
    for level in range(1, c.bit_length() - 1):
        join = joining(level)
        for ch in chains:
            ch["te"] = jnp.dot(ch["t"].astype(BF16), bd(jnp.where(join, ch["lower"], 0.0)),
                               preferred_element_type=F32)
        for ch in chains:
            ch["t"] = ch["t"] - jnp.dot(ch["te"].astype(BF16), bd(ch["t"]), preferred_element_type=F32)
    for ch in chains:
        lh, ll_ = _split2(ch["lower"])
        th = ch["t"].astype(BF16).astype(F32)
        tl = ch["t"] - th
        bth = bd(th)
        lt = dd(lh, bth) + dd(lh, bd(tl)) + dd(ll_, bth)
        ch["res"] = eye - ch["t"] - lt
    for ch in chains:
        ch["t"] = ch["t"] + jnp.dot(ch["t"].astype(BF16), bd(ch["res"]), preferred_element_type=F32)
    for ch in chains:
        a, b = ch["hd"]
        uw = jnp.dot(bd(ch["t"]), jnp.concatenate([a["rhs"], b["rhs"]], axis=0).astype(BF16),
                     preferred_element_type=F32)
        a["u"], a["w"] = uw[:c, :A_DV], uw[:c, A_DV:]
        b["u"], b["w"] = uw[c:, :A_DV], uw[c:, A_DV:]
        ch["qkb"] = bd(ch["qk"])

    by_key = {(ch["bi"], ch["ci"], ch["p"]): ch for ch in chains}
    states = {(bi, h): s_ref[bi, h] for bi in range(nb) for h in range(A_HEADS)}
    for ci in range(n_chunks):
        cur = [by_key[(bi, ci, p)] for bi in range(nb) for p in range(A_HEADS // 2)]
        for ch in cur:
            for hh, d in enumerate(ch["hd"]):
                ws = _dot(jnp.concatenate([d["w"], d["qg"]], axis=0), states[(ch["bi"], 2 * ch["p"] + hh)])
                d["e"] = d["u"] - ws[:c]
                d["o"] = ws[c:]
        for ch in cur:
            a, b = ch["hd"]
            o2 = jnp.dot(ch["qkb"], jnp.concatenate([a["e"], b["e"]], axis=0).astype(BF16),
                         preferred_element_type=F32)
            a["o"] = a["o"] + o2[:c]
            b["o"] = b["o"] + o2[c:]
        for ch in cur:
            for hh, d in enumerate(ch["hd"]):
                key = (ch["bi"], 2 * ch["p"] + hh)
                states[key] = states[key] * d["dl"] + _dot_tn(d["kd"], d["e"])
        r0 = ci * c
        r1 = min(rows, r0 + c)
        for ch in cur:
            for hh, d in enumerate(ch["hd"]):
                h = 2 * ch["p"] + hh
                o = d["o"] if r1 - r0 == c else d["o"][:r1 - r0]
                o_ref[ch["bi"], r0:r1, h * A_DV:(h + 1) * A_DV] = (
                    _rms(o, nwa) * _silu(z_ref[ch["bi"], r0:r1, h * A_DV:(h + 1) * A_DV]))
    for (bi, h), s in states.items():
        s_ref[bi, h] = s

    @pl.when(n == pl.num_programs(1) - 1)
    def _():
        snew_ref[...] = s_ref[...]


def _gdn_pair(proj, conv0, s0, w_conv, gate_par, norm_a, conv_done, side_of_grid=None):
    b, t, _ = proj.shape
    rows = min(GDN_ROWS, t)
    assert t % rows == 0 and (rows % CHUNK == 0 or rows == t)
    n = t // rows
    nb = max(1, min(b, GDN_ROWS // (-(-rows // CHUNK) * CHUNK)))
    assert b % nb == 0
    rp = 0 if conv_done else -(-rows // CHUNK) * CHUNK
    in_specs = [pl.BlockSpec((nb, rows, CONV_DIM), lambda i, j: (i, j, 0)),
                pl.BlockSpec((nb, rows, A_WIDTH), lambda i, j: (i, j, COL_Z // A_WIDTH)),
                pl.BlockSpec((nb, rows, GATE_PAD), lambda i, j: (i, j, COL_G // GATE_PAD)),
                pl.BlockSpec((nb, 8, CONV_DIM), lambda i, j: (i, 0, 0)),
                pl.BlockSpec((nb, A_HEADS, A_DK, A_DV), lambda i, j: (i, 0, 0, 0)),
                pl.BlockSpec((CONV_W, CONV_DIM), lambda i, j: (0, 0)),
                pl.BlockSpec((8, GATE_PAD), lambda i, j: (0, 0)),
                pl.BlockSpec((1, A_DV), lambda i, j: (0, 0))]
    out_specs = [pl.BlockSpec((nb, rows, A_WIDTH), lambda i, j: (i, j, 0)),
                 pl.BlockSpec((nb, A_HEADS, A_DK, A_DV), lambda i, j: (i, 0, 0, 0))]
    out_shape = [jax.ShapeDtypeStruct((b, t, A_WIDTH), F32),
                 jax.ShapeDtypeStruct((b, A_HEADS, A_DK, A_DV), F32)]
    inputs = (proj, proj, proj, conv0, s0, w_conv, gate_par, norm_a)
    n_in, n_out = len(in_specs), len(out_specs)
    side = side_of_grid((b // nb, n)) if side_of_grid is not None else None

    def kernel(*refs):
        n_side_in = len(side["in_specs"]) if side else 0
        gin, sin = refs[:n_in], refs[n_in:n_in + n_side_in]
        gout = refs[n_in + n_side_in:n_in + n_side_in + n_out]
        sout = refs[n_in + n_side_in + n_out:len(refs) - 2]
        side_work = (lambda: side["kernel"](*sin, *sout)) if side else None
        _gdn_pair_kernel(*gin, *gout, *refs[len(refs) - 2:], rows=rows, nb=nb, conv_done=conv_done,
                         side_work=side_work)

    if side:
        inputs = inputs + tuple(side["inputs"])
        in_specs = in_specs + list(side["in_specs"])
        out_specs = out_specs + list(side["out_specs"])
        out_shape = out_shape + list(side["out_shape"])
    outs = pl.pallas_call(
        kernel,
        grid=(b // nb, n),
        in_specs=in_specs,
        out_specs=out_specs,
        out_shape=out_shape,
        scratch_shapes=[pltpu.VMEM((nb, 8 + rp, CONV_DIM), F32),
                        pltpu.VMEM((nb, A_HEADS, A_DK, A_DV), F32)],
        compiler_params=pltpu.CompilerParams(dimension_semantics=("arbitrary", "arbitrary"),
                                             vmem_limit_bytes=VMEM_LIMIT),
        name="gdn_side" if side else "gdn",
    )(*inputs)
    return outs[:n_out], outs[n_out:]


def _slope(h):
    return 2.0 ** (-8.0 * (h + 1) / B_HEADS)


ATTN_SB = MAX_WINDOW
ATTN_SKEW = 2
PAIR = 128 // B_HEAD_DIM


def _attn_fused_kernel(q_ref, kp_ref, kc_ref, vp_ref, vc_ref, nwb_ref, o_ref, kt_ref, vt_ref, *stats):
    hp = pl.program_id(1)
    n = pl.program_id(2)
    sb = ATTN_SB
    blk = STEPS

    lane_q = lax.broadcasted_iota(jnp.int32, (blk, 128), 1)
    low_q = lane_q < B_HEAD_DIM
    qq = lax.broadcasted_iota(jnp.int32, (2 * blk, 2 * blk), 0)
    kk = lax.broadcasted_iota(jnp.int32, (2 * blk, 2 * blk), 1)
    steps_back = (qq & (blk - 1)) + blk - kk
    in_band = (steps_back >= 0) & (steps_back <= STEPS)
    slope_lo = jnp.float32(_slope(0))
    slope_hi = jnp.float32(_slope(1))
    for i in range(1, B_HEADS // PAIR):
        slope_lo = jnp.where(hp == i, _slope(PAIR * i), slope_lo)
        slope_hi = jnp.where(hp == i, _slope(PAIR * i + 1), slope_hi)
    slope = jnp.where(qq < blk, slope_lo, slope_hi)

    biases = []
    for (_, dil) in DILATED:
        bias = jnp.where(in_band, -slope * (steps_back * dil).astype(F32), NEG)
        biases.append((bias, jnp.where(kk >= blk, bias, NEG)))
    n_blk = sb // blk
    work = [(bi, it) for bi in range(len(DILATED)) for it in range(n_blk)]

    def rows(dil, start, size):
        return pl.ds(start, size) if dil == 1 else pl.ds(start, size, stride=dil)

    def key_tile(prev_ref, cur_ref, dil, k0):
        if k0 >= sb:
            return cur_ref[rows(dil, k0 - sb, 2 * blk), :].astype(BF16)
        assert k0 + blk * dil >= sb > k0 + (blk - 1) * dil
        return jnp.concatenate([prev_ref[rows(dil, k0, blk), :].astype(BF16),
                                cur_ref[rows(dil, k0 + blk * dil - sb, blk), :].astype(BF16)], axis=0)

    def issue_scores(bi, it):
        dil = DILATED[bi][1]
        q0 = (it // dil) * (blk * dil) + it % dil
        k0 = sb + q0 - blk * dil
        q = q_ref[rows(dil, q0, blk), :] * (B_HEAD_DIM ** -0.5)
        q2 = jnp.concatenate([jnp.where(low_q, q, 0.0), jnp.where(low_q, 0.0, q)], axis=0).astype(BF16)
        k = key_tile(kp_ref, kc_ref, dil, k0)
        return dict(bi=bi, dil=dil, q0=q0, k0=k0,
                    s=lax.dot_general(q2, k, (((1,), (1,)), ((), ())), preferred_element_type=F32))

    def softmax(w):
        bias, bias_first = biases[w["bi"]]
        s = w.pop("s") + (jnp.where(n == 0, bias_first, bias) if w["q0"] < blk * w["dil"] else bias)
        m = jnp.max(s, axis=-1, keepdims=True)
        p = jnp.exp(s - m)
        w.update(m=m, l=jnp.sum(p, axis=-1, keepdims=True), p=p.astype(BF16))

    def issue_values(w):
        dil, q0 = w["dil"], w["q0"]
        m_s, l_s, num_s = stats[3 * w["bi"]:3 * w["bi"] + 3]
        v = key_tile(vp_ref, vc_ref, dil, w["k0"])
        pv = jnp.dot(w["p"], v, preferred_element_type=F32)
        m_s[rows(dil, q0, blk), :] = jnp.where(low_q, w["m"][:blk], w["m"][blk:])
        l_s[rows(dil, q0, blk), :] = jnp.where(low_q, w["l"][:blk], w["l"][blk:])
        num_s[rows(dil, q0, blk), :] = jnp.where(low_q, pv[:blk], pv[blk:])

    inflight = {}
    for t in range(len(work) + 2 * ATTN_SKEW):
        if t < len(work):
            inflight[t] = issue_scores(*work[t])
        if 0 <= t - ATTN_SKEW < len(work):
            softmax(inflight[t - ATTN_SKEW])
        if 0 <= t - 2 * ATTN_SKEW < len(work):
            issue_values(inflight.pop(t - 2 * ATTN_SKEW))

    nwb = nwb_ref[...]

    def combine(i, carry):
        rs = pl.ds(pl.multiple_of(i * blk, blk), blk)
        ms = [stats[3 * bi][rs, :] for bi in range(len(DILATED))]
        ls = [stats[3 * bi + 1][rs, :] for bi in range(len(DILATED))]
        nums = [stats[3 * bi + 2][rs, :] for bi in range(len(DILATED))]
        mx = jnp.maximum(jnp.maximum(ms[0], ms[1]), ms[2])
        ws = [jnp.exp(m - mx) for m in ms]
        den = ws[0] * ls[0] + ws[1] * ls[1] + ws[2] * ls[2]
        o = (ws[0] * nums[0] + ws[1] * nums[1] + ws[2] * nums[2]) / den
        sq = o * o
        ss_lo = jnp.sum(jnp.where(low_q, sq, 0.0), axis=-1, keepdims=True)
        ss_hi = jnp.sum(jnp.where(low_q, 0.0, sq), axis=-1, keepdims=True)
        mean_sq = jnp.where(low_q, ss_lo, ss_hi) / B_HEAD_DIM
        o_ref[rs, :] = o * lax.rsqrt(mean_sq + EPS) * nwb
        return carry

    lax.fori_loop(0, sb // blk, combine, 0, unroll=4)

    @pl.when(n == pl.num_programs(2) - 1)
    def _():
        kt_ref[...] = kc_ref[...].T
        vt_ref[...] = vc_ref[...].T


def _attn_prompt(qkv, norm_b_pair):
    b, s, _ = qkv.shape
    sb = ATTN_SB
    assert s % sb == 0 and sb == min(MAX_WINDOW, s)
    assert all(w == STEPS * d and sb % w == 0 and d & (d - 1) == 0 for (w, d) in DILATED)
    n_pairs = B_HEADS // PAIR
    blk = (None, sb, 128)
    prev = lambda c0: pl.BlockSpec(blk, lambda i, p, n: (i, jnp.maximum(n - 1, 0), c0 + p))
    cur = lambda c0: pl.BlockSpec(blk, lambda i, p, n: (i, n, c0 + p))
    return pl.pallas_call(
        _attn_fused_kernel,
        grid=(b, n_pairs, s // sb),
        in_specs=[cur(0), prev(n_pairs), cur(n_pairs), prev(2 * n_pairs), cur(2 * n_pairs),
                  pl.BlockSpec((1, 128), lambda i, p, n: (0, 0))],
        out_specs=[pl.BlockSpec(blk, lambda i, p, n: (i, n, p)),
                   pl.BlockSpec((None, 128, sb), lambda i, p, n: (i, p, 0)),
                   pl.BlockSpec((None, 128, sb), lambda i, p, n: (i, p, 0))],
        out_shape=[jax.ShapeDtypeStruct((b, s, B_WIDTH), F32),
                   jax.ShapeDtypeStruct((b, B_WIDTH, sb), F32),
                   jax.ShapeDtypeStruct((b, B_WIDTH, sb), F32)],
        scratch_shapes=[pltpu.VMEM((sb, 128), F32)] * (3 * len(DILATED)),
        compiler_params=pltpu.CompilerParams(
            dimension_semantics=("arbitrary", "arbitrary", "arbitrary"), vmem_limit_bytes=VMEM_LIMIT),
        name="attn_prompt",
    )(qkv, qkv, qkv, qkv, qkv, norm_b_pair)


def _attn_sample_kernel(q_ref, kn_ref, vn_ref, kt_ref, vt_ref, nwb_ref, o_ref, kto_ref, vto_ref, *, n_past, t):
    kn = kn_ref[...]
    vn = vn_ref[...]
    kt = kt_ref[...]
    vt = vt_ref[...]

    prow = lax.broadcasted_iota(jnp.int32, (t, 128), 0)
    plane = lax.broadcasted_iota(jnp.int32, (t, 128), 1)
    place = jnp.where(plane == 128 - t + prow, 1.0, 0.0).astype(BF16)
    tail_lanes = lax.broadcasted_iota(jnp.int32, (B_WIDTH, 128), 1) >= 128 - t
    tn = functools.partial(lax.dot_general, dimension_numbers=(((0,), (0,)), ((), ())),
                           preferred_element_type=F32)

    def shifted(win, new, out_ref):
        hi, lo = _split2(new)
        lo2 = (new - hi.astype(F32) - lo.astype(F32)).astype(BF16)
        new_t = tn(hi, place) + tn(lo, place) + tn(lo2, place)
        sh = pltpu.roll(win, n_past - t, axis=1)
        out_ref[:, :n_past - 128] = sh[:, :n_past - 128]
        out_ref[:, n_past - 128:] = jnp.where(tail_lanes, new_t, sh[:, n_past - 128:])

    shifted(kt, kn, kto_ref)
    shifted(vt, vn, vto_ref)

    rq = B_HEADS * t
    t_bits = t.bit_length() - 1
    d_bits = B_HEAD_DIM.bit_length() - 1
    rowi = lax.broadcasted_iota(jnp.int32, (rq, B_WIDTH), 0)
    lanei = lax.broadcasted_iota(jnp.int32, (rq, B_WIDTH), 1)
    own = lax.shift_right_logical(rowi, t_bits) == lax.shift_right_logical(lanei, d_bits)
    q_rep = jnp.concatenate([q_ref[...]] * B_HEADS, axis=0)
    q_blk = jnp.where(own, q_rep, 0.0).astype(BF16)
    zpad = jnp.zeros((128 - t, B_WIDTH), BF16)
    kn_pad = jnp.concatenate([kn.astype(BF16), zpad], axis=0)
    vn_pad = jnp.concatenate([vn.astype(BF16), zpad], axis=0)
    vt16 = vt.astype(BF16)
    scale = B_HEAD_DIM ** -0.5
    s_c = jnp.dot(q_blk, kt.astype(BF16), preferred_element_type=F32) * scale
    s_n = lax.dot_general(q_blk, kn_pad, (((1,), (1,)), ((), ())), preferred_element_type=F32) * scale

    r1 = lax.broadcasted_iota(jnp.int32, (rq, 1), 0)
    head = lax.shift_right_logical(r1, t_bits)
    slope = jnp.zeros((rq, 1), F32)
    for h in range(B_HEADS):
        slope = jnp.where(head == h, _slope(h), slope)
    tok = r1 & (t - 1)
    delta_c = n_past + tok - lax.broadcasted_iota(jnp.int32, (rq, n_past), 1)
    key_n = lax.broadcasted_iota(jnp.int32, (rq, 128), 1)
    delta_n = tok - key_n
    s_c = s_c - slope * delta_c.astype(F32)
    s_n = s_n - slope * delta_n.astype(F32)
    live_n = (key_n < t) & (delta_n >= 0)

    ms, ls, pcs, pns = [], [], [], []
    for (_, dil) in DILATED:
        valid_c = ((delta_c & (dil - 1)) == 0) & (delta_c <= STEPS * dil)
        valid_n = live_n & ((delta_n & (dil - 1)) == 0) & (delta_n <= STEPS * dil)
        sd_c = jnp.where(valid_c, s_c, NEG)
        sd_n = jnp.where(valid_n, s_n, NEG)
        m = jnp.maximum(jnp.max(sd_c, axis=-1, keepdims=True), jnp.max(sd_n, axis=-1, keepdims=True))
        p_c = jnp.exp(sd_c - m)
        p_n = jnp.exp(sd_n - m)
        ms.append(m)
        ls.append(jnp.sum(p_c, axis=-1, keepdims=True) + jnp.sum(p_n, axis=-1, keepdims=True))
        pcs.append(p_c.astype(BF16))
        pns.append(p_n.astype(BF16))
    num_all = (lax.dot_general(jnp.concatenate(pcs, axis=0), vt16, (((1,), (1,)), ((), ())),
                               preferred_element_type=F32)
               + jnp.dot(jnp.concatenate(pns, axis=0), vn_pad, preferred_element_type=F32))
    nums = [num_all[i * rq:(i + 1) * rq] for i in range(len(DILATED))]
    mx = jnp.maximum(jnp.maximum(ms[0], ms[1]), ms[2])
    ws = [jnp.exp(m - mx) for m in ms]
    den = ws[0] * ls[0] + ws[1] * ls[1] + ws[2] * ls[2]
    o = (ws[0] * nums[0] + ws[1] * nums[1] + ws[2] * nums[2]) / den
    o = jnp.where(own, o, 0.0)
    o = o * lax.rsqrt(jnp.sum(o * o, axis=-1, keepdims=True) / B_HEAD_DIM + EPS) * nwb_ref[...]
    acc = o[0:t, :]
    for h in range(1, B_HEADS):
        acc = acc + o[h * t:(h + 1) * t, :]
    o_ref[...] = acc


def _attn_sample_parts(qkv, win_kt, win_vt, norm_b_tiled, seq_of_step):
    b, t, _ = qkv.shape
    n_past = win_kt.shape[2]
    assert n_past == MAX_WINDOW and n_past % 128 == 0 and t % 8 == 0 and t & (t - 1) == 0 and t <= 128
    assert all(d & (d - 1) == 0 for (_, d) in DILATED)
    col = lambda c: pl.BlockSpec((None, t, B_WIDTH), lambda *g: (seq_of_step(*g), 0, c))
    win = pl.BlockSpec((None, B_WIDTH, n_past), lambda *g: (seq_of_step(*g), 0, 0))
    return dict(
        kernel=functools.partial(_attn_sample_kernel, n_past=n_past, t=t),
        inputs=(qkv, qkv, qkv, win_kt, win_vt, norm_b_tiled),
        in_specs=[col(0), col(1), col(2), win, win, pl.BlockSpec((1, B_WIDTH), lambda *g: (0, 0))],
        out_specs=[col(0), win, win],
        out_shape=[jax.ShapeDtypeStruct((b, t, B_WIDTH), F32),
                   jax.ShapeDtypeStruct((b, B_WIDTH, n_past), F32),
                   jax.ShapeDtypeStruct((b, B_WIDTH, n_past), F32)])


def _attn_sample(qkv, win_kt, win_vt, norm_b_tiled):
    parts = _attn_sample_parts(qkv, win_kt, win_vt, norm_b_tiled, lambda i: i)
    return pl.pallas_call(
        parts["kernel"],
        grid=(qkv.shape[0],),
        in_specs=parts["in_specs"],
        out_specs=parts["out_specs"],
        out_shape=parts["out_shape"],
        compiler_params=pltpu.CompilerParams(dimension_semantics=("arbitrary",),
                                             vmem_limit_bytes=VMEM_LIMIT),
        name="attn_sample",
    )(*parts["inputs"])


def _out_ffn_kernel(x_ref, oa_ref, ob_ref, wo_ref, nf_ref, wg_ref, wu_ref, wd_ref, nfin_ref, y_ref, *, ff_chunk):
    mixed = jnp.concatenate([oa_ref[...], ob_ref[...]], axis=-1).astype(BF16)
    x1 = x_ref[...] + jnp.dot(mixed, wo_ref[...], preferred_element_type=F32)
    hf = _rms(x1, nf_ref[...]).astype(BF16)
    x2 = x1
    for c0 in range(0, wg_ref.shape[1], ff_chunk):
        g = jnp.dot(hf, wg_ref[:, c0:c0 + ff_chunk], preferred_element_type=F32)
        u = jnp.dot(hf, wu_ref[:, c0:c0 + ff_chunk], preferred_element_type=F32)
        act = (_silu(g) * u).astype(BF16)
        x2 = x2 + jnp.dot(act, wd_ref[c0:c0 + ff_chunk, :], preferred_element_type=F32)
    y_ref[...] = _rms(x2, nfin_ref[...])


def _out_ffn(x2d, o_a, o_b, w_out, norm_ffn, w_gate, w_up, w_down, norm_final, tm):
    t = x2d.shape[0]
    d_ff = w_gate.shape[1]
    ff_chunk = d_ff
    once = pl.Buffered(1)
    row = lambda w: pl.BlockSpec((tm, w), lambda i: (i, 0))
    full = lambda a, b: pl.BlockSpec((a, b), lambda i: (0, 0), pipeline_mode=once)
    return pl.pallas_call(
        functools.partial(_out_ffn_kernel, ff_chunk=ff_chunk),
        grid=(t // tm,),
        in_specs=[row(D_MODEL), row(A_WIDTH), row(B_WIDTH),
                  full(D_MODEL, D_MODEL), full(1, D_MODEL),
                  full(D_MODEL, d_ff), full(D_MODEL, d_ff), full(d_ff, D_MODEL), full(1, D_MODEL)],
        out_specs=row(D_MODEL),
        out_shape=jax.ShapeDtypeStruct((t, D_MODEL), F32),
        compiler_params=pltpu.CompilerParams(dimension_semantics=("arbitrary",),
                                             vmem_limit_bytes=VMEM_LIMIT),
        name="out_ffn",
    )(x2d, o_a, o_b, w_out, norm_ffn, w_gate, w_up, w_down, norm_final)


def _layer_params(norm_mix, w_in, w_conv, a_log, dt_bias, norm_out_a, norm_out_b, w_out, norm_ffn,
                  w_gate, w_up, w_down, layer):
    w = w_in[layer]
    n_gate = 2 * A_HEADS
    w_p = (w[:, :COL_G].astype(BF16),
           jnp.pad(w[:, COL_G:COL_G + n_gate], ((0, 0), (0, GATE_PAD - n_gate))).astype(BF16),
           w[:, COL_G + n_gate:].astype(BF16))
    gate_par = jnp.zeros((8, GATE_PAD), F32)
    gate_par = gate_par.at[0, A_HEADS:2 * A_HEADS].set(a_log[layer].astype(F32))
    gate_par = gate_par.at[1, A_HEADS:2 * A_HEADS].set(dt_bias[layer].astype(F32))
    return dict(
        norm_mix=norm_mix[layer].reshape(1, D_MODEL), w_p=w_p, w_conv=w_conv[layer], gate_par=gate_par,
        norm_a=norm_out_a[layer].reshape(1, A_DV),
        norm_b_pair=jnp.tile(norm_out_b[layer], PAIR).reshape(1, PAIR * B_HEAD_DIM),
        norm_b_tiled=jnp.tile(norm_out_b[layer], B_HEADS).reshape(1, B_WIDTH),
        w_out=w_out[layer].astype(BF16), norm_ffn=norm_ffn[layer].reshape(1, D_MODEL),
        w_gate=w_gate[layer].astype(BF16), w_up=w_up[layer].astype(BF16), w_down=w_down[layer].astype(BF16))


def _project(x, conv_buf, p, tm):
    b, t, _ = x.shape
    assert t >= CONV_W - 1
    conv0 = jnp.pad(conv_buf, ((0, 0), (8 - (CONV_W - 1), 0), (0, 0)))
    conv_done = t % tm == 0
    if conv_done:
        proj_a, proj_b, tail = _norm_proj_conv(x, p["norm_mix"], p["w_p"], p["w_conv"], conv0, tm)
        new_conv = tail[:, 8 - (CONV_W - 1):]
    else:
        proj_a, proj_b = _norm_proj(x.reshape(b * t, D_MODEL), p["norm_mix"], p["w_p"], tm)
        proj_a = proj_a.reshape(b, t, COLS_A)
        proj_b = proj_b.reshape(b, t, COLS_B)
        new_conv = proj_a[:, t - (CONV_W - 1):, :CONV_DIM]
    return proj_a, proj_b, new_conv, conv0, conv_done


def _finish(x, o_a, o_b, p, norm_final, tm):
    b, t, _ = x.shape
    y = _out_ffn(x.reshape(b * t, D_MODEL), o_a.reshape(b * t, A_WIDTH), o_b.reshape(b * t, B_WIDTH), p["w_out"],
                 p["norm_ffn"], p["w_gate"], p["w_up"], p["w_down"], norm_final.reshape(1, D_MODEL), tm)
    return y.reshape(b, t, D_MODEL)


def kernel(x_prompt, x_sample, state_conv, state_rec, cache_win_k, cache_win_v, norm_mix, w_in, w_conv, a_log,
           dt_bias, norm_out_a, norm_out_b, w_out, norm_ffn, w_gate, w_up, w_down, norm_final):
    depth = w_in.shape[0]
    assert depth == 1, "the final norm is fused into the block of a single-layer trunk"
    p = _layer_params(norm_mix, w_in, w_conv, a_log, dt_bias, norm_out_a, norm_out_b, w_out, norm_ffn,
                      w_gate, w_up, w_down, 0)
    bp, tp, _ = x_prompt.shape
    bs, ts, _ = x_sample.shape
    tm_p, tm_s = 512, bs * ts
    gdn_args = (p["w_conv"], p["gate_par"], p["norm_a"])

    sa, sb, sc, s_conv0, s_done = _project(x_sample, state_conv[0], p, tm_s)
    n_past = cache_win_k.shape[2]
    to_t = lambda a: jnp.transpose(a, (0, 2, 3, 1)).reshape(bs, B_WIDTH, n_past)
    from_t = lambda a: jnp.transpose(a.reshape(a.shape[0], B_HEADS, B_HEAD_DIM, a.shape[2]), (0, 3, 1, 2))
    win_kt, win_vt = to_t(cache_win_k[0]), to_t(cache_win_v[0])

    def sample_attention_of_grid(grid):
        if grid[0] * grid[1] != bs:
            return None
        return _attn_sample_parts(sb, win_kt, win_vt, p["norm_b_tiled"], lambda i, j: i * grid[1] + j)

    pa, pb, pc, p_conv0, p_done = _project(x_prompt, jnp.zeros((bp, CONV_W - 1, CONV_DIM), F32), p, tm_p)
    zero_rec = jnp.zeros((bp, A_HEADS, A_DK, A_DV), F32)
    (o_a_p, pr), side = _gdn_pair(pa, p_conv0, zero_rec, *gdn_args, p_done, sample_attention_of_grid)
    o_b_s, new_kt, new_vt = side if side else _attn_sample(sb, win_kt, win_vt, p["norm_b_tiled"])
    (o_a_s, sr), _ = _gdn_pair(sa, s_conv0, state_rec[0], *gdn_args, s_done)

    o_b_p, pkt, pvt = _attn_prompt(pb, p["norm_b_pair"])

    yp = _finish(x_prompt, o_a_p, o_b_p, p, norm_final, tm_p)
    ys = _finish(x_sample, o_a_s, o_b_s, p, norm_final, tm_s)
    return (yp, ys, pc[None], pr[None], from_t(pkt)[None], from_t(pvt)[None], sc[None], sr[None],
            from_t(new_kt)[None], from_t(new_vt)[None])
```

```python
import functools

import jax
import jax.numpy as jnp
from jax import lax
from jax.experimental import pallas as pl
from jax.experimental.pallas import tpu as pltpu

F32 = jnp.float32
BF16 = jnp.bfloat16

D_MODEL = 1024
A_HEADS = 4
A_DK = 128
A_DV = 128
A_WIDTH = A_HEADS * A_DV
CONV_W = 4
CONV_DIM = 2 * A_HEADS * A_DK + A_HEADS * A_DV
B_HEADS = 8
B_HEAD_DIM = 64
B_WIDTH = B_HEADS * B_HEAD_DIM
DILATED = ((128, 1), (512, 4), (2048, 16))
STEPS = 128
MAX_WINDOW = 2048
EPS = 1e-6
NEG = -1e30
CHUNK = 64
GATE_PAD = 128
COL_Z = CONV_DIM
COL_G = CONV_DIM + A_WIDTH
COLS_A = COL_G + GATE_PAD
COLS_B = 3 * B_WIDTH
VMEM_LIMIT = 56 * 1024 * 1024


def _dot(a, b):
    return jnp.dot(a.astype(BF16), b.astype(BF16), preferred_element_type=F32)


def _dot_nt(a, b):
    return lax.dot_general(a.astype(BF16), b.astype(BF16), (((1,), (1,)), ((), ())),
                           preferred_element_type=F32)


def _dot_tn(a, b):
    return lax.dot_general(a.astype(BF16), b.astype(BF16), (((0,), (0,)), ((), ())),
                           preferred_element_type=F32)


def _split2(a):
    hi = a.astype(BF16)
    lo = (a - hi.astype(F32)).astype(BF16)
    return hi, lo


def _sigmoid(x):
    return 1.0 / (1.0 + jnp.exp(-x))


def _silu(x):
    return x * _sigmoid(x)


def _softplus(x):
    return jnp.maximum(x, 0.0) + jnp.log(1.0 + jnp.exp(-jnp.abs(x)))


def _rms(x, w):
    return x * lax.rsqrt(jnp.mean(x * x, axis=-1, keepdims=True) + EPS) * w


def _proj(h, wt):
    return lax.dot_general(h, wt, (((1,), (1,)), ((), ())), preferred_element_type=F32)


def _norm_proj_kernel(x_ref, nw_ref, wa_ref, wg_ref, wb_ref, oa_ref, ob_ref):
    h = _rms(x_ref[...], nw_ref[...]).astype(BF16)
    oa_ref[:, :COL_G] = _proj(h, wa_ref[...])
    oa_ref[:, COL_G:] = _proj(h, wg_ref[...])
    ob_ref[...] = _proj(h, wb_ref[...])


def _weight_specs():
    zero = lambda *g: (0, 0)
    return [pl.BlockSpec((c, D_MODEL), zero, pipeline_mode=pl.Buffered(1)) for c in (COL_G, GATE_PAD, COLS_B)]


def _norm_proj(x2d, norm_w, w_p, tm):
    t = x2d.shape[0]
    once = pl.Buffered(1)
    return pl.pallas_call(
        _norm_proj_kernel,
        grid=(t // tm,),
        in_specs=[pl.BlockSpec((tm, D_MODEL), lambda i: (i, 0)),
                  pl.BlockSpec((1, D_MODEL), lambda i: (0, 0), pipeline_mode=once)] + _weight_specs(),
        out_specs=[pl.BlockSpec((tm, COLS_A), lambda i: (i, 0)),
                   pl.BlockSpec((tm, COLS_B), lambda i: (i, 0))],
        out_shape=[jax.ShapeDtypeStruct((t, COLS_A), F32),
                   jax.ShapeDtypeStruct((t, COLS_B), F32)],
        compiler_params=pltpu.CompilerParams(dimension_semantics=("arbitrary",),
                                             vmem_limit_bytes=VMEM_LIMIT),
        name="norm_proj",
    )(x2d, norm_w, *w_p)


def _causal_conv_silu(x_all, wconv_ref, n):
    conv = x_all[8:8 + n] * wconv_ref[CONV_W - 1:CONV_W, :]
    for back in range(1, CONV_W):
        tap = CONV_W - 1 - back
        conv = conv + pltpu.roll(x_all, back, axis=0)[8:8 + n] * wconv_ref[tap:tap + 1, :]
    return _silu(conv)


def _l2norm(x):
    return x * lax.rsqrt(jnp.sum(x * x, axis=-1, keepdims=True) + EPS)


def _norm_proj_conv_kernel(x_ref, nw_ref, wa_ref, wg_ref, wb_ref, wconv_ref, conv0_ref, oa_ref, ob_ref, tail_ref,
                           carry_ref, h_ref, raw_ref):
    i = pl.program_id(1)
    tm = x_ref.shape[0]
    wq = A_HEADS * A_DK
    assert CONV_DIM == 3 * wq and COLS_B == 3 * B_WIDTH
    dd = functools.partial(jnp.dot, preferred_element_type=F32)

    @pl.when(i == 0)
    def _():
        carry_ref[...] = conv0_ref[...]

    def conv_cols(g):
        cs = slice(g * wq, (g + 1) * wq)
        conv = _causal_conv_silu(jnp.concatenate([carry_ref[:, cs], raw_ref[:, cs]], axis=0),
                                 wconv_ref.at[:, cs], tm)
        carry_ref[:, cs] = raw_ref[tm - 8:tm, cs]
        if g == 2:
            oa_ref[:, cs] = conv
            return
        for hd in range(A_HEADS):
            hs = slice(hd * A_DK, (hd + 1) * A_DK)
            out = _l2norm(conv[:, hs])
            oa_ref[:, g * wq + hd * A_DK:g * wq + (hd + 1) * A_DK] = out * (A_DK ** -0.5) if g == 0 else out

    def phase(k):
        if k == 0:
            h_ref[...] = _rms(x_ref[...], nw_ref[...]).astype(BF16)
        h = h_ref[...]
        if k < 3:
            raw_ref[:, k * wq:(k + 1) * wq] = _proj(h, wa_ref[k * wq:(k + 1) * wq, :])
        if k == 0:
            oa_ref[:, CONV_DIM:COL_G] = _proj(h, wa_ref[CONV_DIM:, :])
        else:
            ob_ref[:, (k - 1) * B_WIDTH:k * B_WIDTH] = _proj(h, wb_ref[(k - 1) * B_WIDTH:k * B_WIDTH, :])
            conv_cols(k - 1)
        if k == 3:
            oa_ref[:, COL_G:] = _proj(h, wg_ref[...])

    for k in range(4):
        pl.when(i >= 0)(functools.partial(phase, k))

    @pl.when(i == pl.num_programs(1) - 1)
    def _():
        tail_ref[...] = raw_ref[tm - 8:tm, :]


def _norm_proj_conv(x, norm_w, w_p, w_conv, conv0, tm):
    b, t, _ = x.shape
    assert t % tm == 0 and tm % 8 == 0
    once = pl.Buffered(1)
    return pl.pallas_call(
        _norm_proj_conv_kernel,
        grid=(b, t // tm),
        in_specs=[pl.BlockSpec((None, tm, D_MODEL), lambda i, j: (i, j, 0)),
                  pl.BlockSpec((1, D_MODEL), lambda i, j: (0, 0), pipeline_mode=once)] + _weight_specs() + [
                  pl.BlockSpec((CONV_W, CONV_DIM), lambda i, j: (0, 0), pipeline_mode=once),
                  pl.BlockSpec((None, 8, CONV_DIM), lambda i, j: (i, 0, 0))],
        out_specs=[pl.BlockSpec((None, tm, COLS_A), lambda i, j: (i, j, 0)),
                   pl.BlockSpec((None, tm, COLS_B), lambda i, j: (i, j, 0)),
                   pl.BlockSpec((None, 8, CONV_DIM), lambda i, j: (i, 0, 0))],
        out_shape=[jax.ShapeDtypeStruct((b, t, COLS_A), F32),
                   jax.ShapeDtypeStruct((b, t, COLS_B), F32),
                   jax.ShapeDtypeStruct((b, 8, CONV_DIM), F32)],
        scratch_shapes=[pltpu.VMEM((8, CONV_DIM), F32), pltpu.VMEM((tm, D_MODEL), BF16),
                        pltpu.VMEM((tm, CONV_DIM), F32)],
        compiler_params=pltpu.CompilerParams(dimension_semantics=("arbitrary", "arbitrary"),
                                             vmem_limit_bytes=VMEM_LIMIT),
        name="norm_proj_conv",
    )(x, norm_w, *w_p, w_conv, conv0)


GDN_ROWS = 8 * CHUNK


HALF = 128 // CHUNK


def _gdn_pair_kernel(xa_ref, z_ref, gt_ref, conv0_ref, s0_ref, wconv_ref, gp_ref, nwa_ref,
                     o_ref, snew_ref, ext_ref, s_ref, *, rows, nb, conv_done, side_work=None):
    n = pl.program_id(1)
    c = CHUNK
    n_chunks = -(-rows // c)
    rp = n_chunks * c
    assert A_HEADS % 2 == 0 and HALF == 2 and A_DK == 128 and A_DV == 128

    @pl.when(n == 0)
    def _():
        ext_ref[:, 0:8, :] = conv0_ref[...]
        s_ref[...] = s0_ref[...]

    if side_work is not None:
        side_work()

    row = lax.broadcasted_iota(jnp.int32, (rp, 1), 0)
    live = row < rows
    lane_g = lax.broadcasted_iota(jnp.int32, (rp, GATE_PAD), 1)
    gate_lanes = (lane_g >= A_HEADS) & (lane_g < 2 * A_HEADS) & live

    ii = lax.broadcasted_iota(jnp.int32, (c, 128), 0)
    ll = lax.broadcasted_iota(jnp.int32, (c, 128), 1)
    jj = ll & (c - 1)
    low = ll < c
    incl = ii >= jj
    strict = ii > jj
    eye = jnp.where(ii == jj, 1.0, 0.0).astype(F32)
    ti = lax.broadcasted_iota(jnp.int32, (c, c), 0)
    tj = lax.broadcasted_iota(jnp.int32, (c, c), 1)
    tril = jnp.where(ti >= tj, 1.0, 0.0).astype(BF16)
    dd = functools.partial(jnp.dot, preferred_element_type=F32)

    def exact_tril_dot(x):
        hi, lo = _split2(x)
        lo2 = (x - hi.astype(F32) - lo.astype(F32)).astype(BF16)
        return dd(tril, hi) + dd(tril, lo) + dd(tril, lo2)

    def bd(x):
        return jnp.concatenate([jnp.where(low, x, 0.0), jnp.where(low, 0.0, x)], axis=0).astype(BF16)

    def unstack(y):
        return jnp.where(low, y[:c], y[c:])

    nwa = nwa_ref[...]
    chains = []
    for bi in range(nb):
        if conv_done:
            assert rows == rp
            conv = xa_ref[bi]
        else:
            ext_ref[bi, 8:8 + rows, :] = xa_ref[bi]
            if rows < rp:
                ext_ref[bi, 8 + rows:8 + rp, :] = jnp.zeros((rp - rows, CONV_DIM), F32)
            conv = _causal_conv_silu(ext_ref[bi, 0:8 + rp, :], wconv_ref, rp)
            ext_ref[bi, 0:8, :] = ext_ref[bi, rp:rp + 8, :]
            if rows < rp:
                conv = jnp.where(live, conv, 0.0)

        gt = gt_ref[bi]
        if rows < rp:
            gt = jnp.concatenate([gt, jnp.zeros((rp - rows, GATE_PAD), F32)], axis=0)
        beta_all = jnp.where(live, _sigmoid(gt), 0.0)
        g_all = jnp.where(gate_lanes, -jnp.exp(gp_ref[0:1, :]) * _softplus(gt + gp_ref[1:2, :]), 0.0)

        for ci in range(n_chunks):
            rs = slice(ci * c, (ci + 1) * c)
            gc_all = exact_tril_dot(g_all[rs])
            for p in range(A_HEADS // 2):
                hd = []
                for h in (2 * p, 2 * p + 1):
                    q = conv[rs, h * A_DK:(h + 1) * A_DK]
                    k = conv[rs, A_HEADS * A_DK + h * A_DK:A_HEADS * A_DK + (h + 1) * A_DK]
                    v = conv[rs, 2 * A_HEADS * A_DK + h * A_DV:2 * A_HEADS * A_DK + (h + 1) * A_DV]
                    if not conv_done:
                        q = _l2norm(q) * (A_DK ** -0.5)
                        k = _l2norm(k)
                    beta = beta_all[rs, h:h + 1]
                    g = g_all[rs, A_HEADS + h:A_HEADS + h + 1]
                    gc = gc_all[:, A_HEADS + h:A_HEADS + h + 1]
                    gl = gc_all[c - 1:c, A_HEADS + h:A_HEADS + h + 1]
                    kb = k * beta
                    hd.append(dict(q=q, k=k, kb=kb, g=g, qg=q * jnp.exp(gc), kd=k * jnp.exp(gl - gc),
                                   dl=jnp.exp(gl),
                                   rhs=jnp.concatenate([v * beta, kb * jnp.exp(gc)], axis=1)))
                chains.append(dict(bi=bi, ci=ci, p=p, hd=hd))

    for ch in chains:
        a, b = ch["hd"]
        ch["e_mat"] = exact_tril_dot(jnp.where(strict, jnp.where(low, a["g"], b["g"]), 0.0))
    for ch in chains:
        a, b = ch["hd"]
        k_st = jnp.concatenate([a["k"], b["k"]], axis=0)
        ch["kk"] = _dot_nt(jnp.concatenate([a["kb"], b["kb"]], axis=0), k_st)
        ch["qk"] = _dot_nt(jnp.concatenate([a["q"], b["q"]], axis=0), k_st)
    for ch in chains:
        decay = jnp.where(incl, jnp.exp(jnp.where(incl, ch["e_mat"], 0.0)), 0.0)
        ch["lower"] = jnp.where(strict, unstack(ch["kk"]) * decay, 0.0)
        ch["qk"] = jnp.where(incl, unstack(ch["qk"]) * decay, 0.0)
    def joining(level):
        return strict & ((ii >> (level + 1)) == (jj >> (level + 1))) & ((ii >> level) != (jj >> level))

    for ch in chains:
        ch["t"] = eye - jnp.where(joining(0), ch["lower"], 0.0)
    for level in range(1, c.bit_length() - 1):
        join = joining(level)
        for ch in chains:
            ch["te"] = jnp.dot(ch["t"].astype(BF16), bd(jnp.where(join, ch["lower"], 0.0)),
                               preferred_element_type=F32)
        for ch in chains:
            ch["t"] = ch["t"] - jnp.dot(ch["te"].astype(BF16), bd(ch["t"]), preferred_element_type=F32)
    for ch in chains:
        lh, ll_ = _split2(ch["lower"])
        th = ch["t"].astype(BF16).astype(F32)
        tl = ch["t"] - th
        bth = bd(th)
        lt = dd(lh, bth) + dd(lh, bd(tl)) + dd(ll_, bth)
        ch["res"] = eye - ch["t"] - lt
    for ch in chains:
        ch["t"] = ch["t"] + jnp.dot(ch["t"].astype(BF16), bd(ch["res"]), preferred_element_type=F32)
    for ch in chains:
        a, b = ch["hd"]
        uw = jnp.dot(bd(ch["t"]), jnp.concatenate([a["rhs"], b["rhs"]], axis=0).astype(BF16),
                     preferred_element_type=F32)
        a["u"], a["w"] = uw[:c, :A_DV], uw[:c, A_DV:]
        b["u"], b["w"] = uw[c:, :A_DV], uw[c:, A_DV:]
        ch["qkb"] = bd(ch["qk"])

    by_key = {(ch["bi"], ch["ci"], ch["p"]): ch for ch in chains}
    states = {(bi, h): s_ref[bi, h] for bi in range(nb) for h in range(A_HEADS)}
    for ci in range(n_chunks):
        cur = [by_key[(bi, ci, p)] for bi in range(nb) for p in range(A_HEADS // 2)]
        for ch in cur:
            for hh, d in enumerate(ch["hd"]):
                ws = _dot(jnp.concatenate([d["w"], d["qg"]], axis=0), states[(ch["bi"], 2 * ch["p"] + hh)])
                d["e"] = d["u"] - ws[:c]
                d["o"] = ws[c:]
        for ch in cur:
            a, b = ch["hd"]
            o2 = jnp.dot(ch["qkb"], jnp.concatenate([a["e"], b["e"]], axis=0).astype(BF16),
                         preferred_element_type=F32)
            a["o"] = a["o"] + o2[:c]
            b["o"] = b["o"] + o2[c:]
        for ch in cur:
            for hh, d in enumerate(ch["hd"]):
                key = (ch["bi"], 2 * ch["p"] + hh)
                states[key] = states[key] * d["dl"] + _dot_tn(d["kd"], d["e"])
        r0 = ci * c
        r1 = min(rows, r0 + c)
        for ch in cur:
            for hh, d in enumerate(ch["hd"]):
                h = 2 * ch["p"] + hh
                o = d["o"] if r1 - r0 == c else d["o"][:r1 - r0]
                o_ref[ch["bi"], r0:r1, h * A_DV:(h + 1) * A_DV] = (
                    _rms(o, nwa) * _silu(z_ref[ch["bi"], r0:r1, h * A_DV:(h + 1) * A_DV]))
    for (bi, h), s in states.items():
        s_ref[bi, h] = s

    @pl.when(n == pl.num_programs(1) - 1)
    def _():
        snew_ref[...] = s_ref[...]


def _gdn_pair(proj, conv0, s0, w_conv, gate_par, norm_a, conv_done, side_of_grid=None):
    b, t, _ = proj.shape
    rows = min(GDN_ROWS, t)
    assert t % rows == 0 and (rows % CHUNK == 0 or rows == t)
    n = t // rows
    nb = max(1, min(b, GDN_ROWS // (-(-rows // CHUNK) * CHUNK)))
    assert b % nb == 0
    rp = 0 if conv_done else -(-rows // CHUNK) * CHUNK
    in_specs = [pl.BlockSpec((nb, rows, CONV_DIM), lambda i, j: (i, j, 0)),
                pl.BlockSpec((nb, rows, A_WIDTH), lambda i, j: (i, j, COL_Z // A_WIDTH)),
                pl.BlockSpec((nb, rows, GATE_PAD), lambda i, j: (i, j, COL_G // GATE_PAD)),
                pl.BlockSpec((nb, 8, CONV_DIM), lambda i, j: (i, 0, 0)),
                pl.BlockSpec((nb, A_HEADS, A_DK, A_DV), lambda i, j: (i, 0, 0, 0)),
                pl.BlockSpec((CONV_W, CONV_DIM), lambda i, j: (0, 0)),
                pl.BlockSpec((8, GATE_PAD), lambda i, j: (0, 0)),
                pl.BlockSpec((1, A_DV), lambda i, j: (0, 0))]
    out_specs = [pl.BlockSpec((nb, rows, A_WIDTH), lambda i, j: (i, j, 0)),
                 pl.BlockSpec((nb, A_HEADS, A_DK, A_DV), lambda i, j: (i, 0, 0, 0))]
    out_shape = [jax.ShapeDtypeStruct((b, t, A_WIDTH), F32),
                 jax.ShapeDtypeStruct((b, A_HEADS, A_DK, A_DV), F32)]
    inputs = (proj, proj, proj, conv0, s0, w_conv, gate_par, norm_a)
    n_in, n_out = len(in_specs), len(out_specs)
    side = side_of_grid((b // nb, n)) if side_of_grid is not None else None

    def kernel(*refs):
        n_side_in = len(side["in_specs"]) if side else 0
        gin, sin = refs[:n_in], refs[n_in:n_in + n_side_in]
        gout = refs[n_in + n_side_in:n_in + n_side_in + n_out]
        sout = refs[n_in + n_side_in + n_out:len(refs) - 2]
        side_work = (lambda: side["kernel"](*sin, *sout)) if side else None
        _gdn_pair_kernel(*gin, *gout, *refs[len(refs) - 2:], rows=rows, nb=nb, conv_done=conv_done,
                         side_work=side_work)

    if side:
        inputs = inputs + tuple(side["inputs"])
        in_specs = in_specs + list(side["in_specs"])
        out_specs = out_specs + list(side["out_specs"])
        out_shape = out_shape + list(side["out_shape"])
    outs = pl.pallas_call(
        kernel,
        grid=(b // nb, n),
        in_specs=in_specs,
        out_specs=out_specs,
        out_shape=out_shape,
        scratch_shapes=[pltpu.VMEM((nb, 8 + rp, CONV_DIM), F32),
                        pltpu.VMEM((nb, A_HEADS, A_DK, A_DV), F32)],
        compiler_params=pltpu.CompilerParams(dimension_semantics=("arbitrary", "arbitrary"),
                                             vmem_limit_bytes=VMEM_LIMIT),
        name="gdn_side" if side else "gdn",
    )(*inputs)
    return outs[:n_out], outs[n_out:]


def _slope(h):
    return 2.0 ** (-8.0 * (h + 1) / B_HEADS)


ATTN_SB = MAX_WINDOW
ATTN_SKEW = 2
PAIR = 128 // B_HEAD_DIM


def _attn_fused_kernel(q_ref, kp_ref, kc_ref, vp_ref, vc_ref, nwb_ref, o_ref, kt_ref, vt_ref, *stats):
    hp = pl.program_id(1)
    n = pl.program_id(2)
    sb = ATTN_SB
    blk = STEPS

    lane_q = lax.broadcasted_iota(jnp.int32, (blk, 128), 1)
    low_q = lane_q < B_HEAD_DIM
    qq = lax.broadcasted_iota(jnp.int32, (2 * blk, 2 * blk), 0)
    kk = lax.broadcasted_iota(jnp.int32, (2 * blk, 2 * blk), 1)
    steps_back = (qq & (blk - 1)) + blk - kk
    in_band = (steps_back >= 0) & (steps_back <= STEPS)
    slope_lo = jnp.float32(_slope(0))
    slope_hi = jnp.float32(_slope(1))
    for i in range(1, B_HEADS // PAIR):
        slope_lo = jnp.where(hp == i, _slope(PAIR * i), slope_lo)
        slope_hi = jnp.where(hp == i, _slope(PAIR * i + 1), slope_hi)
    slope = jnp.where(qq < blk, slope_lo, slope_hi)

    biases = []
    for (_, dil) in DILATED:
        bias = jnp.where(in_band, -slope * (steps_back * dil).astype(F32), NEG)
        biases.append((bias, jnp.where(kk >= blk, bias, NEG)))
    n_blk = sb // blk
    work = [(bi, it) for bi in range(len(DILATED)) for it in range(n_blk)]

    def rows(dil, start, size):
        return pl.ds(start, size) if dil == 1 else pl.ds(start, size, stride=dil)

    def key_tile(prev_ref, cur_ref, dil, k0):
        if k0 >= sb:
            return cur_ref[rows(dil, k0 - sb, 2 * blk), :].astype(BF16)
        assert k0 + blk * dil >= sb > k0 + (blk - 1) * dil
        return jnp.concatenate([prev_ref[rows(dil, k0, blk), :].astype(BF16),
                                cur_ref[rows(dil, k0 + blk * dil - sb, blk), :].astype(BF16)], axis=0)

    def issue_scores(bi, it):
        dil = DILATED[bi][1]
        q0 = (it // dil) * (blk * dil) + it % dil
        k0 = sb + q0 - blk * dil
        q = q_ref[rows(dil, q0, blk), :] * (B_HEAD_DIM ** -0.5)
        q2 = jnp.concatenate([jnp.where(low_q, q, 0.0), jnp.where(low_q, 0.0, q)], axis=0).astype(BF16)
        k = key_tile(kp_ref, kc_ref, dil, k0)
        return dict(bi=bi, dil=dil, q0=q0, k0=k0,
                    s=lax.dot_general(q2, k, (((1,), (1,)), ((), ())), preferred_element_type=F32))

    def softmax(w):
        bias, bias_first = biases[w["bi"]]
        s = w.pop("s") + (jnp.where(n == 0, bias_first, bias) if w["q0"] < blk * w["dil"] else bias)
        m = jnp.max(s, axis=-1, keepdims=True)
        p = jnp.exp(s - m)
        w.update(m=m, l=jnp.sum(p, axis=-1, keepdims=True), p=p.astype(BF16))

    def issue_values(w):
        dil, q0 = w["dil"], w["q0"]
        m_s, l_s, num_s = stats[3 * w["bi"]:3 * w["bi"] + 3]
        v = key_tile(vp_ref, vc_ref, dil, w["k0"])
        pv = jnp.dot(w["p"], v, preferred_element_type=F32)
        m_s[rows(dil, q0, blk), :] = jnp.where(low_q, w["m"][:blk], w["m"][blk:])
        l_s[rows(dil, q0, blk), :] = jnp.where(low_q, w["l"][:blk], w["l"][blk:])
        num_s[rows(dil, q0, blk), :] = jnp.where(low_q, pv[:blk], pv[blk:])

    inflight = {}
    for t in range(len(work) + 2 * ATTN_SKEW):
        if t < len(work):
            inflight[t] = issue_scores(*work[t])
        if 0 <= t - ATTN_SKEW < len(work):
            softmax(inflight[t - ATTN_SKEW])
        if 0 <= t - 2 * ATTN_SKEW < len(work):
            issue_values(inflight.pop(t - 2 * ATTN_SKEW))

    nwb = nwb_ref[...]

    def combine(i, carry):
        rs = pl.ds(pl.multiple_of(i * blk, blk), blk)
        ms = [stats[3 * bi][rs, :] for bi in range(len(DILATED))]
        ls = [stats[3 * bi + 1][rs, :] for bi in range(len(DILATED))]
        nums = [stats[3 * bi + 2][rs, :] for bi in range(len(DILATED))]
        mx = jnp.maximum(jnp.maximum(ms[0], ms[1]), ms[2])
        ws = [jnp.exp(m - mx) for m in ms]
        den = ws[0] * ls[0] + ws[1] * ls[1] + ws[2] * ls[2]
        o = (ws[0] * nums[0] + ws[1] * nums[1] + ws[2] * nums[2]) / den
        sq = o * o
        ss_lo = jnp.sum(jnp.where(low_q, sq, 0.0), axis=-1, keepdims=True)
        ss_hi = jnp.sum(jnp.where(low_q, 0.0, sq), axis=-1, keepdims=True)
        mean_sq = jnp.where(low_q, ss_lo, ss_hi) / B_HEAD_DIM
        o_ref[rs, :] = o * lax.rsqrt(mean_sq + EPS) * nwb
        return carry

    lax.fori_loop(0, sb // blk, combine, 0, unroll=4)

    @pl.when(n == pl.num_programs(2) - 1)
    def _():
        kt_ref[...] = kc_ref[...].T
        vt_ref[...] = vc_ref[...].T


def _attn_prompt(qkv, norm_b_pair):
    b, s, _ = qkv.shape
    sb = ATTN_SB
    assert s % sb == 0 and sb == min(MAX_WINDOW, s)
    assert all(w == STEPS * d and sb % w == 0 and d & (d - 1) == 0 for (w, d) in DILATED)
    n_pairs = B_HEADS // PAIR
    blk = (None, sb, 128)
    prev = lambda c0: pl.BlockSpec(blk, lambda i, p, n: (i, jnp.maximum(n - 1, 0), c0 + p))
    cur = lambda c0: pl.BlockSpec(blk, lambda i, p, n: (i, n, c0 + p))
    return pl.pallas_call(
        _attn_fused_kernel,
        grid=(b, n_pairs, s // sb),
        in_specs=[cur(0), prev(n_pairs), cur(n_pairs), prev(2 * n_pairs), cur(2 * n_pairs),
                  pl.BlockSpec((1, 128), lambda i, p, n: (0, 0))],
        out_specs=[pl.BlockSpec(blk, lambda i, p, n: (i, n, p)),
                   pl.BlockSpec((None, 128, sb), lambda i, p, n: (i, p, 0)),
                   pl.BlockSpec((None, 128, sb), lambda i, p, n: (i, p, 0))],
        out_shape=[jax.ShapeDtypeStruct((b, s, B_WIDTH), F32),
                   jax.ShapeDtypeStruct((b, B_WIDTH, sb), F32),
                   jax.ShapeDtypeStruct((b, B_WIDTH, sb), F32)],
        scratch_shapes=[pltpu.VMEM((sb, 128), F32)] * (3 * len(DILATED)),
        compiler_params=pltpu.CompilerParams(
            dimension_semantics=("arbitrary", "arbitrary", "arbitrary"), vmem_limit_bytes=VMEM_LIMIT),
        name="attn_prompt",
    )(qkv, qkv, qkv, qkv, qkv, norm_b_pair)


def _attn_sample_kernel(q_ref, kn_ref, vn_ref, kt_ref, vt_ref, nwb_ref, o_ref, kto_ref, vto_ref, *, n_past, t):
    kn = kn_ref[...]
    vn = vn_ref[...]
    kt = kt_ref[...]
    vt = vt_ref[...]

    prow = lax.broadcasted_iota(jnp.int32, (t, 128), 0)
    plane = lax.broadcasted_iota(jnp.int32, (t, 128), 1)
    place = jnp.where(plane == 128 - t + prow, 1.0, 0.0).astype(BF16)
    tail_lanes = lax.broadcasted_iota(jnp.int32, (B_WIDTH, 128), 1) >= 128 - t
    tn = functools.partial(lax.dot_general, dimension_numbers=(((0,), (0,)), ((), ())),
                           preferred_element_type=F32)

    def shifted(win, new, out_ref):
        hi, lo = _split2(new)
        lo2 = (new - hi.astype(F32) - lo.astype(F32)).astype(BF16)
        new_t = tn(hi, place) + tn(lo, place) + tn(lo2, place)
        sh = pltpu.roll(win, n_past - t, axis=1)
        out_ref[:, :n_past - 128] = sh[:, :n_past - 128]
        out_ref[:, n_past - 128:] = jnp.where(tail_lanes, new_t, sh[:, n_past - 128:])

    shifted(kt, kn, kto_ref)
    shifted(vt, vn, vto_ref)

    rq = B_HEADS * t
    t_bits = t.bit_length() - 1
    d_bits = B_HEAD_DIM.bit_length() - 1
    rowi = lax.broadcasted_iota(jnp.int32, (rq, B_WIDTH), 0)
    lanei = lax.broadcasted_iota(jnp.int32, (rq, B_WIDTH), 1)
    own = lax.shift_right_logical(rowi, t_bits) == lax.shift_right_logical(lanei, d_bits)
    q_rep = jnp.concatenate([q_ref[...]] * B_HEADS, axis=0)
    q_blk = jnp.where(own, q_rep, 0.0).astype(BF16)
    zpad = jnp.zeros((128 - t, B_WIDTH), BF16)
    kn_pad = jnp.concatenate([kn.astype(BF16), zpad], axis=0)
    vn_pad = jnp.concatenate([vn.astype(BF16), zpad], axis=0)
    vt16 = vt.astype(BF16)
    scale = B_HEAD_DIM ** -0.5
    s_c = jnp.dot(q_blk, kt.astype(BF16), preferred_element_type=F32) * scale
    s_n = lax.dot_general(q_blk, kn_pad, (((1,), (1,)), ((), ())), preferred_element_type=F32) * scale

    r1 = lax.broadcasted_iota(jnp.int32, (rq, 1), 0)
    head = lax.shift_right_logical(r1, t_bits)
    slope = jnp.zeros((rq, 1), F32)
    for h in range(B_HEADS):
        slope = jnp.where(head == h, _slope(h), slope)
    tok = r1 & (t - 1)
    delta_c = n_past + tok - lax.broadcasted_iota(jnp.int32, (rq, n_past), 1)
    key_n = lax.broadcasted_iota(jnp.int32, (rq, 128), 1)
    delta_n = tok - key_n
    s_c = s_c - slope * delta_c.astype(F32)
    s_n = s_n - slope * delta_n.astype(F32)
    live_n = (key_n < t) & (delta_n >= 0)

    ms, ls, pcs, pns = [], [], [], []
    for (_, dil) in DILATED:
        valid_c = ((delta_c & (dil - 1)) == 0) & (delta_c <= STEPS * dil)
        valid_n = live_n & ((delta_n & (dil - 1)) == 0) & (delta_n <= STEPS * dil)
        sd_c = jnp.where(valid_c, s_c, NEG)
        sd_n = jnp.where(valid_n, s_n, NEG)
        m = jnp.maximum(jnp.max(sd_c, axis=-1, keepdims=True), jnp.max(sd_n, axis=-1, keepdims=True))
        p_c = jnp.exp(sd_c - m)
        p_n = jnp.exp(sd_n - m)
        ms.append(m)
        ls.append(jnp.sum(p_c, axis=-1, keepdims=True) + jnp.sum(p_n, axis=-1, keepdims=True))
        pcs.append(p_c.astype(BF16))
        pns.append(p_n.astype(BF16))
    num_all = (lax.dot_general(jnp.concatenate(pcs, axis=0), vt16, (((1,), (1,)), ((), ())),
                               preferred_element_type=F32)
               + jnp.dot(jnp.concatenate(pns, axis=0), vn_pad, preferred_element_type=F32))
    nums = [num_all[i * rq:(i + 1) * rq] for i in range(len(DILATED))]
    mx = jnp.maximum(jnp.maximum(ms[0], ms[1]), ms[2])
    ws = [jnp.exp(m - mx) for m in ms]
    den = ws[0] * ls[0] + ws[1] * ls[1] + ws[2] * ls[2]
    o = (ws[0] * nums[0] + ws[1] * nums[1] + ws[2] * nums[2]) / den
    o = jnp.where(own, o, 0.0)
    o = o * lax.rsqrt(jnp.sum(o * o, axis=-1, keepdims=True) / B_HEAD_DIM + EPS) * nwb_ref[...]
    acc = o[0:t, :]
    for h in range(1, B_HEADS):
        acc = acc + o[h * t:(h + 1) * t, :]
    o_ref[...] = acc


def _attn_sample_parts(qkv, win_kt, win_vt, norm_b_tiled, seq_of_step):
    b, t, _ = qkv.shape
    n_past = win_kt.shape[2]
    assert n_past == MAX_WINDOW and n_past % 128 == 0 and t % 8 == 0 and t & (t - 1) == 0 and t <= 128
    assert all(d & (d - 1) == 0 for (_, d) in DILATED)
    col = lambda c: pl.BlockSpec((None, t, B_WIDTH), lambda *g: (seq_of_step(*g), 0, c))
    win = pl.BlockSpec((None, B_WIDTH, n_past), lambda *g: (seq_of_step(*g), 0, 0))
    return dict(
        kernel=functools.partial(_attn_sample_kernel, n_past=n_past, t=t),
        inputs=(qkv, qkv, qkv, win_kt, win_vt, norm_b_tiled),
        in_specs=[col(0), col(1), col(2), win, win, pl.BlockSpec((1, B_WIDTH), lambda *g: (0, 0))],
        out_specs=[col(0), win, win],
        out_shape=[jax.ShapeDtypeStruct((b, t, B_WIDTH), F32),
                   jax.ShapeDtypeStruct((b, B_WIDTH, n_past), F32),
                   jax.ShapeDtypeStruct((b, B_WIDTH, n_past), F32)])


def _attn_sample(qkv, win_kt, win_vt, norm_b_tiled):
    parts = _attn_sample_parts(qkv, win_kt, win_vt, norm_b_tiled, lambda i: i)
    return pl.pallas_call(
        parts["kernel"],
        grid=(qkv.shape[0],),
        in_specs=parts["in_specs"],
        out_specs=parts["out_specs"],
        out_shape=parts["out_shape"],
        compiler_params=pltpu.CompilerParams(dimension_semantics=("arbitrary",),
                                             vmem_limit_bytes=VMEM_LIMIT),
        name="attn_sample",
    )(*parts["inputs"])


def _out_ffn_kernel(x_ref, oa_ref, ob_ref, wo_ref, nf_ref, wg_ref, wu_ref, wd_ref, nfin_ref, y_ref, *, ff_chunk):
    mixed = jnp.concatenate([oa_ref[...], ob_ref[...]], axis=-1).astype(BF16)
    x1 = x_ref[...] + jnp.dot(mixed, wo_ref[...], preferred_element_type=F32)
    hf = _rms(x1, nf_ref[...]).astype(BF16)
    x2 = x1
    for c0 in range(0, wg_ref.shape[1], ff_chunk):
        g = jnp.dot(hf, wg_ref[:, c0:c0 + ff_chunk], preferred_element_type=F32)
        u = jnp.dot(hf, wu_ref[:, c0:c0 + ff_chunk], preferred_element_type=F32)
        act = (_silu(g) * u).astype(BF16)
        x2 = x2 + jnp.dot(act, wd_ref[c0:c0 + ff_chunk, :], preferred_element_type=F32)
    y_ref[...] = _rms(x2, nfin_ref[...])


def _out_ffn(x2d, o_a, o_b, w_out, norm_ffn, w_gate, w_up, w_down, norm_final, tm):
    t = x2d.shape[0]
    d_ff = w_gate.shape[1]
    ff_chunk = d_ff
    once = pl.Buffered(1)
    row = lambda w: pl.BlockSpec((tm, w), lambda i: (i, 0))
    full = lambda a, b: pl.BlockSpec((a, b), lambda i: (0, 0), pipeline_mode=once)
    return pl.pallas_call(
        functools.partial(_out_ffn_kernel, ff_chunk=ff_chunk),
        grid=(t // tm,),
        in_specs=[row(D_MODEL), row(A_WIDTH), row(B_WIDTH),
                  full(D_MODEL, D_MODEL), full(1, D_MODEL),
                  full(D_MODEL, d_ff), full(D_MODEL, d_ff), full(d_ff, D_MODEL), full(1, D_MODEL)],
        out_specs=row(D_MODEL),
        out_shape=jax.ShapeDtypeStruct((t, D_MODEL), F32),
        compiler_params=pltpu.CompilerParams(dimension_semantics=("arbitrary",),
                                             vmem_limit_bytes=VMEM_LIMIT),
        name="out_ffn",
    )(x2d, o_a, o_b, w_out, norm_ffn, w_gate, w_up, w_down, norm_final)


def _split_w_in_kernel(w_ref, wa_ref, wg_ref, wb_ref):
    n_gate = 2 * A_HEADS
    wa_ref[...] = w_ref[0:COL_G, :].astype(BF16)
    gates = jnp.concatenate([w_ref[COL_G:COL_G + n_gate, :],
                             jnp.zeros((GATE_PAD - n_gate, w_ref.shape[1]), F32)], axis=0)
    wg_ref[...] = gates.astype(BF16)
    wb_ref[...] = w_ref[COL_G + n_gate:, :].astype(BF16)


def _split_w_in(wt):
    cols, d = wt.shape
    tl = 256
    assert d % tl == 0 and cols == COL_G + 2 * A_HEADS + COLS_B and (2 * A_HEADS) % 8 == 0
    return pl.pallas_call(
        _split_w_in_kernel,
        grid=(d // tl,),
        in_specs=[pl.BlockSpec((cols, tl), lambda j: (0, j))],
        out_specs=[pl.BlockSpec((c, tl), lambda j: (0, j)) for c in (COL_G, GATE_PAD, COLS_B)],
        out_shape=[jax.ShapeDtypeStruct((c, d), BF16) for c in (COL_G, GATE_PAD, COLS_B)],
        compiler_params=pltpu.CompilerParams(dimension_semantics=("arbitrary",),
                                             vmem_limit_bytes=VMEM_LIMIT),
        name="split_w_in",
    )(wt)


def _layer_params(norm_mix, w_in, w_conv, a_log, dt_bias, norm_out_a, norm_out_b, w_out, norm_ffn,
                  w_gate, w_up, w_down, layer):
    w_p = tuple(_split_w_in(jnp.transpose(w_in[layer])))
    gate_par = jnp.zeros((8, GATE_PAD), F32)
    gate_par = gate_par.at[0, A_HEADS:2 * A_HEADS].set(a_log[layer].astype(F32))
    gate_par = gate_par.at[1, A_HEADS:2 * A_HEADS].set(dt_bias[layer].astype(F32))
    return dict(
        norm_mix=norm_mix[layer].reshape(1, D_MODEL), w_p=w_p, w_conv=w_conv[layer], gate_par=gate_par,
        norm_a=norm_out_a[layer].reshape(1, A_DV),
        norm_b_pair=jnp.tile(norm_out_b[layer], PAIR).reshape(1, PAIR * B_HEAD_DIM),
        norm_b_tiled=jnp.tile(norm_out_b[layer], B_HEADS).reshape(1, B_WIDTH),
        w_out=w_out[layer].astype(BF16), norm_ffn=norm_ffn[layer].reshape(1, D_MODEL),
        w_gate=w_gate[layer].astype(BF16), w_up=w_up[layer].astype(BF16), w_down=w_down[layer].astype(BF16))


def _project(x, conv_buf, p, tm):
    b, t, _ = x.shape
    assert t >= CONV_W - 1
    conv0 = jnp.pad(conv_buf, ((0, 0), (8 - (CONV_W - 1), 0), (0, 0)))
    conv_done = t % tm == 0
    if conv_done:
        proj_a, proj_b, tail = _norm_proj_conv(x, p["norm_mix"], p["w_p"], p["w_conv"], conv0, tm)
        new_conv = tail[:, 8 - (CONV_W - 1):]
    else:
        proj_a, proj_b = _norm_proj(x.reshape(b * t, D_MODEL), p["norm_mix"], p["w_p"], tm)
        proj_a = proj_a.reshape(b, t, COLS_A)
        proj_b = proj_b.reshape(b, t, COLS_B)
        new_conv = proj_a[:, t - (CONV_W - 1):, :CONV_DIM]
    return proj_a, proj_b, new_conv, conv0, conv_done


def _finish(x, o_a, o_b, p, norm_final, tm):
    b, t, _ = x.shape
    y = _out_ffn(x.reshape(b * t, D_MODEL), o_a.reshape(b * t, A_WIDTH), o_b.reshape(b * t, B_WIDTH), p["w_out"],
                 p["norm_ffn"], p["w_gate"], p["w_up"], p["w_down"], norm_final.reshape(1, D_MODEL), tm)
    return y.reshape(b, t, D_MODEL)


def kernel(x_prompt, x_sample, state_conv, state_rec, cache_win_k, cache_win_v, norm_mix, w_in, w_conv, a_log,
           dt_bias, norm_out_a, norm_out_b, w_out, norm_ffn, w_gate, w_up, w_down, norm_final):
    depth = w_in.shape[0]
    assert depth == 1, "the final norm is fused into the block of a single-layer trunk"
    p = _layer_params(norm_mix, w_in, w_conv, a_log, dt_bias, norm_out_a, norm_out_b, w_out, norm_ffn,
                      w_gate, w_up, w_down, 0)
    bp, tp, _ = x_prompt.shape
    bs, ts, _ = x_sample.shape
    tm_p, tm_s = 512, bs * ts
    gdn_args = (p["w_conv"], p["gate_par"], p["norm_a"])

    sa, sb, sc, s_conv0, s_done = _project(x_sample, state_conv[0], p, tm_s)
    n_past = cache_win_k.shape[2]
    to_t = lambda a: jnp.transpose(a, (0, 2, 3, 1)).reshape(bs, B_WIDTH, n_past)
    from_t = lambda a: jnp.transpose(a.reshape(a.shape[0], B_HEADS, B_HEAD_DIM, a.shape[2]), (0, 3, 1, 2))
    win_kt, win_vt = to_t(cache_win_k[0]), to_t(cache_win_v[0])

    def sample_attention_of_grid(grid):
        if grid[0] * grid[1] != bs:
            return None
        return _attn_sample_parts(sb, win_kt, win_vt, p["norm_b_tiled"], lambda i, j: i * grid[1] + j)

    pa, pb, pc, p_conv0, p_done = _project(x_prompt, jnp.zeros((bp, CONV_W - 1, CONV_DIM), F32), p, tm_p)
    zero_rec = jnp.zeros((bp, A_HEADS, A_DK, A_DV), F32)
    (o_a_p, pr), side = _gdn_pair(pa, p_conv0, zero_rec, *gdn_args, p_done, sample_attention_of_grid)
    o_b_s, new_kt, new_vt = side if side else _attn_sample(sb, win_kt, win_vt, p["norm_b_tiled"])
    (o_a_s, sr), _ = _gdn_pair(sa, s_conv0, state_rec[0], *gdn_args, s_done)

    o_b_p, pkt, pvt = _attn_prompt(pb, p["norm_b_pair"])

    yp = _finish(x_prompt, o_a_p, o_b_p, p, norm_final, tm_p)
    ys = _finish(x_sample, o_a_s, o_b_s, p, norm_final, tm_s)
    return (yp, ys, pc[None], pr[None], from_t(pkt)[None], from_t(pvt)[None], sc[None], sr[None],
            from_t(new_kt)[None], from_t(new_vt)[None])
```

```python
import functools

import jax
import jax.numpy as jnp
from jax import lax
from jax.experimental import pallas as pl
from jax.experimental.pallas import tpu as pltpu

F32 = jnp.float32
BF16 = jnp.bfloat16

D_MODEL = 1024
A_HEADS = 4
A_DK = 128
A_DV = 128
A_WIDTH = A_HEADS * A_DV
CONV_W = 4
CONV_DIM = 2 * A_HEADS * A_DK + A_HEADS * A_DV
B_HEADS = 8
B_HEAD_DIM = 64
B_WIDTH = B_HEADS * B_HEAD_DIM
DILATED = ((128, 1), (512, 4), (2048, 16))
STEPS = 128
MAX_WINDOW = 2048
EPS = 1e-6
NEG = -1e30
CHUNK = 64
GATE_PAD = 128
COL_Z = CONV_DIM
COL_G = CONV_DIM + A_WIDTH
COLS_A = COL_G + GATE_PAD
COLS_B = 3 * B_WIDTH
VMEM_LIMIT = 56 * 1024 * 1024


def _dot(a, b):
    return jnp.dot(a.astype(BF16), b.astype(BF16), preferred_element_type=F32)


def _dot_nt(a, b):
    return lax.dot_general(a.astype(BF16), b.astype(BF16), (((1,), (1,)), ((), ())),
                           preferred_element_type=F32)


def _dot_tn(a, b):
    return lax.dot_general(a.astype(BF16), b.astype(BF16), (((0,), (0,)), ((), ())),
                           preferred_element_type=F32)


def _split2(a):
    hi = a.astype(BF16)
    lo = (a - hi.astype(F32)).astype(BF16)
    return hi, lo


def _sigmoid(x):
    return 1.0 / (1.0 + jnp.exp(-x))


def _silu(x):
    return x * _sigmoid(x)


def _softplus(x):
    return jnp.maximum(x, 0.0) + jnp.log(1.0 + jnp.exp(-jnp.abs(x)))


def _rms(x, w):
    return x * lax.rsqrt(jnp.mean(x * x, axis=-1, keepdims=True) + EPS) * w


def _proj(h, wt):
    return lax.dot_general(h, wt, (((1,), (1,)), ((), ())), preferred_element_type=F32)


def _norm_proj_kernel(x_ref, nw_ref, wa_ref, wg_ref, wb_ref, oa_ref, ob_ref):
    h = _rms(x_ref[...], nw_ref[...]).astype(BF16)
    oa_ref[:, :COL_G] = _proj(h, wa_ref[...])
    oa_ref[:, COL_G:] = _proj(h, wg_ref[...])
    ob_ref[...] = _proj(h, wb_ref[...])


def _weight_specs():
    zero = lambda *g: (0, 0)
    return [pl.BlockSpec((c, D_MODEL), zero, pipeline_mode=pl.Buffered(1)) for c in (COL_G, GATE_PAD, COLS_B)]


def _norm_proj(x2d, norm_w, w_p, tm):
    t = x2d.shape[0]
    once = pl.Buffered(1)
    return pl.pallas_call(
        _norm_proj_kernel,
        grid=(t // tm,),
        in_specs=[pl.BlockSpec((tm, D_MODEL), lambda i: (i, 0)),
                  pl.BlockSpec((1, D_MODEL), lambda i: (0, 0), pipeline_mode=once)] + _weight_specs(),
        out_specs=[pl.BlockSpec((tm, COLS_A), lambda i: (i, 0)),
                   pl.BlockSpec((tm, COLS_B), lambda i: (i, 0))],
        out_shape=[jax.ShapeDtypeStruct((t, COLS_A), F32),
                   jax.ShapeDtypeStruct((t, COLS_B), F32)],
        compiler_params=pltpu.CompilerParams(dimension_semantics=("arbitrary",),
                                             vmem_limit_bytes=VMEM_LIMIT),
        name="norm_proj",
    )(x2d, norm_w, *w_p)


def _causal_conv_silu(x_all, wconv_ref, n):
    conv = x_all[8:8 + n] * wconv_ref[CONV_W - 1:CONV_W, :]
    for back in range(1, CONV_W):
        tap = CONV_W - 1 - back
        conv = conv + pltpu.roll(x_all, back, axis=0)[8:8 + n] * wconv_ref[tap:tap + 1, :]
    return _silu(conv)


def _l2norm(x):
    return x * lax.rsqrt(jnp.sum(x * x, axis=-1, keepdims=True) + EPS)


def _norm_proj_conv_kernel(x_ref, nw_ref, wa_ref, wg_ref, wb_ref, wconv_ref, conv0_ref, oa_ref, ob_ref, tail_ref,
                           carry_ref, h_ref, raw_ref):
    i = pl.program_id(1)
    tm = x_ref.shape[0]
    wq = A_HEADS * A_DK
    assert CONV_DIM == 3 * wq and COLS_B == 3 * B_WIDTH
    dd = functools.partial(jnp.dot, preferred_element_type=F32)

    @pl.when(i == 0)
    def _():
        carry_ref[...] = conv0_ref[...]

    def conv_cols(g):
        cs = slice(g * wq, (g + 1) * wq)
        conv = _causal_conv_silu(jnp.concatenate([carry_ref[:, cs], raw_ref[:, cs]], axis=0),
                                 wconv_ref.at[:, cs], tm)
        carry_ref[:, cs] = raw_ref[tm - 8:tm, cs]
        if g == 2:
            oa_ref[:, cs] = conv
            return
        for hd in range(A_HEADS):
            hs = slice(hd * A_DK, (hd + 1) * A_DK)
            out = _l2norm(conv[:, hs])
            oa_ref[:, g * wq + hd * A_DK:g * wq + (hd + 1) * A_DK] = out * (A_DK ** -0.5) if g == 0 else out

    def phase(k):
        if k == 0:
            h_ref[...] = _rms(x_ref[...], nw_ref[...]).astype(BF16)
        h = h_ref[...]
        if k < 3:
            raw_ref[:, k * wq:(k + 1) * wq] = _proj(h, wa_ref[k * wq:(k + 1) * wq, :])
        if k == 0:
            oa_ref[:, CONV_DIM:COL_G] = _proj(h, wa_ref[CONV_DIM:, :])
        else:
            ob_ref[:, (k - 1) * B_WIDTH:k * B_WIDTH] = _proj(h, wb_ref[(k - 1) * B_WIDTH:k * B_WIDTH, :])
            conv_cols(k - 1)
        if k == 3:
            oa_ref[:, COL_G:] = _proj(h, wg_ref[...])

    for k in range(4):
        pl.when(i >= 0)(functools.partial(phase, k))

    @pl.when(i == pl.num_programs(1) - 1)
    def _():
        tail_ref[...] = raw_ref[tm - 8:tm, :]


def _norm_proj_conv(x, norm_w, w_p, w_conv, conv0, tm, side_of_grid=None):
    b, t, _ = x.shape
    assert t % tm == 0 and tm % 8 == 0
    once = pl.Buffered(1)
    grid = (b, t // tm)
    in_specs = [pl.BlockSpec((None, tm, D_MODEL), lambda i, j: (i, j, 0)),
                pl.BlockSpec((1, D_MODEL), lambda i, j: (0, 0), pipeline_mode=once)] + _weight_specs() + [
                pl.BlockSpec((CONV_W, CONV_DIM), lambda i, j: (0, 0), pipeline_mode=once),
                pl.BlockSpec((None, 8, CONV_DIM), lambda i, j: (i, 0, 0))]
    out_specs = [pl.BlockSpec((None, tm, COLS_A), lambda i, j: (i, j, 0)),
                 pl.BlockSpec((None, tm, COLS_B), lambda i, j: (i, j, 0)),
                 pl.BlockSpec((None, 8, CONV_DIM), lambda i, j: (i, 0, 0))]
    out_shape = [jax.ShapeDtypeStruct((b, t, COLS_A), F32),
                 jax.ShapeDtypeStruct((b, t, COLS_B), F32),
                 jax.ShapeDtypeStruct((b, 8, CONV_DIM), F32)]
    inputs = (x, norm_w, *w_p, w_conv, conv0)
    n_in, n_out, n_scratch = len(in_specs), len(out_specs), 3
    side = side_of_grid(grid) if side_of_grid is not None else None

    def kernel(*refs):
        n_side_in = len(side["in_specs"]) if side else 0
        if side:
            side["kernel"](*refs[n_in:n_in + n_side_in], *refs[n_in + n_side_in + n_out:len(refs) - n_scratch])
        _norm_proj_conv_kernel(*refs[:n_in], *refs[n_in + n_side_in:n_in + n_side_in + n_out],
                               *refs[len(refs) - n_scratch:])

    if side:
        inputs = inputs + tuple(side["inputs"])
        in_specs = in_specs + list(side["in_specs"])
        out_specs = out_specs + list(side["out_specs"])
        out_shape = out_shape + list(side["out_shape"])
    outs = pl.pallas_call(
        kernel,
        grid=grid,
        in_specs=in_specs,
        out_specs=out_specs,
        out_shape=out_shape,
        scratch_shapes=[pltpu.VMEM((8, CONV_DIM), F32), pltpu.VMEM((tm, D_MODEL), BF16),
                        pltpu.VMEM((tm, CONV_DIM), F32)],
        compiler_params=pltpu.CompilerParams(dimension_semantics=("arbitrary", "arbitrary"),
                                             vmem_limit_bytes=VMEM_LIMIT),
        name="norm_proj_conv",
    )(*inputs)
    return outs[:n_out], outs[n_out:]


def _cast_weights_parts(weights, grid):
    n_steps = grid[0] * grid[1]
    if n_steps % 2:
        return None
    per = n_steps // 2
    groups = (weights[:len(weights) // 2], weights[len(weights) // 2:])
    if any(w.shape[0] % (per * 16) for g in groups for w in g):
        return None
    step = lambda i, j: i * grid[1] + j
    specs, first = [], []
    for gi, g in enumerate(groups):
        for w in g:
            rows = w.shape[0] // per
            if gi == 0:
                specs.append(pl.BlockSpec((rows, w.shape[1]), lambda i, j: (jnp.minimum(step(i, j), per - 1), 0)))
            else:
                specs.append(pl.BlockSpec((rows, w.shape[1]), lambda i, j: (jnp.maximum(step(i, j) - per, 0), 0)))
            first.append(gi == 0)

    def kernel(*refs):
        n = len(first)
        s = pl.program_id(0) * grid[1] + pl.program_id(1)
        for src, dst, is_first in zip(refs[:n], refs[n:], first):
            def cast(src=src, dst=dst):
                dst[...] = src[...].astype(BF16)
            pl.when(s < per if is_first else s >= per)(cast)

    return dict(kernel=kernel, inputs=tuple(weights), in_specs=specs, out_specs=specs,
                out_shape=[jax.ShapeDtypeStruct(w.shape, BF16) for w in weights])


GDN_ROWS = 8 * CHUNK


HALF = 128 // CHUNK


def _gdn_pair_kernel(xa_ref, z_ref, gt_ref, conv0_ref, s0_ref, wconv_ref, gp_ref, nwa_ref,
                     o_ref, snew_ref, ext_ref, s_ref, *, rows, nb, conv_done, side_work=None):
    n = pl.program_id(1)
    c = CHUNK
    n_chunks = -(-rows // c)
    rp = n_chunks * c
    assert A_HEADS % 2 == 0 and HALF == 2 and A_DK == 128 and A_DV == 128

    @pl.when(n == 0)
    def _():
        ext_ref[:, 0:8, :] = conv0_ref[...]
        s_ref[...] = s0_ref[...]

    if side_work is not None:
        side_work()

    row = lax.broadcasted_iota(jnp.int32, (rp, 1), 0)
    live = row < rows
    lane_g = lax.broadcasted_iota(jnp.int32, (rp, GATE_PAD), 1)
    gate_lanes = (lane_g >= A_HEADS) & (lane_g < 2 * A_HEADS) & live

    ii = lax.broadcasted_iota(jnp.int32, (c, 128), 0)
    ll = lax.broadcasted_iota(jnp.int32, (c, 128), 1)
    jj = ll & (c - 1)
    low = ll < c
    incl = ii >= jj
    strict = ii > jj
    eye = jnp.where(ii == jj, 1.0, 0.0).astype(F32)
    ti = lax.broadcasted_iota(jnp.int32, (c, c), 0)
    tj = lax.broadcasted_iota(jnp.int32, (c, c), 1)
    tril = jnp.where(ti >= tj, 1.0, 0.0).astype(BF16)
    dd = functools.partial(jnp.dot, preferred_element_type=F32)

    def exact_tril_dot(x):
        hi, lo = _split2(x)
        lo2 = (x - hi.astype(F32) - lo.astype(F32)).astype(BF16)
        return dd(tril, hi) + dd(tril, lo) + dd(tril, lo2)

    def bd(x):
        return jnp.concatenate([jnp.where(low, x, 0.0), jnp.where(low, 0.0, x)], axis=0).astype(BF16)

    def unstack(y):
        return jnp.where(low, y[:c], y[c:])

    nwa = nwa_ref[...]
    chains = []
    for bi in range(nb):
        if conv_done:
            assert rows == rp
            conv = xa_ref[bi]
        else:
            ext_ref[bi, 8:8 + rows, :] = xa_ref[bi]
            if rows < rp:
                ext_ref[bi, 8 + rows:8 + rp, :] = jnp.zeros((rp - rows, CONV_DIM), F32)
            conv = _causal_conv_silu(ext_ref[bi, 0:8 + rp, :], wconv_ref, rp)
            ext_ref[bi, 0:8, :] = ext_ref[bi, rp:rp + 8, :]
            if rows < rp:
                conv = jnp.where(live, conv, 0.0)

        gt = gt_ref[bi]
        if rows < rp:
            gt = jnp.concatenate([gt, jnp.zeros((rp - rows, GATE_PAD), F32)], axis=0)
        beta_all = jnp.where(live, _sigmoid(gt), 0.0)
        g_all = jnp.where(gate_lanes, -jnp.exp(gp_ref[0:1, :]) * _softplus(gt + gp_ref[1:2, :]), 0.0)

        for ci in range(n_chunks):
            rs = slice(ci * c, (ci + 1) * c)
            gc_all = exact_tril_dot(g_all[rs])
            for p in range(A_HEADS // 2):
                hd = []
                for h in (2 * p, 2 * p + 1):
                    q = conv[rs, h * A_DK:(h + 1) * A_DK]
                    k = conv[rs, A_HEADS * A_DK + h * A_DK:A_HEADS * A_DK + (h + 1) * A_DK]
                    v = conv[rs, 2 * A_HEADS * A_DK + h * A_DV:2 * A_HEADS * A_DK + (h + 1) * A_DV]
                    if not conv_done:
                        q = _l2norm(q) * (A_DK ** -0.5)
                        k = _l2norm(k)
                    beta = beta_all[rs, h:h + 1]
                    g = g_all[rs, A_HEADS + h:A_HEADS + h + 1]
                    gc = gc_all[:, A_HEADS + h:A_HEADS + h + 1]
                    gl = gc_all[c - 1:c, A_HEADS + h:A_HEADS + h + 1]
                    kb = k * beta
                    hd.append(dict(q=q, k=k, kb=kb, g=g, qg=q * jnp.exp(gc), kd=k * jnp.exp(gl - gc),
                                   dl=jnp.exp(gl),
                                   rhs=jnp.concatenate([v * beta, kb * jnp.exp(gc)], axis=1)))
                chains.append(dict(bi=bi, ci=ci, p=p, hd=hd))

    for ch in chains:
        a, b = ch["hd"]
        ch["e_mat"] = exact_tril_dot(jnp.where(strict, jnp.where(low, a["g"], b["g"]), 0.0))
    for ch in chains:
        a, b = ch["hd"]
        k_st = jnp.concatenate([a["k"], b["k"]], axis=0)
        ch["kk"] = _dot_nt(jnp.concatenate([a["kb"], b["kb"]], axis=0), k_st)
        ch["qk"] = _dot_nt(jnp.concatenate([a["q"], b["q"]], axis=0), k_st)
    for ch in chains:
        decay = jnp.where(incl, jnp.exp(jnp.where(incl, ch["e_mat"], 0.0)), 0.0)
        ch["lower"] = jnp.where(strict, unstack(ch["kk"]) * decay, 0.0)
        ch["qk"] = jnp.where(incl, unstack(ch["qk"]) * decay, 0.0)
    def joining(level):
        return strict & ((ii >> (level + 1)) == (jj >> (level + 1))) & ((ii >> level) != (jj >> level))

    for ch in chains:
        ch["t"] = eye - jnp.where(joining(0), ch["lower"], 0.0)
    for level in range(1, c.bit_length() - 1):
        join = joining(level)
        for ch in chains:
            ch["te"] = jnp.dot(ch["t"].astype(BF16), bd(jnp.where(join, ch["lower"], 0.0)),
                               preferred_element_type=F32)
        for ch in chains:
            ch["t"] = ch["t"] - jnp.dot(ch["te"].astype(BF16), bd(ch["t"]), preferred_element_type=F32)
    for ch in chains:
        lh, ll_ = _split2(ch["lower"])
        th = ch["t"].astype(BF16).astype(F32)
        tl = ch["t"] - th
        bth = bd(th)
        lt = dd(lh, bth) + dd(lh, bd(tl)) + dd(ll_, bth)
        ch["res"] = eye - ch["t"] - lt
    for ch in chains:
        ch["t"] = ch["t"] + jnp.dot(ch["t"].astype(BF16), bd(ch["res"]), preferred_element_type=F32)
    for ch in chains:
        a, b = ch["hd"]
        uw = jnp.dot(bd(ch["t"]), jnp.concatenate([a["rhs"], b["rhs"]], axis=0).astype(BF16),
                     preferred_element_type=F32)
        a["u"], a["w"] = uw[:c, :A_DV], uw[:c, A_DV:]
        b["u"], b["w"] = uw[c:, :A_DV], uw[c:, A_DV:]
        ch["qkb"] = bd(ch["qk"])

    by_key = {(ch["bi"], ch["ci"], ch["p"]): ch for ch in chains}
    states = {(bi, h): s_ref[bi, h] for bi in range(nb) for h in range(A_HEADS)}
    for ci in range(n_chunks):
        cur = [by_key[(bi, ci, p)] for bi in range(nb) for p in range(A_HEADS // 2)]
        for ch in cur:
            for hh, d in enumerate(ch["hd"]):
                ws = _dot(jnp.concatenate([d["w"], d["qg"]], axis=0), states[(ch["bi"], 2 * ch["p"] + hh)])
                d["e"] = d["u"] - ws[:c]
                d["o"] = ws[c:]
        for ch in cur:
            a, b = ch["hd"]
            o2 = jnp.dot(ch["qkb"], jnp.concatenate([a["e"], b["e"]], axis=0).astype(BF16),
                         preferred_element_type=F32)
            a["o"] = a["o"] + o2[:c]
            b["o"] = b["o"] + o2[c:]
        for ch in cur:
            for hh, d in enumerate(ch["hd"]):
                key = (ch["bi"], 2 * ch["p"] + hh)
                states[key] = states[key] * d["dl"] + _dot_tn(d["kd"], d["e"])
        r0 = ci * c
        r1 = min(rows, r0 + c)
        for ch in cur:
            for hh, d in enumerate(ch["hd"]):
                h = 2 * ch["p"] + hh
                o = d["o"] if r1 - r0 == c else d["o"][:r1 - r0]
                o_ref[ch["bi"], r0:r1, h * A_DV:(h + 1) * A_DV] = (
                    _rms(o, nwa) * _silu(z_ref[ch["bi"], r0:r1, h * A_DV:(h + 1) * A_DV]))
    for (bi, h), s in states.items():
        s_ref[bi, h] = s

    @pl.when(n == pl.num_programs(1) - 1)
    def _():
        snew_ref[...] = s_ref[...]


def _gdn_pair(proj, conv0, s0, w_conv, gate_par, norm_a, conv_done, side_of_grid=None):
    b, t, _ = proj.shape
    rows = min(GDN_ROWS, t)
    assert t % rows == 0 and (rows % CHUNK == 0 or rows == t)
    n = t // rows
    nb = max(1, min(b, GDN_ROWS // (-(-rows // CHUNK) * CHUNK)))
    assert b % nb == 0
    rp = 0 if conv_done else -(-rows // CHUNK) * CHUNK
    in_specs = [pl.BlockSpec((nb, rows, CONV_DIM), lambda i, j: (i, j, 0)),
                pl.BlockSpec((nb, rows, A_WIDTH), lambda i, j: (i, j, COL_Z // A_WIDTH)),
                pl.BlockSpec((nb, rows, GATE_PAD), lambda i, j: (i, j, COL_G // GATE_PAD)),
                pl.BlockSpec((nb, 8, CONV_DIM), lambda i, j: (i, 0, 0)),
                pl.BlockSpec((nb, A_HEADS, A_DK, A_DV), lambda i, j: (i, 0, 0, 0)),
                pl.BlockSpec((CONV_W, CONV_DIM), lambda i, j: (0, 0)),
                pl.BlockSpec((8, GATE_PAD), lambda i, j: (0, 0)),
                pl.BlockSpec((1, A_DV), lambda i, j: (0, 0))]
    out_specs = [pl.BlockSpec((nb, rows, A_WIDTH), lambda i, j: (i, j, 0)),
                 pl.BlockSpec((nb, A_HEADS, A_DK, A_DV), lambda i, j: (i, 0, 0, 0))]
    out_shape = [jax.ShapeDtypeStruct((b, t, A_WIDTH), F32),
                 jax.ShapeDtypeStruct((b, A_HEADS, A_DK, A_DV), F32)]
    inputs = (proj, proj, proj, conv0, s0, w_conv, gate_par, norm_a)
    n_in, n_out = len(in_specs), len(out_specs)
    side = side_of_grid((b // nb, n)) if side_of_grid is not None else None

    def kernel(*refs):
        n_side_in = len(side["in_specs"]) if side else 0
        gin, sin = refs[:n_in], refs[n_in:n_in + n_side_in]
        gout = refs[n_in + n_side_in:n_in + n_side_in + n_out]
        sout = refs[n_in + n_side_in + n_out:len(refs) - 2]
        side_work = (lambda: side["kernel"](*sin, *sout)) if side else None
        _gdn_pair_kernel(*gin, *gout, *refs[len(refs) - 2:], rows=rows, nb=nb, conv_done=conv_done,
                         side_work=side_work)

    if side:
        inputs = inputs + tuple(side["inputs"])
        in_specs = in_specs + list(side["in_specs"])
        out_specs = out_specs + list(side["out_specs"])
        out_shape = out_shape + list(side["out_shape"])
    outs = pl.pallas_call(
        kernel,
        grid=(b // nb, n),
        in_specs=in_specs,
        out_specs=out_specs,
        out_shape=out_shape,
        scratch_shapes=[pltpu.VMEM((nb, 8 + rp, CONV_DIM), F32),
                        pltpu.VMEM((nb, A_HEADS, A_DK, A_DV), F32)],
        compiler_params=pltpu.CompilerParams(dimension_semantics=("arbitrary", "arbitrary"),
                                             vmem_limit_bytes=VMEM_LIMIT),
        name="gdn_side" if side else "gdn",
    )(*inputs)
    return outs[:n_out], outs[n_out:]


def _slope(h):
    return 2.0 ** (-8.0 * (h + 1) / B_HEADS)


ATTN_SB = MAX_WINDOW
ATTN_SKEW = 2
PAIR = 128 // B_HEAD_DIM


def _attn_fused_kernel(q_ref, kp_ref, kc_ref, vp_ref, vc_ref, nwb_ref, o_ref, kt_ref, vt_ref, *stats):
    hp = pl.program_id(1)
    n = pl.program_id(2)
    sb = ATTN_SB
    blk = STEPS

    lane_q = lax.broadcasted_iota(jnp.int32, (blk, 128), 1)
    low_q = lane_q < B_HEAD_DIM
    qq = lax.broadcasted_iota(jnp.int32, (2 * blk, 2 * blk), 0)
    kk = lax.broadcasted_iota(jnp.int32, (2 * blk, 2 * blk), 1)
    steps_back = (qq & (blk - 1)) + blk - kk
    in_band = (steps_back >= 0) & (steps_back <= STEPS)
    slope_lo = jnp.float32(_slope(0))
    slope_hi = jnp.float32(_slope(1))
    for i in range(1, B_HEADS // PAIR):
        slope_lo = jnp.where(hp == i, _slope(PAIR * i), slope_lo)
        slope_hi = jnp.where(hp == i, _slope(PAIR * i + 1), slope_hi)
    slope = jnp.where(qq < blk, slope_lo, slope_hi)

    biases = []
    for (_, dil) in DILATED:
        bias = jnp.where(in_band, -slope * (steps_back * dil).astype(F32), NEG)
        biases.append((bias, jnp.where(kk >= blk, bias, NEG)))
    n_blk = sb // blk
    work = [(bi, it) for bi in range(len(DILATED)) for it in range(n_blk)]

    def rows(dil, start, size):
        return pl.ds(start, size) if dil == 1 else pl.ds(start, size, stride=dil)

    def key_tile(prev_ref, cur_ref, dil, k0):
        if k0 >= sb:
            return cur_ref[rows(dil, k0 - sb, 2 * blk), :].astype(BF16)
        assert k0 + blk * dil >= sb > k0 + (blk - 1) * dil
        return jnp.concatenate([prev_ref[rows(dil, k0, blk), :].astype(BF16),
                                cur_ref[rows(dil, k0 + blk * dil - sb, blk), :].astype(BF16)], axis=0)

    def issue_scores(bi, it):
        dil = DILATED[bi][1]
        q0 = (it // dil) * (blk * dil) + it % dil
        k0 = sb + q0 - blk * dil
        q = q_ref[rows(dil, q0, blk), :] * (B_HEAD_DIM ** -0.5)
        q2 = jnp.concatenate([jnp.where(low_q, q, 0.0), jnp.where(low_q, 0.0, q)], axis=0).astype(BF16)
        k = key_tile(kp_ref, kc_ref, dil, k0)
        return dict(bi=bi, dil=dil, q0=q0, k0=k0,
                    s=lax.dot_general(q2, k, (((1,), (1,)), ((), ())), preferred_element_type=F32))

    def softmax(w):
        bias, bias_first = biases[w["bi"]]
        s = w.pop("s") + (jnp.where(n == 0, bias_first, bias) if w["q0"] < blk * w["dil"] else bias)
        m = jnp.max(s, axis=-1, keepdims=True)
        p = jnp.exp(s - m)
        w.update(m=m, l=jnp.sum(p, axis=-1, keepdims=True), p=p.astype(BF16))

    def issue_values(w):
        dil, q0 = w["dil"], w["q0"]
        m_s, l_s, num_s = stats[3 * w["bi"]:3 * w["bi"] + 3]
        v = key_tile(vp_ref, vc_ref, dil, w["k0"])
        pv = jnp.dot(w["p"], v, preferred_element_type=F32)
        m_s[rows(dil, q0, blk), :] = jnp.where(low_q, w["m"][:blk], w["m"][blk:])
        l_s[rows(dil, q0, blk), :] = jnp.where(low_q, w["l"][:blk], w["l"][blk:])
        num_s[rows(dil, q0, blk), :] = jnp.where(low_q, pv[:blk], pv[blk:])

    inflight = {}
    for t in range(len(work) + 2 * ATTN_SKEW):
        if t < len(work):
            inflight[t] = issue_scores(*work[t])
        if 0 <= t - ATTN_SKEW < len(work):
            softmax(inflight[t - ATTN_SKEW])
        if 0 <= t - 2 * ATTN_SKEW < len(work):
            issue_values(inflight.pop(t - 2 * ATTN_SKEW))

    nwb = nwb_ref[...]

    def combine(i, carry):
        rs = pl.ds(pl.multiple_of(i * blk, blk), blk)
        ms = [stats[3 * bi][rs, :] for bi in range(len(DILATED))]
        ls = [stats[3 * bi + 1][rs, :] for bi in range(len(DILATED))]
        nums = [stats[3 * bi + 2][rs, :] for bi in range(len(DILATED))]
        mx = jnp.maximum(jnp.maximum(ms[0], ms[1]), ms[2])
        ws = [jnp.exp(m - mx) for m in ms]
        den = ws[0] * ls[0] + ws[1] * ls[1] + ws[2] * ls[2]
        o = (ws[0] * nums[0] + ws[1] * nums[1] + ws[2] * nums[2]) / den
        sq = o * o
        ss_lo = jnp.sum(jnp.where(low_q, sq, 0.0), axis=-1, keepdims=True)
        ss_hi = jnp.sum(jnp.where(low_q, 0.0, sq), axis=-1, keepdims=True)
        mean_sq = jnp.where(low_q, ss_lo, ss_hi) / B_HEAD_DIM
        o_ref[rs, :] = o * lax.rsqrt(mean_sq + EPS) * nwb
        return carry

    lax.fori_loop(0, sb // blk, combine, 0, unroll=4)

    @pl.when(n == pl.num_programs(2) - 1)
    def _():
        kt_ref[...] = kc_ref[...].T
        vt_ref[...] = vc_ref[...].T


def _attn_prompt(qkv, norm_b_pair):
    b, s, _ = qkv.shape
    sb = ATTN_SB
    assert s % sb == 0 and sb == min(MAX_WINDOW, s)
    assert all(w == STEPS * d and sb % w == 0 and d & (d - 1) == 0 for (w, d) in DILATED)
    n_pairs = B_HEADS // PAIR
    blk = (None, sb, 128)
    prev = lambda c0: pl.BlockSpec(blk, lambda i, p, n: (i, jnp.maximum(n - 1, 0), c0 + p))
    cur = lambda c0: pl.BlockSpec(blk, lambda i, p, n: (i, n, c0 + p))
    return pl.pallas_call(
        _attn_fused_kernel,
        grid=(b, n_pairs, s // sb),
        in_specs=[cur(0), prev(n_pairs), cur(n_pairs), prev(2 * n_pairs), cur(2 * n_pairs),
                  pl.BlockSpec((1, 128), lambda i, p, n: (0, 0))],
        out_specs=[pl.BlockSpec(blk, lambda i, p, n: (i, n, p)),
                   pl.BlockSpec((None, 128, sb), lambda i, p, n: (i, p, 0)),
                   pl.BlockSpec((None, 128, sb), lambda i, p, n: (i, p, 0))],
        out_shape=[jax.ShapeDtypeStruct((b, s, B_WIDTH), F32),
                   jax.ShapeDtypeStruct((b, B_WIDTH, sb), F32),
                   jax.ShapeDtypeStruct((b, B_WIDTH, sb), F32)],
        scratch_shapes=[pltpu.VMEM((sb, 128), F32)] * (3 * len(DILATED)),
        compiler_params=pltpu.CompilerParams(
            dimension_semantics=("arbitrary", "arbitrary", "arbitrary"), vmem_limit_bytes=VMEM_LIMIT),
        name="attn_prompt",
    )(qkv, qkv, qkv, qkv, qkv, norm_b_pair)


def _attn_sample_kernel(q_ref, kn_ref, vn_ref, kt_ref, vt_ref, nwb_ref, o_ref, kto_ref, vto_ref, *, n_past, t):
    kn = kn_ref[...]
    vn = vn_ref[...]
    kt = kt_ref[...]
    vt = vt_ref[...]

    prow = lax.broadcasted_iota(jnp.int32, (t, 128), 0)
    plane = lax.broadcasted_iota(jnp.int32, (t, 128), 1)
    place = jnp.where(plane == 128 - t + prow, 1.0, 0.0).astype(BF16)
    tail_lanes = lax.broadcasted_iota(jnp.int32, (B_WIDTH, 128), 1) >= 128 - t
    tn = functools.partial(lax.dot_general, dimension_numbers=(((0,), (0,)), ((), ())),
                           preferred_element_type=F32)

    def shifted(win, new, out_ref):
        hi, lo = _split2(new)
        lo2 = (new - hi.astype(F32) - lo.astype(F32)).astype(BF16)
        new_t = tn(hi, place) + tn(lo, place) + tn(lo2, place)
        sh = pltpu.roll(win, n_past - t, axis=1)
        out_ref[:, :n_past - 128] = sh[:, :n_past - 128]
        out_ref[:, n_past - 128:] = jnp.where(tail_lanes, new_t, sh[:, n_past - 128:])

    shifted(kt, kn, kto_ref)
    shifted(vt, vn, vto_ref)

    rq = B_HEADS * t
    t_bits = t.bit_length() - 1
    d_bits = B_HEAD_DIM.bit_length() - 1
    rowi = lax.broadcasted_iota(jnp.int32, (rq, B_WIDTH), 0)
    lanei = lax.broadcasted_iota(jnp.int32, (rq, B_WIDTH), 1)
    own = lax.shift_right_logical(rowi, t_bits) == lax.shift_right_logical(lanei, d_bits)
    q_rep = jnp.concatenate([q_ref[...]] * B_HEADS, axis=0)
    q_blk = jnp.where(own, q_rep, 0.0).astype(BF16)
    zpad = jnp.zeros((128 - t, B_WIDTH), BF16)
    kn_pad = jnp.concatenate([kn.astype(BF16), zpad], axis=0)
    vn_pad = jnp.concatenate([vn.astype(BF16), zpad], axis=0)
    vt16 = vt.astype(BF16)
    scale = B_HEAD_DIM ** -0.5
    s_c = jnp.dot(q_blk, kt.astype(BF16), preferred_element_type=F32) * scale
    s_n = lax.dot_general(q_blk, kn_pad, (((1,), (1,)), ((), ())), preferred_element_type=F32) * scale

    r1 = lax.broadcasted_iota(jnp.int32, (rq, 1), 0)
    head = lax.shift_right_logical(r1, t_bits)
    slope = jnp.zeros((rq, 1), F32)
    for h in range(B_HEADS):
        slope = jnp.where(head == h, _slope(h), slope)
    tok = r1 & (t - 1)
    delta_c = n_past + tok - lax.broadcasted_iota(jnp.int32, (rq, n_past), 1)
    key_n = lax.broadcasted_iota(jnp.int32, (rq, 128), 1)
    delta_n = tok - key_n
    s_c = s_c - slope * delta_c.astype(F32)
    s_n = s_n - slope * delta_n.astype(F32)
    live_n = (key_n < t) & (delta_n >= 0)

    ms, ls, pcs, pns = [], [], [], []
    for (_, dil) in DILATED:
        valid_c = ((delta_c & (dil - 1)) == 0) & (delta_c <= STEPS * dil)
        valid_n = live_n & ((delta_n & (dil - 1)) == 0) & (delta_n <= STEPS * dil)
        sd_c = jnp.where(valid_c, s_c, NEG)
        sd_n = jnp.where(valid_n, s_n, NEG)
        m = jnp.maximum(jnp.max(sd_c, axis=-1, keepdims=True), jnp.max(sd_n, axis=-1, keepdims=True))
        p_c = jnp.exp(sd_c - m)
        p_n = jnp.exp(sd_n - m)
        ms.append(m)
        ls.append(jnp.sum(p_c, axis=-1, keepdims=True) + jnp.sum(p_n, axis=-1, keepdims=True))
        pcs.append(p_c.astype(BF16))
        pns.append(p_n.astype(BF16))
    num_all = (lax.dot_general(jnp.concatenate(pcs, axis=0), vt16, (((1,), (1,)), ((), ())),
                               preferred_element_type=F32)
               + jnp.dot(jnp.concatenate(pns, axis=0), vn_pad, preferred_element_type=F32))
    nums = [num_all[i * rq:(i + 1) * rq] for i in range(len(DILATED))]
    mx = jnp.maximum(jnp.maximum(ms[0], ms[1]), ms[2])
    ws = [jnp.exp(m - mx) for m in ms]
    den = ws[0] * ls[0] + ws[1] * ls[1] + ws[2] * ls[2]
    o = (ws[0] * nums[0] + ws[1] * nums[1] + ws[2] * nums[2]) / den
    o = jnp.where(own, o, 0.0)
    o = o * lax.rsqrt(jnp.sum(o * o, axis=-1, keepdims=True) / B_HEAD_DIM + EPS) * nwb_ref[...]
    acc = o[0:t, :]
    for h in range(1, B_HEADS):
        acc = acc + o[h * t:(h + 1) * t, :]
    o_ref[...] = acc


def _attn_sample_parts(qkv, win_kt, win_vt, norm_b_tiled, seq_of_step):
    b, t, _ = qkv.shape
    n_past = win_kt.shape[2]
    assert n_past == MAX_WINDOW and n_past % 128 == 0 and t % 8 == 0 and t & (t - 1) == 0 and t <= 128
    assert all(d & (d - 1) == 0 for (_, d) in DILATED)
    col = lambda c: pl.BlockSpec((None, t, B_WIDTH), lambda *g: (seq_of_step(*g), 0, c))
    win = pl.BlockSpec((None, B_WIDTH, n_past), lambda *g: (seq_of_step(*g), 0, 0))
    return dict(
        kernel=functools.partial(_attn_sample_kernel, n_past=n_past, t=t),
        inputs=(qkv, qkv, qkv, win_kt, win_vt, norm_b_tiled),
        in_specs=[col(0), col(1), col(2), win, win, pl.BlockSpec((1, B_WIDTH), lambda *g: (0, 0))],
        out_specs=[col(0), win, win],
        out_shape=[jax.ShapeDtypeStruct((b, t, B_WIDTH), F32),
                   jax.ShapeDtypeStruct((b, B_WIDTH, n_past), F32),
                   jax.ShapeDtypeStruct((b, B_WIDTH, n_past), F32)])


def _attn_sample(qkv, win_kt, win_vt, norm_b_tiled):
    parts = _attn_sample_parts(qkv, win_kt, win_vt, norm_b_tiled, lambda i: i)
    return pl.pallas_call(
        parts["kernel"],
        grid=(qkv.shape[0],),
        in_specs=parts["in_specs"],
        out_specs=parts["out_specs"],
        out_shape=parts["out_shape"],
        compiler_params=pltpu.CompilerParams(dimension_semantics=("arbitrary",),
                                             vmem_limit_bytes=VMEM_LIMIT),
        name="attn_sample",
    )(*parts["inputs"])


def _out_ffn_kernel(x_ref, oa_ref, ob_ref, wo_ref, nf_ref, wg_ref, wu_ref, wd_ref, nfin_ref, y_ref, *, ff_chunk):
    mixed = jnp.concatenate([oa_ref[...], ob_ref[...]], axis=-1).astype(BF16)
    x1 = x_ref[...] + jnp.dot(mixed, wo_ref[...], preferred_element_type=F32)
    hf = _rms(x1, nf_ref[...]).astype(BF16)
    x2 = x1
    for c0 in range(0, wg_ref.shape[1], ff_chunk):
        g = jnp.dot(hf, wg_ref[:, c0:c0 + ff_chunk], preferred_element_type=F32)
        u = jnp.dot(hf, wu_ref[:, c0:c0 + ff_chunk], preferred_element_type=F32)
        act = (_silu(g) * u).astype(BF16)
        x2 = x2 + jnp.dot(act, wd_ref[c0:c0 + ff_chunk, :], preferred_element_type=F32)
    y_ref[...] = _rms(x2, nfin_ref[...])


def _out_ffn(x2d, o_a, o_b, w_out, norm_ffn, w_gate, w_up, w_down, norm_final, tm):
    t = x2d.shape[0]
    d_ff = w_gate.shape[1]
    ff_chunk = d_ff
    once = pl.Buffered(1)
    row = lambda w: pl.BlockSpec((tm, w), lambda i: (i, 0))
    full = lambda a, b: pl.BlockSpec((a, b), lambda i: (0, 0), pipeline_mode=once)
    return pl.pallas_call(
        functools.partial(_out_ffn_kernel, ff_chunk=ff_chunk),
        grid=(t // tm,),
        in_specs=[row(D_MODEL), row(A_WIDTH), row(B_WIDTH),
                  full(D_MODEL, D_MODEL), full(1, D_MODEL),
                  full(D_MODEL, d_ff), full(D_MODEL, d_ff), full(d_ff, D_MODEL), full(1, D_MODEL)],
        out_specs=row(D_MODEL),
        out_shape=jax.ShapeDtypeStruct((t, D_MODEL), F32),
        compiler_params=pltpu.CompilerParams(dimension_semantics=("arbitrary",),
                                             vmem_limit_bytes=VMEM_LIMIT),
        name="out_ffn",
    )(x2d, o_a, o_b, w_out, norm_ffn, w_gate, w_up, w_down, norm_final)


def _split_w_in_kernel(w_ref, wa_ref, wg_ref, wb_ref):
    n_gate = 2 * A_HEADS
    wa_ref[...] = w_ref[0:COL_G, :].astype(BF16)
    gates = jnp.concatenate([w_ref[COL_G:COL_G + n_gate, :],
                             jnp.zeros((GATE_PAD - n_gate, w_ref.shape[1]), F32)], axis=0)
    wg_ref[...] = gates.astype(BF16)
    wb_ref[...] = w_ref[COL_G + n_gate:, :].astype(BF16)


def _split_w_in(wt):
    cols, d = wt.shape
    tl = 256
    assert d % tl == 0 and cols == COL_G + 2 * A_HEADS + COLS_B and (2 * A_HEADS) % 8 == 0
    return pl.pallas_call(
        _split_w_in_kernel,
        grid=(d // tl,),
        in_specs=[pl.BlockSpec((cols, tl), lambda j: (0, j))],
        out_specs=[pl.BlockSpec((c, tl), lambda j: (0, j)) for c in (COL_G, GATE_PAD, COLS_B)],
        out_shape=[jax.ShapeDtypeStruct((c, d), BF16) for c in (COL_G, GATE_PAD, COLS_B)],
        compiler_params=pltpu.CompilerParams(dimension_semantics=("arbitrary",),
                                             vmem_limit_bytes=VMEM_LIMIT),
        name="split_w_in",
    )(wt)


def _layer_params(norm_mix, w_in, w_conv, a_log, dt_bias, norm_out_a, norm_out_b, w_out, norm_ffn,
                  w_gate, w_up, w_down, layer):
    w_p = tuple(_split_w_in(jnp.transpose(w_in[layer])))
    gate_par = jnp.zeros((8, GATE_PAD), F32)
    gate_par = gate_par.at[0, A_HEADS:2 * A_HEADS].set(a_log[layer].astype(F32))
    gate_par = gate_par.at[1, A_HEADS:2 * A_HEADS].set(dt_bias[layer].astype(F32))
    return dict(
        norm_mix=norm_mix[layer].reshape(1, D_MODEL), w_p=w_p, w_conv=w_conv[layer], gate_par=gate_par,
        norm_a=norm_out_a[layer].reshape(1, A_DV),
        norm_b_pair=jnp.tile(norm_out_b[layer], PAIR).reshape(1, PAIR * B_HEAD_DIM),
        norm_b_tiled=jnp.tile(norm_out_b[layer], B_HEADS).reshape(1, B_WIDTH),
        norm_ffn=norm_ffn[layer].reshape(1, D_MODEL),
        ffn_f32=(w_gate[layer], w_up[layer], w_down[layer], w_out[layer]))


def _project(x, conv_buf, p, tm, side_of_grid=None):
    b, t, _ = x.shape
    assert t >= CONV_W - 1
    conv0 = jnp.pad(conv_buf, ((0, 0), (8 - (CONV_W - 1), 0), (0, 0)))
    conv_done = t % tm == 0
    side = ()
    if conv_done:
        (proj_a, proj_b, tail), side = _norm_proj_conv(x, p["norm_mix"], p["w_p"], p["w_conv"], conv0, tm,
                                                       side_of_grid)
        new_conv = tail[:, 8 - (CONV_W - 1):]
    else:
        proj_a, proj_b = _norm_proj(x.reshape(b * t, D_MODEL), p["norm_mix"], p["w_p"], tm)
        proj_a = proj_a.reshape(b, t, COLS_A)
        proj_b = proj_b.reshape(b, t, COLS_B)
        new_conv = proj_a[:, t - (CONV_W - 1):, :CONV_DIM]
    return proj_a, proj_b, new_conv, conv0, conv_done, side


def _finish(x, o_a, o_b, p, ffn16, norm_final, tm):
    b, t, _ = x.shape
    w_gate, w_up, w_down, w_out = ffn16
    y = _out_ffn(x.reshape(b * t, D_MODEL), o_a.reshape(b * t, A_WIDTH), o_b.reshape(b * t, B_WIDTH), w_out,
                 p["norm_ffn"], w_gate, w_up, w_down, norm_final.reshape(1, D_MODEL), tm)
    return y.reshape(b, t, D_MODEL)


def kernel(x_prompt, x_sample, state_conv, state_rec, cache_win_k, cache_win_v, norm_mix, w_in, w_conv, a_log,
           dt_bias, norm_out_a, norm_out_b, w_out, norm_ffn, w_gate, w_up, w_down, norm_final):
    depth = w_in.shape[0]
    assert depth == 1, "the final norm is fused into the block of a single-layer trunk"
    p = _layer_params(norm_mix, w_in, w_conv, a_log, dt_bias, norm_out_a, norm_out_b, w_out, norm_ffn,
                      w_gate, w_up, w_down, 0)
    bp, tp, _ = x_prompt.shape
    bs, ts, _ = x_sample.shape
    tm_p, tm_s = 512, bs * ts
    gdn_args = (p["w_conv"], p["gate_par"], p["norm_a"])

    sa, sb, sc, s_conv0, s_done, _ = _project(x_sample, state_conv[0], p, tm_s)
    n_past = cache_win_k.shape[2]
    to_t = lambda a: jnp.transpose(a, (0, 2, 3, 1)).reshape(bs, B_WIDTH, n_past)
    from_t = lambda a: jnp.transpose(a.reshape(a.shape[0], B_HEADS, B_HEAD_DIM, a.shape[2]), (0, 3, 1, 2))
    win_kt, win_vt = to_t(cache_win_k[0]), to_t(cache_win_v[0])

    def sample_attention_of_grid(grid):
        if grid[0] * grid[1] != bs:
            return None
        return _attn_sample_parts(sb, win_kt, win_vt, p["norm_b_tiled"], lambda i, j: i * grid[1] + j)

    pa, pb, pc, p_conv0, p_done, ffn16 = _project(
        x_prompt, jnp.zeros((bp, CONV_W - 1, CONV_DIM), F32), p, tm_p,
        lambda grid: _cast_weights_parts(p["ffn_f32"], grid))
    if not ffn16:
        ffn16 = tuple(w.astype(BF16) for w in p["ffn_f32"])
    zero_rec = jnp.zeros((bp, A_HEADS, A_DK, A_DV), F32)
    (o_a_p, pr), side = _gdn_pair(pa, p_conv0, zero_rec, *gdn_args, p_done, sample_attention_of_grid)
    o_b_s, new_kt, new_vt = side if side else _attn_sample(sb, win_kt, win_vt, p["norm_b_tiled"])
    (o_a_s, sr), _ = _gdn_pair(sa, s_conv0, state_rec[0], *gdn_args, s_done)

    o_b_p, pkt, pvt = _attn_prompt(pb, p["norm_b_pair"])

    yp = _finish(x_prompt, o_a_p, o_b_p, p, ffn16, norm_final, tm_p)
    ys = _finish(x_sample, o_a_s, o_b_s, p, ffn16, norm_final, tm_s)
    return (yp, ys, pc[None], pr[None], from_t(pkt)[None], from_t(pvt)[None], sc[None], sr[None],
            from_t(new_kt)[None], from_t(new_vt)[None])
```

```python
import functools

import jax
import jax.numpy as jnp
from jax import lax
from jax.experimental import pallas as pl
from jax.experimental.pallas import tpu as pltpu

F32 = jnp.float32
BF16 = jnp.bfloat16

D_MODEL = 1024
A_HEADS = 4
A_DK = 128
A_DV = 128
A_WIDTH = A_HEADS * A_DV
CONV_W = 4
CONV_DIM = 2 * A_HEADS * A_DK + A_HEADS * A_DV
B_HEADS = 8
B_HEAD_DIM = 64
B_WIDTH = B_HEADS * B_HEAD_DIM
DILATED = ((128, 1), (512, 4), (2048, 16))
STEPS = 128
MAX_WINDOW = 2048
EPS = 1e-6
NEG = -1e30
CHUNK = 64
GATE_PAD = 128
COL_Z = CONV_DIM
COL_G = CONV_DIM + A_WIDTH
COLS_A = COL_G + GATE_PAD
COLS_B = 3 * B_WIDTH
VMEM_LIMIT = 56 * 1024 * 1024


def _dot(a, b):
    return jnp.dot(a.astype(BF16), b.astype(BF16), preferred_element_type=F32)


def _dot_nt(a, b):
    return lax.dot_general(a.astype(BF16), b.astype(BF16), (((1,), (1,)), ((), ())),
                           preferred_element_type=F32)


def _dot_tn(a, b):
    return lax.dot_general(a.astype(BF16), b.astype(BF16), (((0,), (0,)), ((), ())),
                           preferred_element_type=F32)


def _split2(a):
    hi = a.astype(BF16)
    lo = (a - hi.astype(F32)).astype(BF16)
    return hi, lo


def _sigmoid(x):
    return 1.0 / (1.0 + jnp.exp(-x))


def _silu(x):
    return x * _sigmoid(x)


def _softplus(x):
    return jnp.maximum(x, 0.0) + jnp.log(1.0 + jnp.exp(-jnp.abs(x)))


def _rms(x, w):
    return x * lax.rsqrt(jnp.mean(x * x, axis=-1, keepdims=True) + EPS) * w


def _proj(h, wt):
    return lax.dot_general(h, wt, (((1,), (1,)), ((), ())), preferred_element_type=F32)


def _norm_proj_kernel(x_ref, nw_ref, wa_ref, wg_ref, wb_ref, oa_ref, ob_ref):
    h = _rms(x_ref[...], nw_ref[...]).astype(BF16)
    oa_ref[:, :COL_G] = _proj(h, wa_ref[...])
    oa_ref[:, COL_G:] = _proj(h, wg_ref[...])
    ob_ref[...] = _proj(h, wb_ref[...])


def _weight_specs():
    zero = lambda *g: (0, 0)
    return [pl.BlockSpec((c, D_MODEL), zero, pipeline_mode=pl.Buffered(1)) for c in (COL_G, GATE_PAD, COLS_B)]


def _norm_proj(x2d, norm_w, w_p, tm):
    t = x2d.shape[0]
    once = pl.Buffered(1)
    return pl.pallas_call(
        _norm_proj_kernel,
        grid=(t // tm,),
        in_specs=[pl.BlockSpec((tm, D_MODEL), lambda i: (i, 0)),
                  pl.BlockSpec((1, D_MODEL), lambda i: (0, 0), pipeline_mode=once)] + _weight_specs(),
        out_specs=[pl.BlockSpec((tm, COLS_A), lambda i: (i, 0)),
                   pl.BlockSpec((tm, COLS_B), lambda i: (i, 0))],
        out_shape=[jax.ShapeDtypeStruct((t, COLS_A), F32),
                   jax.ShapeDtypeStruct((t, COLS_B), F32)],
        compiler_params=pltpu.CompilerParams(dimension_semantics=("arbitrary",),
                                             vmem_limit_bytes=VMEM_LIMIT),
        name="norm_proj",
    )(x2d, norm_w, *w_p)


def _causal_conv_silu(x_all, wconv_ref, n):
    conv = x_all[8:8 + n] * wconv_ref[CONV_W - 1:CONV_W, :]
    for back in range(1, CONV_W):
        tap = CONV_W - 1 - back
        conv = conv + pltpu.roll(x_all, back, axis=0)[8:8 + n] * wconv_ref[tap:tap + 1, :]
    return _silu(conv)


def _l2norm(x):
    return x * lax.rsqrt(jnp.sum(x * x, axis=-1, keepdims=True) + EPS)


def _norm_proj_conv_kernel(x_ref, nw_ref, wa_ref, wg_ref, wb_ref, wconv_ref, conv0_ref, oa_ref, ob_ref, tail_ref,
                           carry_ref, h_ref, raw_ref):
    i = pl.program_id(1)
    tm = x_ref.shape[0]
    wq = A_HEADS * A_DK
    assert CONV_DIM == 3 * wq and COLS_B == 3 * B_WIDTH
    dd = functools.partial(jnp.dot, preferred_element_type=F32)

    @pl.when(i == 0)
    def _():
        carry_ref[...] = conv0_ref[...]

    def conv_cols(g):
        cs = slice(g * wq, (g + 1) * wq)
        conv = _causal_conv_silu(jnp.concatenate([carry_ref[:, cs], raw_ref[:, cs]], axis=0),
                                 wconv_ref.at[:, cs], tm)
        carry_ref[:, cs] = raw_ref[tm - 8:tm, cs]
        if g == 2:
            oa_ref[:, cs] = conv
            return
        for hd in range(A_HEADS):
            hs = slice(hd * A_DK, (hd + 1) * A_DK)
            out = _l2norm(conv[:, hs])
            oa_ref[:, g * wq + hd * A_DK:g * wq + (hd + 1) * A_DK] = out * (A_DK ** -0.5) if g == 0 else out

    def phase(k):
        if k == 0:
            h_ref[...] = _rms(x_ref[...], nw_ref[...]).astype(BF16)
        h = h_ref[...]
        if k < 3:
            raw_ref[:, k * wq:(k + 1) * wq] = _proj(h, wa_ref[k * wq:(k + 1) * wq, :])
        if k == 0:
            oa_ref[:, CONV_DIM:COL_G] = _proj(h, wa_ref[CONV_DIM:, :])
        else:
            ob_ref[:, (k - 1) * B_WIDTH:k * B_WIDTH] = _proj(h, wb_ref[(k - 1) * B_WIDTH:k * B_WIDTH, :])
            conv_cols(k - 1)
        if k == 3:
            oa_ref[:, COL_G:] = _proj(h, wg_ref[...])

    for k in range(4):
        pl.when(i >= 0)(functools.partial(phase, k))

    @pl.when(i == pl.num_programs(1) - 1)
    def _():
        tail_ref[...] = raw_ref[tm - 8:tm, :]


def _norm_proj_conv(x, norm_w, w_p, w_conv, conv0, tm, side_of_grid=None):
    b, t, _ = x.shape
    assert t % tm == 0 and tm % 8 == 0
    once = pl.Buffered(1)
    grid = (b, t // tm)
    in_specs = [pl.BlockSpec((None, tm, D_MODEL), lambda i, j: (i, j, 0)),
                pl.BlockSpec((1, D_MODEL), lambda i, j: (0, 0), pipeline_mode=once)] + _weight_specs() + [
                pl.BlockSpec((CONV_W, CONV_DIM), lambda i, j: (0, 0), pipeline_mode=once),
                pl.BlockSpec((None, 8, CONV_DIM), lambda i, j: (i, 0, 0))]
    out_specs = [pl.BlockSpec((None, tm, COLS_A), lambda i, j: (i, j, 0)),
                 pl.BlockSpec((None, tm, COLS_B), lambda i, j: (i, j, 0)),
                 pl.BlockSpec((None, 8, CONV_DIM), lambda i, j: (i, 0, 0))]
    out_shape = [jax.ShapeDtypeStruct((b, t, COLS_A), F32),
                 jax.ShapeDtypeStruct((b, t, COLS_B), F32),
                 jax.ShapeDtypeStruct((b, 8, CONV_DIM), F32)]
    inputs = (x, norm_w, *w_p, w_conv, conv0)
    n_in, n_out, n_scratch = len(in_specs), len(out_specs), 3
    side = side_of_grid(grid) if side_of_grid is not None else None

    def kernel(*refs):
        n_side_in = len(side["in_specs"]) if side else 0
        if side:
            side["kernel"](*refs[n_in:n_in + n_side_in], *refs[n_in + n_side_in + n_out:len(refs) - n_scratch])
        _norm_proj_conv_kernel(*refs[:n_in], *refs[n_in + n_side_in:n_in + n_side_in + n_out],
                               *refs[len(refs) - n_scratch:])

    if side:
        inputs = inputs + tuple(side["inputs"])
        in_specs = in_specs + list(side["in_specs"])
        out_specs = out_specs + list(side["out_specs"])
        out_shape = out_shape + list(side["out_shape"])
    outs = pl.pallas_call(
        kernel,
        grid=grid,
        in_specs=in_specs,
        out_specs=out_specs,
        out_shape=out_shape,
        scratch_shapes=[pltpu.VMEM((8, CONV_DIM), F32), pltpu.VMEM((tm, D_MODEL), BF16),
                        pltpu.VMEM((tm, CONV_DIM), F32)],
        compiler_params=pltpu.CompilerParams(dimension_semantics=("arbitrary", "arbitrary"),
                                             vmem_limit_bytes=VMEM_LIMIT),
        name="norm_proj_conv",
    )(*inputs)
    return outs[:n_out], outs[n_out:]


def _cast_weights_parts(weights, grid):
    n_steps = grid[0] * grid[1]
    if n_steps % 2:
        return None
    per = n_steps // 2
    groups = (weights[:len(weights) // 2], weights[len(weights) // 2:])
    if any(w.shape[0] % (per * 16) for g in groups for w in g):
        return None
    step = lambda i, j: i * grid[1] + j
    specs, first = [], []
    for gi, g in enumerate(groups):
        for w in g:
            rows = w.shape[0] // per
            if gi == 0:
                specs.append(pl.BlockSpec((rows, w.shape[1]), lambda i, j: (jnp.minimum(step(i, j), per - 1), 0)))
            else:
                specs.append(pl.BlockSpec((rows, w.shape[1]), lambda i, j: (jnp.maximum(step(i, j) - per, 0), 0)))
            first.append(gi == 0)

    def kernel(*refs):
        n = len(first)
        s = pl.program_id(0) * grid[1] + pl.program_id(1)
        for src, dst, is_first in zip(refs[:n], refs[n:], first):
            def cast(src=src, dst=dst):
                dst[...] = src[...].astype(BF16)
            pl.when(s < per if is_first else s >= per)(cast)

    return dict(kernel=kernel, inputs=tuple(weights), in_specs=specs, out_specs=specs,
                out_shape=[jax.ShapeDtypeStruct(w.shape, BF16) for w in weights])


GDN_ROWS = 8 * CHUNK


HALF = 128 // CHUNK


def _gdn_pair_kernel(xa_ref, z_ref, gt_ref, conv0_ref, s0_ref, wconv_ref, gp_ref, nwa_ref,
                     o_ref, snew_ref, ext_ref, s_ref, *, rows, nb, conv_done, side_work=None):
    n = pl.program_id(1)
    c = CHUNK
    n_chunks = -(-rows // c)
    rp = n_chunks * c
    assert A_HEADS % 2 == 0 and HALF == 2 and A_DK == 128 and A_DV == 128

    @pl.when(n == 0)
    def _():
        ext_ref[:, 0:8, :] = conv0_ref[...]
        s_ref[...] = s0_ref[...]

    if side_work is not None:
        side_work()

    row = lax.broadcasted_iota(jnp.int32, (rp, 1), 0)
    live = row < rows
    lane_g = lax.broadcasted_iota(jnp.int32, (rp, GATE_PAD), 1)
    gate_lanes = (lane_g >= A_HEADS) & (lane_g < 2 * A_HEADS) & live

    ii = lax.broadcasted_iota(jnp.int32, (c, 128), 0)
    ll = lax.broadcasted_iota(jnp.int32, (c, 128), 1)
    jj = ll & (c - 1)
    low = ll < c
    incl = ii >= jj
    strict = ii > jj
    eye = jnp.where(ii == jj, 1.0, 0.0).astype(F32)
    ti = lax.broadcasted_iota(jnp.int32, (c, c), 0)
    tj = lax.broadcasted_iota(jnp.int32, (c, c), 1)
    tril = jnp.where(ti >= tj, 1.0, 0.0).astype(BF16)
    dd = functools.partial(jnp.dot, preferred_element_type=F32)

    def exact_tril_dot(x):
        hi, lo = _split2(x)
        lo2 = (x - hi.astype(F32) - lo.astype(F32)).astype(BF16)
        return dd(tril, hi) + dd(tril, lo) + dd(tril, lo2)

    def bd(x):
        return jnp.concatenate([jnp.where(low, x, 0.0), jnp.where(low, 0.0, x)], axis=0).astype(BF16)

    def unstack(y):
        return jnp.where(low, y[:c], y[c:])

    nwa = nwa_ref[...]
    chains = []
    for bi in range(nb):
        if conv_done:
            assert rows == rp
            conv = xa_ref[bi]
        else:
            ext_ref[bi, 8:8 + rows, :] = xa_ref[bi]
            if rows < rp:
                ext_ref[bi, 8 + rows:8 + rp, :] = jnp.zeros((rp - rows, CONV_DIM), F32)
            conv = _causal_conv_silu(ext_ref[bi, 0:8 + rp, :], wconv_ref, rp)
            ext_ref[bi, 0:8, :] = ext_ref[bi, rp:rp + 8, :]
            if rows < rp:
                conv = jnp.where(live, conv, 0.0)

        gt = gt_ref[bi]
        if rows < rp:
            gt = jnp.concatenate([gt, jnp.zeros((rp - rows, GATE_PAD), F32)], axis=0)
        beta_all = jnp.where(live, _sigmoid(gt), 0.0)
        g_all = jnp.where(gate_lanes, -jnp.exp(gp_ref[0:1, :]) * _softplus(gt + gp_ref[1:2, :]), 0.0)

        for ci in range(n_chunks):
            rs = slice(ci * c, (ci + 1) * c)
            gc_all = exact_tril_dot(g_all[rs])
            for p in range(A_HEADS // 2):
                hd = []
                for h in (2 * p, 2 * p + 1):
                    q = conv[rs, h * A_DK:(h + 1) * A_DK]
                    k = conv[rs, A_HEADS * A_DK + h * A_DK:A_HEADS * A_DK + (h + 1) * A_DK]
                    v = conv[rs, 2 * A_HEADS * A_DK + h * A_DV:2 * A_HEADS * A_DK + (h + 1) * A_DV]
                    if not conv_done:
                        q = _l2norm(q) * (A_DK ** -0.5)
                        k = _l2norm(k)
                    beta = beta_all[rs, h:h + 1]
                    g = g_all[rs, A_HEADS + h:A_HEADS + h + 1]
                    gc = gc_all[:, A_HEADS + h:A_HEADS + h + 1]
                    gl = gc_all[c - 1:c, A_HEADS + h:A_HEADS + h + 1]
                    kb = k * beta
                    hd.append(dict(q=q, k=k, kb=kb, g=g, qg=q * jnp.exp(gc), kd=k * jnp.exp(gl - gc),
                                   dl=jnp.exp(gl),
                                   rhs=jnp.concatenate([v * beta, kb * jnp.exp(gc)], axis=1)))
                chains.append(dict(bi=bi, ci=ci, p=p, hd=hd))

    for ch in chains:
        a, b = ch["hd"]
        ch["e_mat"] = exact_tril_dot(jnp.where(strict, jnp.where(low, a["g"], b["g"]), 0.0))
    for ch in chains:
        a, b = ch["hd"]
        k_st = jnp.concatenate([a["k"], b["k"]], axis=0)
        ch["kk"] = _dot_nt(jnp.concatenate([a["kb"], b["kb"]], axis=0), k_st)
        ch["qk"] = _dot_nt(jnp.concatenate([a["q"], b["q"]], axis=0), k_st)
    for ch in chains:
        decay = jnp.where(incl, jnp.exp(jnp.where(incl, ch["e_mat"], 0.0)), 0.0)
        ch["lower"] = jnp.where(strict, unstack(ch["kk"]) * decay, 0.0)
        ch["qk"] = jnp.where(incl, unstack(ch["qk"]) * decay, 0.0)
    def joining(level):
        return strict & ((ii >> (level + 1)) == (jj >> (level + 1))) & ((ii >> level) != (jj >> level))

    for ch in chains:
        ch["t"] = eye - jnp.where(joining(0), ch["lower"], 0.0)
    for level in range(1, c.bit_length() - 1):
        join = joining(level)
        for ch in chains:
            ch["te"] = jnp.dot(ch["t"].astype(BF16), bd(jnp.where(join, ch["lower"], 0.0)),
                               preferred_element_type=F32)
        for ch in chains:
            ch["t"] = ch["t"] - jnp.dot(ch["te"].astype(BF16), bd(ch["t"]), preferred_element_type=F32)
    for ch in chains:
        lh, ll_ = _split2(ch["lower"])
        th = ch["t"].astype(BF16).astype(F32)
        tl = ch["t"] - th
        bth = bd(th)
        lt = dd(lh, bth) + dd(lh, bd(tl)) + dd(ll_, bth)
        ch["res"] = eye - ch["t"] - lt
    for ch in chains:
        ch["t"] = ch["t"] + jnp.dot(ch["t"].astype(BF16), bd(ch["res"]), preferred_element_type=F32)
    for ch in chains:
        a, b = ch["hd"]
        uw = jnp.dot(bd(ch["t"]), jnp.concatenate([a["rhs"], b["rhs"]], axis=0).astype(BF16),
                     preferred_element_type=F32)
        a["u"], a["w"] = uw[:c, :A_DV], uw[:c, A_DV:]
        b["u"], b["w"] = uw[c:, :A_DV], uw[c:, A_DV:]
        ch["qkb"] = bd(ch["qk"])

    by_key = {(ch["bi"], ch["ci"], ch["p"]): ch for ch in chains}
    states = {(bi, h): s_ref[bi, h] for bi in range(nb) for h in range(A_HEADS)}
    for ci in range(n_chunks):
        cur = [by_key[(bi, ci, p)] for bi in range(nb) for p in range(A_HEADS // 2)]
        for ch in cur:
            for hh, d in enumerate(ch["hd"]):
                ws = _dot(jnp.concatenate([d["w"], d["qg"]], axis=0), states[(ch["bi"], 2 * ch["p"] + hh)])
                d["e"] = d["u"] - ws[:c]
                d["o"] = ws[c:]
        for ch in cur:
            a, b = ch["hd"]
            o2 = jnp.dot(ch["qkb"], jnp.concatenate([a["e"], b["e"]], axis=0).astype(BF16),
                         preferred_element_type=F32)
            a["o"] = a["o"] + o2[:c]
            b["o"] = b["o"] + o2[c:]
        for ch in cur:
            for hh, d in enumerate(ch["hd"]):
                key = (ch["bi"], 2 * ch["p"] + hh)
                states[key] = states[key] * d["dl"] + _dot_tn(d["kd"], d["e"])
        r0 = ci * c
        r1 = min(rows, r0 + c)
        for ch in cur:
            for hh, d in enumerate(ch["hd"]):
                h = 2 * ch["p"] + hh
                o = d["o"] if r1 - r0 == c else d["o"][:r1 - r0]
                o_ref[ch["bi"], r0:r1, h * A_DV:(h + 1) * A_DV] = (
                    _rms(o, nwa) * _silu(z_ref[ch["bi"], r0:r1, h * A_DV:(h + 1) * A_DV]))
    for (bi, h), s in states.items():
        s_ref[bi, h] = s

    @pl.when(n == pl.num_programs(1) - 1)
    def _():
        snew_ref[...] = s_ref[...]


def _gdn_pair(proj, conv0, s0, w_conv, gate_par, norm_a, conv_done, side_of_grid=None):
    b, t, _ = proj.shape
    rows = min(GDN_ROWS, t)
    assert t % rows == 0 and (rows % CHUNK == 0 or rows == t)
    n = t // rows
    nb = max(1, min(b, GDN_ROWS // (-(-rows // CHUNK) * CHUNK)))
    assert b % nb == 0
    rp = 0 if conv_done else -(-rows // CHUNK) * CHUNK
    in_specs = [pl.BlockSpec((nb, rows, CONV_DIM), lambda i, j: (i, j, 0)),
                pl.BlockSpec((nb, rows, A_WIDTH), lambda i, j: (i, j, COL_Z // A_WIDTH)),
                pl.BlockSpec((nb, rows, GATE_PAD), lambda i, j: (i, j, COL_G // GATE_PAD)),
                pl.BlockSpec((nb, 8, CONV_DIM), lambda i, j: (i, 0, 0)),
                pl.BlockSpec((nb, A_HEADS, A_DK, A_DV), lambda i, j: (i, 0, 0, 0)),
                pl.BlockSpec((CONV_W, CONV_DIM), lambda i, j: (0, 0)),
                pl.BlockSpec((8, GATE_PAD), lambda i, j: (0, 0)),
                pl.BlockSpec((1, A_DV), lambda i, j: (0, 0))]
    out_specs = [pl.BlockSpec((nb, rows, A_WIDTH), lambda i, j: (i, j, 0)),
                 pl.BlockSpec((nb, A_HEADS, A_DK, A_DV), lambda i, j: (i, 0, 0, 0))]
    out_shape = [jax.ShapeDtypeStruct((b, t, A_WIDTH), F32),
                 jax.ShapeDtypeStruct((b, A_HEADS, A_DK, A_DV), F32)]
    inputs = (proj, proj, proj, conv0, s0, w_conv, gate_par, norm_a)
    n_in, n_out = len(in_specs), len(out_specs)
    side = side_of_grid((b // nb, n)) if side_of_grid is not None else None

    def kernel(*refs):
        n_side_in = len(side["in_specs"]) if side else 0
        gin, sin = refs[:n_in], refs[n_in:n_in + n_side_in]
        gout = refs[n_in + n_side_in:n_in + n_side_in + n_out]
        sout = refs[n_in + n_side_in + n_out:len(refs) - 2]
        side_work = (lambda: side["kernel"](*sin, *sout)) if side else None
        _gdn_pair_kernel(*gin, *gout, *refs[len(refs) - 2:], rows=rows, nb=nb, conv_done=conv_done,
                         side_work=side_work)

    if side:
        inputs = inputs + tuple(side["inputs"])
        in_specs = in_specs + list(side["in_specs"])
        out_specs = out_specs + list(side["out_specs"])
        out_shape = out_shape + list(side["out_shape"])
    outs = pl.pallas_call(
        kernel,
        grid=(b // nb, n),
        in_specs=in_specs,
        out_specs=out_specs,
        out_shape=out_shape,
        scratch_shapes=[pltpu.VMEM((nb, 8 + rp, CONV_DIM), F32),
                        pltpu.VMEM((nb, A_HEADS, A_DK, A_DV), F32)],
        compiler_params=pltpu.CompilerParams(dimension_semantics=("arbitrary", "arbitrary"),
                                             vmem_limit_bytes=VMEM_LIMIT),
        name="gdn_side" if side else "gdn",
    )(*inputs)
    return outs[:n_out], outs[n_out:]


def _slope(h):
    return 2.0 ** (-8.0 * (h + 1) / B_HEADS)


ATTN_SB = MAX_WINDOW
ATTN_SKEW = 2
PAIR = 128 // B_HEAD_DIM


def _attn_fused_kernel(q_ref, kp_ref, kc_ref, vp_ref, vc_ref, nwb_ref, o_ref, kt_ref, vt_ref, *stats):
    hp = pl.program_id(1)
    n = pl.program_id(2)
    sb = ATTN_SB
    blk = STEPS

    lane_q = lax.broadcasted_iota(jnp.int32, (blk, 128), 1)
    low_q = lane_q < B_HEAD_DIM
    qq = lax.broadcasted_iota(jnp.int32, (2 * blk, 2 * blk), 0)
    kk = lax.broadcasted_iota(jnp.int32, (2 * blk, 2 * blk), 1)
    steps_back = (qq & (blk - 1)) + blk - kk
    in_band = (steps_back >= 0) & (steps_back <= STEPS)
    slope_lo = jnp.float32(_slope(0))
    slope_hi = jnp.float32(_slope(1))
    for i in range(1, B_HEADS // PAIR):
        slope_lo = jnp.where(hp == i, _slope(PAIR * i), slope_lo)
        slope_hi = jnp.where(hp == i, _slope(PAIR * i + 1), slope_hi)
    slope = jnp.where(qq < blk, slope_lo, slope_hi)

    biases = []
    for (_, dil) in DILATED:
        bias = jnp.where(in_band, -slope * (steps_back * dil).astype(F32), NEG)
        biases.append((bias, jnp.where(kk >= blk, bias, NEG)))
    n_blk = sb // blk
    work = [(bi, it) for bi in range(len(DILATED)) for it in range(n_blk)]

    def rows(dil, start, size):
        return pl.ds(start, size) if dil == 1 else pl.ds(start, size, stride=dil)

    def key_tile(prev_ref, cur_ref, dil, k0):
        if k0 >= sb:
            return cur_ref[rows(dil, k0 - sb, 2 * blk), :].astype(BF16)
        assert k0 + blk * dil >= sb > k0 + (blk - 1) * dil
        return jnp.concatenate([prev_ref[rows(dil, k0, blk), :].astype(BF16),
                                cur_ref[rows(dil, k0 + blk * dil - sb, blk), :].astype(BF16)], axis=0)

    def issue_scores(bi, it):
        dil = DILATED[bi][1]
        q0 = (it // dil) * (blk * dil) + it % dil
        k0 = sb + q0 - blk * dil
        q = q_ref[rows(dil, q0, blk), :] * (B_HEAD_DIM ** -0.5)
        q2 = jnp.concatenate([jnp.where(low_q, q, 0.0), jnp.where(low_q, 0.0, q)], axis=0).astype(BF16)
        k = key_tile(kp_ref, kc_ref, dil, k0)
        return dict(bi=bi, dil=dil, q0=q0, k0=k0,
                    s=lax.dot_general(q2, k, (((1,), (1,)), ((), ())), preferred_element_type=F32))

    def softmax(w):
        bias, bias_first = biases[w["bi"]]
        s = w.pop("s") + (jnp.where(n == 0, bias_first, bias) if w["q0"] < blk * w["dil"] else bias)
        m = jnp.max(s, axis=-1, keepdims=True)
        p = jnp.exp(s - m)
        w.update(m=m, l=jnp.sum(p, axis=-1, keepdims=True), p=p.astype(BF16))

    def issue_values(w):
        dil, q0 = w["dil"], w["q0"]
        m_s, l_s, num_s = stats[3 * w["bi"]:3 * w["bi"] + 3]
        v = key_tile(vp_ref, vc_ref, dil, w["k0"])
        pv = jnp.dot(w["p"], v, preferred_element_type=F32)
        m_s[rows(dil, q0, blk), :] = jnp.where(low_q, w["m"][:blk], w["m"][blk:])
        l_s[rows(dil, q0, blk), :] = jnp.where(low_q, w["l"][:blk], w["l"][blk:])
        num_s[rows(dil, q0, blk), :] = jnp.where(low_q, pv[:blk], pv[blk:])

    inflight = {}
    for t in range(len(work) + 2 * ATTN_SKEW):
        if t < len(work):
            inflight[t] = issue_scores(*work[t])
        if 0 <= t - ATTN_SKEW < len(work):
            softmax(inflight[t - ATTN_SKEW])
        if 0 <= t - 2 * ATTN_SKEW < len(work):
            issue_values(inflight.pop(t - 2 * ATTN_SKEW))

    nwb = nwb_ref[...]

    def combine(i, carry):
        rs = pl.ds(pl.multiple_of(i * blk, blk), blk)
        ms = [stats[3 * bi][rs, :] for bi in range(len(DILATED))]
        ls = [stats[3 * bi + 1][rs, :] for bi in range(len(DILATED))]
        nums = [stats[3 * bi + 2][rs, :] for bi in range(len(DILATED))]
        mx = jnp.maximum(jnp.maximum(ms[0], ms[1]), ms[2])
        ws = [jnp.exp(m - mx) for m in ms]
        den = ws[0] * ls[0] + ws[1] * ls[1] + ws[2] * ls[2]
        o = (ws[0] * nums[0] + ws[1] * nums[1] + ws[2] * nums[2]) / den
        sq = o * o
        ss_lo = jnp.sum(jnp.where(low_q, sq, 0.0), axis=-1, keepdims=True)
        ss_hi = jnp.sum(jnp.where(low_q, 0.0, sq), axis=-1, keepdims=True)
        mean_sq = jnp.where(low_q, ss_lo, ss_hi) / B_HEAD_DIM
        o_ref[rs, :] = o * lax.rsqrt(mean_sq + EPS) * nwb
        return carry

    lax.fori_loop(0, sb // blk, combine, 0, unroll=True)

    @pl.when(n == pl.num_programs(2) - 1)
    def _():
        kt_ref[...] = kc_ref[...].T
        vt_ref[...] = vc_ref[...].T


def _attn_prompt(qkv, norm_b_pair):
    b, s, _ = qkv.shape
    sb = ATTN_SB
    assert s % sb == 0 and sb == min(MAX_WINDOW, s)
    assert all(w == STEPS * d and sb % w == 0 and d & (d - 1) == 0 for (w, d) in DILATED)
    n_pairs = B_HEADS // PAIR
    blk = (None, sb, 128)
    prev = lambda c0: pl.BlockSpec(blk, lambda i, p, n: (i, jnp.maximum(n - 1, 0), c0 + p))
    cur = lambda c0: pl.BlockSpec(blk, lambda i, p, n: (i, n, c0 + p))
    return pl.pallas_call(
        _attn_fused_kernel,
        grid=(b, n_pairs, s // sb),
        in_specs=[cur(0), prev(n_pairs), cur(n_pairs), prev(2 * n_pairs), cur(2 * n_pairs),
                  pl.BlockSpec((1, 128), lambda i, p, n: (0, 0))],
        out_specs=[pl.BlockSpec(blk, lambda i, p, n: (i, n, p)),
                   pl.BlockSpec((None, 128, sb), lambda i, p, n: (i, p, 0)),
                   pl.BlockSpec((None, 128, sb), lambda i, p, n: (i, p, 0))],
        out_shape=[jax.ShapeDtypeStruct((b, s, B_WIDTH), F32),
                   jax.ShapeDtypeStruct((b, B_WIDTH, sb), F32),
                   jax.ShapeDtypeStruct((b, B_WIDTH, sb), F32)],
        scratch_shapes=[pltpu.VMEM((sb, 128), F32)] * (3 * len(DILATED)),
        compiler_params=pltpu.CompilerParams(
            dimension_semantics=("arbitrary", "arbitrary", "arbitrary"), vmem_limit_bytes=VMEM_LIMIT),
        name="attn_prompt",
    )(qkv, qkv, qkv, qkv, qkv, norm_b_pair)


def _attn_sample_kernel(q_ref, kn_ref, vn_ref, kt_ref, vt_ref, nwb_ref, o_ref, kto_ref, vto_ref, *, n_past, t):
    kn = kn_ref[...]
    vn = vn_ref[...]
    kt = kt_ref[...]
    vt = vt_ref[...]

    prow = lax.broadcasted_iota(jnp.int32, (t, 128), 0)
    plane = lax.broadcasted_iota(jnp.int32, (t, 128), 1)
    place = jnp.where(plane == 128 - t + prow, 1.0, 0.0).astype(BF16)
    tail_lanes = lax.broadcasted_iota(jnp.int32, (B_WIDTH, 128), 1) >= 128 - t
    tn = functools.partial(lax.dot_general, dimension_numbers=(((0,), (0,)), ((), ())),
                           preferred_element_type=F32)

    def shifted(win, new, out_ref):
        hi, lo = _split2(new)
        lo2 = (new - hi.astype(F32) - lo.astype(F32)).astype(BF16)
        new_t = tn(hi, place) + tn(lo, place) + tn(lo2, place)
        sh = pltpu.roll(win, n_past - t, axis=1)
        out_ref[:, :n_past - 128] = sh[:, :n_past - 128]
        out_ref[:, n_past - 128:] = jnp.where(tail_lanes, new_t, sh[:, n_past - 128:])

    shifted(kt, kn, kto_ref)
    shifted(vt, vn, vto_ref)

    rq = B_HEADS * t
    t_bits = t.bit_length() - 1
    d_bits = B_HEAD_DIM.bit_length() - 1
    rowi = lax.broadcasted_iota(jnp.int32, (rq, B_WIDTH), 0)
    lanei = lax.broadcasted_iota(jnp.int32, (rq, B_WIDTH), 1)
    own = lax.shift_right_logical(rowi, t_bits) == lax.shift_right_logical(lanei, d_bits)
    q_rep = jnp.concatenate([q_ref[...]] * B_HEADS, axis=0)
    q_blk = jnp.where(own, q_rep, 0.0).astype(BF16)
    zpad = jnp.zeros((128 - t, B_WIDTH), BF16)
    kn_pad = jnp.concatenate([kn.astype(BF16), zpad], axis=0)
    vn_pad = jnp.concatenate([vn.astype(BF16), zpad], axis=0)
    vt16 = vt.astype(BF16)
    scale = B_HEAD_DIM ** -0.5
    s_c = jnp.dot(q_blk, kt.astype(BF16), preferred_element_type=F32) * scale
    s_n = lax.dot_general(q_blk, kn_pad, (((1,), (1,)), ((), ())), preferred_element_type=F32) * scale

    r1 = lax.broadcasted_iota(jnp.int32, (rq, 1), 0)
    head = lax.shift_right_logical(r1, t_bits)
    slope = jnp.zeros((rq, 1), F32)
    for h in range(B_HEADS):
        slope = jnp.where(head == h, _slope(h), slope)
    tok = r1 & (t - 1)
    delta_c = n_past + tok - lax.broadcasted_iota(jnp.int32, (rq, n_past), 1)
    key_n = lax.broadcasted_iota(jnp.int32, (rq, 128), 1)
    delta_n = tok - key_n
    s_c = s_c - slope * delta_c.astype(F32)
    s_n = s_n - slope * delta_n.astype(F32)
    live_n = (key_n < t) & (delta_n >= 0)

    ms, ls, pcs, pns = [], [], [], []
    for (_, dil) in DILATED:
        valid_c = ((delta_c & (dil - 1)) == 0) & (delta_c <= STEPS * dil)
        valid_n = live_n & ((delta_n & (dil - 1)) == 0) & (delta_n <= STEPS * dil)
        sd_c = jnp.where(valid_c, s_c, NEG)
        sd_n = jnp.where(valid_n, s_n, NEG)
        m = jnp.maximum(jnp.max(sd_c, axis=-1, keepdims=True), jnp.max(sd_n, axis=-1, keepdims=True))
        p_c = jnp.exp(sd_c - m)
        p_n = jnp.exp(sd_n - m)
        ms.append(m)
        ls.append(jnp.sum(p_c, axis=-1, keepdims=True) + jnp.sum(p_n, axis=-1, keepdims=True))
        pcs.append(p_c.astype(BF16))
        pns.append(p_n.astype(BF16))
    num_all = (lax.dot_general(jnp.concatenate(pcs, axis=0), vt16, (((1,), (1,)), ((), ())),
                               preferred_element_type=F32)
               + jnp.dot(jnp.concatenate(pns, axis=0), vn_pad, preferred_element_type=F32))
    nums = [num_all[i * rq:(i + 1) * rq] for i in range(len(DILATED))]
    mx = jnp.maximum(jnp.maximum(ms[0], ms[1]), ms[2])
    ws = [jnp.exp(m - mx) for m in ms]
    den = ws[0] * ls[0] + ws[1] * ls[1] + ws[2] * ls[2]
    o = (ws[0] * nums[0] + ws[1] * nums[1] + ws[2] * nums[2]) / den
    o = jnp.where(own, o, 0.0)
    o = o * lax.rsqrt(jnp.sum(o * o, axis=-1, keepdims=True) / B_HEAD_DIM + EPS) * nwb_ref[...]
    acc = o[0:t, :]
    for h in range(1, B_HEADS):
        acc = acc + o[h * t:(h + 1) * t, :]
    o_ref[...] = acc


def _attn_sample_parts(qkv, win_kt, win_vt, norm_b_tiled, seq_of_step):
    b, t, _ = qkv.shape
    n_past = win_kt.shape[2]
    assert n_past == MAX_WINDOW and n_past % 128 == 0 and t % 8 == 0 and t & (t - 1) == 0 and t <= 128
    assert all(d & (d - 1) == 0 for (_, d) in DILATED)
    col = lambda c: pl.BlockSpec((None, t, B_WIDTH), lambda *g: (seq_of_step(*g), 0, c))
    win = pl.BlockSpec((None, B_WIDTH, n_past), lambda *g: (seq_of_step(*g), 0, 0))
    return dict(
        kernel=functools.partial(_attn_sample_kernel, n_past=n_past, t=t),
        inputs=(qkv, qkv, qkv, win_kt, win_vt, norm_b_tiled),
        in_specs=[col(0), col(1), col(2), win, win, pl.BlockSpec((1, B_WIDTH), lambda *g: (0, 0))],
        out_specs=[col(0), win, win],
        out_shape=[jax.ShapeDtypeStruct((b, t, B_WIDTH), F32),
                   jax.ShapeDtypeStruct((b, B_WIDTH, n_past), F32),
                   jax.ShapeDtypeStruct((b, B_WIDTH, n_past), F32)])


def _attn_sample(qkv, win_kt, win_vt, norm_b_tiled):
    parts = _attn_sample_parts(qkv, win_kt, win_vt, norm_b_tiled, lambda i: i)
    return pl.pallas_call(
        parts["kernel"],
        grid=(qkv.shape[0],),
        in_specs=parts["in_specs"],
        out_specs=parts["out_specs"],
        out_shape=parts["out_shape"],
        compiler_params=pltpu.CompilerParams(dimension_semantics=("arbitrary",),
                                             vmem_limit_bytes=VMEM_LIMIT),
        name="attn_sample",
    )(*parts["inputs"])


def _out_ffn_kernel(x_ref, oa_ref, ob_ref, wo_ref, nf_ref, wg_ref, wu_ref, wd_ref, nfin_ref, y_ref, *, ff_chunk):
    mixed = jnp.concatenate([oa_ref[...], ob_ref[...]], axis=-1).astype(BF16)
    x1 = x_ref[...] + jnp.dot(mixed, wo_ref[...], preferred_element_type=F32)
    hf = _rms(x1, nf_ref[...]).astype(BF16)
    x2 = x1
    for c0 in range(0, wg_ref.shape[1], ff_chunk):
        g = jnp.dot(hf, wg_ref[:, c0:c0 + ff_chunk], preferred_element_type=F32)
        u = jnp.dot(hf, wu_ref[:, c0:c0 + ff_chunk], preferred_element_type=F32)
        act = (_silu(g) * u).astype(BF16)
        x2 = x2 + jnp.dot(act, wd_ref[c0:c0 + ff_chunk, :], preferred_element_type=F32)
    y_ref[...] = _rms(x2, nfin_ref[...])


def _out_ffn(x2d, o_a, o_b, w_out, norm_ffn, w_gate, w_up, w_down, norm_final, tm):
    t = x2d.shape[0]
    d_ff = w_gate.shape[1]
    ff_chunk = d_ff
    once = pl.Buffered(1)
    row = lambda w: pl.BlockSpec((tm, w), lambda i: (i, 0))
    full = lambda a, b: pl.BlockSpec((a, b), lambda i: (0, 0), pipeline_mode=once)
    return pl.pallas_call(
        functools.partial(_out_ffn_kernel, ff_chunk=ff_chunk),
        grid=(t // tm,),
        in_specs=[row(D_MODEL), row(A_WIDTH), row(B_WIDTH),
                  full(D_MODEL, D_MODEL), full(1, D_MODEL),
                  full(D_MODEL, d_ff), full(D_MODEL, d_ff), full(d_ff, D_MODEL), full(1, D_MODEL)],
        out_specs=row(D_MODEL),
        out_shape=jax.ShapeDtypeStruct((t, D_MODEL), F32),
        compiler_params=pltpu.CompilerParams(dimension_semantics=("arbitrary",),
                                             vmem_limit_bytes=VMEM_LIMIT),
        name="out_ffn",
    )(x2d, o_a, o_b, w_out, norm_ffn, w_gate, w_up, w_down, norm_final)


def _split_w_in_kernel(w_ref, wa_ref, wg_ref, wb_ref):
    n_gate = 2 * A_HEADS
    wa_ref[...] = w_ref[0:COL_G, :].astype(BF16)
    gates = jnp.concatenate([w_ref[COL_G:COL_G + n_gate, :],
                             jnp.zeros((GATE_PAD - n_gate, w_ref.shape[1]), F32)], axis=0)
    wg_ref[...] = gates.astype(BF16)
    wb_ref[...] = w_ref[COL_G + n_gate:, :].astype(BF16)


def _split_w_in(wt):
    cols, d = wt.shape
    tl = 256
    assert d % tl == 0 and cols == COL_G + 2 * A_HEADS + COLS_B and (2 * A_HEADS) % 8 == 0
    return pl.pallas_call(
        _split_w_in_kernel,
        grid=(d // tl,),
        in_specs=[pl.BlockSpec((cols, tl), lambda j: (0, j))],
        out_specs=[pl.BlockSpec((c, tl), lambda j: (0, j)) for c in (COL_G, GATE_PAD, COLS_B)],
        out_shape=[jax.ShapeDtypeStruct((c, d), BF16) for c in (COL_G, GATE_PAD, COLS_B)],
        compiler_params=pltpu.CompilerParams(dimension_semantics=("arbitrary",),
                                             vmem_limit_bytes=VMEM_LIMIT),
        name="split_w_in",
    )(wt)


def _layer_params(norm_mix, w_in, w_conv, a_log, dt_bias, norm_out_a, norm_out_b, w_out, norm_ffn,
                  w_gate, w_up, w_down, layer):
    w_p = tuple(_split_w_in(jnp.transpose(w_in[layer])))
    gate_par = jnp.zeros((8, GATE_PAD), F32)
    gate_par = gate_par.at[0, A_HEADS:2 * A_HEADS].set(a_log[layer].astype(F32))
    gate_par = gate_par.at[1, A_HEADS:2 * A_HEADS].set(dt_bias[layer].astype(F32))
    return dict(
        norm_mix=norm_mix[layer].reshape(1, D_MODEL), w_p=w_p, w_conv=w_conv[layer], gate_par=gate_par,
        norm_a=norm_out_a[layer].reshape(1, A_DV),
        norm_b_pair=jnp.tile(norm_out_b[layer], PAIR).reshape(1, PAIR * B_HEAD_DIM),
        norm_b_tiled=jnp.tile(norm_out_b[layer], B_HEADS).reshape(1, B_WIDTH),
        norm_ffn=norm_ffn[layer].reshape(1, D_MODEL),
        ffn_f32=(w_gate[layer], w_up[layer], w_down[layer], w_out[layer]))


def _project(x, conv_buf, p, tm, side_of_grid=None):
    b, t, _ = x.shape
    assert t >= CONV_W - 1
    conv0 = jnp.pad(conv_buf, ((0, 0), (8 - (CONV_W - 1), 0), (0, 0)))
    conv_done = t % tm == 0
    side = ()
    if conv_done:
        (proj_a, proj_b, tail), side = _norm_proj_conv(x, p["norm_mix"], p["w_p"], p["w_conv"], conv0, tm,
                                                       side_of_grid)
        new_conv = tail[:, 8 - (CONV_W - 1):]
    else:
        proj_a, proj_b = _norm_proj(x.reshape(b * t, D_MODEL), p["norm_mix"], p["w_p"], tm)
        proj_a = proj_a.reshape(b, t, COLS_A)
        proj_b = proj_b.reshape(b, t, COLS_B)
        new_conv = proj_a[:, t - (CONV_W - 1):, :CONV_DIM]
    return proj_a, proj_b, new_conv, conv0, conv_done, side


def _finish(x, o_a, o_b, p, ffn16, norm_final, tm):
    b, t, _ = x.shape
    w_gate, w_up, w_down, w_out = ffn16
    y = _out_ffn(x.reshape(b * t, D_MODEL), o_a.reshape(b * t, A_WIDTH), o_b.reshape(b * t, B_WIDTH), w_out,
                 p["norm_ffn"], w_gate, w_up, w_down, norm_final.reshape(1, D_MODEL), tm)
    return y.reshape(b, t, D_MODEL)


def kernel(x_prompt, x_sample, state_conv, state_rec, cache_win_k, cache_win_v, norm_mix, w_in, w_conv, a_log,
           dt_bias, norm_out_a, norm_out_b, w_out, norm_ffn, w_gate, w_up, w_down, norm_final):
    depth = w_in.shape[0]
    assert depth == 1, "the final norm is fused into the block of a single-layer trunk"
    p = _layer_params(norm_mix, w_in, w_conv, a_log, dt_bias, norm_out_a, norm_out_b, w_out, norm_ffn,
                      w_gate, w_up, w_down, 0)
    bp, tp, _ = x_prompt.shape
    bs, ts, _ = x_sample.shape
    tm_p, tm_s = 512, bs * ts
    gdn_args = (p["w_conv"], p["gate_par"], p["norm_a"])

    sa, sb, sc, s_conv0, s_done, _ = _project(x_sample, state_conv[0], p, tm_s)
    n_past = cache_win_k.shape[2]
    to_t = lambda a: jnp.transpose(a, (0, 2, 3, 1)).reshape(bs, B_WIDTH, n_past)
    from_t = lambda a: jnp.transpose(a.reshape(a.shape[0], B_HEADS, B_HEAD_DIM, a.shape[2]), (0, 3, 1, 2))
    win_kt, win_vt = to_t(cache_win_k[0]), to_t(cache_win_v[0])

    def sample_attention_of_grid(grid):
        if grid[0] * grid[1] != bs:
            return None
        return _attn_sample_parts(sb, win_kt, win_vt, p["norm_b_tiled"], lambda i, j: i * grid[1] + j)

    pa, pb, pc, p_conv0, p_done, ffn16 = _project(
        x_prompt, jnp.zeros((bp, CONV_W - 1, CONV_DIM), F32), p, tm_p,
        lambda grid: _cast_weights_parts(p["ffn_f32"], grid))
    if not ffn16:
        ffn16 = tuple(w.astype(BF16) for w in p["ffn_f32"])
    zero_rec = jnp.zeros((bp, A_HEADS, A_DK, A_DV), F32)
    (o_a_p, pr), side = _gdn_pair(pa, p_conv0, zero_rec, *gdn_args, p_done, sample_attention_of_grid)
    o_b_s, new_kt, new_vt = side if side else _attn_sample(sb, win_kt, win_vt, p["norm_b_tiled"])
    (o_a_s, sr), _ = _gdn_pair(sa, s_conv0, state_rec[0], *gdn_args, s_done)

    o_b_p, pkt, pvt = _attn_prompt(pb, p["norm_b_pair"])

    yp = _finish(x_prompt, o_a_p, o_b_p, p, ffn16, norm_final, tm_p)
    ys = _finish(x_sample, o_a_s, o_b_s, p, ffn16, norm_final, tm_s)
    return (yp, ys, pc[None], pr[None], from_t(pkt)[None], from_t(pvt)[None], sc[None], sr[None],
            from_t(new_kt)[None], from_t(new_vt)[None])
```

```python
import functools

import jax
import jax.numpy as jnp
from jax import lax
from jax.experimental import pallas as pl
from jax.experimental.pallas import tpu as pltpu

F32 = jnp.float32
BF16 = jnp.bfloat16

D_MODEL = 1024
A_HEADS = 4
A_DK = 128
A_DV = 128
A_WIDTH = A_HEADS * A_DV
CONV_W = 4
CONV_DIM = 2 * A_HEADS * A_DK + A_HEADS * A_DV
B_HEADS = 8
B_HEAD_DIM = 64
B_WIDTH = B_HEADS * B_HEAD_DIM
DILATED = ((128, 1), (512, 4), (2048, 16))
STEPS = 128
MAX_WINDOW = 2048
EPS = 1e-6
NEG = -1e30
CHUNK = 64
GATE_PAD = 128
COL_Z = CONV_DIM
COL_G = CONV_DIM + A_WIDTH
COLS_A = COL_G + GATE_PAD
COLS_B = 3 * B_WIDTH
VMEM_LIMIT = 56 * 1024 * 1024


def _dot(a, b):
    return jnp.dot(a.astype(BF16), b.astype(BF16), preferred_element_type=F32)


def _dot_nt(a, b):
    return lax.dot_general(a.astype(BF16), b.astype(BF16), (((1,), (1,)), ((), ())),
                           preferred_element_type=F32)


def _dot_tn(a, b):
    return lax.dot_general(a.astype(BF16), b.astype(BF16), (((0,), (0,)), ((), ())),
                           preferred_element_type=F32)


def _split2(a):
    hi = a.astype(BF16)
    lo = (a - hi.astype(F32)).astype(BF16)
    return hi, lo


def _sigmoid(x):
    return 1.0 / (1.0 + jnp.exp(-x))


def _silu(x):
    return x * _sigmoid(x)


def _softplus(x):
    return jnp.maximum(x, 0.0) + jnp.log(1.0 + jnp.exp(-jnp.abs(x)))


def _rms(x, w):
    return x * lax.rsqrt(jnp.mean(x * x, axis=-1, keepdims=True) + EPS) * w


def _proj(h, wt):
    return lax.dot_general(h, wt, (((1,), (1,)), ((), ())), preferred_element_type=F32)


def _norm_proj_kernel(x_ref, nw_ref, wa_ref, wg_ref, wb_ref, oa_ref, ob_ref):
    h = _rms(x_ref[...], nw_ref[...]).astype(BF16)
    oa_ref[:, :COL_G] = _proj(h, wa_ref[...])
    oa_ref[:, COL_G:] = _proj(h, wg_ref[...])
    ob_ref[...] = _proj(h, wb_ref[...])


def _weight_specs():
    zero = lambda *g: (0, 0)
    return [pl.BlockSpec((c, D_MODEL), zero, pipeline_mode=pl.Buffered(1)) for c in (COL_G, GATE_PAD, COLS_B)]


def _norm_proj(x2d, norm_w, w_p, tm):
    t = x2d.shape[0]
    once = pl.Buffered(1)
    return pl.pallas_call(
        _norm_proj_kernel,
        grid=(t // tm,),
        in_specs=[pl.BlockSpec((tm, D_MODEL), lambda i: (i, 0)),
                  pl.BlockSpec((1, D_MODEL), lambda i: (0, 0), pipeline_mode=once)] + _weight_specs(),
        out_specs=[pl.BlockSpec((tm, COLS_A), lambda i: (i, 0)),
                   pl.BlockSpec((tm, COLS_B), lambda i: (i, 0))],
        out_shape=[jax.ShapeDtypeStruct((t, COLS_A), F32),
                   jax.ShapeDtypeStruct((t, COLS_B), F32)],
        compiler_params=pltpu.CompilerParams(dimension_semantics=("arbitrary",),
                                             vmem_limit_bytes=VMEM_LIMIT),
        name="norm_proj",
    )(x2d, norm_w, *w_p)


def _causal_conv_silu(x_all, wconv_ref, n):
    conv = x_all[8:8 + n] * wconv_ref[CONV_W - 1:CONV_W, :]
    for back in range(1, CONV_W):
        tap = CONV_W - 1 - back
        conv = conv + pltpu.roll(x_all, back, axis=0)[8:8 + n] * wconv_ref[tap:tap + 1, :]
    return _silu(conv)


def _l2norm(x):
    return x * lax.rsqrt(jnp.sum(x * x, axis=-1, keepdims=True) + EPS)


def _norm_proj_conv_kernel(x_ref, nw_ref, wa_ref, wg_ref, wb_ref, wconv_ref, conv0_ref, oa_ref, ob_ref, tail_ref,
                           carry_ref, h_ref, raw_ref):
    i = pl.program_id(1)
    tm = x_ref.shape[0]
    wq = A_HEADS * A_DK
    assert CONV_DIM == 3 * wq and COLS_B == 3 * B_WIDTH
    dd = functools.partial(jnp.dot, preferred_element_type=F32)

    @pl.when(i == 0)
    def _():
        carry_ref[...] = conv0_ref[...]

    def conv_cols(g):
        cs = slice(g * wq, (g + 1) * wq)
        conv = _causal_conv_silu(jnp.concatenate([carry_ref[:, cs], raw_ref[:, cs]], axis=0),
                                 wconv_ref.at[:, cs], tm)
        carry_ref[:, cs] = raw_ref[tm - 8:tm, cs]
        if g == 2:
            oa_ref[:, cs] = conv
            return
        for hd in range(A_HEADS):
            hs = slice(hd * A_DK, (hd + 1) * A_DK)
            out = _l2norm(conv[:, hs])
            oa_ref[:, g * wq + hd * A_DK:g * wq + (hd + 1) * A_DK] = out * (A_DK ** -0.5) if g == 0 else out

    def phase(k):
        if k == 0:
            h_ref[...] = _rms(x_ref[...], nw_ref[...]).astype(BF16)
        h = h_ref[...]
        if k < 3:
            raw_ref[:, k * wq:(k + 1) * wq] = _proj(h, wa_ref[k * wq:(k + 1) * wq, :])
        if k == 0:
            oa_ref[:, CONV_DIM:COL_G] = _proj(h, wa_ref[CONV_DIM:, :])
        else:
            ob_ref[:, (k - 1) * B_WIDTH:k * B_WIDTH] = _proj(h, wb_ref[(k - 1) * B_WIDTH:k * B_WIDTH, :])
            conv_cols(k - 1)
        if k == 3:
            oa_ref[:, COL_G:] = _proj(h, wg_ref[...])

    for k in range(4):
        pl.when(i >= 0)(functools.partial(phase, k))

    @pl.when(i == pl.num_programs(1) - 1)
    def _():
        tail_ref[...] = raw_ref[tm - 8:tm, :]


def _norm_proj_conv(x, norm_w, w_p, w_conv, conv0, tm, side_of_grid=None):
    b, t, _ = x.shape
    assert t % tm == 0 and tm % 8 == 0
    once = pl.Buffered(1)
    grid = (b, t // tm)
    in_specs = [pl.BlockSpec((None, tm, D_MODEL), lambda i, j: (i, j, 0)),
                pl.BlockSpec((1, D_MODEL), lambda i, j: (0, 0), pipeline_mode=once)] + _weight_specs() + [
                pl.BlockSpec((CONV_W, CONV_DIM), lambda i, j: (0, 0), pipeline_mode=once),
                pl.BlockSpec((None, 8, CONV_DIM), lambda i, j: (i, 0, 0))]
    out_specs = [pl.BlockSpec((None, tm, COLS_A), lambda i, j: (i, j, 0)),
                 pl.BlockSpec((None, tm, COLS_B), lambda i, j: (i, j, 0)),
                 pl.BlockSpec((None, 8, CONV_DIM), lambda i, j: (i, 0, 0))]
    out_shape = [jax.ShapeDtypeStruct((b, t, COLS_A), F32),
                 jax.ShapeDtypeStruct((b, t, COLS_B), F32),
                 jax.ShapeDtypeStruct((b, 8, CONV_DIM), F32)]
    inputs = (x, norm_w, *w_p, w_conv, conv0)
    n_in, n_out, n_scratch = len(in_specs), len(out_specs), 3
    side = side_of_grid(grid) if side_of_grid is not None else None

    def kernel(*refs):
        n_side_in = len(side["in_specs"]) if side else 0
        if side:
            side["kernel"](*refs[n_in:n_in + n_side_in], *refs[n_in + n_side_in + n_out:len(refs) - n_scratch])
        _norm_proj_conv_kernel(*refs[:n_in], *refs[n_in + n_side_in:n_in + n_side_in + n_out],
                               *refs[len(refs) - n_scratch:])

    if side:
        inputs = inputs + tuple(side["inputs"])
        in_specs = in_specs + list(side["in_specs"])
        out_specs = out_specs + list(side["out_specs"])
        out_shape = out_shape + list(side["out_shape"])
    outs = pl.pallas_call(
        kernel,
        grid=grid,
        in_specs=in_specs,
        out_specs=out_specs,
        out_shape=out_shape,
        scratch_shapes=[pltpu.VMEM((8, CONV_DIM), F32), pltpu.VMEM((tm, D_MODEL), BF16),
                        pltpu.VMEM((tm, CONV_DIM), F32)],
        compiler_params=pltpu.CompilerParams(dimension_semantics=("arbitrary", "arbitrary"),
                                             vmem_limit_bytes=VMEM_LIMIT),
        name="norm_proj_conv",
    )(*inputs)
    return outs[:n_out], outs[n_out:]


def _cast_weights_parts(weights, grid):
    n_steps = grid[0] * grid[1]
    if n_steps % 2:
        return None
    per = n_steps // 2
    groups = (weights[:len(weights) // 2], weights[len(weights) // 2:])
    if any(w.shape[0] % (per * 16) for g in groups for w in g):
        return None
    step = lambda i, j: i * grid[1] + j
    specs, first = [], []
    for gi, g in enumerate(groups):
        for w in g:
            rows = w.shape[0] // per
            if gi == 0:
                specs.append(pl.BlockSpec((rows, w.shape[1]), lambda i, j: (jnp.minimum(step(i, j), per - 1), 0)))
            else:
                specs.append(pl.BlockSpec((rows, w.shape[1]), lambda i, j: (jnp.maximum(step(i, j) - per, 0), 0)))
            first.append(gi == 0)

    def kernel(*refs):
        n = len(first)
        s = pl.program_id(0) * grid[1] + pl.program_id(1)
        for src, dst, is_first in zip(refs[:n], refs[n:], first):
            def cast(src=src, dst=dst):
                dst[...] = src[...].astype(BF16)
            pl.when(s < per if is_first else s >= per)(cast)

    return dict(kernel=kernel, inputs=tuple(weights), in_specs=specs, out_specs=specs,
                out_shape=[jax.ShapeDtypeStruct(w.shape, BF16) for w in weights])


GDN_ROWS = 8 * CHUNK


HALF = 128 // CHUNK


def _gdn_pair_kernel(xa_ref, z_ref, gt_ref, conv0_ref, s0_ref, wconv_ref, gp_ref, nwa_ref,
                     o_ref, snew_ref, ext_ref, s_ref, *, rows, nb, conv_done, side_work=None):
    n = pl.program_id(1)
    c = CHUNK
    n_chunks = -(-rows // c)
    rp = n_chunks * c
    assert A_HEADS % 2 == 0 and HALF == 2 and A_DK == 128 and A_DV == 128

    @pl.when(n == 0)
    def _():
        ext_ref[:, 0:8, :] = conv0_ref[...]
        s_ref[...] = s0_ref[...]

    if side_work is not None:
        side_work()

    row = lax.broadcasted_iota(jnp.int32, (rp, 1), 0)
    live = row < rows
    lane_g = lax.broadcasted_iota(jnp.int32, (rp, GATE_PAD), 1)
    gate_lanes = (lane_g >= A_HEADS) & (lane_g < 2 * A_HEADS) & live

    ii = lax.broadcasted_iota(jnp.int32, (c, 128), 0)
    ll = lax.broadcasted_iota(jnp.int32, (c, 128), 1)
    jj = ll & (c - 1)
    low = ll < c
    incl = ii >= jj
    strict = ii > jj
    eye = jnp.where(ii == jj, 1.0, 0.0).astype(F32)
    ti = lax.broadcasted_iota(jnp.int32, (c, c), 0)
    tj = lax.broadcasted_iota(jnp.int32, (c, c), 1)
    tril = jnp.where(ti >= tj, 1.0, 0.0).astype(BF16)
    dd = functools.partial(jnp.dot, preferred_element_type=F32)

    def exact_tril_dot(x):
        hi, lo = _split2(x)
        lo2 = (x - hi.astype(F32) - lo.astype(F32)).astype(BF16)
        return dd(tril, hi) + dd(tril, lo) + dd(tril, lo2)

    def bd(x):
        return jnp.concatenate([jnp.where(low, x, 0.0), jnp.where(low, 0.0, x)], axis=0).astype(BF16)

    def unstack(y):
        return jnp.where(low, y[:c], y[c:])

    nwa = nwa_ref[...]
    chains = []
    for bi in range(nb):
        if conv_done:
            assert rows == rp
            conv = xa_ref[bi]
        else:
            ext_ref[bi, 8:8 + rows, :] = xa_ref[bi]
            if rows < rp:
                ext_ref[bi, 8 + rows:8 + rp, :] = jnp.zeros((rp - rows, CONV_DIM), F32)
            conv = _causal_conv_silu(ext_ref[bi, 0:8 + rp, :], wconv_ref, rp)
            ext_ref[bi, 0:8, :] = ext_ref[bi, rp:rp + 8, :]
            if rows < rp:
                conv = jnp.where(live, conv, 0.0)

        gt = gt_ref[bi]
        if rows < rp:
            gt = jnp.concatenate([gt, jnp.zeros((rp - rows, GATE_PAD), F32)], axis=0)
        beta_all = jnp.where(live, _sigmoid(gt), 0.0)
        g_all = jnp.where(gate_lanes, -jnp.exp(gp_ref[0:1, :]) * _softplus(gt + gp_ref[1:2, :]), 0.0)

        for ci in range(n_chunks):
            rs = slice(ci * c, (ci + 1) * c)
            gc_all = exact_tril_dot(g_all[rs])
            for p in range(A_HEADS // 2):
                hd = []
                for h in (2 * p, 2 * p + 1):
                    q = conv[rs, h * A_DK:(h + 1) * A_DK]
                    k = conv[rs, A_HEADS * A_DK + h * A_DK:A_HEADS * A_DK + (h + 1) * A_DK]
                    v = conv[rs, 2 * A_HEADS * A_DK + h * A_DV:2 * A_HEADS * A_DK + (h + 1) * A_DV]
                    if not conv_done:
                        q = _l2norm(q) * (A_DK ** -0.5)
                        k = _l2norm(k)
                    beta = beta_all[rs, h:h + 1]
                    g = g_all[rs, A_HEADS + h:A_HEADS + h + 1]
                    gc = gc_all[:, A_HEADS + h:A_HEADS + h + 1]
                    gl = gc_all[c - 1:c, A_HEADS + h:A_HEADS + h + 1]
                    kb = k * beta
                    hd.append(dict(q=q, k=k, kb=kb, g=g, qg=q * jnp.exp(gc), kd=k * jnp.exp(gl - gc),
                                   dl=jnp.exp(gl),
                                   rhs=jnp.concatenate([v * beta, kb * jnp.exp(gc)], axis=1)))
                chains.append(dict(bi=bi, ci=ci, p=p, hd=hd))

    for ch in chains:
        a, b = ch["hd"]
        ch["e_mat"] = exact_tril_dot(jnp.where(strict, jnp.where(low, a["g"], b["g"]), 0.0))
    for ch in chains:
        a, b = ch["hd"]
        k_st = jnp.concatenate([a["k"], b["k"]], axis=0)
        ch["kk"] = _dot_nt(jnp.concatenate([a["kb"], b["kb"]], axis=0), k_st)
        ch["qk"] = _dot_nt(jnp.concatenate([a["q"], b["q"]], axis=0), k_st)
    for ch in chains:
        decay = jnp.where(incl, jnp.exp(jnp.where(incl, ch["e_mat"], 0.0)), 0.0)
        ch["lower"] = jnp.where(strict, unstack(ch["kk"]) * decay, 0.0)
        ch["qk"] = jnp.where(incl, unstack(ch["qk"]) * decay, 0.0)
    def joining(level):
        return strict & ((ii >> (level + 1)) == (jj >> (level + 1))) & ((ii >> level) != (jj >> level))

    for ch in chains:
        ch["t"] = eye - jnp.where(joining(0), ch["lower"], 0.0)
    for level in range(1, c.bit_length() - 1):
        join = joining(level)
        for ch in chains:
            ch["te"] = jnp.dot(ch["t"].astype(BF16), bd(jnp.where(join, ch["lower"], 0.0)),
                               preferred_element_type=F32)
        for ch in chains:
            ch["t"] = ch["t"] - jnp.dot(ch["te"].astype(BF16), bd(ch["t"]), preferred_element_type=F32)
    for ch in chains:
        lh, ll_ = _split2(ch["lower"])
        th = ch["t"].astype(BF16).astype(F32)
        tl = ch["t"] - th
        bth = bd(th)
        lt = dd(lh, bth) + dd(lh, bd(tl)) + dd(ll_, bth)
        ch["res"] = eye - ch["t"] - lt
    for ch in chains:
        ch["t"] = ch["t"] + jnp.dot(ch["t"].astype(BF16), bd(ch["res"]), preferred_element_type=F32)
    for ch in chains:
        a, b = ch["hd"]
        uw = jnp.dot(bd(ch["t"]), jnp.concatenate([a["rhs"], b["rhs"]], axis=0).astype(BF16),
                     preferred_element_type=F32)
        a["u"], a["w"] = uw[:c, :A_DV], uw[:c, A_DV:]
        b["u"], b["w"] = uw[c:, :A_DV], uw[c:, A_DV:]
        ch["qkb"] = bd(ch["qk"])

    by_key = {(ch["bi"], ch["ci"], ch["p"]): ch for ch in chains}
    states = {(bi, h): s_ref[bi, h] for bi in range(nb) for h in range(A_HEADS)}
    for ci in range(n_chunks):
        cur = [by_key[(bi, ci, p)] for bi in range(nb) for p in range(A_HEADS // 2)]
        for ch in cur:
            for hh, d in enumerate(ch["hd"]):
                ws = _dot(jnp.concatenate([d["w"], d["qg"]], axis=0), states[(ch["bi"], 2 * ch["p"] + hh)])
                d["e"] = d["u"] - ws[:c]
                d["o"] = ws[c:]
        for ch in cur:
            a, b = ch["hd"]
            o2 = jnp.dot(ch["qkb"], jnp.concatenate([a["e"], b["e"]], axis=0).astype(BF16),
                         preferred_element_type=F32)
            a["o"] = a["o"] + o2[:c]
            b["o"] = b["o"] + o2[c:]
        for ch in cur:
            for hh, d in enumerate(ch["hd"]):
                key = (ch["bi"], 2 * ch["p"] + hh)
                states[key] = states[key] * d["dl"] + _dot_tn(d["kd"], d["e"])
        r0 = ci * c
        r1 = min(rows, r0 + c)
        for ch in cur:
            for hh, d in enumerate(ch["hd"]):
                h = 2 * ch["p"] + hh
                o = d["o"] if r1 - r0 == c else d["o"][:r1 - r0]
                o_ref[ch["bi"], r0:r1, h * A_DV:(h + 1) * A_DV] = (
                    _rms(o, nwa) * _silu(z_ref[ch["bi"], r0:r1, h * A_DV:(h + 1) * A_DV]))
    for (bi, h), s in states.items():
        s_ref[bi, h] = s

    @pl.when(n == pl.num_programs(1) - 1)
    def _():
        snew_ref[...] = s_ref[...]


def _gdn_pair(proj, conv0, s0, w_conv, gate_par, norm_a, conv_done, side_of_grid=None):
    b, t, _ = proj.shape
    rows = min(GDN_ROWS, t)
    assert t % rows == 0 and (rows % CHUNK == 0 or rows == t)
    n = t // rows
    nb = max(1, min(b, GDN_ROWS // (-(-rows // CHUNK) * CHUNK)))
    assert b % nb == 0
    rp = 0 if conv_done else -(-rows // CHUNK) * CHUNK
    in_specs = [pl.BlockSpec((nb, rows, CONV_DIM), lambda i, j: (i, j, 0)),
                pl.BlockSpec((nb, rows, A_WIDTH), lambda i, j: (i, j, COL_Z // A_WIDTH)),
                pl.BlockSpec((nb, rows, GATE_PAD), lambda i, j: (i, j, COL_G // GATE_PAD)),
                pl.BlockSpec((nb, 8, CONV_DIM), lambda i, j: (i, 0, 0)),
                pl.BlockSpec((nb, A_HEADS, A_DK, A_DV), lambda i, j: (i, 0, 0, 0)),
                pl.BlockSpec((CONV_W, CONV_DIM), lambda i, j: (0, 0)),
                pl.BlockSpec((8, GATE_PAD), lambda i, j: (0, 0)),
                pl.BlockSpec((1, A_DV), lambda i, j: (0, 0))]
    out_specs = [pl.BlockSpec((nb, rows, A_WIDTH), lambda i, j: (i, j, 0)),
                 pl.BlockSpec((nb, A_HEADS, A_DK, A_DV), lambda i, j: (i, 0, 0, 0))]
    out_shape = [jax.ShapeDtypeStruct((b, t, A_WIDTH), F32),
                 jax.ShapeDtypeStruct((b, A_HEADS, A_DK, A_DV), F32)]
    inputs = (proj, proj, proj, conv0, s0, w_conv, gate_par, norm_a)
    n_in, n_out = len(in_specs), len(out_specs)
    side = side_of_grid((b // nb, n)) if side_of_grid is not None else None

    def kernel(*refs):
        n_side_in = len(side["in_specs"]) if side else 0
        gin, sin = refs[:n_in], refs[n_in:n_in + n_side_in]
        gout = refs[n_in + n_side_in:n_in + n_side_in + n_out]
        sout = refs[n_in + n_side_in + n_out:len(refs) - 2]
        side_work = (lambda: side["kernel"](*sin, *sout)) if side else None
        _gdn_pair_kernel(*gin, *gout, *refs[len(refs) - 2:], rows=rows, nb=nb, conv_done=conv_done,
                         side_work=side_work)

    if side:
        inputs = inputs + tuple(side["inputs"])
        in_specs = in_specs + list(side["in_specs"])
        out_specs = out_specs + list(side["out_specs"])
        out_shape = out_shape + list(side["out_shape"])
    outs = pl.pallas_call(
        kernel,
        grid=(b // nb, n),
        in_specs=in_specs,
        out_specs=out_specs,
        out_shape=out_shape,
        scratch_shapes=[pltpu.VMEM((nb, 8 + rp, CONV_DIM), F32),
                        pltpu.VMEM((nb, A_HEADS, A_DK, A_DV), F32)],
        compiler_params=pltpu.CompilerParams(dimension_semantics=("arbitrary", "arbitrary"),
                                             vmem_limit_bytes=VMEM_LIMIT),
        name="gdn_side" if side else "gdn",
    )(*inputs)
    return outs[:n_out], outs[n_out:]


def _slope(h):
    return 2.0 ** (-8.0 * (h + 1) / B_HEADS)


ATTN_SB = MAX_WINDOW
ATTN_SKEW = 2
PAIR = 128 // B_HEAD_DIM


def _attn_fused_kernel(q_ref, kp_ref, kc_ref, vp_ref, vc_ref, nwb_ref, o_ref, kt_ref, vt_ref, *stats):
    hp = pl.program_id(1)
    n = pl.program_id(2)
    sb = ATTN_SB
    blk = STEPS

    lane_q = lax.broadcasted_iota(jnp.int32, (blk, 128), 1)
    low_q = lane_q < B_HEAD_DIM
    qq = lax.broadcasted_iota(jnp.int32, (2 * blk, 2 * blk), 0)
    kk = lax.broadcasted_iota(jnp.int32, (2 * blk, 2 * blk), 1)
    steps_back = (qq & (blk - 1)) + blk - kk
    in_band = (steps_back >= 0) & (steps_back <= STEPS)
    slope_lo = jnp.float32(_slope(0))
    slope_hi = jnp.float32(_slope(1))
    for i in range(1, B_HEADS // PAIR):
        slope_lo = jnp.where(hp == i, _slope(PAIR * i), slope_lo)
        slope_hi = jnp.where(hp == i, _slope(PAIR * i + 1), slope_hi)
    slope = jnp.where(qq < blk, slope_lo, slope_hi)

    biases = []
    for (_, dil) in DILATED:
        bias = jnp.where(in_band, -slope * (steps_back * dil).astype(F32), NEG)
        biases.append((bias, jnp.where(kk >= blk, bias, NEG)))
    n_blk = sb // blk
    work = [(bi, it) for bi in range(len(DILATED)) for it in range(n_blk)]

    def rows(dil, start, size):
        return pl.ds(start, size) if dil == 1 else pl.ds(start, size, stride=dil)

    def key_tile(prev_ref, cur_ref, dil, k0):
        if k0 >= sb:
            return cur_ref[rows(dil, k0 - sb, 2 * blk), :].astype(BF16)
        assert k0 + blk * dil >= sb > k0 + (blk - 1) * dil
        return jnp.concatenate([prev_ref[rows(dil, k0, blk), :].astype(BF16),
                                cur_ref[rows(dil, k0 + blk * dil - sb, blk), :].astype(BF16)], axis=0)

    def issue_scores(bi, it):
        dil = DILATED[bi][1]
        q0 = (it // dil) * (blk * dil) + it % dil
        k0 = sb + q0 - blk * dil
        q = q_ref[rows(dil, q0, blk), :] * (B_HEAD_DIM ** -0.5)
        q2 = jnp.concatenate([jnp.where(low_q, q, 0.0), jnp.where(low_q, 0.0, q)], axis=0).astype(BF16)
        k = key_tile(kp_ref, kc_ref, dil, k0)
        return dict(bi=bi, dil=dil, q0=q0, k0=k0,
                    s=lax.dot_general(q2, k, (((1,), (1,)), ((), ())), preferred_element_type=F32))

    def softmax(w):
        bias, bias_first = biases[w["bi"]]
        s = w.pop("s") + (jnp.where(n == 0, bias_first, bias) if w["q0"] < blk * w["dil"] else bias)
        m = jnp.max(s, axis=-1, keepdims=True)
        p = jnp.exp(s - m)
        w.update(m=m, l=jnp.sum(p, axis=-1, keepdims=True), p=p.astype(BF16))

    def issue_values(w):
        dil, q0 = w["dil"], w["q0"]
        m_s, l_s, num_s = stats[3 * w["bi"]:3 * w["bi"] + 3]
        v = key_tile(vp_ref, vc_ref, dil, w["k0"])
        pv = jnp.dot(w["p"], v, preferred_element_type=F32)
        m_s[rows(dil, q0, blk), :] = jnp.where(low_q, w["m"][:blk], w["m"][blk:])
        l_s[rows(dil, q0, blk), :] = jnp.where(low_q, w["l"][:blk], w["l"][blk:])
        num_s[rows(dil, q0, blk), :] = jnp.where(low_q, pv[:blk], pv[blk:])

    inflight = {}
    for t in range(len(work) + 2 * ATTN_SKEW):
        if t < len(work):
            inflight[t] = issue_scores(*work[t])
        if 0 <= t - ATTN_SKEW < len(work):
            softmax(inflight[t - ATTN_SKEW])
        if 0 <= t - 2 * ATTN_SKEW < len(work):
            issue_values(inflight.pop(t - 2 * ATTN_SKEW))

    nwb = nwb_ref[...]

    def combine(i, carry):
        rs = pl.ds(pl.multiple_of(i * blk, blk), blk)
        ms = [stats[3 * bi][rs, :] for bi in range(len(DILATED))]
        ls = [stats[3 * bi + 1][rs, :] for bi in range(len(DILATED))]
        nums = [stats[3 * bi + 2][rs, :] for bi in range(len(DILATED))]
        mx = jnp.maximum(jnp.maximum(ms[0], ms[1]), ms[2])
        ws = [jnp.exp(m - mx) for m in ms]
        den = ws[0] * ls[0] + ws[1] * ls[1] + ws[2] * ls[2]
        o = (ws[0] * nums[0] + ws[1] * nums[1] + ws[2] * nums[2]) / den
        sq = o * o
        ss_lo = jnp.sum(jnp.where(low_q, sq, 0.0), axis=-1, keepdims=True)
        ss_hi = jnp.sum(jnp.where(low_q, 0.0, sq), axis=-1, keepdims=True)
        mean_sq = jnp.where(low_q, ss_lo, ss_hi) / B_HEAD_DIM
        o_ref[rs, :] = o * lax.rsqrt(mean_sq + EPS) * nwb
        return carry

    lax.fori_loop(0, sb // blk, combine, 0, unroll=True)

    @pl.when(n == pl.num_programs(2) - 1)
    def _():
        kt_ref[...] = kc_ref[...].T
        vt_ref[...] = vc_ref[...].T


def _attn_prompt(qkv, norm_b_pair):
    b, s, _ = qkv.shape
    sb = ATTN_SB
    assert s % sb == 0 and sb == min(MAX_WINDOW, s)
    assert all(w == STEPS * d and sb % w == 0 and d & (d - 1) == 0 for (w, d) in DILATED)
    n_pairs = B_HEADS // PAIR
    blk = (None, sb, 128)
    prev = lambda c0: pl.BlockSpec(blk, lambda i, p, n: (i, jnp.maximum(n - 1, 0), c0 + p))
    cur = lambda c0: pl.BlockSpec(blk, lambda i, p, n: (i, n, c0 + p))
    return pl.pallas_call(
        _attn_fused_kernel,
        grid=(b, n_pairs, s // sb),
        in_specs=[cur(0), prev(n_pairs), cur(n_pairs), prev(2 * n_pairs), cur(2 * n_pairs),
                  pl.BlockSpec((1, 128), lambda i, p, n: (0, 0))],
        out_specs=[pl.BlockSpec(blk, lambda i, p, n: (i, n, p)),
                   pl.BlockSpec((None, 128, sb), lambda i, p, n: (i, p, 0)),
                   pl.BlockSpec((None, 128, sb), lambda i, p, n: (i, p, 0))],
        out_shape=[jax.ShapeDtypeStruct((b, s, B_WIDTH), F32),
                   jax.ShapeDtypeStruct((b, B_WIDTH, sb), F32),
                   jax.ShapeDtypeStruct((b, B_WIDTH, sb), F32)],
        scratch_shapes=[pltpu.VMEM((sb, 128), F32)] * (3 * len(DILATED)),
        compiler_params=pltpu.CompilerParams(
            dimension_semantics=("arbitrary", "arbitrary", "arbitrary"), vmem_limit_bytes=VMEM_LIMIT),
        name="attn_prompt",
    )(qkv, qkv, qkv, qkv, qkv, norm_b_pair)


def _attn_sample_kernel(q_ref, kn_ref, vn_ref, kt_ref, vt_ref, nwb_ref, o_ref, kto_ref, vto_ref, *, n_past, t):
    kn = kn_ref[...]
    vn = vn_ref[...]
    kt = kt_ref[...]
    vt = vt_ref[...]

    prow = lax.broadcasted_iota(jnp.int32, (t, 128), 0)
    plane = lax.broadcasted_iota(jnp.int32, (t, 128), 1)
    place = jnp.where(plane == 128 - t + prow, 1.0, 0.0).astype(BF16)
    tail_lanes = lax.broadcasted_iota(jnp.int32, (B_WIDTH, 128), 1) >= 128 - t
    tn = functools.partial(lax.dot_general, dimension_numbers=(((0,), (0,)), ((), ())),
                           preferred_element_type=F32)

    def shifted(win, new, out_ref):
        hi, lo = _split2(new)
        lo2 = (new - hi.astype(F32) - lo.astype(F32)).astype(BF16)
        new_t = tn(hi, place) + tn(lo, place) + tn(lo2, place)
        sh = pltpu.roll(win, n_past - t, axis=1)
        out_ref[:, :n_past - 128] = sh[:, :n_past - 128]
        out_ref[:, n_past - 128:] = jnp.where(tail_lanes, new_t, sh[:, n_past - 128:])

    shifted(kt, kn, kto_ref)
    shifted(vt, vn, vto_ref)

    rq = B_HEADS * t
    t_bits = t.bit_length() - 1
    d_bits = B_HEAD_DIM.bit_length() - 1
    rowi = lax.broadcasted_iota(jnp.int32, (rq, B_WIDTH), 0)
    lanei = lax.broadcasted_iota(jnp.int32, (rq, B_WIDTH), 1)
    own = lax.shift_right_logical(rowi, t_bits) == lax.shift_right_logical(lanei, d_bits)
    q_rep = jnp.concatenate([q_ref[...]] * B_HEADS, axis=0)
    q_blk = jnp.where(own, q_rep, 0.0).astype(BF16)
    zpad = jnp.zeros((128 - t, B_WIDTH), BF16)
    kn_pad = jnp.concatenate([kn.astype(BF16), zpad], axis=0)
    vn_pad = jnp.concatenate([vn.astype(BF16), zpad], axis=0)
    vt16 = vt.astype(BF16)
    scale = B_HEAD_DIM ** -0.5
    s_c = jnp.dot(q_blk, kt.astype(BF16), preferred_element_type=F32) * scale
    s_n = lax.dot_general(q_blk, kn_pad, (((1,), (1,)), ((), ())), preferred_element_type=F32) * scale

    r1 = lax.broadcasted_iota(jnp.int32, (rq, 1), 0)
    head = lax.shift_right_logical(r1, t_bits)
    slope = jnp.zeros((rq, 1), F32)
    for h in range(B_HEADS):
        slope = jnp.where(head == h, _slope(h), slope)
    tok = r1 & (t - 1)
    delta_c = n_past + tok - lax.broadcasted_iota(jnp.int32, (rq, n_past), 1)
    key_n = lax.broadcasted_iota(jnp.int32, (rq, 128), 1)
    delta_n = tok - key_n
    s_c = s_c - slope * delta_c.astype(F32)
    s_n = s_n - slope * delta_n.astype(F32)
    live_n = (key_n < t) & (delta_n >= 0)

    ms, ls, pcs, pns = [], [], [], []
    for (_, dil) in DILATED:
        valid_c = ((delta_c & (dil - 1)) == 0) & (delta_c <= STEPS * dil)
        valid_n = live_n & ((delta_n & (dil - 1)) == 0) & (delta_n <= STEPS * dil)
        sd_c = jnp.where(valid_c, s_c, NEG)
        sd_n = jnp.where(valid_n, s_n, NEG)
        m = jnp.maximum(jnp.max(sd_c, axis=-1, keepdims=True), jnp.max(sd_n, axis=-1, keepdims=True))
        p_c = jnp.exp(sd_c - m)
        p_n = jnp.exp(sd_n - m)
        ms.append(m)
        ls.append(jnp.sum(p_c, axis=-1, keepdims=True) + jnp.sum(p_n, axis=-1, keepdims=True))
        pcs.append(p_c.astype(BF16))
        pns.append(p_n.astype(BF16))
    num_all = (lax.dot_general(jnp.concatenate(pcs, axis=0), vt16, (((1,), (1,)), ((), ())),
                               preferred_element_type=F32)
               + jnp.dot(jnp.concatenate(pns, axis=0), vn_pad, preferred_element_type=F32))
    nums = [num_all[i * rq:(i + 1) * rq] for i in range(len(DILATED))]
    mx = jnp.maximum(jnp.maximum(ms[0], ms[1]), ms[2])
    ws = [jnp.exp(m - mx) for m in ms]
    den = ws[0] * ls[0] + ws[1] * ls[1] + ws[2] * ls[2]
    o = (ws[0] * nums[0] + ws[1] * nums[1] + ws[2] * nums[2]) / den
    o = jnp.where(own, o, 0.0)
    o = o * lax.rsqrt(jnp.sum(o * o, axis=-1, keepdims=True) / B_HEAD_DIM + EPS) * nwb_ref[...]
    acc = o[0:t, :]
    for h in range(1, B_HEADS):
        acc = acc + o[h * t:(h + 1) * t, :]
    o_ref[...] = acc


def _attn_sample_parts(qkv, win_kt, win_vt, norm_b_tiled, seq_of_step):
    b, t, _ = qkv.shape
    n_past = win_kt.shape[2]
    assert n_past == MAX_WINDOW and n_past % 128 == 0 and t % 8 == 0 and t & (t - 1) == 0 and t <= 128
    assert all(d & (d - 1) == 0 for (_, d) in DILATED)
    col = lambda c: pl.BlockSpec((None, t, B_WIDTH), lambda *g: (seq_of_step(*g), 0, c))
    win = pl.BlockSpec((None, B_WIDTH, n_past), lambda *g: (seq_of_step(*g), 0, 0))
    return dict(
        kernel=functools.partial(_attn_sample_kernel, n_past=n_past, t=t),
        inputs=(qkv, qkv, qkv, win_kt, win_vt, norm_b_tiled),
        in_specs=[col(0), col(1), col(2), win, win, pl.BlockSpec((1, B_WIDTH), lambda *g: (0, 0))],
        out_specs=[col(0), win, win],
        out_shape=[jax.ShapeDtypeStruct((b, t, B_WIDTH), F32),
                   jax.ShapeDtypeStruct((b, B_WIDTH, n_past), F32),
                   jax.ShapeDtypeStruct((b, B_WIDTH, n_past), F32)])


def _attn_sample(qkv, win_kt, win_vt, norm_b_tiled):
    parts = _attn_sample_parts(qkv, win_kt, win_vt, norm_b_tiled, lambda i: i)
    return pl.pallas_call(
        parts["kernel"],
        grid=(qkv.shape[0],),
        in_specs=parts["in_specs"],
        out_specs=parts["out_specs"],
        out_shape=parts["out_shape"],
        compiler_params=pltpu.CompilerParams(dimension_semantics=("arbitrary",),
                                             vmem_limit_bytes=VMEM_LIMIT),
        name="attn_sample",
    )(*parts["inputs"])


def _out_ffn_rows(x_ref, oa_ref, ob_ref, y_ref, wo_ref, nf_ref, wg_ref, wu_ref, wd_ref, nfin_ref):
    mixed = jnp.concatenate([oa_ref[...], ob_ref[...]], axis=-1).astype(BF16)
    x1 = x_ref[...] + jnp.dot(mixed, wo_ref[...], preferred_element_type=F32)
    hf = _rms(x1, nf_ref[...]).astype(BF16)
    g = jnp.dot(hf, wg_ref[...], preferred_element_type=F32)
    u = jnp.dot(hf, wu_ref[...], preferred_element_type=F32)
    act = (_silu(g) * u).astype(BF16)
    x2 = x1 + jnp.dot(act, wd_ref[...], preferred_element_type=F32)
    y_ref[...] = _rms(x2, nfin_ref[...])


def _out_ffn_kernel(x_ref, oa_ref, ob_ref, xs_ref, oas_ref, obs_ref, *rest):
    weights, (y_ref, ys_ref) = rest[:-2], rest[-2:]
    i = pl.program_id(0)
    last = pl.num_programs(0) - 1
    pl.when(i < last)(lambda: _out_ffn_rows(x_ref, oa_ref, ob_ref, y_ref, *weights))
    pl.when(i == last)(lambda: _out_ffn_rows(xs_ref, oas_ref, obs_ref, ys_ref, *weights))


def _out_ffn(long_rows, short_rows, w_out, norm_ffn, w_gate, w_up, w_down, norm_final, tm):
    t, ts = long_rows[0].shape[0], short_rows[0].shape[0]
    d_ff = w_gate.shape[1]
    n = t // tm
    assert t % tm == 0 and ts % 8 == 0
    once = pl.Buffered(1)
    row = lambda w: pl.BlockSpec((tm, w), lambda i: (jnp.minimum(i, n - 1), 0))
    short = lambda w: pl.BlockSpec((ts, w), lambda i: (0, 0))
    full = lambda a, b: pl.BlockSpec((a, b), lambda i: (0, 0), pipeline_mode=once)
    widths = (D_MODEL, A_WIDTH, B_WIDTH)
    return pl.pallas_call(
        _out_ffn_kernel,
        grid=(n + 1,),
        in_specs=[row(w) for w in widths] + [short(w) for w in widths] + [
            full(D_MODEL, D_MODEL), full(1, D_MODEL),
            full(D_MODEL, d_ff), full(D_MODEL, d_ff), full(d_ff, D_MODEL), full(1, D_MODEL)],
        out_specs=[row(D_MODEL), short(D_MODEL)],
        out_shape=[jax.ShapeDtypeStruct((t, D_MODEL), F32), jax.ShapeDtypeStruct((ts, D_MODEL), F32)],
        compiler_params=pltpu.CompilerParams(dimension_semantics=("arbitrary",),
                                             vmem_limit_bytes=VMEM_LIMIT),
        name="out_ffn",
    )(*long_rows, *short_rows, w_out, norm_ffn, w_gate, w_up, w_down, norm_final)


def _split_w_in_kernel(w_ref, wa_ref, wg_ref, wb_ref):
    n_gate = 2 * A_HEADS
    wa_ref[...] = w_ref[0:COL_G, :].astype(BF16)
    gates = jnp.concatenate([w_ref[COL_G:COL_G + n_gate, :],
                             jnp.zeros((GATE_PAD - n_gate, w_ref.shape[1]), F32)], axis=0)
    wg_ref[...] = gates.astype(BF16)
    wb_ref[...] = w_ref[COL_G + n_gate:, :].astype(BF16)


def _split_w_in(wt):
    cols, d = wt.shape
    tl = 256
    assert d % tl == 0 and cols == COL_G + 2 * A_HEADS + COLS_B and (2 * A_HEADS) % 8 == 0
    return pl.pallas_call(
        _split_w_in_kernel,
        grid=(d // tl,),
        in_specs=[pl.BlockSpec((cols, tl), lambda j: (0, j))],
        out_specs=[pl.BlockSpec((c, tl), lambda j: (0, j)) for c in (COL_G, GATE_PAD, COLS_B)],
        out_shape=[jax.ShapeDtypeStruct((c, d), BF16) for c in (COL_G, GATE_PAD, COLS_B)],
        compiler_params=pltpu.CompilerParams(dimension_semantics=("arbitrary",),
                                             vmem_limit_bytes=VMEM_LIMIT),
        name="split_w_in",
    )(wt)


def _layer_params(norm_mix, w_in, w_conv, a_log, dt_bias, norm_out_a, norm_out_b, w_out, norm_ffn,
                  w_gate, w_up, w_down, layer):
    w_p = tuple(_split_w_in(jnp.transpose(w_in[layer])))
    gate_par = jnp.zeros((8, GATE_PAD), F32)
    gate_par = gate_par.at[0, A_HEADS:2 * A_HEADS].set(a_log[layer].astype(F32))
    gate_par = gate_par.at[1, A_HEADS:2 * A_HEADS].set(dt_bias[layer].astype(F32))
    return dict(
        norm_mix=norm_mix[layer].reshape(1, D_MODEL), w_p=w_p, w_conv=w_conv[layer], gate_par=gate_par,
        norm_a=norm_out_a[layer].reshape(1, A_DV),
        norm_b_pair=jnp.tile(norm_out_b[layer], PAIR).reshape(1, PAIR * B_HEAD_DIM),
        norm_b_tiled=jnp.tile(norm_out_b[layer], B_HEADS).reshape(1, B_WIDTH),
        norm_ffn=norm_ffn[layer].reshape(1, D_MODEL),
        ffn_f32=(w_gate[layer], w_up[layer], w_down[layer], w_out[layer]))


def _project(x, conv_buf, p, tm, side_of_grid=None):
    b, t, _ = x.shape
    assert t >= CONV_W - 1
    conv0 = jnp.pad(conv_buf, ((0, 0), (8 - (CONV_W - 1), 0), (0, 0)))
    conv_done = t % tm == 0
    side = ()
    if conv_done:
        (proj_a, proj_b, tail), side = _norm_proj_conv(x, p["norm_mix"], p["w_p"], p["w_conv"], conv0, tm,
                                                       side_of_grid)
        new_conv = tail[:, 8 - (CONV_W - 1):]
    else:
        proj_a, proj_b = _norm_proj(x.reshape(b * t, D_MODEL), p["norm_mix"], p["w_p"], tm)
        proj_a = proj_a.reshape(b, t, COLS_A)
        proj_b = proj_b.reshape(b, t, COLS_B)
        new_conv = proj_a[:, t - (CONV_W - 1):, :CONV_DIM]
    return proj_a, proj_b, new_conv, conv0, conv_done, side


def _finish(long, short, p, ffn16, norm_final, tm):
    rows = lambda x, o_a, o_b: (x.reshape(-1, D_MODEL), o_a.reshape(-1, A_WIDTH), o_b.reshape(-1, B_WIDTH))
    w_gate, w_up, w_down, w_out = ffn16
    y_long, y_short = _out_ffn(rows(*long), rows(*short), w_out, p["norm_ffn"], w_gate, w_up, w_down,
                               norm_final.reshape(1, D_MODEL), tm)
    return y_long.reshape(long[0].shape), y_short.reshape(short[0].shape)


def kernel(x_prompt, x_sample, state_conv, state_rec, cache_win_k, cache_win_v, norm_mix, w_in, w_conv, a_log,
           dt_bias, norm_out_a, norm_out_b, w_out, norm_ffn, w_gate, w_up, w_down, norm_final):
    depth = w_in.shape[0]
    assert depth == 1, "the final norm is fused into the block of a single-layer trunk"
    p = _layer_params(norm_mix, w_in, w_conv, a_log, dt_bias, norm_out_a, norm_out_b, w_out, norm_ffn,
                      w_gate, w_up, w_down, 0)
    bp, tp, _ = x_prompt.shape
    bs, ts, _ = x_sample.shape
    tm_p, tm_s = 512, bs * ts
    gdn_args = (p["w_conv"], p["gate_par"], p["norm_a"])

    sa, sb, sc, s_conv0, s_done, _ = _project(x_sample, state_conv[0], p, tm_s)
    n_past = cache_win_k.shape[2]
    to_t = lambda a: jnp.transpose(a, (0, 2, 3, 1)).reshape(bs, B_WIDTH, n_past)
    from_t = lambda a: jnp.transpose(a.reshape(a.shape[0], B_HEADS, B_HEAD_DIM, a.shape[2]), (0, 3, 1, 2))
    win_kt, win_vt = to_t(cache_win_k[0]), to_t(cache_win_v[0])

    def sample_attention_of_grid(grid):
        if grid[0] * grid[1] != bs:
            return None
        return _attn_sample_parts(sb, win_kt, win_vt, p["norm_b_tiled"], lambda i, j: i * grid[1] + j)

    pa, pb, pc, p_conv0, p_done, ffn16 = _project(
        x_prompt, jnp.zeros((bp, CONV_W - 1, CONV_DIM), F32), p, tm_p,
        lambda grid: _cast_weights_parts(p["ffn_f32"], grid))
    if not ffn16:
        ffn16 = tuple(w.astype(BF16) for w in p["ffn_f32"])
    zero_rec = jnp.zeros((bp, A_HEADS, A_DK, A_DV), F32)
    (o_a_p, pr), side = _gdn_pair(pa, p_conv0, zero_rec, *gdn_args, p_done, sample_attention_of_grid)
    o_b_s, new_kt, new_vt = side if side else _attn_sample(sb, win_kt, win_vt, p["norm_b_tiled"])
    (o_a_s, sr), _ = _gdn_pair(sa, s_conv0, state_rec[0], *gdn_args, s_done)

    o_b_p, pkt, pvt = _attn_prompt(pb, p["norm_b_pair"])

    yp, ys = _finish((x_prompt, o_a_p, o_b_p), (x_sample, o_a_s, o_b_s), p, ffn16, norm_final, tm_p)
    return (yp, ys, pc[None], pr[None], from_t(pkt)[None], from_t(pvt)[None], sc[None], sr[None],
            from_t(new_kt)[None], from_t(new_vt)[None])
```

```python
import functools

import jax
import jax.numpy as jnp
from jax import lax
from jax.experimental import pallas as pl
from jax.experimental.pallas import tpu as pltpu

F32 = jnp.float32
BF16 = jnp.bfloat16

D_MODEL = 1024
A_HEADS = 4
A_DK = 128
A_DV = 128
A_WIDTH = A_HEADS * A_DV
CONV_W = 4
CONV_DIM = 2 * A_HEADS * A_DK + A_HEADS * A_DV
B_HEADS = 8
B_HEAD_DIM = 64
B_WIDTH = B_HEADS * B_HEAD_DIM
DILATED = ((128, 1), (512, 4), (2048, 16))
STEPS = 128
MAX_WINDOW = 2048
EPS = 1e-6
NEG = -1e30
CHUNK = 64
GATE_PAD = 128
COL_Z = CONV_DIM
COL_G = CONV_DIM + A_WIDTH
COLS_A = COL_G + GATE_PAD
COLS_B = 3 * B_WIDTH
VMEM_LIMIT = 56 * 1024 * 1024


def _dot(a, b):
    return jnp.dot(a.astype(BF16), b.astype(BF16), preferred_element_type=F32)


def _dot_nt(a, b):
    return lax.dot_general(a.astype(BF16), b.astype(BF16), (((1,), (1,)), ((), ())),
                           preferred_element_type=F32)


def _dot_tn(a, b):
    return lax.dot_general(a.astype(BF16), b.astype(BF16), (((0,), (0,)), ((), ())),
                           preferred_element_type=F32)


def _split2(a):
    hi = a.astype(BF16)
    lo = (a - hi.astype(F32)).astype(BF16)
    return hi, lo


def _sigmoid(x):
    return 1.0 / (1.0 + jnp.exp(-x))


def _silu(x):
    return x * _sigmoid(x)


def _softplus(x):
    return jnp.maximum(x, 0.0) + jnp.log(1.0 + jnp.exp(-jnp.abs(x)))


def _rms(x, w):
    return x * lax.rsqrt(jnp.mean(x * x, axis=-1, keepdims=True) + EPS) * w


def _proj(h, wt):
    return lax.dot_general(h, wt, (((1,), (1,)), ((), ())), preferred_element_type=F32)


def _norm_proj_kernel(x_ref, nw_ref, wa_ref, wg_ref, wb_ref, oa_ref, ob_ref):
    h = _rms(x_ref[...], nw_ref[...]).astype(BF16)
    oa_ref[:, :COL_G] = _proj(h, wa_ref[...])
    oa_ref[:, COL_G:] = _proj(h, wg_ref[...])
    ob_ref[...] = _proj(h, wb_ref[...])


def _weight_specs():
    zero = lambda *g: (0, 0)
    return [pl.BlockSpec((c, D_MODEL), zero, pipeline_mode=pl.Buffered(1)) for c in (COL_G, GATE_PAD, COLS_B)]


def _norm_proj(x2d, norm_w, w_p, tm):
    t = x2d.shape[0]
    once = pl.Buffered(1)
    return pl.pallas_call(
        _norm_proj_kernel,
        grid=(t // tm,),
        in_specs=[pl.BlockSpec((tm, D_MODEL), lambda i: (i, 0)),
                  pl.BlockSpec((1, D_MODEL), lambda i: (0, 0), pipeline_mode=once)] + _weight_specs(),
        out_specs=[pl.BlockSpec((tm, COLS_A), lambda i: (i, 0)),
                   pl.BlockSpec((tm, COLS_B), lambda i: (i, 0))],
        out_shape=[jax.ShapeDtypeStruct((t, COLS_A), F32),
                   jax.ShapeDtypeStruct((t, COLS_B), F32)],
        compiler_params=pltpu.CompilerParams(dimension_semantics=("arbitrary",),
                                             vmem_limit_bytes=VMEM_LIMIT),
        name="norm_proj",
    )(x2d, norm_w, *w_p)


def _causal_conv_silu(x_all, wconv_ref, n):
    conv = x_all[8:8 + n] * wconv_ref[CONV_W - 1:CONV_W, :]
    for back in range(1, CONV_W):
        tap = CONV_W - 1 - back
        conv = conv + pltpu.roll(x_all, back, axis=0)[8:8 + n] * wconv_ref[tap:tap + 1, :]
    return _silu(conv)


def _l2norm(x):
    return x * lax.rsqrt(jnp.sum(x * x, axis=-1, keepdims=True) + EPS)


def _norm_proj_conv_kernel(x_ref, nw_ref, wa_ref, wg_ref, wb_ref, wconv_ref, conv0_ref, oa_ref, ob_ref, tail_ref,
                           carry_ref, h_ref, raw_ref):
    i = pl.program_id(1)
    tm = x_ref.shape[0]
    wq = A_HEADS * A_DK
    assert CONV_DIM == 3 * wq and COLS_B == 3 * B_WIDTH
    dd = functools.partial(jnp.dot, preferred_element_type=F32)

    @pl.when(i == 0)
    def _():
        carry_ref[...] = conv0_ref[...]

    def conv_cols(g):
        cs = slice(g * wq, (g + 1) * wq)
        conv = _causal_conv_silu(jnp.concatenate([carry_ref[:, cs], raw_ref[:, cs]], axis=0),
                                 wconv_ref.at[:, cs], tm)
        carry_ref[:, cs] = raw_ref[tm - 8:tm, cs]
        if g == 2:
            oa_ref[:, cs] = conv
            return
        for hd in range(A_HEADS):
            hs = slice(hd * A_DK, (hd + 1) * A_DK)
            out = _l2norm(conv[:, hs])
            oa_ref[:, g * wq + hd * A_DK:g * wq + (hd + 1) * A_DK] = out * (A_DK ** -0.5) if g == 0 else out

    def phase(k):
        if k == 0:
            h_ref[...] = _rms(x_ref[...], nw_ref[...]).astype(BF16)
        h = h_ref[...]
        if k < 3:
            raw_ref[:, k * wq:(k + 1) * wq] = _proj(h, wa_ref[k * wq:(k + 1) * wq, :])
        if k == 0:
            oa_ref[:, CONV_DIM:COL_G] = _proj(h, wa_ref[CONV_DIM:, :])
        else:
            ob_ref[:, (k - 1) * B_WIDTH:k * B_WIDTH] = _proj(h, wb_ref[(k - 1) * B_WIDTH:k * B_WIDTH, :])
            conv_cols(k - 1)
        if k == 3:
            oa_ref[:, COL_G:] = _proj(h, wg_ref[...])

    for k in range(4):
        phase(k)

    @pl.when(i == pl.num_programs(1) - 1)
    def _():
        tail_ref[...] = raw_ref[tm - 8:tm, :]


def _norm_proj_conv(x, norm_w, w_p, w_conv, conv0, tm, side_of_grid=None):
    b, t, _ = x.shape
    assert t % tm == 0 and tm % 8 == 0
    once = pl.Buffered(1)
    grid = (b, t // tm)
    in_specs = [pl.BlockSpec((None, tm, D_MODEL), lambda i, j: (i, j, 0)),
                pl.BlockSpec((1, D_MODEL), lambda i, j: (0, 0), pipeline_mode=once)] + _weight_specs() + [
                pl.BlockSpec((CONV_W, CONV_DIM), lambda i, j: (0, 0), pipeline_mode=once),
                pl.BlockSpec((None, 8, CONV_DIM), lambda i, j: (i, 0, 0))]
    out_specs = [pl.BlockSpec((None, tm, COLS_A), lambda i, j: (i, j, 0)),
                 pl.BlockSpec((None, tm, COLS_B), lambda i, j: (i, j, 0)),
                 pl.BlockSpec((None, 8, CONV_DIM), lambda i, j: (i, 0, 0))]
    out_shape = [jax.ShapeDtypeStruct((b, t, COLS_A), F32),
                 jax.ShapeDtypeStruct((b, t, COLS_B), F32),
                 jax.ShapeDtypeStruct((b, 8, CONV_DIM), F32)]
    inputs = (x, norm_w, *w_p, w_conv, conv0)
    n_in, n_out, n_scratch = len(in_specs), len(out_specs), 3
    side = side_of_grid(grid) if side_of_grid is not None else None

    def kernel(*refs):
        n_side_in = len(side["in_specs"]) if side else 0
        if side:
            side["kernel"](*refs[n_in:n_in + n_side_in], *refs[n_in + n_side_in + n_out:len(refs) - n_scratch])
        _norm_proj_conv_kernel(*refs[:n_in], *refs[n_in + n_side_in:n_in + n_side_in + n_out],
                               *refs[len(refs) - n_scratch:])

    if side:
        inputs = inputs + tuple(side["inputs"])
        in_specs = in_specs + list(side["in_specs"])
        out_specs = out_specs + list(side["out_specs"])
        out_shape = out_shape + list(side["out_shape"])
    outs = pl.pallas_call(
        kernel,
        grid=grid,
        in_specs=in_specs,
        out_specs=out_specs,
        out_shape=out_shape,
        scratch_shapes=[pltpu.VMEM((8, CONV_DIM), F32), pltpu.VMEM((tm, D_MODEL), BF16),
                        pltpu.VMEM((tm, CONV_DIM), F32)],
        compiler_params=pltpu.CompilerParams(dimension_semantics=("arbitrary", "arbitrary"),
                                             vmem_limit_bytes=VMEM_LIMIT),
        name="norm_proj_conv",
    )(*inputs)
    return outs[:n_out], outs[n_out:]


def _cast_weights_parts(weights, grid):
    n_steps = grid[0] * grid[1]
    if n_steps % 2:
        return None
    per = n_steps // 2
    groups = (weights[:len(weights) // 2], weights[len(weights) // 2:])
    if any(w.shape[0] % (per * 16) for g in groups for w in g):
        return None
    step = lambda i, j: i * grid[1] + j
    specs, first = [], []
    for gi, g in enumerate(groups):
        for w in g:
            rows = w.shape[0] // per
            if gi == 0:
                specs.append(pl.BlockSpec((rows, w.shape[1]), lambda i, j: (jnp.minimum(step(i, j), per - 1), 0)))
            else:
                specs.append(pl.BlockSpec((rows, w.shape[1]), lambda i, j: (jnp.maximum(step(i, j) - per, 0), 0)))
            first.append(gi == 0)

    def kernel(*refs):
        n = len(first)
        s = pl.program_id(0) * grid[1] + pl.program_id(1)
        for src, dst, is_first in zip(refs[:n], refs[n:], first):
            def cast(src=src, dst=dst):
                dst[...] = src[...].astype(BF16)
            pl.when(s < per if is_first else s >= per)(cast)

    return dict(kernel=kernel, inputs=tuple(weights), in_specs=specs, out_specs=specs,
                out_shape=[jax.ShapeDtypeStruct(w.shape, BF16) for w in weights])


GDN_ROWS = 8 * CHUNK


HALF = 128 // CHUNK


def _gdn_pair_kernel(xa_ref, z_ref, gt_ref, conv0_ref, s0_ref, wconv_ref, gp_ref, nwa_ref,
                     o_ref, snew_ref, ext_ref, s_ref, *, rows, nb, conv_done, side_work=None):
    n = pl.program_id(1)
    c = CHUNK
    n_chunks = -(-rows // c)
    rp = n_chunks * c
    assert A_HEADS % 2 == 0 and HALF == 2 and A_DK == 128 and A_DV == 128

    @pl.when(n == 0)
    def _():
        ext_ref[:, 0:8, :] = conv0_ref[...]
        s_ref[...] = s0_ref[...]

    if side_work is not None:
        side_work()

    row = lax.broadcasted_iota(jnp.int32, (rp, 1), 0)
    live = row < rows
    lane_g = lax.broadcasted_iota(jnp.int32, (rp, GATE_PAD), 1)
    gate_lanes = (lane_g >= A_HEADS) & (lane_g < 2 * A_HEADS) & live

    ii = lax.broadcasted_iota(jnp.int32, (c, 128), 0)
    ll = lax.broadcasted_iota(jnp.int32, (c, 128), 1)
    jj = ll & (c - 1)
    low = ll < c
    incl = ii >= jj
    strict = ii > jj
    eye = jnp.where(ii == jj, 1.0, 0.0).astype(F32)
    ti = lax.broadcasted_iota(jnp.int32, (c, c), 0)
    tj = lax.broadcasted_iota(jnp.int32, (c, c), 1)
    tril = jnp.where(ti >= tj, 1.0, 0.0).astype(BF16)
    dd = functools.partial(jnp.dot, preferred_element_type=F32)

    def exact_tril_dot(x):
        hi, lo = _split2(x)
        lo2 = (x - hi.astype(F32) - lo.astype(F32)).astype(BF16)
        return dd(tril, hi) + dd(tril, lo) + dd(tril, lo2)

    def bd(x):
        return jnp.concatenate([jnp.where(low, x, 0.0), jnp.where(low, 0.0, x)], axis=0).astype(BF16)

    def unstack(y):
        return jnp.where(low, y[:c], y[c:])

    nwa = nwa_ref[...]
    chains = []
    for bi in range(nb):
        if conv_done:
            assert rows == rp
            conv = xa_ref[bi]
        else:
            ext_ref[bi, 8:8 + rows, :] = xa_ref[bi]
            if rows < rp:
                ext_ref[bi, 8 + rows:8 + rp, :] = jnp.zeros((rp - rows, CONV_DIM), F32)
            conv = _causal_conv_silu(ext_ref[bi, 0:8 + rp, :], wconv_ref, rp)
            ext_ref[bi, 0:8, :] = ext_ref[bi, rp:rp + 8, :]
            if rows < rp:
                conv = jnp.where(live, conv, 0.0)

        gt = gt_ref[bi]
        if rows < rp:
            gt = jnp.concatenate([gt, jnp.zeros((rp - rows, GATE_PAD), F32)], axis=0)
        beta_all = jnp.where(live, _sigmoid(gt), 0.0)
        g_all = jnp.where(gate_lanes, -jnp.exp(gp_ref[0:1, :]) * _softplus(gt + gp_ref[1:2, :]), 0.0)

        for ci in range(n_chunks):
            rs = slice(ci * c, (ci + 1) * c)
            gc_all = exact_tril_dot(g_all[rs])
            for p in range(A_HEADS // 2):
                hd = []
                for h in (2 * p, 2 * p + 1):
                    q = conv[rs, h * A_DK:(h + 1) * A_DK]
                    k = conv[rs, A_HEADS * A_DK + h * A_DK:A_HEADS * A_DK + (h + 1) * A_DK]
                    v = conv[rs, 2 * A_HEADS * A_DK + h * A_DV:2 * A_HEADS * A_DK + (h + 1) * A_DV]
                    if not conv_done:
                        q = _l2norm(q) * (A_DK ** -0.5)
                        k = _l2norm(k)
                    beta = beta_all[rs, h:h + 1]
                    g = g_all[rs, A_HEADS + h:A_HEADS + h + 1]
                    gc = gc_all[:, A_HEADS + h:A_HEADS + h + 1]
                    gl = gc_all[c - 1:c, A_HEADS + h:A_HEADS + h + 1]
                    kb = k * beta
                    hd.append(dict(q=q, k=k, kb=kb, g=g, qg=q * jnp.exp(gc), kd=k * jnp.exp(gl - gc),
                                   dl=jnp.exp(gl),
                                   rhs=jnp.concatenate([v * beta, kb * jnp.exp(gc)], axis=1)))
                chains.append(dict(bi=bi, ci=ci, p=p, hd=hd))

    for ch in chains:
        a, b = ch["hd"]
        ch["e_mat"] = exact_tril_dot(jnp.where(strict, jnp.where(low, a["g"], b["g"]), 0.0))
    for ch in chains:
        a, b = ch["hd"]
        k_st = jnp.concatenate([a["k"], b["k"]], axis=0)
        ch["kk"] = _dot_nt(jnp.concatenate([a["kb"], b["kb"]], axis=0), k_st)
        ch["qk"] = _dot_nt(jnp.concatenate([a["q"], b["q"]], axis=0), k_st)
    for ch in chains:
        decay = jnp.where(incl, jnp.exp(jnp.where(incl, ch["e_mat"], 0.0)), 0.0)
        ch["lower"] = jnp.where(strict, unstack(ch["kk"]) * decay, 0.0)
        ch["qk"] = jnp.where(incl, unstack(ch["qk"]) * decay, 0.0)
    def joining(level):
        return strict & ((ii >> (level + 1)) == (jj >> (level + 1))) & ((ii >> level) != (jj >> level))

    for ch in chains:
        ch["t"] = eye - jnp.where(joining(0), ch["lower"], 0.0)
    for level in range(1, c.bit_length() - 1):
        join = joining(level)
        for ch in chains:
            ch["te"] = jnp.dot(ch["t"].astype(BF16), bd(jnp.where(join, ch["lower"], 0.0)),
                               preferred_element_type=F32)
        for ch in chains:
            ch["t"] = ch["t"] - jnp.dot(ch["te"].astype(BF16), bd(ch["t"]), preferred_element_type=F32)
    for ch in chains:
        lh, ll_ = _split2(ch["lower"])
        th = ch["t"].astype(BF16).astype(F32)
        tl = ch["t"] - th
        bth = bd(th)
        lt = dd(lh, bth) + dd(lh, bd(tl)) + dd(ll_, bth)
        ch["res"] = eye - ch["t"] - lt
    for ch in chains:
        ch["t"] = ch["t"] + jnp.dot(ch["t"].astype(BF16), bd(ch["res"]), preferred_element_type=F32)
    for ch in chains:
        a, b = ch["hd"]
        uw = jnp.dot(bd(ch["t"]), jnp.concatenate([a["rhs"], b["rhs"]], axis=0).astype(BF16),
                     preferred_element_type=F32)
        a["u"], a["w"] = uw[:c, :A_DV], uw[:c, A_DV:]
        b["u"], b["w"] = uw[c:, :A_DV], uw[c:, A_DV:]
        ch["qkb"] = bd(ch["qk"])

    by_key = {(ch["bi"], ch["ci"], ch["p"]): ch for ch in chains}
    states = {(bi, h): s_ref[bi, h] for bi in range(nb) for h in range(A_HEADS)}
    for ci in range(n_chunks):
        cur = [by_key[(bi, ci, p)] for bi in range(nb) for p in range(A_HEADS // 2)]
        for ch in cur:
            for hh, d in enumerate(ch["hd"]):
                ws = _dot(jnp.concatenate([d["w"], d["qg"]], axis=0), states[(ch["bi"], 2 * ch["p"] + hh)])
                d["e"] = d["u"] - ws[:c]
                d["o"] = ws[c:]
        for ch in cur:
            a, b = ch["hd"]
            o2 = jnp.dot(ch["qkb"], jnp.concatenate([a["e"], b["e"]], axis=0).astype(BF16),
                         preferred_element_type=F32)
            a["o"] = a["o"] + o2[:c]
            b["o"] = b["o"] + o2[c:]
        for ch in cur:
            for hh, d in enumerate(ch["hd"]):
                key = (ch["bi"], 2 * ch["p"] + hh)
                states[key] = states[key] * d["dl"] + _dot_tn(d["kd"], d["e"])
        r0 = ci * c
        r1 = min(rows, r0 + c)
        for ch in cur:
            for hh, d in enumerate(ch["hd"]):
                h = 2 * ch["p"] + hh
                o = d["o"] if r1 - r0 == c else d["o"][:r1 - r0]
                o_ref[ch["bi"], r0:r1, h * A_DV:(h + 1) * A_DV] = (
                    _rms(o, nwa) * _silu(z_ref[ch["bi"], r0:r1, h * A_DV:(h + 1) * A_DV]))
    for (bi, h), s in states.items():
        s_ref[bi, h] = s

    @pl.when(n == pl.num_programs(1) - 1)
    def _():
        snew_ref[...] = s_ref[...]


def _gdn_pair(proj, conv0, s0, w_conv, gate_par, norm_a, conv_done, side_of_grid=None):
    b, t, _ = proj.shape
    rows = min(GDN_ROWS, t)
    assert t % rows == 0 and (rows % CHUNK == 0 or rows == t)
    n = t // rows
    nb = max(1, min(b, GDN_ROWS // (-(-rows // CHUNK) * CHUNK)))
    assert b % nb == 0
    rp = 0 if conv_done else -(-rows // CHUNK) * CHUNK
    in_specs = [pl.BlockSpec((nb, rows, CONV_DIM), lambda i, j: (i, j, 0)),
                pl.BlockSpec((nb, rows, A_WIDTH), lambda i, j: (i, j, COL_Z // A_WIDTH)),
                pl.BlockSpec((nb, rows, GATE_PAD), lambda i, j: (i, j, COL_G // GATE_PAD)),
                pl.BlockSpec((nb, 8, CONV_DIM), lambda i, j: (i, 0, 0)),
                pl.BlockSpec((nb, A_HEADS, A_DK, A_DV), lambda i, j: (i, 0, 0, 0)),
                pl.BlockSpec((CONV_W, CONV_DIM), lambda i, j: (0, 0)),
                pl.BlockSpec((8, GATE_PAD), lambda i, j: (0, 0)),
                pl.BlockSpec((1, A_DV), lambda i, j: (0, 0))]
    out_specs = [pl.BlockSpec((nb, rows, A_WIDTH), lambda i, j: (i, j, 0)),
                 pl.BlockSpec((nb, A_HEADS, A_DK, A_DV), lambda i, j: (i, 0, 0, 0))]
    out_shape = [jax.ShapeDtypeStruct((b, t, A_WIDTH), F32),
                 jax.ShapeDtypeStruct((b, A_HEADS, A_DK, A_DV), F32)]
    inputs = (proj, proj, proj, conv0, s0, w_conv, gate_par, norm_a)
    n_in, n_out = len(in_specs), len(out_specs)
    side = side_of_grid((b // nb, n)) if side_of_grid is not None else None

    def kernel(*refs):
        n_side_in = len(side["in_specs"]) if side else 0
        gin, sin = refs[:n_in], refs[n_in:n_in + n_side_in]
        gout = refs[n_in + n_side_in:n_in + n_side_in + n_out]
        sout = refs[n_in + n_side_in + n_out:len(refs) - 2]
        side_work = (lambda: side["kernel"](*sin, *sout)) if side else None
        _gdn_pair_kernel(*gin, *gout, *refs[len(refs) - 2:], rows=rows, nb=nb, conv_done=conv_done,
                         side_work=side_work)

    if side:
        inputs = inputs + tuple(side["inputs"])
        in_specs = in_specs + list(side["in_specs"])
        out_specs = out_specs + list(side["out_specs"])
        out_shape = out_shape + list(side["out_shape"])
    outs = pl.pallas_call(
        kernel,
        grid=(b // nb, n),
        in_specs=in_specs,
        out_specs=out_specs,
        out_shape=out_shape,
        scratch_shapes=[pltpu.VMEM((nb, 8 + rp, CONV_DIM), F32),
                        pltpu.VMEM((nb, A_HEADS, A_DK, A_DV), F32)],
        compiler_params=pltpu.CompilerParams(dimension_semantics=("arbitrary", "arbitrary"),
                                             vmem_limit_bytes=VMEM_LIMIT),
        name="gdn_side" if side else "gdn",
    )(*inputs)
    return outs[:n_out], outs[n_out:]


def _slope(h):
    return 2.0 ** (-8.0 * (h + 1) / B_HEADS)


ATTN_SB = MAX_WINDOW
ATTN_SKEW = 2
PAIR = 128 // B_HEAD_DIM


def _attn_fused_kernel(q_ref, kp_ref, kc_ref, vp_ref, vc_ref, nwb_ref, o_ref, kt_ref, vt_ref, *stats):
    hp = pl.program_id(1)
    n = pl.program_id(2)
    sb = ATTN_SB
    blk = STEPS

    lane_q = lax.broadcasted_iota(jnp.int32, (blk, 128), 1)
    low_q = lane_q < B_HEAD_DIM
    qq = lax.broadcasted_iota(jnp.int32, (2 * blk, 2 * blk), 0)
    kk = lax.broadcasted_iota(jnp.int32, (2 * blk, 2 * blk), 1)
    steps_back = (qq & (blk - 1)) + blk - kk
    in_band = (steps_back >= 0) & (steps_back <= STEPS)
    slope_lo = jnp.float32(_slope(0))
    slope_hi = jnp.float32(_slope(1))
    for i in range(1, B_HEADS // PAIR):
        slope_lo = jnp.where(hp == i, _slope(PAIR * i), slope_lo)
        slope_hi = jnp.where(hp == i, _slope(PAIR * i + 1), slope_hi)
    slope = jnp.where(qq < blk, slope_lo, slope_hi)

    biases = []
    for (_, dil) in DILATED:
        bias = jnp.where(in_band, -slope * (steps_back * dil).astype(F32), NEG)
        biases.append((bias, jnp.where(kk >= blk, bias, NEG)))
    n_blk = sb // blk
    work = [(bi, it) for bi in range(len(DILATED)) for it in range(n_blk)]

    def rows(dil, start, size):
        return pl.ds(start, size) if dil == 1 else pl.ds(start, size, stride=dil)

    def key_tile(prev_ref, cur_ref, dil, k0):
        if k0 >= sb:
            return cur_ref[rows(dil, k0 - sb, 2 * blk), :].astype(BF16)
        assert k0 + blk * dil >= sb > k0 + (blk - 1) * dil
        return jnp.concatenate([prev_ref[rows(dil, k0, blk), :].astype(BF16),
                                cur_ref[rows(dil, k0 + blk * dil - sb, blk), :].astype(BF16)], axis=0)

    def issue_scores(bi, it):
        dil = DILATED[bi][1]
        q0 = (it // dil) * (blk * dil) + it % dil
        k0 = sb + q0 - blk * dil
        q = q_ref[rows(dil, q0, blk), :] * (B_HEAD_DIM ** -0.5)
        q2 = jnp.concatenate([jnp.where(low_q, q, 0.0), jnp.where(low_q, 0.0, q)], axis=0).astype(BF16)
        k = key_tile(kp_ref, kc_ref, dil, k0)
        return dict(bi=bi, dil=dil, q0=q0, k0=k0,
                    s=lax.dot_general(q2, k, (((1,), (1,)), ((), ())), preferred_element_type=F32))

    def softmax(w):
        bias, bias_first = biases[w["bi"]]
        s = w.pop("s") + (jnp.where(n == 0, bias_first, bias) if w["q0"] < blk * w["dil"] else bias)
        m = jnp.max(s, axis=-1, keepdims=True)
        p = jnp.exp(s - m)
        w.update(m=m, l=jnp.sum(p, axis=-1, keepdims=True), p=p.astype(BF16))

    def issue_values(w):
        dil, q0 = w["dil"], w["q0"]
        m_s, l_s, num_s = stats[3 * w["bi"]:3 * w["bi"] + 3]
        v = key_tile(vp_ref, vc_ref, dil, w["k0"])
        pv = jnp.dot(w["p"], v, preferred_element_type=F32)
        m_s[rows(dil, q0, blk), :] = jnp.where(low_q, w["m"][:blk], w["m"][blk:])
        l_s[rows(dil, q0, blk), :] = jnp.where(low_q, w["l"][:blk], w["l"][blk:])
        num_s[rows(dil, q0, blk), :] = jnp.where(low_q, pv[:blk], pv[blk:])

    inflight = {}
    for t in range(len(work) + 2 * ATTN_SKEW):
        if t < len(work):
            inflight[t] = issue_scores(*work[t])
        if 0 <= t - ATTN_SKEW < len(work):
            softmax(inflight[t - ATTN_SKEW])
        if 0 <= t - 2 * ATTN_SKEW < len(work):
            issue_values(inflight.pop(t - 2 * ATTN_SKEW))

    nwb = nwb_ref[...]

    def combine(i, carry):
        rs = pl.ds(pl.multiple_of(i * blk, blk), blk)
        ms = [stats[3 * bi][rs, :] for bi in range(len(DILATED))]
        ls = [stats[3 * bi + 1][rs, :] for bi in range(len(DILATED))]
        nums = [stats[3 * bi + 2][rs, :] for bi in range(len(DILATED))]
        mx = jnp.maximum(jnp.maximum(ms[0], ms[1]), ms[2])
        ws = [jnp.exp(m - mx) for m in ms]
        den = ws[0] * ls[0] + ws[1] * ls[1] + ws[2] * ls[2]
        o = (ws[0] * nums[0] + ws[1] * nums[1] + ws[2] * nums[2]) / den
        sq = o * o
        ss_lo = jnp.sum(jnp.where(low_q, sq, 0.0), axis=-1, keepdims=True)
        ss_hi = jnp.sum(jnp.where(low_q, 0.0, sq), axis=-1, keepdims=True)
        mean_sq = jnp.where(low_q, ss_lo, ss_hi) / B_HEAD_DIM
        o_ref[rs, :] = o * lax.rsqrt(mean_sq + EPS) * nwb
        return carry

    lax.fori_loop(0, sb // blk, combine, 0, unroll=True)

    @pl.when(n == pl.num_programs(2) - 1)
    def _():
        kt_ref[...] = kc_ref[...].T
        vt_ref[...] = vc_ref[...].T


def _attn_prompt(qkv, norm_b_pair):
    b, s, _ = qkv.shape
    sb = ATTN_SB
    assert s % sb == 0 and sb == min(MAX_WINDOW, s)
    assert all(w == STEPS * d and sb % w == 0 and d & (d - 1) == 0 for (w, d) in DILATED)
    n_pairs = B_HEADS // PAIR
    blk = (None, sb, 128)
    prev = lambda c0: pl.BlockSpec(blk, lambda i, p, n: (i, jnp.maximum(n - 1, 0), c0 + p))
    cur = lambda c0: pl.BlockSpec(blk, lambda i, p, n: (i, n, c0 + p))
    return pl.pallas_call(
        _attn_fused_kernel,
        grid=(b, n_pairs, s // sb),
        in_specs=[cur(0), prev(n_pairs), cur(n_pairs), prev(2 * n_pairs), cur(2 * n_pairs),
                  pl.BlockSpec((1, 128), lambda i, p, n: (0, 0))],
        out_specs=[pl.BlockSpec(blk, lambda i, p, n: (i, n, p)),
                   pl.BlockSpec((None, 128, sb), lambda i, p, n: (i, p, 0)),
                   pl.BlockSpec((None, 128, sb), lambda i, p, n: (i, p, 0))],
        out_shape=[jax.ShapeDtypeStruct((b, s, B_WIDTH), F32),
                   jax.ShapeDtypeStruct((b, B_WIDTH, sb), F32),
                   jax.ShapeDtypeStruct((b, B_WIDTH, sb), F32)],
        scratch_shapes=[pltpu.VMEM((sb, 128), F32)] * (3 * len(DILATED)),
        compiler_params=pltpu.CompilerParams(
            dimension_semantics=("arbitrary", "arbitrary", "arbitrary"), vmem_limit_bytes=VMEM_LIMIT),
        name="attn_prompt",
    )(qkv, qkv, qkv, qkv, qkv, norm_b_pair)


def _attn_sample_kernel(q_ref, kn_ref, vn_ref, kt_ref, vt_ref, nwb_ref, o_ref, kto_ref, vto_ref, *, n_past, t):
    kn = kn_ref[...]
    vn = vn_ref[...]
    kt = kt_ref[...]
    vt = vt_ref[...]

    prow = lax.broadcasted_iota(jnp.int32, (t, 128), 0)
    plane = lax.broadcasted_iota(jnp.int32, (t, 128), 1)
    place = jnp.where(plane == 128 - t + prow, 1.0, 0.0).astype(BF16)
    tail_lanes = lax.broadcasted_iota(jnp.int32, (B_WIDTH, 128), 1) >= 128 - t
    tn = functools.partial(lax.dot_general, dimension_numbers=(((0,), (0,)), ((), ())),
                           preferred_element_type=F32)

    def shifted(win, new, out_ref):
        hi, lo = _split2(new)
        lo2 = (new - hi.astype(F32) - lo.astype(F32)).astype(BF16)
        new_t = tn(hi, place) + tn(lo, place) + tn(lo2, place)
        sh = pltpu.roll(win, n_past - t, axis=1)
        out_ref[:, :n_past - 128] = sh[:, :n_past - 128]
        out_ref[:, n_past - 128:] = jnp.where(tail_lanes, new_t, sh[:, n_past - 128:])

    shifted(kt, kn, kto_ref)
    shifted(vt, vn, vto_ref)

    rq = B_HEADS * t
    t_bits = t.bit_length() - 1
    d_bits = B_HEAD_DIM.bit_length() - 1
    rowi = lax.broadcasted_iota(jnp.int32, (rq, B_WIDTH), 0)
    lanei = lax.broadcasted_iota(jnp.int32, (rq, B_WIDTH), 1)
    own = lax.shift_right_logical(rowi, t_bits) == lax.shift_right_logical(lanei, d_bits)
    q_rep = jnp.concatenate([q_ref[...]] * B_HEADS, axis=0)
    q_blk = jnp.where(own, q_rep, 0.0).astype(BF16)
    zpad = jnp.zeros((128 - t, B_WIDTH), BF16)
    kn_pad = jnp.concatenate([kn.astype(BF16), zpad], axis=0)
    vn_pad = jnp.concatenate([vn.astype(BF16), zpad], axis=0)
    vt16 = vt.astype(BF16)
    scale = B_HEAD_DIM ** -0.5
    s_c = jnp.dot(q_blk, kt.astype(BF16), preferred_element_type=F32) * scale
    s_n = lax.dot_general(q_blk, kn_pad, (((1,), (1,)), ((), ())), preferred_element_type=F32) * scale

    r1 = lax.broadcasted_iota(jnp.int32, (rq, 1), 0)
    head = lax.shift_right_logical(r1, t_bits)
    slope = jnp.zeros((rq, 1), F32)
    for h in range(B_HEADS):
        slope = jnp.where(head == h, _slope(h), slope)
    tok = r1 & (t - 1)
    delta_c = n_past + tok - lax.broadcasted_iota(jnp.int32, (rq, n_past), 1)
    key_n = lax.broadcasted_iota(jnp.int32, (rq, 128), 1)
    delta_n = tok - key_n
    s_c = s_c - slope * delta_c.astype(F32)
    s_n = s_n - slope * delta_n.astype(F32)
    live_n = (key_n < t) & (delta_n >= 0)

    ms, ls, pcs, pns = [], [], [], []
    for (_, dil) in DILATED:
        valid_c = ((delta_c & (dil - 1)) == 0) & (delta_c <= STEPS * dil)
        valid_n = live_n & ((delta_n & (dil - 1)) == 0) & (delta_n <= STEPS * dil)
        sd_c = jnp.where(valid_c, s_c, NEG)
        sd_n = jnp.where(valid_n, s_n, NEG)
        m = jnp.maximum(jnp.max(sd_c, axis=-1, keepdims=True), jnp.max(sd_n, axis=-1, keepdims=True))
        p_c = jnp.exp(sd_c - m)
        p_n = jnp.exp(sd_n - m)
        ms.append(m)
        ls.append(jnp.sum(p_c, axis=-1, keepdims=True) + jnp.sum(p_n, axis=-1, keepdims=True))
        pcs.append(p_c.astype(BF16))
        pns.append(p_n.astype(BF16))
    num_all = (lax.dot_general(jnp.concatenate(pcs, axis=0), vt16, (((1,), (1,)), ((), ())),
                               preferred_element_type=F32)
               + jnp.dot(jnp.concatenate(pns, axis=0), vn_pad, preferred_element_type=F32))
    nums = [num_all[i * rq:(i + 1) * rq] for i in range(len(DILATED))]
    mx = jnp.maximum(jnp.maximum(ms[0], ms[1]), ms[2])
    ws = [jnp.exp(m - mx) for m in ms]
    den = ws[0] * ls[0] + ws[1] * ls[1] + ws[2] * ls[2]
    o = (ws[0] * nums[0] + ws[1] * nums[1] + ws[2] * nums[2]) / den
    o = jnp.where(own, o, 0.0)
    o = o * lax.rsqrt(jnp.sum(o * o, axis=-1, keepdims=True) / B_HEAD_DIM + EPS) * nwb_ref[...]
    acc = o[0:t, :]
    for h in range(1, B_HEADS):
        acc = acc + o[h * t:(h + 1) * t, :]
    o_ref[...] = acc


def _attn_sample_parts(qkv, win_kt, win_vt, norm_b_tiled, seq_of_step):
    b, t, _ = qkv.shape
    n_past = win_kt.shape[2]
    assert n_past == MAX_WINDOW and n_past % 128 == 0 and t % 8 == 0 and t & (t - 1) == 0 and t <= 128
    assert all(d & (d - 1) == 0 for (_, d) in DILATED)
    col = lambda c: pl.BlockSpec((None, t, B_WIDTH), lambda *g: (seq_of_step(*g), 0, c))
    win = pl.BlockSpec((None, B_WIDTH, n_past), lambda *g: (seq_of_step(*g), 0, 0))
    return dict(
        kernel=functools.partial(_attn_sample_kernel, n_past=n_past, t=t),
        inputs=(qkv, qkv, qkv, win_kt, win_vt, norm_b_tiled),
        in_specs=[col(0), col(1), col(2), win, win, pl.BlockSpec((1, B_WIDTH), lambda *g: (0, 0))],
        out_specs=[col(0), win, win],
        out_shape=[jax.ShapeDtypeStruct((b, t, B_WIDTH), F32),
                   jax.ShapeDtypeStruct((b, B_WIDTH, n_past), F32),
                   jax.ShapeDtypeStruct((b, B_WIDTH, n_past), F32)])


def _attn_sample(qkv, win_kt, win_vt, norm_b_tiled):
    parts = _attn_sample_parts(qkv, win_kt, win_vt, norm_b_tiled, lambda i: i)
    return pl.pallas_call(
        parts["kernel"],
        grid=(qkv.shape[0],),
        in_specs=parts["in_specs"],
        out_specs=parts["out_specs"],
        out_shape=parts["out_shape"],
        compiler_params=pltpu.CompilerParams(dimension_semantics=("arbitrary",),
                                             vmem_limit_bytes=VMEM_LIMIT),
        name="attn_sample",
    )(*parts["inputs"])


def _out_ffn_rows(x_ref, oa_ref, ob_ref, y_ref, wo_ref, nf_ref, wg_ref, wu_ref, wd_ref, nfin_ref):
    mixed = jnp.concatenate([oa_ref[...], ob_ref[...]], axis=-1).astype(BF16)
    x1 = x_ref[...] + jnp.dot(mixed, wo_ref[...], preferred_element_type=F32)
    hf = _rms(x1, nf_ref[...]).astype(BF16)
    g = jnp.dot(hf, wg_ref[...], preferred_element_type=F32)
    u = jnp.dot(hf, wu_ref[...], preferred_element_type=F32)
    act = (_silu(g) * u).astype(BF16)
    x2 = x1 + jnp.dot(act, wd_ref[...], preferred_element_type=F32)
    y_ref[...] = _rms(x2, nfin_ref[...])


def _out_ffn_kernel(x_ref, oa_ref, ob_ref, xs_ref, oas_ref, obs_ref, *rest):
    weights, (y_ref, ys_ref) = rest[:-2], rest[-2:]
    i = pl.program_id(0)
    last = pl.num_programs(0) - 1
    pl.when(i < last)(lambda: _out_ffn_rows(x_ref, oa_ref, ob_ref, y_ref, *weights))
    pl.when(i == last)(lambda: _out_ffn_rows(xs_ref, oas_ref, obs_ref, ys_ref, *weights))


def _out_ffn(long_rows, short_rows, w_out, norm_ffn, w_gate, w_up, w_down, norm_final, tm):
    t, ts = long_rows[0].shape[0], short_rows[0].shape[0]
    d_ff = w_gate.shape[1]
    n = t // tm
    assert t % tm == 0 and ts % 8 == 0
    once = pl.Buffered(1)
    row = lambda w: pl.BlockSpec((tm, w), lambda i: (jnp.minimum(i, n - 1), 0))
    short = lambda w: pl.BlockSpec((ts, w), lambda i: (0, 0))
    full = lambda a, b: pl.BlockSpec((a, b), lambda i: (0, 0), pipeline_mode=once)
    widths = (D_MODEL, A_WIDTH, B_WIDTH)
    return pl.pallas_call(
        _out_ffn_kernel,
        grid=(n + 1,),
        in_specs=[row(w) for w in widths] + [short(w) for w in widths] + [
            full(D_MODEL, D_MODEL), full(1, D_MODEL),
            full(D_MODEL, d_ff), full(D_MODEL, d_ff), full(d_ff, D_MODEL), full(1, D_MODEL)],
        out_specs=[row(D_MODEL), short(D_MODEL)],
        out_shape=[jax.ShapeDtypeStruct((t, D_MODEL), F32), jax.ShapeDtypeStruct((ts, D_MODEL), F32)],
        compiler_params=pltpu.CompilerParams(dimension_semantics=("arbitrary",),
                                             vmem_limit_bytes=VMEM_LIMIT),
        name="out_ffn",
    )(*long_rows, *short_rows, w_out, norm_ffn, w_gate, w_up, w_down, norm_final)


def _split_w_in_kernel(w_ref, wa_ref, wg_ref, wb_ref):
    n_gate = 2 * A_HEADS
    wa_ref[...] = w_ref[0:COL_G, :].astype(BF16)
    gates = jnp.concatenate([w_ref[COL_G:COL_G + n_gate, :],
                             jnp.zeros((GATE_PAD - n_gate, w_ref.shape[1]), F32)], axis=0)
    wg_ref[...] = gates.astype(BF16)
    wb_ref[...] = w_ref[COL_G + n_gate:, :].astype(BF16)


def _split_w_in(wt):
    cols, d = wt.shape
    tl = 256
    assert d % tl == 0 and cols == COL_G + 2 * A_HEADS + COLS_B and (2 * A_HEADS) % 8 == 0
    return pl.pallas_call(
        _split_w_in_kernel,
        grid=(d // tl,),
        in_specs=[pl.BlockSpec((cols, tl), lambda j: (0, j))],
        out_specs=[pl.BlockSpec((c, tl), lambda j: (0, j)) for c in (COL_G, GATE_PAD, COLS_B)],
        out_shape=[jax.ShapeDtypeStruct((c, d), BF16) for c in (COL_G, GATE_PAD, COLS_B)],
        compiler_params=pltpu.CompilerParams(dimension_semantics=("arbitrary",),
                                             vmem_limit_bytes=VMEM_LIMIT),
        name="split_w_in",
    )(wt)


def _layer_params(norm_mix, w_in, w_conv, a_log, dt_bias, norm_out_a, norm_out_b, w_out, norm_ffn,
                  w_gate, w_up, w_down, layer):
    w_p = tuple(_split_w_in(jnp.transpose(w_in[layer])))
    gate_par = jnp.zeros((8, GATE_PAD), F32)
    gate_par = gate_par.at[0, A_HEADS:2 * A_HEADS].set(a_log[layer].astype(F32))
    gate_par = gate_par.at[1, A_HEADS:2 * A_HEADS].set(dt_bias[layer].astype(F32))
    return dict(
        norm_mix=norm_mix[layer].reshape(1, D_MODEL), w_p=w_p, w_conv=w_conv[layer], gate_par=gate_par,
        norm_a=norm_out_a[layer].reshape(1, A_DV),
        norm_b_pair=jnp.tile(norm_out_b[layer], PAIR).reshape(1, PAIR * B_HEAD_DIM),
        norm_b_tiled=jnp.tile(norm_out_b[layer], B_HEADS).reshape(1, B_WIDTH),
        norm_ffn=norm_ffn[layer].reshape(1, D_MODEL),
        ffn_f32=(w_gate[layer], w_up[layer], w_down[layer], w_out[layer]))


def _project(x, conv_buf, p, tm, side_of_grid=None):
    b, t, _ = x.shape
    assert t >= CONV_W - 1
    conv0 = jnp.pad(conv_buf, ((0, 0), (8 - (CONV_W - 1), 0), (0, 0)))
    conv_done = t % tm == 0
    side = ()
    if conv_done:
        (proj_a, proj_b, tail), side = _norm_proj_conv(x, p["norm_mix"], p["w_p"], p["w_conv"], conv0, tm,
                                                       side_of_grid)
        new_conv = tail[:, 8 - (CONV_W - 1):]
    else:
        proj_a, proj_b = _norm_proj(x.reshape(b * t, D_MODEL), p["norm_mix"], p["w_p"], tm)
        proj_a = proj_a.reshape(b, t, COLS_A)
        proj_b = proj_b.reshape(b, t, COLS_B)
        new_conv = proj_a[:, t - (CONV_W - 1):, :CONV_DIM]
    return proj_a, proj_b, new_conv, conv0, conv_done, side


def _finish(long, short, p, ffn16, norm_final, tm):
    rows = lambda x, o_a, o_b: (x.reshape(-1, D_MODEL), o_a.reshape(-1, A_WIDTH), o_b.reshape(-1, B_WIDTH))
    w_gate, w_up, w_down, w_out = ffn16
    y_long, y_short = _out_ffn(rows(*long), rows(*short), w_out, p["norm_ffn"], w_gate, w_up, w_down,
                               norm_final.reshape(1, D_MODEL), tm)
    return y_long.reshape(long[0].shape), y_short.reshape(short[0].shape)


def kernel(x_prompt, x_sample, state_conv, state_rec, cache_win_k, cache_win_v, norm_mix, w_in, w_conv, a_log,
           dt_bias, norm_out_a, norm_out_b, w_out, norm_ffn, w_gate, w_up, w_down, norm_final):
    depth = w_in.shape[0]
    assert depth == 1, "the final norm is fused into the block of a single-layer trunk"
    p = _layer_params(norm_mix, w_in, w_conv, a_log, dt_bias, norm_out_a, norm_out_b, w_out, norm_ffn,
                      w_gate, w_up, w_down, 0)
    bp, tp, _ = x_prompt.shape
    bs, ts, _ = x_sample.shape
    tm_p, tm_s = 512, bs * ts
    gdn_args = (p["w_conv"], p["gate_par"], p["norm_a"])

    sa, sb, sc, s_conv0, s_done, _ = _project(x_sample, state_conv[0], p, tm_s)
    n_past = cache_win_k.shape[2]
    to_t = lambda a: jnp.transpose(a, (0, 2, 3, 1)).reshape(bs, B_WIDTH, n_past)
    from_t = lambda a: jnp.transpose(a.reshape(a.shape[0], B_HEADS, B_HEAD_DIM, a.shape[2]), (0, 3, 1, 2))
    win_kt, win_vt = to_t(cache_win_k[0]), to_t(cache_win_v[0])

    def sample_attention_of_grid(grid):
        if grid[0] * grid[1] != bs:
            return None
        return _attn_sample_parts(sb, win_kt, win_vt, p["norm_b_tiled"], lambda i, j: i * grid[1] + j)

    pa, pb, pc, p_conv0, p_done, ffn16 = _project(
        x_prompt, jnp.zeros((bp, CONV_W - 1, CONV_DIM), F32), p, tm_p,
        lambda grid: _cast_weights_parts(p["ffn_f32"], grid))
    if not ffn16:
        ffn16 = tuple(w.astype(BF16) for w in p["ffn_f32"])
    zero_rec = jnp.zeros((bp, A_HEADS, A_DK, A_DV), F32)
    (o_a_p, pr), side = _gdn_pair(pa, p_conv0, zero_rec, *gdn_args, p_done, sample_attention_of_grid)
    o_b_s, new_kt, new_vt = side if side else _attn_sample(sb, win_kt, win_vt, p["norm_b_tiled"])
    (o_a_s, sr), _ = _gdn_pair(sa, s_conv0, state_rec[0], *gdn_args, s_done)

    o_b_p, pkt, pvt = _attn_prompt(pb, p["norm_b_pair"])

    yp, ys = _finish((x_prompt, o_a_p, o_b_p), (x_sample, o_a_s, o_b_s), p, ffn16, norm_final, tm_p)
    return (yp, ys, pc[None], pr[None], from_t(pkt)[None], from_t(pvt)[None], sc[None], sr[None],
            from_t(new_kt)[None], from_t(new_vt)[None])
```
